```python
import math
import jax, jax.numpy as jnp
from jax import lax
import numpy as np

D_MODEL = 1024
BATCH = 8
SEQ = 2048
DEPTH = 1
DEC_BATCH = 128
DEC_SEQ = 8
PAST_LEN = 16384
PAGE_SIZE = 128

RET_HEADS = 4
RET_DK = D_MODEL // (2 * RET_HEADS)
RET_DV = 2 * RET_DK
GLA_HEADS = 4
GLA_DK = D_MODEL // (2 * GLA_HEADS)
GLA_DV = D_MODEL // GLA_HEADS
GLA_GATE_RANK = 16
GLA_GATE_NORM = 16.0
CHUNK = 64
ROPE_BASE = 10000.0
N_EXPERTS = 32
TOP_K = 4
D_FF = D_MODEL
SWIGLU_LIMIT = 7.0
SWIGLU_ALPHA = 1.702
N_BRANCH = 2
EPS = 1e-6

RET_QK = RET_HEADS * RET_DK
RET_V = RET_HEADS * RET_DV
GLA_QK = GLA_HEADS * GLA_DK
GLA_V = GLA_HEADS * GLA_DV
SPLITS = (RET_QK, RET_QK, RET_V, RET_V, GLA_QK, GLA_QK, GLA_V, GLA_V, GLA_GATE_RANK, N_BRANCH * D_MODEL)
D_IN_PROJ = sum(SPLITS)

kernel_name = "hybrid_retnet_gla_moe_step"


def rms_norm(x, g):
    xf = x.astype(jnp.float32)
    y = xf * lax.rsqrt(jnp.mean(xf * xf, axis=-1, keepdims=True) + EPS)
    return y * g.astype(jnp.float32)


def rotary(x, pos):
    half = x.shape[-1] // 2
    inv = ROPE_BASE ** (-jnp.arange(half, dtype=jnp.float32) / half)
    ang = pos.astype(jnp.float32)[:, None] * inv[None, :]
    cos = jnp.cos(ang)[None, :, None, :]
    sin = jnp.sin(ang)[None, :, None, :]
    x1, x2 = x[..., :half], x[..., half:]
    return jnp.concatenate([x1 * cos - x2 * sin, x1 * sin + x2 * cos], axis=-1)


def retention_chunked(q, k, v, s0):
    B, H, T, dk = q.shape
    dv = v.shape[-1]
    C = math.gcd(CHUNK, T)
    N = T // C
    log_gamma = jnp.log1p(-jnp.exp2(-5.0 - jnp.arange(H, dtype=jnp.float32)))
    qc = q.reshape(B, H, N, C, dk)
    kc = k.reshape(B, H, N, C, dk)
    vc = v.reshape(B, H, N, C, dv)
    idx = jnp.arange(C, dtype=jnp.float32)
    rel = idx[:, None] - idx[None, :]
    decay = jnp.where(rel >= 0, jnp.exp(log_gamma[:, None, None] * jnp.maximum(rel, 0.0)), 0.0)
    scores = jnp.einsum('bhncd,bhnmd->bhncm', qc, kc) * decay[None, :, None]
    o_intra = jnp.einsum('bhncm,bhnmv->bhncv', scores, vc)
    k_to_end = kc * jnp.exp(log_gamma[:, None] * (C - 1 - idx))[None, :, None, :, None]
    u = jnp.einsum('bhncd,bhncv->bhndv', k_to_end, vc)
    chunk_decay = jnp.exp(log_gamma * C)[None, :, None, None]

    def step(s, u_n):
        return chunk_decay * s + u_n, s

    s_last, s_prev = lax.scan(step, s0, jnp.moveaxis(u, 2, 0))
    s_prev = jnp.moveaxis(s_prev, 0, 2)
    q_from_start = qc * jnp.exp(log_gamma[:, None] * (idx + 1.0))[None, :, None, :, None]
    o_cross = jnp.einsum('bhncd,bhndv->bhncv', q_from_start, s_prev)
    return (o_intra + o_cross).reshape(B, H, T, dv), s_last


def gla_chunked(q, k, v, log_a, s0):
    B, H, T, dk = q.shape
    dv = v.shape[-1]
    C = math.gcd(CHUNK, T)
    N = T // C
    qc = q.reshape(B, H, N, C, dk)
    kc = k.reshape(B, H, N, C, dk)
    vc = v.reshape(B, H, N, C, dv)
    b = jnp.cumsum(log_a.reshape(B, H, N, C, dk), axis=3)
    b_last = b[:, :, :, -1:, :]
    q_in = qc * jnp.exp(b)
    k_in = kc * jnp.exp(-b)
    causal = jnp.tril(jnp.ones((C, C), dtype=bool))
    scores = jnp.where(causal, jnp.einsum('bhncd,bhnmd->bhncm', q_in, k_in), 0.0)
    o_intra = jnp.einsum('bhncm,bhnmv->bhncv', scores, vc)
    k_end = kc * jnp.exp(b_last - b)
    u = jnp.einsum('bhncd,bhncv->bhndv', k_end, vc)
    dec = jnp.exp(b_last[:, :, :, 0, :])

    def step(s, inp):
        d_n, u_n = inp
        return d_n[..., None] * s + u_n, s

    s_last, s_prev = lax.scan(step, s0, (jnp.moveaxis(dec, 2, 0), jnp.moveaxis(u, 2, 0)))
    s_prev = jnp.moveaxis(s_prev, 0, 2)
    o_cross = jnp.einsum('bhncd,bhndv->bhncv', q_in, s_prev)
    return (o_intra + o_cross).reshape(B, H, T, dv), s_last


def to_heads(t, n_heads):
    B, T, _ = t.shape
    return t.reshape(B, T, n_heads, -1).transpose(0, 2, 1, 3)


def from_heads(t):
    B, H, T, d = t.shape
    return t.transpose(0, 2, 1, 3).reshape(B, T, H * d)


def mixer(h, pos0, s_ret, s_gla, w_in, w_gk_up, b_gk, g_gla_norm, w_ret_o, w_gla_o, w_out):
    B, T, _ = h.shape
    proj = h @ w_in
    points = [int(p) for p in np.cumsum(SPLITS)[:-1]]
    rq, rk, rv, rg, gq, gkk, gv, gg, glr, mg = jnp.split(proj, points, axis=-1)
    pos = pos0 + jnp.arange(T)
    rq_h = rotary(rq.reshape(B, T, RET_HEADS, RET_DK), pos).transpose(0, 2, 1, 3)
    rk_h = (rotary(rk.reshape(B, T, RET_HEADS, RET_DK), pos) * (RET_DK ** -0.5)).transpose(0, 2, 1, 3)
    o_ret, s_ret_new = retention_chunked(rq_h, rk_h, to_heads(rv, RET_HEADS), s_ret)
    mu = jnp.mean(o_ret, axis=-1, keepdims=True)
    var = jnp.mean(jnp.square(o_ret - mu), axis=-1, keepdims=True)
    o_ret = (o_ret - mu) * lax.rsqrt(var + EPS)
    out_ret = jax.nn.silu(rg) * from_heads(o_ret)
    log_a = jax.nn.log_sigmoid(glr @ w_gk_up + b_gk) / GLA_GATE_NORM
    o_gla, s_gla_new = gla_chunked(to_heads(gq * (GLA_DK ** -0.5), GLA_HEADS), to_heads(gkk, GLA_HEADS),
                                   to_heads(gv, GLA_HEADS), to_heads(log_a, GLA_HEADS), s_gla)
    o_gla = o_gla * lax.rsqrt(jnp.mean(o_gla * o_gla, axis=-1, keepdims=True) + EPS) * g_gla_norm
    out_gla = jax.nn.silu(gg) * from_heads(o_gla)
    gate_ret, gate_gla = jnp.split(jax.nn.sigmoid(mg), N_BRANCH, axis=-1)
    merged = gate_ret * (out_ret @ w_ret_o) + gate_gla * (out_gla @ w_gla_o)
    return merged @ w_out, s_ret_new, s_gla_new


def moe(h, w_router, b_router, w_up, b_up, w_down, b_down):
    B, T, D = h.shape
    hf = h.reshape(B * T, D)
    logits = (hf @ w_router + b_router).astype(jnp.float32)
    top_vals, top_idx = lax.top_k(logits, TOP_K)
    probs = jax.nn.softmax(top_vals, axis=-1)
    combine = jnp.sum(jax.nn.one_hot(top_idx, N_EXPERTS, dtype=jnp.float32) * probs[..., None], axis=1)
    y = jnp.zeros((B * T, D), jnp.float32)
    for e in range(N_EXPERTS):
        gu = hf @ w_up[e] + b_up[e]
        gate = jnp.minimum(gu[:, :D_FF], SWIGLU_LIMIT)
        up = jnp.clip(gu[:, D_FF:], -SWIGLU_LIMIT, SWIGLU_LIMIT)
        act = (up + 1.0) * gate * jax.nn.sigmoid(SWIGLU_ALPHA * gate)
        y = y + combine[:, e:e + 1] * (act @ w_down[e] + b_down[e])
    return y.reshape(B, T, D)


def trunk(x, c, s_ret_all, s_gla_all, pos0, w_ada, b_ada, g_norm_mix, g_norm_ffn, w_in, w_gk_up, b_gk,
          g_gla_norm, w_ret_o, w_gla_o, w_out, w_router, b_router, w_up, b_up, w_down, b_down, g_final):
    h = x.astype(jnp.float32)
    cf = jax.nn.silu(c.astype(jnp.float32))
    ret_states, gla_states = [], []
    for l in range(DEPTH):
        ada = cf @ w_ada[l] + b_ada[l]
        sh1, sc1, gt1, sh2, sc2, gt2 = [a[:, None, :] for a in jnp.split(ada, 6, axis=-1)]
        n1 = rms_norm(h, g_norm_mix[l]) * (1.0 + sc1) + sh1
        mix, sr, sg = mixer(n1, pos0, s_ret_all[l].astype(jnp.float32), s_gla_all[l].astype(jnp.float32),
                            w_in[l], w_gk_up[l], b_gk[l], g_gla_norm[l], w_ret_o[l], w_gla_o[l], w_out[l])
        h = h + gt1 * mix
        n2 = rms_norm(h, g_norm_ffn[l]) * (1.0 + sc2) + sh2
        h = h + gt2 * moe(n2, w_router[l], b_router[l], w_up[l], b_up[l], w_down[l], b_down[l])
        ret_states.append(sr)
        gla_states.append(sg)
    y = rms_norm(h, g_final).astype(x.dtype)
    return y, jnp.stack(ret_states).astype(x.dtype), jnp.stack(gla_states).astype(x.dtype)


def setup_inputs(seed: int = 0) -> dict:
    key = jax.random.key(seed)
    ks = jax.random.split(key, 32)
    nrm = lambda k, shape, s: jax.random.normal(k, shape, jnp.float32) * s
    L = DEPTH
    return {
        "x_prompt": nrm(ks[0], (BATCH, SEQ, D_MODEL), 1.0),
        "x_sample": nrm(ks[1], (DEC_BATCH, DEC_SEQ, D_MODEL), 1.0),
        "c_prompt": nrm(ks[2], (BATCH, D_MODEL), 1.0),
        "c_sample": nrm(ks[3], (DEC_BATCH, D_MODEL), 1.0),
        "state_ret": nrm(ks[4], (L, DEC_BATCH, RET_HEADS, RET_DK, RET_DV), 0.5),
        "state_gla": nrm(ks[5], (L, DEC_BATCH, GLA_HEADS, GLA_DK, GLA_DV), 0.3),
        "w_ada": nrm(ks[6], (L, D_MODEL, 6 * D_MODEL), 0.5 * D_MODEL ** -0.5),
        "b_ada": nrm(ks[7], (L, 6 * D_MODEL), 0.02),
        "g_norm_mix": 1.0 + nrm(ks[8], (L, D_MODEL), 0.02),
        "g_norm_ffn": 1.0 + nrm(ks[9], (L, D_MODEL), 0.02),
        "w_in": nrm(ks[10], (L, D_MODEL, D_IN_PROJ), D_MODEL ** -0.5),
        "w_gk_up": nrm(ks[11], (L, GLA_GATE_RANK, GLA_QK), GLA_GATE_RANK ** -0.5),
        "b_gk": nrm(ks[12], (L, GLA_QK), 0.1),
        "g_gla_norm": 1.0 + nrm(ks[13], (L, GLA_DV), 0.02),
        "w_ret_o": nrm(ks[14], (L, RET_V, D_MODEL), RET_V ** -0.5),
        "w_gla_o": nrm(ks[15], (L, GLA_V, D_MODEL), GLA_V ** -0.5),
        "w_out": nrm(ks[16], (L, D_MODEL, D_MODEL), D_MODEL ** -0.5),
        "w_router": nrm(ks[17], (L, D_MODEL, N_EXPERTS), D_MODEL ** -0.5),
        "b_router": nrm(ks[18], (L, N_EXPERTS), 0.01),
        "w_up": nrm(ks[19], (L, N_EXPERTS, D_MODEL, 2 * D_FF), D_MODEL ** -0.5),
        "b_up": nrm(ks[20], (L, N_EXPERTS, 2 * D_FF), 0.02),
        "w_down": nrm(ks[21], (L, N_EXPERTS, D_FF, D_MODEL), D_FF ** -0.5),
        "b_down": nrm(ks[22], (L, N_EXPERTS, D_MODEL), 0.02),
        "g_final": 1.0 + nrm(ks[23], (D_MODEL,), 0.02),
    }


def reference(x_prompt, x_sample, c_prompt, c_sample, state_ret, state_gla, w_ada, b_ada, g_norm_mix, g_norm_ffn,
              w_in, w_gk_up, b_gk, g_gla_norm, w_ret_o, w_gla_o, w_out, w_router, b_router, w_up, b_up,
              w_down, b_down, g_final):
    weights = (w_ada, b_ada, g_norm_mix, g_norm_ffn, w_in, w_gk_up, b_gk, g_gla_norm, w_ret_o, w_gla_o, w_out,
               w_router, b_router, w_up, b_up, w_down, b_down, g_final)
    zero_ret = jnp.zeros((DEPTH, x_prompt.shape[0], RET_HEADS, RET_DK, RET_DV), jnp.float32)
    zero_gla = jnp.zeros((DEPTH, x_prompt.shape[0], GLA_HEADS, GLA_DK, GLA_DV), jnp.float32)
    y_prompt, ret_prompt, gla_prompt = trunk(x_prompt, c_prompt, zero_ret, zero_gla, 0, *weights)
    y_sample, ret_sample, gla_sample = trunk(x_sample, c_sample, state_ret, state_gla, PAST_LEN, *weights)
    return (y_prompt, y_sample, ret_prompt, gla_prompt, ret_sample, gla_sample)
```

```python
import functools

import jax
import jax.numpy as jnp
from jax import lax
from jax.experimental import pallas as pl
from jax.experimental.pallas import tpu as pltpu

F32 = jnp.float32
BF16 = jnp.bfloat16

N_HEADS = 4
GLA_GATE_RANK = 16
GLA_GATE_NORM = 16.0
GLA_CHUNK = 64
ROPE_BASE = 10000.0
N_EXPERTS = 32
TOP_K = 4
SWIGLU_LIMIT = 7.0
SWIGLU_ALPHA = 1.702
EPS = 1e-6
PAST_LEN = 16384
N_SEG = 8

VMEM_LIMIT = 56 * 1024 * 1024


def _mm(a, b):
    return jnp.dot(a, b, preferred_element_type=F32)


def _mm_nt(a, b):
    return lax.dot_general(a, b, (((1,), (1,)), ((), ())), preferred_element_type=F32)


def _silu(x):
    return x * jax.nn.sigmoid(x)


def _split_hi_lo(x):
    hi = x.astype(BF16)
    lo = (x - hi.astype(F32)).astype(BF16)
    return hi, lo


def _rms_mod(x3, g, sc, sh):
    ms = jnp.mean(x3 * x3, axis=-1, keepdims=True)
    return x3 * lax.rsqrt(ms + EPS) * g * (1.0 + sc) + sh


def _ada_kernel(c_ref, w_ref, b_ref, o_ref):
    cf = _silu(c_ref[...])
    o_ref[0] = _mm(cf.astype(BF16), w_ref[...].astype(BF16)) + b_ref[0]


def _ada(c_all, w_ada, b_ada):
    bc, d = c_all.shape
    n = w_ada.shape[1] // d
    return pl.pallas_call(
        _ada_kernel,
        grid=(n,),
        in_specs=[pl.BlockSpec((bc, d), lambda j: (0, 0)),
                  pl.BlockSpec((d, d), lambda j: (0, j)),
                  pl.BlockSpec((1, 1, d), lambda j: (j, 0, 0))],
        out_specs=pl.BlockSpec((1, bc, d), lambda j: (j, 0, 0)),
        out_shape=jax.ShapeDtypeStruct((n, bc, d), F32),
        compiler_params=pltpu.CompilerParams(dimension_semantics=("arbitrary",), vmem_limit_bytes=VMEM_LIMIT),
        name="ada",
    )(c_all, w_ada, b_ada.reshape(n, 1, d))


class _Tiles:
    def __init__(self, b, t, bs, ts, tm):
        assert t % tm == 0 and (bs * ts) % tm == 0 and tm % ts == 0
        self.b, self.t, self.bs, self.ts, self.tm = b, t, bs, ts, tm
        self.tpb = t // tm
        self.n_pt = b * self.tpb
        self.gs = tm // ts
        self.n_st = (bs * ts) // tm
        self.n = self.n_pt + self.n_st
        self.n_tok = b * t + bs * ts

    def xp_spec(self, d):
        last, tpb = self.n_pt - 1, self.tpb
        return pl.BlockSpec((1, self.tm, d), lambda i: (jnp.minimum(i, last) // tpb, jnp.minimum(i, last) % tpb, 0))

    def xs_spec(self, d):
        n_pt = self.n_pt
        return pl.BlockSpec((self.gs, self.ts, d), lambda i: (jnp.maximum(i - n_pt, 0), 0, 0))

    def adap_spec(self, which, d):
        last, tpb = self.n_pt - 1, self.tpb
        return pl.BlockSpec((1, 1, 1, d), lambda i: (which, jnp.minimum(i, last) // tpb, 0, 0))

    def adas_spec(self, which, d):
        n_pt = self.n_pt
        return pl.BlockSpec((1, self.gs, 1, d), lambda i: (which, jnp.maximum(i - n_pt, 0), 0, 0))

    def tok_spec(self, width):
        return pl.BlockSpec((self.tm, width), lambda i: (i, 0))

    def seg_spec(self, seg, d):
        return pl.BlockSpec((1, self.tm, d), lambda i: (seg, i, 0))


def _resident(shape):
    zeros = (0,) * len(shape)
    return pl.BlockSpec(shape, lambda i: zeros, pipeline_mode=pl.Buffered(1))


def _inproj_kernel(n_pt, d, xp_ref, xs_ref, shp_ref, scp_ref, shs_ref, scs_ref, g_ref, w_ref, wl_ref,
                   proj_ref, glr_ref):
    i = pl.program_id(0)

    def body(x3, sh, sc):
        n = _rms_mod(x3, g_ref[...], sc, sh).reshape(-1, d).astype(BF16)
        for s in range(N_SEG):
            proj_ref[s] = _mm(n, w_ref[:, s * d:(s + 1) * d]).astype(BF16)
        glr_ref[...] = _mm(n, wl_ref[...])

    @pl.when(i < n_pt)
    def _():
        body(xp_ref[...], shp_ref[0], scp_ref[0])

    @pl.when(i >= n_pt)
    def _():
        body(xs_ref[...], shs_ref[0], scs_ref[0])


def _inproj(tl, x_p, x_s, ada_p, ada_s, g_mix, w_main, w_glr):
    d = x_p.shape[-1]
    return pl.pallas_call(
        functools.partial(_inproj_kernel, tl.n_pt, d),
        grid=(tl.n,),
        in_specs=[tl.xp_spec(d), tl.xs_spec(d),
                  tl.adap_spec(0, d), tl.adap_spec(1, d), tl.adas_spec(0, d), tl.adas_spec(1, d),
                  _resident((1, 1, d)), _resident((d, N_SEG * d)), _resident((d, GLA_GATE_RANK))],
        out_specs=[pl.BlockSpec((N_SEG, tl.tm, d), lambda i: (0, i, 0)), tl.tok_spec(GLA_GATE_RANK)],
        out_shape=[jax.ShapeDtypeStruct((N_SEG, tl.n_tok, d), BF16),
                   jax.ShapeDtypeStruct((tl.n_tok, GLA_GATE_RANK), F32)],
        compiler_params=pltpu.CompilerParams(dimension_semantics=("arbitrary",), vmem_limit_bytes=VMEM_LIMIT),
        name="inproj",
    )(x_p, x_s, ada_p, ada_p, ada_s, ada_s, g_mix.reshape(1, 1, d), w_main, w_glr)


def _rope_tables(pos0, t, dk):
    half = dk // 2
    inv = ROPE_BASE ** (-jnp.arange(half, dtype=jnp.float32) / half)
    pos = pos0 + jnp.arange(t)
    ang = pos.astype(jnp.float32)[:, None] * inv[None, :]
    cos, sin = jnp.cos(ang), jnp.sin(ang)
    return jnp.concatenate([cos, cos], axis=-1), jnp.concatenate([-sin, sin], axis=-1)


def _ret_tables(c, dk, dv):
    h = N_HEADS
    log_gamma = jnp.log1p(-jnp.exp2(-5.0 - jnp.arange(h, dtype=jnp.float32)))
    idx = jnp.arange(c, dtype=jnp.float32)
    rel = idx[:, None] - idx[None, :]
    dmask = jnp.where(rel >= 0, jnp.exp(log_gamma[:, None, None] * jnp.maximum(rel, 0.0)), 0.0)
    kdec = jnp.exp(log_gamma[:, None] * (c - 1 - idx))
    qdec = jnp.exp(log_gamma[:, None] * (idx + 1.0))
    cdec = jnp.exp(log_gamma * c)
    return (dmask,
            jnp.broadcast_to(qdec[:, :, None], (h, c, dk)),
            jnp.broadcast_to(kdec[:, :, None], (h, c, dk)),
            jnp.broadcast_to(cdec[:, None, None], (h, 1, dv)))


def _rot(x, cos_f, sin_f):
    return x * cos_f + pltpu.roll(x, x.shape[-1] // 2, 1) * sin_f


def _cross_and_update(q_lhs, k_end, vh, states, masks):
    if masks is None:
        (s,) = states
        return _mm(q_lhs, s.astype(BF16)), [_mm(k_end.T.astype(BF16), vh)]
    cross, incs = None, []
    for s, m in zip(states, masks):
        c = _mm(q_lhs, s.astype(BF16))
        cross = c if cross is None else jnp.where(m, c, cross)
        incs.append(_mm(jnp.where(m, k_end, 0.0).T.astype(BF16), vh))
    return cross, incs


def _ret_head(q, k, vh, gh, states, masks, cos_f, sin_f, dmask, qdec, kdec, cdec):
    dk = q.shape[-1]
    q = _rot(q, cos_f, sin_f)
    k = _rot(k, cos_f, sin_f) * (dk ** -0.5)
    scores = _mm_nt(q.astype(BF16), k.astype(BF16)) * dmask
    cross, incs = _cross_and_update((q * qdec).astype(BF16), k * kdec, vh, states, masks)
    o = _mm(scores.astype(BF16), vh) + cross
    new_states = [cdec * s + u for s, u in zip(states, incs)]
    mu = jnp.mean(o, axis=-1, keepdims=True)
    oc = o - mu
    var = jnp.mean(oc * oc, axis=-1, keepdims=True)
    return _silu(gh) * (oc * lax.rsqrt(var + EPS)), new_states


def _gla_head(q, k, vh, gh, b, states, masks, c, gnorm, causal):
    dk = q.shape[-1]
    b_t = b.T
    if masks is None:
        b_last = b[c - 1:c, :]
    else:
        b_last = None
        for g, m in enumerate(masks):
            row = b[g * c + c - 1:g * c + c, :]
            b_last = row if b_last is None else jnp.where(m, row, b_last)
    q_in = (q * (dk ** -0.5) * jnp.exp(b)).astype(BF16)
    k_in = (k * jnp.exp(-b)).astype(BF16)
    scores = jnp.where(causal, _mm_nt(q_in, k_in), 0.0)
    cross, incs = _cross_and_update(q_in, k * jnp.exp(b_last - b), vh, states, masks)
    o = _mm(scores.astype(BF16), vh) + cross
    new_states = [jnp.exp(b_t[:, g * c + c - 1:g * c + c]) * s + u for g, (s, u) in enumerate(zip(states, incs))]
    o = o * lax.rsqrt(jnp.mean(o * o, axis=-1, keepdims=True) + EPS) * gnorm
    return _silu(gh) * o, new_states


def _log_a(glr, wgk, bgk):
    z = _mm(glr.astype(BF16), wgk) + bgk
    return (jnp.minimum(z, 0.0) - jnp.log1p(jnp.exp(-jnp.abs(z)))) / GLA_GATE_NORM


def _causal(c):
    return lax.broadcasted_iota(jnp.int32, (c, c), 0) >= lax.broadcasted_iota(jnp.int32, (c, c), 1)


def _mixp_kernel(d, tb, rqk_ref, rv_ref, rg_ref, gqk_ref, gv_ref, gg_ref, glr_ref, cos_ref, sin_ref,
                 dmask_ref, qdec_ref, kdec_ref, cdec_ref, tri_ref, wgk_ref, bgk_ref, gn_ref,
                 oret_ref, ogla_ref, sret_ref, sgla_ref, sr_s, sg_s):
    t = pl.program_id(1)
    dk, dv, hq = d // 8, d // 4, d // 2

    @pl.when(t == 0)
    def _():
        sr_s[...] = jnp.zeros_like(sr_s)
        sg_s[...] = jnp.zeros_like(sg_s)

    cos_f, sin_f = cos_ref[...], sin_ref[...]
    for h in range(N_HEADS):
        o, (s_new,) = _ret_head(rqk_ref[0, :, h * dk:(h + 1) * dk].astype(F32),
                                rqk_ref[0, :, hq + h * dk:hq + (h + 1) * dk].astype(F32),
                                rv_ref[0, :, h * dv:(h + 1) * dv],
                                rg_ref[0, :, h * dv:(h + 1) * dv].astype(F32),
                                [sr_s[h]], None, cos_f, sin_f, dmask_ref[h], qdec_ref[h], kdec_ref[h],
                                cdec_ref[h])
        sr_s[h] = s_new
        oret_ref[:, h * dv:(h + 1) * dv] = o.astype(BF16)

    la = _log_a(glr_ref[...], wgk_ref[...], bgk_ref[...])
    la_hi, la_lo = _split_hi_lo(la)
    tri = tri_ref[...]
    b = _mm(tri, la_hi) + _mm(tri, la_lo)
    cg = GLA_CHUNK
    causal = _causal(cg)
    gnorm = gn_ref[...]
    for c in range(tb // cg):
        r0, r1 = c * cg, (c + 1) * cg
        for h in range(N_HEADS):
            o, (s_new,) = _gla_head(gqk_ref[0, r0:r1, h * dk:(h + 1) * dk].astype(F32),
                                    gqk_ref[0, r0:r1, hq + h * dk:hq + (h + 1) * dk].astype(F32),
                                    gv_ref[0, r0:r1, h * dv:(h + 1) * dv],
                                    gg_ref[0, r0:r1, h * dv:(h + 1) * dv].astype(F32),
                                    b[r0:r1, h * dk:(h + 1) * dk], [sg_s[h]], None, cg, gnorm, causal)
            sg_s[h] = s_new
            ogla_ref[r0:r1, h * dv:(h + 1) * dv] = o.astype(BF16)

    @pl.when(t == pl.num_programs(1) - 1)
    def _():
        sret_ref[0] = sr_s[...]
        sgla_ref[0] = sg_s[...]


def _chunk_tri(tb, cg):
    i = jnp.arange(tb)
    return ((i[:, None] >= i[None, :]) & (i[:, None] // cg == i[None, :] // cg)).astype(BF16)


def _mix_prompt(b, t, d, tb, proj, glr, w_gk, b_gk, g_gla):
    dk, dv, hq, h = d // 8, d // 4, d // 2, N_HEADS
    ntb = t // tb
    cos_f, sin_f = _rope_tables(0, t, dk)
    dmask, qdec, kdec, cdec = _ret_tables(tb, dk, dv)
    tri = _chunk_tri(tb, GLA_CHUNK)

    def seg(s):
        return pl.BlockSpec((1, tb, d), lambda bi, ti: (s, bi * ntb + ti, 0))

    def const(shape):
        zeros = (0,) * len(shape)
        return pl.BlockSpec(shape, lambda bi, ti: zeros)

    state_spec = pl.BlockSpec((1, h, dk, dv), lambda bi, ti: (bi, 0, 0, 0))
    tok_spec = pl.BlockSpec((tb, d), lambda bi, ti: (bi * ntb + ti, 0))
    return pl.pallas_call(
        functools.partial(_mixp_kernel, d, tb),
        grid=(b, ntb),
        in_specs=[seg(0), seg(1), seg(2), seg(3), seg(4), seg(5),
                  pl.BlockSpec((tb, GLA_GATE_RANK), lambda bi, ti: (bi * ntb + ti, 0)),
                  pl.BlockSpec((tb, dk), lambda bi, ti: (ti, 0)), pl.BlockSpec((tb, dk), lambda bi, ti: (ti, 0)),
                  const((h, tb, tb)), const((h, tb, dk)), const((h, tb, dk)), const((h, 1, dv)),
                  const((tb, tb)), const((GLA_GATE_RANK, hq)), const((1, hq)), const((1, dv))],
        out_specs=[tok_spec, tok_spec, state_spec, state_spec],
        out_shape=[jax.ShapeDtypeStruct((b * t, d), BF16), jax.ShapeDtypeStruct((b * t, d), BF16),
                   jax.ShapeDtypeStruct((b, h, dk, dv), F32), jax.ShapeDtypeStruct((b, h, dk, dv), F32)],
        scratch_shapes=[pltpu.VMEM((h, dk, dv), F32), pltpu.VMEM((h, dk, dv), F32)],
        compiler_params=pltpu.CompilerParams(dimension_semantics=("arbitrary", "arbitrary"),
                                             vmem_limit_bytes=VMEM_LIMIT),
        name="mix_prompt",
    )(proj, proj, proj, proj, proj, proj, glr, cos_f, sin_f, dmask, qdec, kdec, cdec, tri,
      w_gk, b_gk, g_gla)


def _mixs_kernel(d, ts, gsz, rqk_ref, rv_ref, rg_ref, gqk_ref, gv_ref, gg_ref, glr_ref, cos_ref, sin_ref,
                 dmask_ref, qdec_ref, kdec_ref, cdec_ref, wgk_ref, bgk_ref, gn_ref, sr_in, sg_in,
                 oret_ref, ogla_ref, sr_out, sg_out):
    dk, dv, hq = d // 8, d // 4, d // 2
    pair = 2 * ts
    cos_f, sin_f = cos_ref[...], sin_ref[...]
    gnorm = gn_ref[...]
    ri = lax.broadcasted_iota(jnp.int32, (pair, pair), 0)
    ci = lax.broadcasted_iota(jnp.int32, (pair, pair), 1)
    causal = jnp.logical_and(ri >= ci, (ri < ts) == (ci < ts))
    tri = causal.astype(F32).astype(BF16)
    first = lax.broadcasted_iota(jnp.int32, (pair, 1), 0) < ts
    masks = [first, jnp.logical_not(first)]

    def body(j, carry):
        rows = pl.ds(pl.multiple_of(j * pair, pair), pair)
        s0, s1 = 2 * j, 2 * j + 1
        la_hi, la_lo = _split_hi_lo(_log_a(glr_ref[rows, :], wgk_ref[...], bgk_ref[...]))
        b = _mm(tri, la_hi) + _mm(tri, la_lo)
        for h in range(N_HEADS):
            o, (n0, n1) = _ret_head(rqk_ref[0, rows, h * dk:(h + 1) * dk].astype(F32),
                                    rqk_ref[0, rows, hq + h * dk:hq + (h + 1) * dk].astype(F32),
                                    rv_ref[0, rows, h * dv:(h + 1) * dv],
                                    rg_ref[0, rows, h * dv:(h + 1) * dv].astype(F32),
                                    [sr_in[s0, h], sr_in[s1, h]], masks, cos_f, sin_f,
                                    dmask_ref[h], qdec_ref[h], kdec_ref[h], cdec_ref[h])
            sr_out[s0, h] = n0
            sr_out[s1, h] = n1
            oret_ref[rows, h * dv:(h + 1) * dv] = o.astype(BF16)
            o, (n0, n1) = _gla_head(gqk_ref[0, rows, h * dk:(h + 1) * dk].astype(F32),
                                    gqk_ref[0, rows, hq + h * dk:hq + (h + 1) * dk].astype(F32),
                                    gv_ref[0, rows, h * dv:(h + 1) * dv],
                                    gg_ref[0, rows, h * dv:(h + 1) * dv].astype(F32),
                                    b[:, h * dk:(h + 1) * dk], [sg_in[s0, h], sg_in[s1, h]], masks, ts,
                                    gnorm, causal)
            sg_out[s0, h] = n0
            sg_out[s1, h] = n1
            ogla_ref[rows, h * dv:(h + 1) * dv] = o.astype(BF16)
        return carry

    lax.fori_loop(0, gsz // 2, body, 0)


def _pair_tables(ts, dk, dv):
    cos_f, sin_f = _rope_tables(PAST_LEN, ts, dk)
    dmask, qdec, kdec, cdec = _ret_tables(ts, dk, dv)
    zero = jnp.zeros_like(dmask)
    dmask2 = jnp.concatenate([jnp.concatenate([dmask, zero], axis=2), jnp.concatenate([zero, dmask], axis=2)], axis=1)

    def twice(a, axis):
        return jnp.concatenate([a, a], axis=axis)

    return twice(cos_f, 0), twice(sin_f, 0), dmask2, twice(qdec, 1), twice(kdec, 1), cdec


def _mix_sample(bs, ts, d, n_prompt_tok, gsz, proj, glr, state_ret, state_gla, w_gk, b_gk, g_gla):
    dk, dv, hq, h = d // 8, d // 4, d // 2, N_HEADS
    assert GLA_CHUNK % ts == 0 and bs % gsz == 0 and gsz % 2 == 0 and n_prompt_tok % (gsz * ts) == 0
    rows = gsz * ts
    pair = 2 * ts
    row0 = n_prompt_tok // rows
    cos_f, sin_f, dmask, qdec, kdec, cdec = _pair_tables(ts, dk, dv)

    def seg(s):
        return pl.BlockSpec((1, rows, d), lambda i: (s, row0 + i, 0))

    def const(shape):
        zeros = (0,) * len(shape)
        return pl.BlockSpec(shape, lambda i: zeros)

    state_spec = pl.BlockSpec((gsz, h, dk, dv), lambda i: (i, 0, 0, 0))
    tok_spec = pl.BlockSpec((rows, d), lambda i: (i, 0))
    return pl.pallas_call(
        functools.partial(_mixs_kernel, d, ts, gsz),
        grid=(bs // gsz,),
        in_specs=[seg(0), seg(1), seg(2), seg(3), seg(4), seg(5),
                  pl.BlockSpec((rows, GLA_GATE_RANK), lambda i: (row0 + i, 0)),
                  const((pair, dk)), const((pair, dk)),
                  const((h, pair, pair)), const((h, pair, dk)), const((h, pair, dk)), const((h, 1, dv)),
                  const((GLA_GATE_RANK, hq)), const((1, hq)), const((1, dv)),
                  state_spec, state_spec],
        out_specs=[tok_spec, tok_spec, state_spec, state_spec],
        out_shape=[jax.ShapeDtypeStruct((bs * ts, d), BF16), jax.ShapeDtypeStruct((bs * ts, d), BF16),
                   jax.ShapeDtypeStruct((bs, h, dk, dv), F32), jax.ShapeDtypeStruct((bs, h, dk, dv), F32)],
        compiler_params=pltpu.CompilerParams(dimension_semantics=("arbitrary",), vmem_limit_bytes=VMEM_LIMIT),
        name="mix_sample",
    )(proj, proj, proj, proj, proj, proj, glr, cos_f, sin_f, dmask, qdec, kdec, cdec, w_gk, b_gk, g_gla,
      state_ret, state_gla)


def _outproj_kernel(n_pt, d, tm, orp_ref, ogp_ref, ors_ref, ogs_ref, mgr_ref, mgg_ref, xp_ref, xs_ref,
                    gtp_ref, shp_ref, scp_ref, gts_ref, shs_ref, scs_ref, g_ref,
                    wro_ref, wgo_ref, wo_ref, wrh_ref, wrl_ref, br_ref,
                    h_ref, n2_ref, idx_ref, rank_ref, prob_ref, cnt_ref, carry_s):
    i = pl.program_id(0)
    e = N_EXPERTS

    @pl.when(i == 0)
    def _():
        carry_s[...] = jnp.zeros_like(carry_s)

    def body(out_ret, out_gla, x3, gt, sh, sc):
        a = _mm(out_ret, wro_ref[...])
        b = _mm(out_gla, wgo_ref[...])
        merged = jax.nn.sigmoid(mgr_ref[0].astype(F32)) * a + jax.nn.sigmoid(mgg_ref[0].astype(F32)) * b
        mix = _mm(merged.astype(BF16), wo_ref[...])
        h3 = x3 + gt * mix.reshape(x3.shape)
        h_ref[...] = h3.reshape(-1, d)
        n2 = _rms_mod(h3, g_ref[...], sc, sh).reshape(-1, d)
        n2_ref[...] = n2.astype(BF16)

        n_hi, n_lo = _split_hi_lo(n2)
        logits = _mm(n_hi, wrh_ref[...]) + _mm(n_lo, wrh_ref[...]) + _mm(n_hi, wrl_ref[...]) + br_ref[...]
        iota = lax.broadcasted_iota(jnp.int32, (tm, e), 1)
        work = logits
        vals, idxs = [], []
        for _ in range(TOP_K):
            m = jnp.max(work, axis=-1, keepdims=True)
            ik = jnp.min(jnp.where(work == m, iota, e), axis=-1, keepdims=True)
            vals.append(m)
            idxs.append(ik)
            work = jnp.where(iota == ik, -jnp.inf, work)
        ex = [jnp.exp(v - vals[0]) for v in vals]
        den = ex[0] + ex[1] + ex[2] + ex[3]

        onehot = jnp.zeros((tm, e), F32)
        for ik in idxs:
            onehot = onehot + (iota == ik).astype(F32)
        ltri = (lax.broadcasted_iota(jnp.int32, (tm, tm), 0) > lax.broadcasted_iota(jnp.int32, (tm, tm), 1))
        cum = _mm(ltri.astype(F32).astype(BF16), onehot.astype(BF16)) + carry_s[...]
        lane = lax.broadcasted_iota(jnp.int32, (tm, TOP_K), 1)
        idx_o = jnp.zeros((tm, TOP_K), jnp.int32)
        rank_o = jnp.zeros((tm, TOP_K), jnp.int32)
        prob_o = jnp.zeros((tm, TOP_K), F32)
        for k in range(TOP_K):
            rk = jnp.sum(jnp.where(iota == idxs[k], cum, 0.0), axis=-1, keepdims=True).astype(jnp.int32)
            idx_o = jnp.where(lane == k, idxs[k], idx_o)
            rank_o = jnp.where(lane == k, rk, rank_o)
            prob_o = jnp.where(lane == k, ex[k] / den, prob_o)
        idx_ref[...] = idx_o
        rank_ref[...] = rank_o
        prob_ref[...] = prob_o
        carry_s[...] = carry_s[...] + jnp.sum(onehot, axis=0, keepdims=True)

    @pl.when(i < n_pt)
    def _():
        body(orp_ref[...], ogp_ref[...], xp_ref[...], gtp_ref[0], shp_ref[0], scp_ref[0])

    @pl.when(i >= n_pt)
    def _():
        body(ors_ref[...], ogs_ref[...], xs_ref[...], gts_ref[0], shs_ref[0], scs_ref[0])

    @pl.when(i == pl.num_programs(0) - 1)
    def _():
        cnt_ref[...] = carry_s[...].astype(jnp.int32)


def _outproj(tl, oret_p, ogla_p, oret_s, ogla_s, proj, x_p, x_s, ada_p, ada_s, g_ffn,
             w_ret_o, w_gla_o, w_out, w_r_hi, w_r_lo, b_router):
    d = x_p.shape[-1]
    tm, e, n_pt = tl.tm, N_EXPERTS, tl.n_pt
    last = n_pt - 1
    p_spec = pl.BlockSpec((tm, d), lambda i: (jnp.minimum(i, last), 0))
    s_spec = pl.BlockSpec((tm, d), lambda i: (jnp.maximum(i - n_pt, 0), 0))
    return pl.pallas_call(
        functools.partial(_outproj_kernel, n_pt, d, tm),
        grid=(tl.n,),
        in_specs=[p_spec, p_spec, s_spec, s_spec, tl.seg_spec(6, d), tl.seg_spec(7, d),
                  tl.xp_spec(d), tl.xs_spec(d),
                  tl.adap_spec(2, d), tl.adap_spec(3, d), tl.adap_spec(4, d),
                  tl.adas_spec(2, d), tl.adas_spec(3, d), tl.adas_spec(4, d),
                  _resident((1, 1, d)), _resident((d, d)), _resident((d, d)), _resident((d, d)),
                  _resident((d, e)), _resident((d, e)), _resident((1, e))],
        out_specs=[tl.tok_spec(d), tl.tok_spec(d), tl.tok_spec(TOP_K), tl.tok_spec(TOP_K), tl.tok_spec(TOP_K),
                   pl.BlockSpec((1, e), lambda i: (0, 0))],
        out_shape=[jax.ShapeDtypeStruct((tl.n_tok, d), F32), jax.ShapeDtypeStruct((tl.n_tok, d), BF16),
                   jax.ShapeDtypeStruct((tl.n_tok, TOP_K), jnp.int32),
                   jax.ShapeDtypeStruct((tl.n_tok, TOP_K), jnp.int32),
                   jax.ShapeDtypeStruct((tl.n_tok, TOP_K), F32),
                   jax.ShapeDtypeStruct((1, e), jnp.int32)],
        scratch_shapes=[pltpu.VMEM((1, e), F32)],
        compiler_params=pltpu.CompilerParams(dimension_semantics=("arbitrary",), vmem_limit_bytes=VMEM_LIMIT),
        name="outproj",
    )(oret_p, ogla_p, oret_s, ogla_s, proj, proj, x_p, x_s, ada_p, ada_p, ada_p, ada_s, ada_s, ada_s,
      g_ffn.reshape(1, 1, d), w_ret_o, w_gla_o, w_out, w_r_hi, w_r_lo, b_router.reshape(1, e))


def _expert_kernel(f, te_ref, na_ref, x_ref, wu_ref, bu_ref, wd_ref, bd_ref, y_ref, wu_s, wd_s):
    j = pl.program_id(0)
    active = j < na_ref[0]
    first = jnp.logical_or(j == 0, te_ref[j] != te_ref[jnp.maximum(j - 1, 0)])

    @pl.when(jnp.logical_and(active, first))
    def _():
        wu_s[...] = wu_ref[0].astype(BF16)
        wd_s[...] = wd_ref[0].astype(BF16)

    @pl.when(active)
    def _():
        gu = _mm(x_ref[...], wu_s[...]) + bu_ref[0]
        gate = jnp.minimum(gu[:, :f], SWIGLU_LIMIT)
        up = jnp.clip(gu[:, f:], -SWIGLU_LIMIT, SWIGLU_LIMIT)
        act = (up + 1.0) * gate * jax.nn.sigmoid(SWIGLU_ALPHA * gate)
        y_ref[...] = (_mm(act.astype(BF16), wd_s[...]) + bd_ref[0]).astype(BF16)


def _experts(xs, tile_expert, n_active, w_up, b_up, w_down, b_down, tme):
    r, d = xs.shape
    e, _, f2 = w_up.shape
    f = f2 // 2
    n_tiles = r // tme

    def row_map(j, te, na):
        return (jnp.minimum(j, na[0] - 1), 0)

    def w_map(j, te, na):
        return (te[jnp.minimum(j, na[0] - 1)], 0, 0)

    return pl.pallas_call(
        functools.partial(_expert_kernel, f),
        grid_spec=pltpu.PrefetchScalarGridSpec(
            num_scalar_prefetch=2,
            grid=(n_tiles,),
            in_specs=[pl.BlockSpec((tme, d), row_map),
                      pl.BlockSpec((1, d, f2), w_map), pl.BlockSpec((1, 1, f2), w_map),
                      pl.BlockSpec((1, f, d), w_map), pl.BlockSpec((1, 1, d), w_map)],
            out_specs=pl.BlockSpec((tme, d), row_map),
            scratch_shapes=[pltpu.VMEM((d, f2), BF16), pltpu.VMEM((f, d), BF16)]),
        out_shape=jax.ShapeDtypeStruct((r, d), BF16),
        compiler_params=pltpu.CompilerParams(dimension_semantics=("arbitrary",), vmem_limit_bytes=VMEM_LIMIT),
        name="experts",
    )(tile_expert, n_active, xs, w_up, b_up.reshape(e, 1, f2), w_down, b_down.reshape(e, 1, d))


def _final_kernel(n_pt, d, h_ref, yg_ref, prob_ref, gtp_ref, gts_ref, g_ref, yp_ref, ys_ref):
    i = pl.program_id(0)
    p = prob_ref[...]
    moe = p[:, 0:1] * yg_ref[0].astype(F32)
    for k in range(1, TOP_K):
        moe = moe + p[:, k:k + 1] * yg_ref[k].astype(F32)

    def body(gt, shape):
        h3 = h_ref[...].reshape(shape) + gt * moe.reshape(shape)
        ms = jnp.mean(h3 * h3, axis=-1, keepdims=True)
        return h3 * lax.rsqrt(ms + EPS) * g_ref[...]

    @pl.when(i < n_pt)
    def _():
        yp_ref[...] = body(gtp_ref[0], yp_ref.shape)

    @pl.when(i >= n_pt)
    def _():
        ys_ref[...] = body(gts_ref[0], ys_ref.shape)


def _final(tl, h, yg, probs, ada_p, ada_s, g_final, d):
    return pl.pallas_call(
        functools.partial(_final_kernel, tl.n_pt, d),
        grid=(tl.n,),
        in_specs=[tl.tok_spec(d), pl.BlockSpec((TOP_K, tl.tm, d), lambda i: (0, i, 0)), tl.tok_spec(TOP_K),
                  tl.adap_spec(5, d), tl.adas_spec(5, d), _resident((1, 1, d))],
        out_specs=[tl.xp_spec(d), tl.xs_spec(d)],
        out_shape=[jax.ShapeDtypeStruct((tl.b, tl.t, d), F32), jax.ShapeDtypeStruct((tl.bs, tl.ts, d), F32)],
        compiler_params=pltpu.CompilerParams(dimension_semantics=("arbitrary",), vmem_limit_bytes=VMEM_LIMIT),
        name="final",
    )(h, yg, probs, ada_p, ada_s, g_final.reshape(1, 1, d))


def _pick(n, pref):
    t = min(n, pref)
    while n % t:
        t //= 2
    return t


def _forward(x_prompt, x_sample, c_prompt, c_sample, state_ret, state_gla, w_ada, b_ada, g_norm_mix, g_norm_ffn,
             w_in, w_gk_up, b_gk, g_gla_norm, w_ret_o, w_gla_o, w_out, w_router, b_router, w_up, b_up,
             w_down, b_down, g_final, *, tm, tb, gsz, tme):
    b, t, d = x_prompt.shape
    bs, ts, _ = x_sample.shape
    assert w_ada.shape[0] == 1, "single layer only"
    e = N_EXPERTS
    tl = _Tiles(b, t, bs, ts, tm)
    n_tok = tl.n_tok

    ada = _ada(jnp.concatenate([c_prompt, c_sample], axis=0), w_ada[0], b_ada[0])
    ada_p = ada[:, :b].reshape(6, b, 1, d)
    ada_s = ada[:, b:].reshape(6, bs, 1, d)

    w_in0 = w_in[0]
    n_main = 6 * d
    w_main = jnp.concatenate([w_in0[:, :n_main], w_in0[:, n_main + GLA_GATE_RANK:]], axis=1).astype(BF16)
    w_glr = w_in0[:, n_main:n_main + GLA_GATE_RANK].astype(BF16)
    proj, glr = _inproj(tl, x_prompt, x_sample, ada_p, ada_s, g_norm_mix[0], w_main, w_glr)

    w_gk = w_gk_up[0].astype(BF16)
    bgk = b_gk[0].reshape(1, -1)
    ggn = g_gla_norm[0].reshape(1, -1)
    oret_p, ogla_p, sret_p, sgla_p = _mix_prompt(b, t, d, tb, proj, glr, w_gk, bgk, ggn)
    oret_s, ogla_s, sret_s, sgla_s = _mix_sample(bs, ts, d, b * t, gsz, proj, glr, state_ret[0], state_gla[0],
                                                 w_gk, bgk, ggn)

    w_r = w_router[0]
    w_r_hi = w_r.astype(BF16)
    w_r_lo = (w_r - w_r_hi.astype(F32)).astype(BF16)
    h, n2, idx, rank, probs, counts = _outproj(
        tl, oret_p, ogla_p, oret_s, ogla_s, proj, x_prompt, x_sample, ada_p, ada_s, g_norm_ffn[0],
        w_ret_o[0].astype(BF16), w_gla_o[0].astype(BF16), w_out[0].astype(BF16), w_r_hi, w_r_lo, b_router[0])

    counts = counts[0]
    gsize = ((counts + tme - 1) // tme) * tme
    ends = jnp.cumsum(gsize)
    offs = ends - gsize
    pos = offs[idx] + rank
    max_tiles = (n_tok * TOP_K) // tme + e
    n_active = (ends[-1] // tme).astype(jnp.int32).reshape(1)
    tile_start = jnp.arange(max_tiles, dtype=jnp.int32) * tme
    tile_expert = jnp.minimum(jnp.sum((ends[None, :] <= tile_start[:, None]).astype(jnp.int32), axis=1), e - 1)

    xs = jnp.zeros((max_tiles * tme, d), BF16).at[pos.reshape(-1)].set(jnp.repeat(n2, TOP_K, axis=0))
    ys = _experts(xs, tile_expert, n_active, w_up[0], b_up[0], w_down[0], b_down[0], tme)
    yg = ys[pos.T]

    y_p, y_s = _final(tl, h, yg, probs, ada_p, ada_s, g_final, d)
    return (y_p, y_s, sret_p[None], sgla_p[None], sret_s[None], sgla_s[None])


def kernel(x_prompt, x_sample, c_prompt, c_sample, state_ret, state_gla, w_ada, b_ada, g_norm_mix, g_norm_ffn,
           w_in, w_gk_up, b_gk, g_gla_norm, w_ret_o, w_gla_o, w_out, w_router, b_router, w_up, b_up,
           w_down, b_down, g_final):
    t = x_prompt.shape[1]
    bs, ts = x_sample.shape[0], x_sample.shape[1]
    return _forward(x_prompt, x_sample, c_prompt, c_sample, state_ret, state_gla, w_ada, b_ada, g_norm_mix,
                    g_norm_ffn, w_in, w_gk_up, b_gk, g_gla_norm, w_ret_o, w_gla_o, w_out, w_router, b_router,
                    w_up, b_up, w_down, b_down, g_final,
                    tm=_pick(bs * ts, 512), tb=_pick(t, 256), gsz=_pick(bs, 8), tme=512)
```

```python
import functools

import jax
import jax.numpy as jnp
from jax import lax
from jax.experimental import pallas as pl
from jax.experimental.pallas import tpu as pltpu
from jax.experimental.pallas import tpu_sc as plsc

F32 = jnp.float32
BF16 = jnp.bfloat16

N_HEADS = 4
GLA_GATE_RANK = 16
GLA_GATE_NORM = 16.0
GLA_CHUNK = 64
ROPE_BASE = 10000.0
N_EXPERTS = 32
TOP_K = 4
SWIGLU_LIMIT = 7.0
SWIGLU_ALPHA = 1.702
EPS = 1e-6
PAST_LEN = 16384
N_SEG = 8

VMEM_LIMIT = 56 * 1024 * 1024


def _mm(a, b):
    return jnp.dot(a, b, preferred_element_type=F32)


def _mm_nt(a, b):
    return lax.dot_general(a, b, (((1,), (1,)), ((), ())), preferred_element_type=F32)


def _silu(x):
    return x * jax.nn.sigmoid(x)


def _split_hi_lo(x):
    hi = x.astype(BF16)
    lo = (x - hi.astype(F32)).astype(BF16)
    return hi, lo


def _pack_pair(x):
    w = x.shape[1] // 2
    lo = lax.bitcast_convert_type(x[:, :w].astype(BF16).astype(F32), jnp.uint32)
    hi = lax.bitcast_convert_type(x[:, w:].astype(BF16).astype(F32), jnp.uint32)
    return (hi & jnp.uint32(0xFFFF0000)) | (lo >> 16)


def _unpack_pair(p):
    lo = lax.bitcast_convert_type(p << 16, F32)
    hi = lax.bitcast_convert_type(p & jnp.uint32(0xFFFF0000), F32)
    return lo, hi


def _rms_mod(x3, g, sc, sh):
    ms = jnp.mean(x3 * x3, axis=-1, keepdims=True)
    return x3 * lax.rsqrt(ms + EPS) * g * (1.0 + sc) + sh


def _ada_kernel(c_ref, w_ref, b_ref, o_ref):
    cf = _silu(c_ref[...])
    o_ref[0] = _mm(cf.astype(BF16), w_ref[...].astype(BF16)) + b_ref[0]


def _ada(c_all, w_ada, b_ada):
    bc, d = c_all.shape
    n = w_ada.shape[1] // d
    return pl.pallas_call(
        _ada_kernel,
        grid=(n,),
        in_specs=[pl.BlockSpec((bc, d), lambda j: (0, 0)),
                  pl.BlockSpec((d, d), lambda j: (0, j)),
                  pl.BlockSpec((1, 1, d), lambda j: (j, 0, 0))],
        out_specs=pl.BlockSpec((1, bc, d), lambda j: (j, 0, 0)),
        out_shape=jax.ShapeDtypeStruct((n, bc, d), F32),
        compiler_params=pltpu.CompilerParams(dimension_semantics=("arbitrary",), vmem_limit_bytes=VMEM_LIMIT),
        name="ada",
    )(c_all, w_ada, b_ada.reshape(n, 1, d))


class _Tiles:
    def __init__(self, b, t, bs, ts, tm):
        assert t % tm == 0 and (bs * ts) % tm == 0 and tm % ts == 0
        self.b, self.t, self.bs, self.ts, self.tm = b, t, bs, ts, tm
        self.tpb = t // tm
        self.n_pt = b * self.tpb
        self.gs = tm // ts
        self.n_st = (bs * ts) // tm
        self.n = self.n_pt + self.n_st
        self.n_tok = b * t + bs * ts

    def xp_spec(self, d):
        last, tpb = self.n_pt - 1, self.tpb
        return pl.BlockSpec((1, self.tm, d), lambda i: (jnp.minimum(i, last) // tpb, jnp.minimum(i, last) % tpb, 0))

    def xs_spec(self, d):
        n_pt = self.n_pt
        return pl.BlockSpec((self.gs, self.ts, d), lambda i: (jnp.maximum(i - n_pt, 0), 0, 0))

    def adap_spec(self, which, d):
        last, tpb = self.n_pt - 1, self.tpb
        return pl.BlockSpec((1, 1, 1, d), lambda i: (which, jnp.minimum(i, last) // tpb, 0, 0))

    def adas_spec(self, which, d):
        n_pt = self.n_pt
        return pl.BlockSpec((1, self.gs, 1, d), lambda i: (which, jnp.maximum(i - n_pt, 0), 0, 0))

    def tok_spec(self, width):
        return pl.BlockSpec((self.tm, width), lambda i: (i, 0))

    def seg_spec(self, seg, d):
        return pl.BlockSpec((1, self.tm, d), lambda i: (seg, i, 0))


def _resident(shape):
    zeros = (0,) * len(shape)
    return pl.BlockSpec(shape, lambda i: zeros, pipeline_mode=pl.Buffered(1))


def _inproj_kernel(n_pt, d, xp_ref, xs_ref, shp_ref, scp_ref, shs_ref, scs_ref, g_ref, w_ref, wl_ref,
                   proj_ref, glr_ref):
    i = pl.program_id(0)

    def body(x3, sh, sc):
        n = _rms_mod(x3, g_ref[...], sc, sh).reshape(-1, d).astype(BF16)
        for s in range(N_SEG):
            proj_ref[s] = _mm(n, w_ref[:, s * d:(s + 1) * d]).astype(BF16)
        glr_ref[...] = _mm(n, wl_ref[...])

    @pl.when(i < n_pt)
    def _():
        body(xp_ref[...], shp_ref[0], scp_ref[0])

    @pl.when(i >= n_pt)
    def _():
        body(xs_ref[...], shs_ref[0], scs_ref[0])


def _inproj(tl, x_p, x_s, ada_p, ada_s, g_mix, w_main, w_glr):
    d = x_p.shape[-1]
    return pl.pallas_call(
        functools.partial(_inproj_kernel, tl.n_pt, d),
        grid=(tl.n,),
        in_specs=[tl.xp_spec(d), tl.xs_spec(d),
                  tl.adap_spec(0, d), tl.adap_spec(1, d), tl.adas_spec(0, d), tl.adas_spec(1, d),
                  _resident((1, 1, d)), _resident((d, N_SEG * d)), _resident((d, GLA_GATE_RANK))],
        out_specs=[pl.BlockSpec((N_SEG, tl.tm, d), lambda i: (0, i, 0)), tl.tok_spec(GLA_GATE_RANK)],
        out_shape=[jax.ShapeDtypeStruct((N_SEG, tl.n_tok, d), BF16),
                   jax.ShapeDtypeStruct((tl.n_tok, GLA_GATE_RANK), F32)],
        compiler_params=pltpu.CompilerParams(dimension_semantics=("arbitrary",), vmem_limit_bytes=VMEM_LIMIT),
        name="inproj",
    )(x_p, x_s, ada_p, ada_p, ada_s, ada_s, g_mix.reshape(1, 1, d), w_main, w_glr)


def _rope_tables(pos0, t, dk):
    half = dk // 2
    inv = ROPE_BASE ** (-jnp.arange(half, dtype=jnp.float32) / half)
    pos = pos0 + jnp.arange(t)
    ang = pos.astype(jnp.float32)[:, None] * inv[None, :]
    cos, sin = jnp.cos(ang), jnp.sin(ang)
    return jnp.concatenate([cos, cos], axis=-1), jnp.concatenate([-sin, sin], axis=-1)


def _ret_tables(c, dk, dv):
    h = N_HEADS
    log_gamma = jnp.log1p(-jnp.exp2(-5.0 - jnp.arange(h, dtype=jnp.float32)))
    idx = jnp.arange(c, dtype=jnp.float32)
    rel = idx[:, None] - idx[None, :]
    dmask = jnp.where(rel >= 0, jnp.exp(log_gamma[:, None, None] * jnp.maximum(rel, 0.0)), 0.0)
    kdec = jnp.exp(log_gamma[:, None] * (c - 1 - idx))
    qdec = jnp.exp(log_gamma[:, None] * (idx + 1.0))
    cdec = jnp.exp(log_gamma * c)
    return (dmask,
            jnp.broadcast_to(qdec[:, :, None], (h, c, dk)),
            jnp.broadcast_to(kdec[:, :, None], (h, c, dk)),
            jnp.broadcast_to(cdec[:, None, None], (h, 1, dv)))


def _rot(x, cos_f, sin_f):
    return x * cos_f + pltpu.roll(x, x.shape[-1] // 2, 1) * sin_f


def _cross_and_update(q_lhs, k_end, vh, states, masks):
    if masks is None:
        (s,) = states
        return _mm(q_lhs, s.astype(BF16)), [_mm(k_end.T.astype(BF16), vh)]
    cross, incs = None, []
    for s, m in zip(states, masks):
        c = _mm(q_lhs, s.astype(BF16))
        cross = c if cross is None else jnp.where(m, c, cross)
        incs.append(_mm(jnp.where(m, k_end, 0.0).T.astype(BF16), vh))
    return cross, incs


def _ret_head(q, k, vh, gh, states, masks, cos_f, sin_f, dmask, qdec, kdec, cdec):
    dk = q.shape[-1]
    q = _rot(q, cos_f, sin_f)
    k = _rot(k, cos_f, sin_f) * (dk ** -0.5)
    scores = _mm_nt(q.astype(BF16), k.astype(BF16)) * dmask
    cross, incs = _cross_and_update((q * qdec).astype(BF16), k * kdec, vh, states, masks)
    o = _mm(scores.astype(BF16), vh) + cross
    new_states = [cdec * s + u for s, u in zip(states, incs)]
    mu = jnp.mean(o, axis=-1, keepdims=True)
    oc = o - mu
    var = jnp.mean(oc * oc, axis=-1, keepdims=True)
    return _silu(gh) * (oc * lax.rsqrt(var + EPS)), new_states


def _gla_head(q, k, vh, gh, b, states, masks, c, gnorm, causal):
    dk = q.shape[-1]
    b_t = b.T
    if masks is None:
        b_last = b[c - 1:c, :]
    else:
        b_last = None
        for g, m in enumerate(masks):
            row = b[g * c + c - 1:g * c + c, :]
            b_last = row if b_last is None else jnp.where(m, row, b_last)
    q_in = (q * (dk ** -0.5) * jnp.exp(b)).astype(BF16)
    k_in = (k * jnp.exp(-b)).astype(BF16)
    scores = jnp.where(causal, _mm_nt(q_in, k_in), 0.0)
    cross, incs = _cross_and_update(q_in, k * jnp.exp(b_last - b), vh, states, masks)
    o = _mm(scores.astype(BF16), vh) + cross
    new_states = [jnp.exp(b_t[:, g * c + c - 1:g * c + c]) * s + u for g, (s, u) in enumerate(zip(states, incs))]
    o = o * lax.rsqrt(jnp.mean(o * o, axis=-1, keepdims=True) + EPS) * gnorm
    return _silu(gh) * o, new_states


def _log_a(glr, wgk, bgk):
    z = _mm(glr.astype(BF16), wgk) + bgk
    return (jnp.minimum(z, 0.0) - jnp.log1p(jnp.exp(-jnp.abs(z)))) / GLA_GATE_NORM


def _causal(c):
    return lax.broadcasted_iota(jnp.int32, (c, c), 0) >= lax.broadcasted_iota(jnp.int32, (c, c), 1)


def _mixp_kernel(d, tb, rqk_ref, rv_ref, rg_ref, gqk_ref, gv_ref, gg_ref, glr_ref, cos_ref, sin_ref,
                 dmask_ref, qdec_ref, kdec_ref, cdec_ref, tri_ref, wgk_ref, bgk_ref, gn_ref,
                 oret_ref, ogla_ref, sret_ref, sgla_ref, sr_s, sg_s):
    t = pl.program_id(1)
    dk, dv, hq = d // 8, d // 4, d // 2

    @pl.when(t == 0)
    def _():
        sr_s[...] = jnp.zeros_like(sr_s)
        sg_s[...] = jnp.zeros_like(sg_s)

    cos_f, sin_f = cos_ref[...], sin_ref[...]
    for h in range(N_HEADS):
        o, (s_new,) = _ret_head(rqk_ref[0, :, h * dk:(h + 1) * dk].astype(F32),
                                rqk_ref[0, :, hq + h * dk:hq + (h + 1) * dk].astype(F32),
                                rv_ref[0, :, h * dv:(h + 1) * dv],
                                rg_ref[0, :, h * dv:(h + 1) * dv].astype(F32),
                                [sr_s[h]], None, cos_f, sin_f, dmask_ref[h], qdec_ref[h], kdec_ref[h],
                                cdec_ref[h])
        sr_s[h] = s_new
        oret_ref[:, h * dv:(h + 1) * dv] = o.astype(BF16)

    la = _log_a(glr_ref[...], wgk_ref[...], bgk_ref[...])
    la_hi, la_lo = _split_hi_lo(la)
    tri = tri_ref[...]
    b = _mm(tri, la_hi) + _mm(tri, la_lo)
    cg = GLA_CHUNK
    causal = _causal(cg)
    gnorm = gn_ref[...]
    for c in range(tb // cg):
        r0, r1 = c * cg, (c + 1) * cg
        for h in range(N_HEADS):
            o, (s_new,) = _gla_head(gqk_ref[0, r0:r1, h * dk:(h + 1) * dk].astype(F32),
                                    gqk_ref[0, r0:r1, hq + h * dk:hq + (h + 1) * dk].astype(F32),
                                    gv_ref[0, r0:r1, h * dv:(h + 1) * dv],
                                    gg_ref[0, r0:r1, h * dv:(h + 1) * dv].astype(F32),
                                    b[r0:r1, h * dk:(h + 1) * dk], [sg_s[h]], None, cg, gnorm, causal)
            sg_s[h] = s_new
            ogla_ref[r0:r1, h * dv:(h + 1) * dv] = o.astype(BF16)

    @pl.when(t == pl.num_programs(1) - 1)
    def _():
        sret_ref[0] = sr_s[...]
        sgla_ref[0] = sg_s[...]


def _chunk_tri(tb, cg):
    i = jnp.arange(tb)
    return ((i[:, None] >= i[None, :]) & (i[:, None] // cg == i[None, :] // cg)).astype(BF16)


def _mix_prompt(b, t, d, tb, proj, glr, w_gk, b_gk, g_gla):
    dk, dv, hq, h = d // 8, d // 4, d // 2, N_HEADS
    ntb = t // tb
    cos_f, sin_f = _rope_tables(0, t, dk)
    dmask, qdec, kdec, cdec = _ret_tables(tb, dk, dv)
    tri = _chunk_tri(tb, GLA_CHUNK)

    def seg(s):
        return pl.BlockSpec((1, tb, d), lambda bi, ti: (s, bi * ntb + ti, 0))

    def const(shape):
        zeros = (0,) * len(shape)
        return pl.BlockSpec(shape, lambda bi, ti: zeros)

    state_spec = pl.BlockSpec((1, h, dk, dv), lambda bi, ti: (bi, 0, 0, 0))
    tok_spec = pl.BlockSpec((tb, d), lambda bi, ti: (bi * ntb + ti, 0))
    return pl.pallas_call(
        functools.partial(_mixp_kernel, d, tb),
        grid=(b, ntb),
        in_specs=[seg(0), seg(1), seg(2), seg(3), seg(4), seg(5),
                  pl.BlockSpec((tb, GLA_GATE_RANK), lambda bi, ti: (bi * ntb + ti, 0)),
                  pl.BlockSpec((tb, dk), lambda bi, ti: (ti, 0)), pl.BlockSpec((tb, dk), lambda bi, ti: (ti, 0)),
                  const((h, tb, tb)), const((h, tb, dk)), const((h, tb, dk)), const((h, 1, dv)),
                  const((tb, tb)), const((GLA_GATE_RANK, hq)), const((1, hq)), const((1, dv))],
        out_specs=[tok_spec, tok_spec, state_spec, state_spec],
        out_shape=[jax.ShapeDtypeStruct((b * t, d), BF16), jax.ShapeDtypeStruct((b * t, d), BF16),
                   jax.ShapeDtypeStruct((b, h, dk, dv), F32), jax.ShapeDtypeStruct((b, h, dk, dv), F32)],
        scratch_shapes=[pltpu.VMEM((h, dk, dv), F32), pltpu.VMEM((h, dk, dv), F32)],
        compiler_params=pltpu.CompilerParams(dimension_semantics=("arbitrary", "arbitrary"),
                                             vmem_limit_bytes=VMEM_LIMIT),
        name="mix_prompt",
    )(proj, proj, proj, proj, proj, proj, glr, cos_f, sin_f, dmask, qdec, kdec, cdec, tri,
      w_gk, b_gk, g_gla)


def _mixs_kernel(d, ts, gsz, rqk_ref, rv_ref, rg_ref, gqk_ref, gv_ref, gg_ref, glr_ref, cos_ref, sin_ref,
                 dmask_ref, qdec_ref, kdec_ref, cdec_ref, wgk_ref, bgk_ref, gn_ref, sr_in, sg_in,
                 oret_ref, ogla_ref, sr_out, sg_out):
    dk, dv, hq = d // 8, d // 4, d // 2
    pair = 2 * ts
    cos_f, sin_f = cos_ref[...], sin_ref[...]
    gnorm = gn_ref[...]
    ri = lax.broadcasted_iota(jnp.int32, (pair, pair), 0)
    ci = lax.broadcasted_iota(jnp.int32, (pair, pair), 1)
    causal = jnp.logical_and(ri >= ci, (ri < ts) == (ci < ts))
    tri = causal.astype(F32).astype(BF16)
    first = lax.broadcasted_iota(jnp.int32, (pair, 1), 0) < ts
    masks = [first, jnp.logical_not(first)]

    def body(j, carry):
        rows = pl.ds(pl.multiple_of(j * pair, pair), pair)
        s0, s1 = 2 * j, 2 * j + 1
        la_hi, la_lo = _split_hi_lo(_log_a(glr_ref[rows, :], wgk_ref[...], bgk_ref[...]))
        b = _mm(tri, la_hi) + _mm(tri, la_lo)
        for h in range(N_HEADS):
            o, (n0, n1) = _ret_head(rqk_ref[0, rows, h * dk:(h + 1) * dk].astype(F32),
                                    rqk_ref[0, rows, hq + h * dk:hq + (h + 1) * dk].astype(F32),
                                    rv_ref[0, rows, h * dv:(h + 1) * dv],
                                    rg_ref[0, rows, h * dv:(h + 1) * dv].astype(F32),
                                    [sr_in[s0, h], sr_in[s1, h]], masks, cos_f, sin_f,
                                    dmask_ref[h], qdec_ref[h], kdec_ref[h], cdec_ref[h])
            sr_out[s0, h] = n0
            sr_out[s1, h] = n1
            oret_ref[rows, h * dv:(h + 1) * dv] = o.astype(BF16)
            o, (n0, n1) = _gla_head(gqk_ref[0, rows, h * dk:(h + 1) * dk].astype(F32),
                                    gqk_ref[0, rows, hq + h * dk:hq + (h + 1) * dk].astype(F32),
                                    gv_ref[0, rows, h * dv:(h + 1) * dv],
                                    gg_ref[0, rows, h * dv:(h + 1) * dv].astype(F32),
                                    b[:, h * dk:(h + 1) * dk], [sg_in[s0, h], sg_in[s1, h]], masks, ts,
                                    gnorm, causal)
            sg_out[s0, h] = n0
            sg_out[s1, h] = n1
            ogla_ref[rows, h * dv:(h + 1) * dv] = o.astype(BF16)
        return carry

    lax.fori_loop(0, gsz // 2, body, 0)


def _pair_tables(ts, dk, dv):
    cos_f, sin_f = _rope_tables(PAST_LEN, ts, dk)
    dmask, qdec, kdec, cdec = _ret_tables(ts, dk, dv)
    zero = jnp.zeros_like(dmask)
    dmask2 = jnp.concatenate([jnp.concatenate([dmask, zero], axis=2), jnp.concatenate([zero, dmask], axis=2)], axis=1)

    def twice(a, axis):
        return jnp.concatenate([a, a], axis=axis)

    return twice(cos_f, 0), twice(sin_f, 0), dmask2, twice(qdec, 1), twice(kdec, 1), cdec


def _mix_sample(bs, ts, d, n_prompt_tok, gsz, proj, glr, state_ret, state_gla, w_gk, b_gk, g_gla):
    dk, dv, hq, h = d // 8, d // 4, d // 2, N_HEADS
    assert GLA_CHUNK % ts == 0 and bs % gsz == 0 and gsz % 2 == 0 and n_prompt_tok % (gsz * ts) == 0
    rows = gsz * ts
    pair = 2 * ts
    row0 = n_prompt_tok // rows
    cos_f, sin_f, dmask, qdec, kdec, cdec = _pair_tables(ts, dk, dv)

    def seg(s):
        return pl.BlockSpec((1, rows, d), lambda i: (s, row0 + i, 0))

    def const(shape):
        zeros = (0,) * len(shape)
        return pl.BlockSpec(shape, lambda i: zeros)

    state_spec = pl.BlockSpec((gsz, h, dk, dv), lambda i: (i, 0, 0, 0))
    tok_spec = pl.BlockSpec((rows, d), lambda i: (i, 0))
    return pl.pallas_call(
        functools.partial(_mixs_kernel, d, ts, gsz),
        grid=(bs // gsz,),
        in_specs=[seg(0), seg(1), seg(2), seg(3), seg(4), seg(5),
                  pl.BlockSpec((rows, GLA_GATE_RANK), lambda i: (row0 + i, 0)),
                  const((pair, dk)), const((pair, dk)),
                  const((h, pair, pair)), const((h, pair, dk)), const((h, pair, dk)), const((h, 1, dv)),
                  const((GLA_GATE_RANK, hq)), const((1, hq)), const((1, dv)),
                  state_spec, state_spec],
        out_specs=[tok_spec, tok_spec, state_spec, state_spec],
        out_shape=[jax.ShapeDtypeStruct((bs * ts, d), BF16), jax.ShapeDtypeStruct((bs * ts, d), BF16),
                   jax.ShapeDtypeStruct((bs, h, dk, dv), F32), jax.ShapeDtypeStruct((bs, h, dk, dv), F32)],
        compiler_params=pltpu.CompilerParams(dimension_semantics=("arbitrary",), vmem_limit_bytes=VMEM_LIMIT),
        name="mix_sample",
    )(proj, proj, proj, proj, proj, proj, glr, cos_f, sin_f, dmask, qdec, kdec, cdec, w_gk, b_gk, g_gla,
      state_ret, state_gla)


def _outproj_kernel(n_pt, d, tm, orp_ref, ogp_ref, ors_ref, ogs_ref, mgr_ref, mgg_ref, xp_ref, xs_ref,
                    gtp_ref, shp_ref, scp_ref, gts_ref, shs_ref, scs_ref, g_ref,
                    wro_ref, wgo_ref, wo_ref, wrh_ref, wrl_ref, br_ref,
                    h_ref, n2_ref, idx_ref, rank_ref, prob_ref, cnt_ref, carry_s):
    i = pl.program_id(0)
    e = N_EXPERTS

    @pl.when(i == 0)
    def _():
        carry_s[...] = jnp.zeros_like(carry_s)

    def body(out_ret, out_gla, x3, gt, sh, sc):
        a = _mm(out_ret, wro_ref[...])
        b = _mm(out_gla, wgo_ref[...])
        merged = jax.nn.sigmoid(mgr_ref[0].astype(F32)) * a + jax.nn.sigmoid(mgg_ref[0].astype(F32)) * b
        mix = _mm(merged.astype(BF16), wo_ref[...])
        h3 = x3 + gt * mix.reshape(x3.shape)
        h_ref[...] = h3.reshape(-1, d)
        n2 = _rms_mod(h3, g_ref[...], sc, sh).reshape(-1, d)
        n2_ref[...] = _pack_pair(n2)

        n_hi, n_lo = _split_hi_lo(n2)
        logits = _mm(n_hi, wrh_ref[...]) + _mm(n_lo, wrh_ref[...]) + _mm(n_hi, wrl_ref[...]) + br_ref[...]
        iota = lax.broadcasted_iota(jnp.int32, (tm, e), 1)
        work = logits
        vals, idxs = [], []
        for _ in range(TOP_K):
            m = jnp.max(work, axis=-1, keepdims=True)
            ik = jnp.min(jnp.where(work == m, iota, e), axis=-1, keepdims=True)
            vals.append(m)
            idxs.append(ik)
            work = jnp.where(iota == ik, -jnp.inf, work)
        ex = [jnp.exp(v - vals[0]) for v in vals]
        den = ex[0] + ex[1] + ex[2] + ex[3]

        onehot = jnp.zeros((tm, e), F32)
        for ik in idxs:
            onehot = onehot + (iota == ik).astype(F32)
        ltri = (lax.broadcasted_iota(jnp.int32, (tm, tm), 0) > lax.broadcasted_iota(jnp.int32, (tm, tm), 1))
        cum = _mm(ltri.astype(F32).astype(BF16), onehot.astype(BF16)) + carry_s[...]
        lane = lax.broadcasted_iota(jnp.int32, (tm, TOP_K), 1)
        idx_o = jnp.zeros((tm, TOP_K), jnp.int32)
        rank_o = jnp.zeros((tm, TOP_K), jnp.int32)
        prob_o = jnp.zeros((tm, TOP_K), F32)
        for k in range(TOP_K):
            rk = jnp.sum(jnp.where(iota == idxs[k], cum, 0.0), axis=-1, keepdims=True).astype(jnp.int32)
            idx_o = jnp.where(lane == k, idxs[k], idx_o)
            rank_o = jnp.where(lane == k, rk, rank_o)
            prob_o = jnp.where(lane == k, ex[k] / den, prob_o)
        idx_ref[...] = idx_o
        rank_ref[...] = rank_o
        prob_ref[...] = prob_o
        carry_s[...] = carry_s[...] + jnp.sum(onehot, axis=0, keepdims=True)

    @pl.when(i < n_pt)
    def _():
        body(orp_ref[...], ogp_ref[...], xp_ref[...], gtp_ref[0], shp_ref[0], scp_ref[0])

    @pl.when(i >= n_pt)
    def _():
        body(ors_ref[...], ogs_ref[...], xs_ref[...], gts_ref[0], shs_ref[0], scs_ref[0])

    @pl.when(i == pl.num_programs(0) - 1)
    def _():
        cnt_ref[...] = carry_s[...].astype(jnp.int32)


def _outproj(tl, oret_p, ogla_p, oret_s, ogla_s, proj, x_p, x_s, ada_p, ada_s, g_ffn,
             w_ret_o, w_gla_o, w_out, w_r_hi, w_r_lo, b_router):
    d = x_p.shape[-1]
    tm, e, n_pt = tl.tm, N_EXPERTS, tl.n_pt
    last = n_pt - 1
    p_spec = pl.BlockSpec((tm, d), lambda i: (jnp.minimum(i, last), 0))
    s_spec = pl.BlockSpec((tm, d), lambda i: (jnp.maximum(i - n_pt, 0), 0))
    return pl.pallas_call(
        functools.partial(_outproj_kernel, n_pt, d, tm),
        grid=(tl.n,),
        in_specs=[p_spec, p_spec, s_spec, s_spec, tl.seg_spec(6, d), tl.seg_spec(7, d),
                  tl.xp_spec(d), tl.xs_spec(d),
                  tl.adap_spec(2, d), tl.adap_spec(3, d), tl.adap_spec(4, d),
                  tl.adas_spec(2, d), tl.adas_spec(3, d), tl.adas_spec(4, d),
                  _resident((1, 1, d)), _resident((d, d)), _resident((d, d)), _resident((d, d)),
                  _resident((d, e)), _resident((d, e)), _resident((1, e))],
        out_specs=[tl.tok_spec(d), tl.tok_spec(d // 2), tl.tok_spec(TOP_K), tl.tok_spec(TOP_K), tl.tok_spec(TOP_K),
                   pl.BlockSpec((1, e), lambda i: (0, 0))],
        out_shape=[jax.ShapeDtypeStruct((tl.n_tok, d), F32), jax.ShapeDtypeStruct((tl.n_tok, d // 2), jnp.uint32),
                   jax.ShapeDtypeStruct((tl.n_tok, TOP_K), jnp.int32),
                   jax.ShapeDtypeStruct((tl.n_tok, TOP_K), jnp.int32),
                   jax.ShapeDtypeStruct((tl.n_tok, TOP_K), F32),
                   jax.ShapeDtypeStruct((1, e), jnp.int32)],
        scratch_shapes=[pltpu.VMEM((1, e), F32)],
        compiler_params=pltpu.CompilerParams(dimension_semantics=("arbitrary",), vmem_limit_bytes=VMEM_LIMIT),
        name="outproj",
    )(oret_p, ogla_p, oret_s, ogla_s, proj, proj, x_p, x_s, ada_p, ada_p, ada_p, ada_s, ada_s, ada_s,
      g_ffn.reshape(1, 1, d), w_ret_o, w_gla_o, w_out, w_r_hi, w_r_lo, b_router.reshape(1, e))


def _expert_kernel(f, te_ref, na_ref, x_ref, wu_ref, bu_ref, wd_ref, bd_ref, y_ref, wu_s, wd_s):
    j = pl.program_id(0)
    active = j < na_ref[0]
    first = jnp.logical_or(j == 0, te_ref[j] != te_ref[jnp.maximum(j - 1, 0)])

    @pl.when(jnp.logical_and(active, first))
    def _():
        wu_s[...] = wu_ref[0].astype(BF16)
        wd_s[...] = wd_ref[0].astype(BF16)

    @pl.when(active)
    def _():
        x_lo, x_hi = _unpack_pair(x_ref[...])
        half = x_lo.shape[1]
        gu = _mm(x_lo.astype(BF16), wu_s[:half, :]) + _mm(x_hi.astype(BF16), wu_s[half:, :]) + bu_ref[0]
        gate = jnp.minimum(gu[:, :f], SWIGLU_LIMIT)
        up = jnp.clip(gu[:, f:], -SWIGLU_LIMIT, SWIGLU_LIMIT)
        act = (up + 1.0) * gate * jax.nn.sigmoid(SWIGLU_ALPHA * gate)
        y_ref[...] = _pack_pair(_mm(act.astype(BF16), wd_s[...]) + bd_ref[0])


def _experts(xs, tile_expert, n_active, w_up, b_up, w_down, b_down, tme):
    r = xs.shape[0]
    e, d, f2 = w_up.shape
    f = f2 // 2
    n_tiles = r // tme

    def row_map(j, te, na):
        return (jnp.minimum(j, na[0] - 1), 0)

    def w_map(j, te, na):
        return (te[jnp.minimum(j, na[0] - 1)], 0, 0)

    return pl.pallas_call(
        functools.partial(_expert_kernel, f),
        grid_spec=pltpu.PrefetchScalarGridSpec(
            num_scalar_prefetch=2,
            grid=(n_tiles,),
            in_specs=[pl.BlockSpec((tme, d // 2), row_map),
                      pl.BlockSpec((1, d, f2), w_map), pl.BlockSpec((1, 1, f2), w_map),
                      pl.BlockSpec((1, f, d), w_map), pl.BlockSpec((1, 1, d), w_map)],
            out_specs=pl.BlockSpec((tme, d // 2), row_map),
            scratch_shapes=[pltpu.VMEM((d, f2), BF16), pltpu.VMEM((f, d), BF16)]),
        out_shape=jax.ShapeDtypeStruct((r, d // 2), jnp.uint32),
        compiler_params=pltpu.CompilerParams(dimension_semantics=("arbitrary",), vmem_limit_bytes=VMEM_LIMIT),
        name="experts",
    )(tile_expert, n_active, xs, w_up, b_up.reshape(e, 1, f2), w_down, b_down.reshape(e, 1, d))


def _sc_mesh():
    return plsc.VectorSubcoreMesh(core_axis_name="core", subcore_axis_name="subcore")


def _sc_split(n_rows, max_chunk):
    info = plsc.get_sparse_core_info()
    n_workers = info.num_cores * info.num_subcores
    assert n_rows % (8 * n_workers) == 0
    per_w = n_rows // n_workers
    chunk = 8
    while chunk * 2 <= max_chunk and per_w % (chunk * 2) == 0:
        chunk *= 2
    return info.num_cores, n_workers, per_w, chunk


def _sc_dispatch(x, pos, n_rows):
    n, w = x.shape
    nc, nw, per_w, chunk = _sc_split(n, 32)
    n_ch = per_w // chunk
    idx = pos.T.reshape(TOP_K, nw, n_ch, chunk).transpose(1, 0, 2, 3).reshape(nw, TOP_K * n_ch, chunk)

    @functools.partial(
        pl.kernel, out_type=jax.ShapeDtypeStruct((n_rows, w), x.dtype), mesh=_sc_mesh(),
        scratch_types=[pltpu.VMEM((TOP_K * n_ch, chunk), jnp.int32), pltpu.VMEM((chunk, w), x.dtype),
                       pltpu.SemaphoreType.DMA])
    def scatter_rows(x_hbm, i_hbm, o_hbm, idx_v, rows_v, sem):
        wid = lax.axis_index("subcore") * nc + lax.axis_index("core")
        base = wid * per_w
        pltpu.sync_copy(i_hbm.at[wid], idx_v)

        @pl.loop(0, n_ch)
        def _(j):
            pltpu.sync_copy(x_hbm.at[pl.ds(base + j * chunk, chunk)], rows_v)
            copies = [pltpu.async_copy(rows_v, o_hbm.at[idx_v.at[k * n_ch + j]], sem) for k in range(TOP_K)]
            for c in copies:
                c.wait()

    return scatter_rows(x, idx)


def _sc_gather(table, idx):
    m = idx.shape[0]
    w = table.shape[1]
    nc, _, per_w, chunk = _sc_split(m, 64)

    @functools.partial(
        pl.kernel, out_type=jax.ShapeDtypeStruct((m, w), table.dtype), mesh=_sc_mesh(),
        scratch_types=[pltpu.VMEM((per_w,), jnp.int32), pltpu.VMEM((chunk, w), table.dtype),
                       pltpu.SemaphoreType.DMA])
    def gather_rows(t_hbm, i_hbm, o_hbm, idx_v, rows_v, sem):
        wid = lax.axis_index("subcore") * nc + lax.axis_index("core")
        base = wid * per_w
        pltpu.sync_copy(i_hbm.at[pl.ds(base, per_w)], idx_v)

        @pl.loop(0, per_w // chunk)
        def _(j):
            off = pl.multiple_of(j * chunk, chunk)
            pltpu.async_copy(t_hbm.at[idx_v.at[pl.ds(off, chunk)]], rows_v, sem).wait()
            pltpu.sync_copy(rows_v, o_hbm.at[pl.ds(base + off, chunk)])

    return gather_rows(table, idx)


def _final_kernel(n_pt, d, h_ref, yg_ref, prob_ref, gtp_ref, gts_ref, g_ref, yp_ref, ys_ref):
    i = pl.program_id(0)
    p = prob_ref[...]
    moe_lo, moe_hi = None, None
    for k in range(TOP_K):
        lo, hi = _unpack_pair(yg_ref[k])
        pk = p[:, k:k + 1]
        moe_lo = pk * lo if moe_lo is None else moe_lo + pk * lo
        moe_hi = pk * hi if moe_hi is None else moe_hi + pk * hi
    moe = jnp.concatenate([moe_lo, moe_hi], axis=1)

    def body(gt, shape):
        h3 = h_ref[...].reshape(shape) + gt * moe.reshape(shape)
        ms = jnp.mean(h3 * h3, axis=-1, keepdims=True)
        return h3 * lax.rsqrt(ms + EPS) * g_ref[...]

    @pl.when(i < n_pt)
    def _():
        yp_ref[...] = body(gtp_ref[0], yp_ref.shape)

    @pl.when(i >= n_pt)
    def _():
        ys_ref[...] = body(gts_ref[0], ys_ref.shape)


def _final(tl, h, yg, probs, ada_p, ada_s, g_final, d):
    return pl.pallas_call(
        functools.partial(_final_kernel, tl.n_pt, d),
        grid=(tl.n,),
        in_specs=[tl.tok_spec(d), pl.BlockSpec((TOP_K, tl.tm, d // 2), lambda i: (0, i, 0)), tl.tok_spec(TOP_K),
                  tl.adap_spec(5, d), tl.adas_spec(5, d), _resident((1, 1, d))],
        out_specs=[tl.xp_spec(d), tl.xs_spec(d)],
        out_shape=[jax.ShapeDtypeStruct((tl.b, tl.t, d), F32), jax.ShapeDtypeStruct((tl.bs, tl.ts, d), F32)],
        compiler_params=pltpu.CompilerParams(dimension_semantics=("arbitrary",), vmem_limit_bytes=VMEM_LIMIT),
        name="final",
    )(h, yg, probs, ada_p, ada_s, g_final.reshape(1, 1, d))


def _pick(n, pref):
    t = min(n, pref)
    while n % t:
        t //= 2
    return t


def _forward(x_prompt, x_sample, c_prompt, c_sample, state_ret, state_gla, w_ada, b_ada, g_norm_mix, g_norm_ffn,
             w_in, w_gk_up, b_gk, g_gla_norm, w_ret_o, w_gla_o, w_out, w_router, b_router, w_up, b_up,
             w_down, b_down, g_final, *, tm, tb, gsz, tme):
    b, t, d = x_prompt.shape
    bs, ts, _ = x_sample.shape
    assert w_ada.shape[0] == 1, "single layer only"
    e = N_EXPERTS
    tl = _Tiles(b, t, bs, ts, tm)
    n_tok = tl.n_tok

    ada = _ada(jnp.concatenate([c_prompt, c_sample], axis=0), w_ada[0], b_ada[0])
    ada_p = ada[:, :b].reshape(6, b, 1, d)
    ada_s = ada[:, b:].reshape(6, bs, 1, d)

    w_in0 = w_in[0]
    n_main = 6 * d
    w_main = jnp.concatenate([w_in0[:, :n_main], w_in0[:, n_main + GLA_GATE_RANK:]], axis=1).astype(BF16)
    w_glr = w_in0[:, n_main:n_main + GLA_GATE_RANK].astype(BF16)
    proj, glr = _inproj(tl, x_prompt, x_sample, ada_p, ada_s, g_norm_mix[0], w_main, w_glr)

    w_gk = w_gk_up[0].astype(BF16)
    bgk = b_gk[0].reshape(1, -1)
    ggn = g_gla_norm[0].reshape(1, -1)
    oret_p, ogla_p, sret_p, sgla_p = _mix_prompt(b, t, d, tb, proj, glr, w_gk, bgk, ggn)
    oret_s, ogla_s, sret_s, sgla_s = _mix_sample(bs, ts, d, b * t, gsz, proj, glr, state_ret[0], state_gla[0],
                                                 w_gk, bgk, ggn)

    w_r = w_router[0]
    w_r_hi = w_r.astype(BF16)
    w_r_lo = (w_r - w_r_hi.astype(F32)).astype(BF16)
    h, n2, idx, rank, probs, counts = _outproj(
        tl, oret_p, ogla_p, oret_s, ogla_s, proj, x_prompt, x_sample, ada_p, ada_s, g_norm_ffn[0],
        w_ret_o[0].astype(BF16), w_gla_o[0].astype(BF16), w_out[0].astype(BF16), w_r_hi, w_r_lo, b_router[0])

    counts = counts[0]
    gsize = ((counts + tme - 1) // tme) * tme
    ends = jnp.cumsum(gsize)
    offs = ends - gsize
    pos = offs[idx] + rank
    max_tiles = (n_tok * TOP_K) // tme + e
    n_active = (ends[-1] // tme).astype(jnp.int32).reshape(1)
    tile_start = jnp.arange(max_tiles, dtype=jnp.int32) * tme
    tile_expert = jnp.minimum(jnp.sum((ends[None, :] <= tile_start[:, None]).astype(jnp.int32), axis=1), e - 1)

    xs = _sc_dispatch(n2, pos, max_tiles * tme)
    ys = _experts(xs, tile_expert, n_active, w_up[0], b_up[0], w_down[0], b_down[0], tme)
    yg = _sc_gather(ys, pos.T.reshape(-1)).reshape(TOP_K, n_tok, d // 2)

    y_p, y_s = _final(tl, h, yg, probs, ada_p, ada_s, g_final, d)
    return (y_p, y_s, sret_p[None], sgla_p[None], sret_s[None], sgla_s[None])


def kernel(x_prompt, x_sample, c_prompt, c_sample, state_ret, state_gla, w_ada, b_ada, g_norm_mix, g_norm_ffn,
           w_in, w_gk_up, b_gk, g_gla_norm, w_ret_o, w_gla_o, w_out, w_router, b_router, w_up, b_up,
           w_down, b_down, g_final):
    t = x_prompt.shape[1]
    bs, ts = x_sample.shape[0], x_sample.shape[1]
    return _forward(x_prompt, x_sample, c_prompt, c_sample, state_ret, state_gla, w_ada, b_ada, g_norm_mix,
                    g_norm_ffn, w_in, w_gk_up, b_gk, g_gla_norm, w_ret_o, w_gla_o, w_out, w_router, b_router,
                    w_up, b_up, w_down, b_down, g_final,
                    tm=_pick(bs * ts, 512), tb=_pick(t, 256), gsz=_pick(bs, 8), tme=512)
```

```python
import functools

import jax
import jax.numpy as jnp
from jax import lax
from jax.experimental import pallas as pl
from jax.experimental.pallas import tpu as pltpu
from jax.experimental.pallas import tpu_sc as plsc

F32 = jnp.float32
BF16 = jnp.bfloat16

N_HEADS = 4
GLA_GATE_RANK = 16
GLA_GATE_NORM = 16.0
GLA_CHUNK = 64
ROPE_BASE = 10000.0
N_EXPERTS = 32
TOP_K = 4
SWIGLU_LIMIT = 7.0
SWIGLU_ALPHA = 1.702
EPS = 1e-6
PAST_LEN = 16384
N_SEG = 8

VMEM_LIMIT = 56 * 1024 * 1024


def _mm(a, b):
    return jnp.dot(a, b, preferred_element_type=F32)


def _mm_nt(a, b):
    return lax.dot_general(a, b, (((1,), (1,)), ((), ())), preferred_element_type=F32)


def _silu(x):
    return x * jax.nn.sigmoid(x)


def _split_hi_lo(x):
    hi = x.astype(BF16)
    lo = (x - hi.astype(F32)).astype(BF16)
    return hi, lo


def _pack_pair(x):
    w = x.shape[1] // 2
    lo = lax.bitcast_convert_type(x[:, :w].astype(BF16).astype(F32), jnp.uint32)
    hi = lax.bitcast_convert_type(x[:, w:].astype(BF16).astype(F32), jnp.uint32)
    return (hi & jnp.uint32(0xFFFF0000)) | (lo >> 16)


def _unpack_pair(p):
    lo = lax.bitcast_convert_type(p << 16, F32)
    hi = lax.bitcast_convert_type(p & jnp.uint32(0xFFFF0000), F32)
    return lo, hi


def _rms_mod(x3, g, sc, sh):
    ms = jnp.mean(x3 * x3, axis=-1, keepdims=True)
    return x3 * lax.rsqrt(ms + EPS) * g * (1.0 + sc) + sh


def _ada_kernel(c_ref, w_ref, b_ref, o_ref):
    cf = _silu(c_ref[...])
    o_ref[0] = _mm(cf.astype(BF16), w_ref[...].astype(BF16)) + b_ref[0]


def _ada(c_all, w_ada, b_ada):
    bc, d = c_all.shape
    n = w_ada.shape[1] // d
    return pl.pallas_call(
        _ada_kernel,
        grid=(n,),
        in_specs=[pl.BlockSpec((bc, d), lambda j: (0, 0)),
                  pl.BlockSpec((d, d), lambda j: (0, j)),
                  pl.BlockSpec((1, 1, d), lambda j: (j, 0, 0))],
        out_specs=pl.BlockSpec((1, bc, d), lambda j: (j, 0, 0)),
        out_shape=jax.ShapeDtypeStruct((n, bc, d), F32),
        compiler_params=pltpu.CompilerParams(dimension_semantics=("arbitrary",), vmem_limit_bytes=VMEM_LIMIT),
        name="ada",
    )(c_all, w_ada, b_ada.reshape(n, 1, d))


class _Tiles:
    def __init__(self, b, t, bs, ts, tm):
        assert t % tm == 0 and (bs * ts) % tm == 0 and tm % ts == 0
        self.b, self.t, self.bs, self.ts, self.tm = b, t, bs, ts, tm
        self.tpb = t // tm
        self.n_pt = b * self.tpb
        self.gs = tm // ts
        self.n_st = (bs * ts) // tm
        self.n = self.n_pt + self.n_st
        self.n_tok = b * t + bs * ts

    def xp_spec(self, d):
        last, tpb = self.n_pt - 1, self.tpb
        return pl.BlockSpec((1, self.tm, d), lambda i: (jnp.minimum(i, last) // tpb, jnp.minimum(i, last) % tpb, 0))

    def xs_spec(self, d):
        n_pt = self.n_pt
        return pl.BlockSpec((self.gs, self.ts, d), lambda i: (jnp.maximum(i - n_pt, 0), 0, 0))

    def adap_spec(self, which, d):
        last, tpb = self.n_pt - 1, self.tpb
        return pl.BlockSpec((1, 1, 1, d), lambda i: (which, jnp.minimum(i, last) // tpb, 0, 0))

    def adas_spec(self, which, d):
        n_pt = self.n_pt
        return pl.BlockSpec((1, self.gs, 1, d), lambda i: (which, jnp.maximum(i - n_pt, 0), 0, 0))

    def tok_spec(self, width):
        return pl.BlockSpec((self.tm, width), lambda i: (i, 0))

    def seg_spec(self, seg, d):
        return pl.BlockSpec((1, self.tm, d), lambda i: (seg, i, 0))


def _resident(shape):
    zeros = (0,) * len(shape)
    return pl.BlockSpec(shape, lambda i: zeros, pipeline_mode=pl.Buffered(1))


def _inproj_kernel(n_pt, d, xp_ref, xs_ref, shp_ref, scp_ref, shs_ref, scs_ref, g_ref, w_ref, wl_ref,
                   proj_ref, glr_ref):
    i = pl.program_id(0)

    def body(x3, sh, sc):
        n = _rms_mod(x3, g_ref[...], sc, sh).reshape(-1, d).astype(BF16)
        for s in range(N_SEG):
            proj_ref[s] = _mm(n, w_ref[:, s * d:(s + 1) * d]).astype(BF16)
        glr_ref[...] = _mm(n, wl_ref[...])

    @pl.when(i < n_pt)
    def _():
        body(xp_ref[...], shp_ref[0], scp_ref[0])

    @pl.when(i >= n_pt)
    def _():
        body(xs_ref[...], shs_ref[0], scs_ref[0])


def _inproj(tl, x_p, x_s, ada_p, ada_s, g_mix, w_main, w_glr):
    d = x_p.shape[-1]
    return pl.pallas_call(
        functools.partial(_inproj_kernel, tl.n_pt, d),
        grid=(tl.n,),
        in_specs=[tl.xp_spec(d), tl.xs_spec(d),
                  tl.adap_spec(0, d), tl.adap_spec(1, d), tl.adas_spec(0, d), tl.adas_spec(1, d),
                  _resident((1, 1, d)), _resident((d, N_SEG * d)), _resident((d, GLA_GATE_RANK))],
        out_specs=[pl.BlockSpec((N_SEG, tl.tm, d), lambda i: (0, i, 0)), tl.tok_spec(GLA_GATE_RANK)],
        out_shape=[jax.ShapeDtypeStruct((N_SEG, tl.n_tok, d), BF16),
                   jax.ShapeDtypeStruct((tl.n_tok, GLA_GATE_RANK), F32)],
        compiler_params=pltpu.CompilerParams(dimension_semantics=("arbitrary",), vmem_limit_bytes=VMEM_LIMIT),
        name="inproj",
    )(x_p, x_s, ada_p, ada_p, ada_s, ada_s, g_mix.reshape(1, 1, d), w_main, w_glr)


def _rope_tables(pos0, t, dk):
    half = dk // 2
    inv = ROPE_BASE ** (-jnp.arange(half, dtype=jnp.float32) / half)
    pos = pos0 + jnp.arange(t)
    ang = pos.astype(jnp.float32)[:, None] * inv[None, :]
    cos, sin = jnp.cos(ang), jnp.sin(ang)
    return jnp.concatenate([cos, cos], axis=-1), jnp.concatenate([-sin, sin], axis=-1)


def _ret_tables(c, dk, dv):
    h = N_HEADS
    log_gamma = jnp.log1p(-jnp.exp2(-5.0 - jnp.arange(h, dtype=jnp.float32)))
    idx = jnp.arange(c, dtype=jnp.float32)
    rel = idx[:, None] - idx[None, :]
    dmask = jnp.where(rel >= 0, jnp.exp(log_gamma[:, None, None] * jnp.maximum(rel, 0.0)), 0.0)
    kdec = jnp.exp(log_gamma[:, None] * (c - 1 - idx))
    qdec = jnp.exp(log_gamma[:, None] * (idx + 1.0))
    cdec = jnp.exp(log_gamma * c)
    return (dmask,
            jnp.broadcast_to(qdec[:, :, None], (h, c, dk)),
            jnp.broadcast_to(kdec[:, :, None], (h, c, dk)),
            jnp.broadcast_to(cdec[:, None, None], (h, 1, dv)))


def _rot(x, cos_f, sin_f):
    return x * cos_f + pltpu.roll(x, x.shape[-1] // 2, 1) * sin_f


def _cross_and_update(q_lhs, k_end, vh, states, masks):
    if masks is None:
        (s,) = states
        return _mm(q_lhs, s.astype(BF16)), [_mm(k_end.T.astype(BF16), vh)]
    cross, incs = None, []
    for s, m in zip(states, masks):
        c = _mm(q_lhs, s.astype(BF16))
        cross = c if cross is None else jnp.where(m, c, cross)
        incs.append(_mm(jnp.where(m, k_end, 0.0).T.astype(BF16), vh))
    return cross, incs


def _ret_head(q, k, vh, gh, states, masks, cos_f, sin_f, dmask, qdec, kdec, cdec):
    dk = q.shape[-1]
    q = _rot(q, cos_f, sin_f)
    k = _rot(k, cos_f, sin_f) * (dk ** -0.5)
    scores = _mm_nt(q.astype(BF16), k.astype(BF16)) * dmask
    cross, incs = _cross_and_update((q * qdec).astype(BF16), k * kdec, vh, states, masks)
    o = _mm(scores.astype(BF16), vh) + cross
    new_states = [cdec * s + u for s, u in zip(states, incs)]
    mu = jnp.mean(o, axis=-1, keepdims=True)
    oc = o - mu
    var = jnp.mean(oc * oc, axis=-1, keepdims=True)
    return _silu(gh) * (oc * lax.rsqrt(var + EPS)), new_states


def _gla_head(q, k, vh, gh, b, states, masks, c, gnorm, causal):
    dk = q.shape[-1]
    b_t = b.T
    if masks is None:
        b_last = b[c - 1:c, :]
    else:
        b_last = None
        for g, m in enumerate(masks):
            row = b[g * c + c - 1:g * c + c, :]
            b_last = row if b_last is None else jnp.where(m, row, b_last)
    q_in = (q * (dk ** -0.5) * jnp.exp(b)).astype(BF16)
    k_in = (k * jnp.exp(-b)).astype(BF16)
    scores = jnp.where(causal, _mm_nt(q_in, k_in), 0.0)
    cross, incs = _cross_and_update(q_in, k * jnp.exp(b_last - b), vh, states, masks)
    o = _mm(scores.astype(BF16), vh) + cross
    new_states = [jnp.exp(b_t[:, g * c + c - 1:g * c + c]) * s + u for g, (s, u) in enumerate(zip(states, incs))]
    o = o * lax.rsqrt(jnp.mean(o * o, axis=-1, keepdims=True) + EPS) * gnorm
    return _silu(gh) * o, new_states


def _log_a(glr, wgk, bgk):
    z = _mm(glr.astype(BF16), wgk) + bgk
    return (jnp.minimum(z, 0.0) - jnp.log1p(jnp.exp(-jnp.abs(z)))) / GLA_GATE_NORM


def _causal(c):
    return lax.broadcasted_iota(jnp.int32, (c, c), 0) >= lax.broadcasted_iota(jnp.int32, (c, c), 1)


def _mixp_kernel(d, tb, rqk_ref, rv_ref, rg_ref, gqk_ref, gv_ref, gg_ref, glr_ref, cos_ref, sin_ref,
                 dmask_ref, qdec_ref, kdec_ref, cdec_ref, tri_ref, wgk_ref, bgk_ref, gn_ref,
                 oret_ref, ogla_ref, sret_ref, sgla_ref, sr_s, sg_s):
    t = pl.program_id(1)
    dk, dv, hq = d // 8, d // 4, d // 2

    @pl.when(t == 0)
    def _():
        sr_s[...] = jnp.zeros_like(sr_s)
        sg_s[...] = jnp.zeros_like(sg_s)

    cos_f, sin_f = cos_ref[...], sin_ref[...]
    for h in range(N_HEADS):
        o, (s_new,) = _ret_head(rqk_ref[0, :, h * dk:(h + 1) * dk].astype(F32),
                                rqk_ref[0, :, hq + h * dk:hq + (h + 1) * dk].astype(F32),
                                rv_ref[0, :, h * dv:(h + 1) * dv],
                                rg_ref[0, :, h * dv:(h + 1) * dv].astype(F32),
                                [sr_s[h]], None, cos_f, sin_f, dmask_ref[h], qdec_ref[h], kdec_ref[h],
                                cdec_ref[h])
        sr_s[h] = s_new
        oret_ref[:, h * dv:(h + 1) * dv] = o.astype(BF16)

    la = _log_a(glr_ref[...], wgk_ref[...], bgk_ref[...])
    la_hi, la_lo = _split_hi_lo(la)
    tri = tri_ref[...]
    b = _mm(tri, la_hi) + _mm(tri, la_lo)
    cg = GLA_CHUNK
    causal = _causal(cg)
    gnorm = gn_ref[...]
    for c in range(tb // cg):
        r0, r1 = c * cg, (c + 1) * cg
        for h in range(N_HEADS):
            o, (s_new,) = _gla_head(gqk_ref[0, r0:r1, h * dk:(h + 1) * dk].astype(F32),
                                    gqk_ref[0, r0:r1, hq + h * dk:hq + (h + 1) * dk].astype(F32),
                                    gv_ref[0, r0:r1, h * dv:(h + 1) * dv],
                                    gg_ref[0, r0:r1, h * dv:(h + 1) * dv].astype(F32),
                                    b[r0:r1, h * dk:(h + 1) * dk], [sg_s[h]], None, cg, gnorm, causal)
            sg_s[h] = s_new
            ogla_ref[r0:r1, h * dv:(h + 1) * dv] = o.astype(BF16)

    @pl.when(t == pl.num_programs(1) - 1)
    def _():
        sret_ref[0] = sr_s[...]
        sgla_ref[0] = sg_s[...]


def _chunk_tri(tb, cg):
    i = jnp.arange(tb)
    return ((i[:, None] >= i[None, :]) & (i[:, None] // cg == i[None, :] // cg)).astype(BF16)


def _mix_prompt(b, t, d, tb, proj, glr, w_gk, b_gk, g_gla):
    dk, dv, hq, h = d // 8, d // 4, d // 2, N_HEADS
    ntb = t // tb
    cos_f, sin_f = _rope_tables(0, t, dk)
    dmask, qdec, kdec, cdec = _ret_tables(tb, dk, dv)
    tri = _chunk_tri(tb, GLA_CHUNK)

    def seg(s):
        return pl.BlockSpec((1, tb, d), lambda bi, ti: (s, bi * ntb + ti, 0))

    def const(shape):
        zeros = (0,) * len(shape)
        return pl.BlockSpec(shape, lambda bi, ti: zeros)

    state_spec = pl.BlockSpec((1, h, dk, dv), lambda bi, ti: (bi, 0, 0, 0))
    tok_spec = pl.BlockSpec((tb, d), lambda bi, ti: (bi * ntb + ti, 0))
    return pl.pallas_call(
        functools.partial(_mixp_kernel, d, tb),
        grid=(b, ntb),
        in_specs=[seg(0), seg(1), seg(2), seg(3), seg(4), seg(5),
                  pl.BlockSpec((tb, GLA_GATE_RANK), lambda bi, ti: (bi * ntb + ti, 0)),
                  pl.BlockSpec((tb, dk), lambda bi, ti: (ti, 0)), pl.BlockSpec((tb, dk), lambda bi, ti: (ti, 0)),
                  const((h, tb, tb)), const((h, tb, dk)), const((h, tb, dk)), const((h, 1, dv)),
                  const((tb, tb)), const((GLA_GATE_RANK, hq)), const((1, hq)), const((1, dv))],
        out_specs=[tok_spec, tok_spec, state_spec, state_spec],
        out_shape=[jax.ShapeDtypeStruct((b * t, d), BF16), jax.ShapeDtypeStruct((b * t, d), BF16),
                   jax.ShapeDtypeStruct((b, h, dk, dv), F32), jax.ShapeDtypeStruct((b, h, dk, dv), F32)],
        scratch_shapes=[pltpu.VMEM((h, dk, dv), F32), pltpu.VMEM((h, dk, dv), F32)],
        compiler_params=pltpu.CompilerParams(dimension_semantics=("arbitrary", "arbitrary"),
                                             vmem_limit_bytes=VMEM_LIMIT),
        name="mix_prompt",
    )(proj, proj, proj, proj, proj, proj, glr, cos_f, sin_f, dmask, qdec, kdec, cdec, tri,
      w_gk, b_gk, g_gla)


def _mixs_kernel(d, ts, gsz, rqk_ref, rv_ref, rg_ref, gqk_ref, gv_ref, gg_ref, glr_ref, cos_ref, sin_ref,
                 dmask_ref, qdec_ref, kdec_ref, cdec_ref, wgk_ref, bgk_ref, gn_ref, sr_in, sg_in,
                 oret_ref, ogla_ref, sr_out, sg_out):
    dk, dv, hq = d // 8, d // 4, d // 2
    pair = 2 * ts
    cos_f, sin_f = cos_ref[...], sin_ref[...]
    gnorm = gn_ref[...]
    ri = lax.broadcasted_iota(jnp.int32, (pair, pair), 0)
    ci = lax.broadcasted_iota(jnp.int32, (pair, pair), 1)
    causal = jnp.logical_and(ri >= ci, (ri < ts) == (ci < ts))
    tri = causal.astype(F32).astype(BF16)
    first = lax.broadcasted_iota(jnp.int32, (pair, 1), 0) < ts
    masks = [first, jnp.logical_not(first)]

    def body(j, carry):
        rows = pl.ds(pl.multiple_of(j * pair, pair), pair)
        s0, s1 = 2 * j, 2 * j + 1
        la_hi, la_lo = _split_hi_lo(_log_a(glr_ref[rows, :], wgk_ref[...], bgk_ref[...]))
        b = _mm(tri, la_hi) + _mm(tri, la_lo)
        for h in range(N_HEADS):
            o, (n0, n1) = _ret_head(rqk_ref[0, rows, h * dk:(h + 1) * dk].astype(F32),
                                    rqk_ref[0, rows, hq + h * dk:hq + (h + 1) * dk].astype(F32),
                                    rv_ref[0, rows, h * dv:(h + 1) * dv],
                                    rg_ref[0, rows, h * dv:(h + 1) * dv].astype(F32),
                                    [sr_in[s0, h], sr_in[s1, h]], masks, cos_f, sin_f,
                                    dmask_ref[h], qdec_ref[h], kdec_ref[h], cdec_ref[h])
            sr_out[s0, h] = n0
            sr_out[s1, h] = n1
            oret_ref[rows, h * dv:(h + 1) * dv] = o.astype(BF16)
            o, (n0, n1) = _gla_head(gqk_ref[0, rows, h * dk:(h + 1) * dk].astype(F32),
                                    gqk_ref[0, rows, hq + h * dk:hq + (h + 1) * dk].astype(F32),
                                    gv_ref[0, rows, h * dv:(h + 1) * dv],
                                    gg_ref[0, rows, h * dv:(h + 1) * dv].astype(F32),
                                    b[:, h * dk:(h + 1) * dk], [sg_in[s0, h], sg_in[s1, h]], masks, ts,
                                    gnorm, causal)
            sg_out[s0, h] = n0
            sg_out[s1, h] = n1
            ogla_ref[rows, h * dv:(h + 1) * dv] = o.astype(BF16)
        return carry

    lax.fori_loop(0, gsz // 2, body, 0)


def _pair_tables(ts, dk, dv):
    cos_f, sin_f = _rope_tables(PAST_LEN, ts, dk)
    dmask, qdec, kdec, cdec = _ret_tables(ts, dk, dv)
    zero = jnp.zeros_like(dmask)
    dmask2 = jnp.concatenate([jnp.concatenate([dmask, zero], axis=2), jnp.concatenate([zero, dmask], axis=2)], axis=1)

    def twice(a, axis):
        return jnp.concatenate([a, a], axis=axis)

    return twice(cos_f, 0), twice(sin_f, 0), dmask2, twice(qdec, 1), twice(kdec, 1), cdec


def _mix_sample(bs, ts, d, n_prompt_tok, gsz, proj, glr, state_ret, state_gla, w_gk, b_gk, g_gla):
    dk, dv, hq, h = d // 8, d // 4, d // 2, N_HEADS
    assert GLA_CHUNK % ts == 0 and bs % gsz == 0 and gsz % 2 == 0 and n_prompt_tok % (gsz * ts) == 0
    rows = gsz * ts
    pair = 2 * ts
    row0 = n_prompt_tok // rows
    cos_f, sin_f, dmask, qdec, kdec, cdec = _pair_tables(ts, dk, dv)

    def seg(s):
        return pl.BlockSpec((1, rows, d), lambda i: (s, row0 + i, 0))

    def const(shape):
        zeros = (0,) * len(shape)
        return pl.BlockSpec(shape, lambda i: zeros)

    state_spec = pl.BlockSpec((gsz, h, dk, dv), lambda i: (i, 0, 0, 0))
    tok_spec = pl.BlockSpec((rows, d), lambda i: (i, 0))
    return pl.pallas_call(
        functools.partial(_mixs_kernel, d, ts, gsz),
        grid=(bs // gsz,),
        in_specs=[seg(0), seg(1), seg(2), seg(3), seg(4), seg(5),
                  pl.BlockSpec((rows, GLA_GATE_RANK), lambda i: (row0 + i, 0)),
                  const((pair, dk)), const((pair, dk)),
                  const((h, pair, pair)), const((h, pair, dk)), const((h, pair, dk)), const((h, 1, dv)),
                  const((GLA_GATE_RANK, hq)), const((1, hq)), const((1, dv)),
                  state_spec, state_spec],
        out_specs=[tok_spec, tok_spec, state_spec, state_spec],
        out_shape=[jax.ShapeDtypeStruct((bs * ts, d), BF16), jax.ShapeDtypeStruct((bs * ts, d), BF16),
                   jax.ShapeDtypeStruct((bs, h, dk, dv), F32), jax.ShapeDtypeStruct((bs, h, dk, dv), F32)],
        compiler_params=pltpu.CompilerParams(dimension_semantics=("arbitrary",), vmem_limit_bytes=VMEM_LIMIT),
        name="mix_sample",
    )(proj, proj, proj, proj, proj, proj, glr, cos_f, sin_f, dmask, qdec, kdec, cdec, w_gk, b_gk, g_gla,
      state_ret, state_gla)


def _outproj_kernel(n_pt, d, tm, orp_ref, ogp_ref, ors_ref, ogs_ref, mgr_ref, mgg_ref, xp_ref, xs_ref,
                    gtp_ref, shp_ref, scp_ref, gts_ref, shs_ref, scs_ref, g_ref,
                    wro_ref, wgo_ref, wo_ref, wrh_ref, wrl_ref, br_ref,
                    h_ref, n2_ref, idx_ref, rank_ref, prob_ref, cnt_ref, carry_s):
    i = pl.program_id(0)
    e = N_EXPERTS

    @pl.when(i == 0)
    def _():
        carry_s[...] = jnp.zeros_like(carry_s)

    def body(out_ret, out_gla, x3, gt, sh, sc):
        a = _mm(out_ret, wro_ref[...])
        b = _mm(out_gla, wgo_ref[...])
        merged = jax.nn.sigmoid(mgr_ref[0].astype(F32)) * a + jax.nn.sigmoid(mgg_ref[0].astype(F32)) * b
        mix = _mm(merged.astype(BF16), wo_ref[...])
        h3 = x3 + gt * mix.reshape(x3.shape)
        h_ref[...] = h3.reshape(-1, d)
        n2 = _rms_mod(h3, g_ref[...], sc, sh).reshape(-1, d)
        n2_ref[...] = _pack_pair(n2)

        n_hi, n_lo = _split_hi_lo(n2)
        logits = _mm(n_hi, wrh_ref[...]) + _mm(n_lo, wrh_ref[...]) + _mm(n_hi, wrl_ref[...]) + br_ref[...]
        iota = lax.broadcasted_iota(jnp.int32, (tm, e), 1)
        work = logits
        vals, idxs = [], []
        for _ in range(TOP_K):
            m = jnp.max(work, axis=-1, keepdims=True)
            ik = jnp.min(jnp.where(work == m, iota, e), axis=-1, keepdims=True)
            vals.append(m)
            idxs.append(ik)
            work = jnp.where(iota == ik, -jnp.inf, work)
        ex = [jnp.exp(v - vals[0]) for v in vals]
        den = ex[0] + ex[1] + ex[2] + ex[3]

        onehot = jnp.zeros((tm, e), F32)
        for ik in idxs:
            onehot = onehot + (iota == ik).astype(F32)
        ltri = (lax.broadcasted_iota(jnp.int32, (tm, tm), 0) > lax.broadcasted_iota(jnp.int32, (tm, tm), 1))
        cum = _mm(ltri.astype(F32).astype(BF16), onehot.astype(BF16)) + carry_s[...]
        lane = lax.broadcasted_iota(jnp.int32, (tm, TOP_K), 1)
        idx_o = jnp.zeros((tm, TOP_K), jnp.int32)
        rank_o = jnp.zeros((tm, TOP_K), jnp.int32)
        prob_o = jnp.zeros((tm, TOP_K), F32)
        for k in range(TOP_K):
            rk = jnp.sum(jnp.where(iota == idxs[k], cum, 0.0), axis=-1, keepdims=True).astype(jnp.int32)
            idx_o = jnp.where(lane == k, idxs[k], idx_o)
            rank_o = jnp.where(lane == k, rk, rank_o)
            prob_o = jnp.where(lane == k, ex[k] / den, prob_o)
        idx_ref[...] = idx_o
        rank_ref[...] = rank_o
        prob_ref[...] = prob_o
        carry_s[...] = carry_s[...] + jnp.sum(onehot, axis=0, keepdims=True)

    @pl.when(i < n_pt)
    def _():
        body(orp_ref[...], ogp_ref[...], xp_ref[...], gtp_ref[0], shp_ref[0], scp_ref[0])

    @pl.when(i >= n_pt)
    def _():
        body(ors_ref[...], ogs_ref[...], xs_ref[...], gts_ref[0], shs_ref[0], scs_ref[0])

    @pl.when(i == pl.num_programs(0) - 1)
    def _():
        cnt_ref[...] = carry_s[...].astype(jnp.int32)


def _outproj(tl, oret_p, ogla_p, oret_s, ogla_s, proj, x_p, x_s, ada_p, ada_s, g_ffn,
             w_ret_o, w_gla_o, w_out, w_r_hi, w_r_lo, b_router):
    d = x_p.shape[-1]
    tm, e, n_pt = tl.tm, N_EXPERTS, tl.n_pt
    last = n_pt - 1
    p_spec = pl.BlockSpec((tm, d), lambda i: (jnp.minimum(i, last), 0))
    s_spec = pl.BlockSpec((tm, d), lambda i: (jnp.maximum(i - n_pt, 0), 0))
    return pl.pallas_call(
        functools.partial(_outproj_kernel, n_pt, d, tm),
        grid=(tl.n,),
        in_specs=[p_spec, p_spec, s_spec, s_spec, tl.seg_spec(6, d), tl.seg_spec(7, d),
                  tl.xp_spec(d), tl.xs_spec(d),
                  tl.adap_spec(2, d), tl.adap_spec(3, d), tl.adap_spec(4, d),
                  tl.adas_spec(2, d), tl.adas_spec(3, d), tl.adas_spec(4, d),
                  _resident((1, 1, d)), _resident((d, d)), _resident((d, d)), _resident((d, d)),
                  _resident((d, e)), _resident((d, e)), _resident((1, e))],
        out_specs=[tl.tok_spec(d), tl.tok_spec(d // 2), tl.tok_spec(TOP_K), tl.tok_spec(TOP_K), tl.tok_spec(TOP_K),
                   pl.BlockSpec((1, e), lambda i: (0, 0))],
        out_shape=[jax.ShapeDtypeStruct((tl.n_tok, d), F32), jax.ShapeDtypeStruct((tl.n_tok, d // 2), jnp.uint32),
                   jax.ShapeDtypeStruct((tl.n_tok, TOP_K), jnp.int32),
                   jax.ShapeDtypeStruct((tl.n_tok, TOP_K), jnp.int32),
                   jax.ShapeDtypeStruct((tl.n_tok, TOP_K), F32),
                   jax.ShapeDtypeStruct((1, e), jnp.int32)],
        scratch_shapes=[pltpu.VMEM((1, e), F32)],
        compiler_params=pltpu.CompilerParams(dimension_semantics=("arbitrary",), vmem_limit_bytes=VMEM_LIMIT),
        name="outproj",
    )(oret_p, ogla_p, oret_s, ogla_s, proj, proj, x_p, x_s, ada_p, ada_p, ada_p, ada_s, ada_s, ada_s,
      g_ffn.reshape(1, 1, d), w_ret_o, w_gla_o, w_out, w_r_hi, w_r_lo, b_router.reshape(1, e))


EXPERT_SUBTILES = 2


def _expert_kernel(f, te_ref, na_ref, tr_ref, x_ref, wu_ref, bu_ref, wd_ref, bd_ref, y_ref, wu_s, wd_s):
    j = pl.program_id(0)
    active = j < na_ref[0]
    first = jnp.logical_or(j == 0, te_ref[j] != te_ref[jnp.maximum(j - 1, 0)])

    @pl.when(jnp.logical_and(active, first))
    def _():
        wu_s[...] = wu_ref[0].astype(BF16)
        wd_s[...] = wd_ref[0].astype(BF16)

    slab = x_ref.shape[0] // EXPERT_SUBTILES
    half = x_ref.shape[1]
    for s in range(EXPERT_SUBTILES):
        @pl.when(jnp.logical_and(active, tr_ref[j] > s * slab))
        def _():
            rows = slice(s * slab, (s + 1) * slab)
            x_lo, x_hi = _unpack_pair(x_ref[rows, :])
            gu = _mm(x_lo.astype(BF16), wu_s[:half, :]) + _mm(x_hi.astype(BF16), wu_s[half:, :]) + bu_ref[0]
            gate = jnp.minimum(gu[:, :f], SWIGLU_LIMIT)
            up = jnp.clip(gu[:, f:], -SWIGLU_LIMIT, SWIGLU_LIMIT)
            act = (up + 1.0) * gate * jax.nn.sigmoid(SWIGLU_ALPHA * gate)
            y_ref[rows, :] = _pack_pair(_mm(act.astype(BF16), wd_s[...]) + bd_ref[0])


def _experts(xs, tile_expert, n_active, tile_rows, w_up, b_up, w_down, b_down, tme):
    r = xs.shape[0]
    e, d, f2 = w_up.shape
    f = f2 // 2
    n_tiles = r // tme

    def row_map(j, te, na, tr):
        return (jnp.minimum(j, na[0] - 1), 0)

    def w_map(j, te, na, tr):
        return (te[jnp.minimum(j, na[0] - 1)], 0, 0)

    return pl.pallas_call(
        functools.partial(_expert_kernel, f),
        grid_spec=pltpu.PrefetchScalarGridSpec(
            num_scalar_prefetch=3,
            grid=(n_tiles,),
            in_specs=[pl.BlockSpec((tme, d // 2), row_map),
                      pl.BlockSpec((1, d, f2), w_map), pl.BlockSpec((1, 1, f2), w_map),
                      pl.BlockSpec((1, f, d), w_map), pl.BlockSpec((1, 1, d), w_map)],
            out_specs=pl.BlockSpec((tme, d // 2), row_map),
            scratch_shapes=[pltpu.VMEM((d, f2), BF16), pltpu.VMEM((f, d), BF16)]),
        out_shape=jax.ShapeDtypeStruct((r, d // 2), jnp.uint32),
        compiler_params=pltpu.CompilerParams(dimension_semantics=("arbitrary",), vmem_limit_bytes=VMEM_LIMIT),
        name="experts",
    )(tile_expert, n_active, tile_rows, xs, w_up, b_up.reshape(e, 1, f2), w_down, b_down.reshape(e, 1, d))


def _sc_mesh():
    return plsc.VectorSubcoreMesh(core_axis_name="core", subcore_axis_name="subcore")


def _sc_split(n_rows, max_chunk):
    info = plsc.get_sparse_core_info()
    n_workers = info.num_cores * info.num_subcores
    assert n_rows % (8 * n_workers) == 0
    per_w = n_rows // n_workers
    chunk = 8
    while chunk * 2 <= max_chunk and per_w % (chunk * 2) == 0:
        chunk *= 2
    return info.num_cores, n_workers, per_w, chunk


def _sc_dispatch(x, pos, n_rows):
    n, w = x.shape
    nc, nw, per_w, chunk = _sc_split(n, 32)
    n_ch = per_w // chunk
    idx = pos.T.reshape(TOP_K, nw, n_ch, chunk).transpose(1, 0, 2, 3).reshape(nw, TOP_K * n_ch, chunk)

    @functools.partial(
        pl.kernel, out_type=jax.ShapeDtypeStruct((n_rows, w), x.dtype), mesh=_sc_mesh(),
        scratch_types=[pltpu.VMEM((TOP_K * n_ch, chunk), jnp.int32), pltpu.VMEM((2, chunk, w), x.dtype),
                       pltpu.SemaphoreType.DMA((2,)), pltpu.SemaphoreType.DMA((2,))])
    def scatter_rows(x_hbm, i_hbm, o_hbm, idx_v, rows_v, rsem, wsem):
        wid = lax.axis_index("subcore") * nc + lax.axis_index("core")
        base = wid * per_w
        pltpu.sync_copy(i_hbm.at[wid], idx_v)

        def read(j, slot):
            return pltpu.make_async_copy(x_hbm.at[pl.ds(base + j * chunk, chunk)], rows_v.at[slot], rsem.at[slot])

        def write(j, slot, k):
            return pltpu.make_async_copy(rows_v.at[slot], o_hbm.at[idx_v.at[k * n_ch + j]], wsem.at[slot])

        read(0, 0).start()

        @pl.loop(0, n_ch, step=2)
        def _(j0):
            for b in range(2):
                j = j0 + b

                @pl.when(j < n_ch)
                def _():
                    read(j, b).wait()

                    @pl.when(j + 1 < n_ch)
                    def _():
                        @pl.when(j >= 1)
                        def _():
                            for k in range(TOP_K):
                                write(j - 1, 1 - b, k).wait()

                        read(j + 1, 1 - b).start()

                    for k in range(TOP_K):
                        write(j, b, k).start()

        for jj in range(max(n_ch - 2, 0), n_ch):
            for k in range(TOP_K):
                write(jj, jj % 2, k).wait()

    return scatter_rows(x, idx)


def _sc_gather(table, idx):
    m = idx.shape[0]
    w = table.shape[1]
    nc, _, per_w, chunk = _sc_split(m, 64)
    n_ch = per_w // chunk

    @functools.partial(
        pl.kernel, out_type=jax.ShapeDtypeStruct((m, w), table.dtype), mesh=_sc_mesh(),
        scratch_types=[pltpu.VMEM((per_w,), jnp.int32), pltpu.VMEM((2, chunk, w), table.dtype),
                       pltpu.SemaphoreType.DMA((2,)), pltpu.SemaphoreType.DMA((2,))])
    def gather_rows(t_hbm, i_hbm, o_hbm, idx_v, rows_v, gsem, wsem):
        wid = lax.axis_index("subcore") * nc + lax.axis_index("core")
        base = wid * per_w
        pltpu.sync_copy(i_hbm.at[pl.ds(base, per_w)], idx_v)

        def gather(j, slot):
            off = pl.multiple_of(j * chunk, chunk)
            return pltpu.make_async_copy(t_hbm.at[idx_v.at[pl.ds(off, chunk)]], rows_v.at[slot], gsem.at[slot])

        def write(j, slot):
            off = pl.multiple_of(j * chunk, chunk)
            return pltpu.make_async_copy(rows_v.at[slot], o_hbm.at[pl.ds(base + off, chunk)], wsem.at[slot])

        gather(0, 0).start()

        @pl.loop(0, n_ch, step=2)
        def _(j0):
            for b in range(2):
                j = j0 + b

                @pl.when(j < n_ch)
                def _():
                    gather(j, b).wait()

                    @pl.when(j + 1 < n_ch)
                    def _():
                        @pl.when(j >= 1)
                        def _():
                            write(j - 1, 1 - b).wait()

                        gather(j + 1, 1 - b).start()

                    write(j, b).start()

        for jj in range(max(n_ch - 2, 0), n_ch):
            write(jj, jj % 2).wait()

    return gather_rows(table, idx)


def _final_kernel(n_pt, d, h_ref, yg_ref, prob_ref, gtp_ref, gts_ref, g_ref, yp_ref, ys_ref):
    i = pl.program_id(0)
    p = prob_ref[...]
    moe_lo, moe_hi = None, None
    for k in range(TOP_K):
        lo, hi = _unpack_pair(yg_ref[k])
        pk = p[:, k:k + 1]
        moe_lo = pk * lo if moe_lo is None else moe_lo + pk * lo
        moe_hi = pk * hi if moe_hi is None else moe_hi + pk * hi
    moe = jnp.concatenate([moe_lo, moe_hi], axis=1)

    def body(gt, shape):
        h3 = h_ref[...].reshape(shape) + gt * moe.reshape(shape)
        ms = jnp.mean(h3 * h3, axis=-1, keepdims=True)
        return h3 * lax.rsqrt(ms + EPS) * g_ref[...]

    @pl.when(i < n_pt)
    def _():
        yp_ref[...] = body(gtp_ref[0], yp_ref.shape)

    @pl.when(i >= n_pt)
    def _():
        ys_ref[...] = body(gts_ref[0], ys_ref.shape)


def _final(tl, h, yg, probs, ada_p, ada_s, g_final, d):
    return pl.pallas_call(
        functools.partial(_final_kernel, tl.n_pt, d),
        grid=(tl.n,),
        in_specs=[tl.tok_spec(d), pl.BlockSpec((TOP_K, tl.tm, d // 2), lambda i: (0, i, 0)), tl.tok_spec(TOP_K),
                  tl.adap_spec(5, d), tl.adas_spec(5, d), _resident((1, 1, d))],
        out_specs=[tl.xp_spec(d), tl.xs_spec(d)],
        out_shape=[jax.ShapeDtypeStruct((tl.b, tl.t, d), F32), jax.ShapeDtypeStruct((tl.bs, tl.ts, d), F32)],
        compiler_params=pltpu.CompilerParams(dimension_semantics=("arbitrary",), vmem_limit_bytes=VMEM_LIMIT),
        name="final",
    )(h, yg, probs, ada_p, ada_s, g_final.reshape(1, 1, d))


def _pick(n, pref):
    t = min(n, pref)
    while n % t:
        t //= 2
    return t


def _forward(x_prompt, x_sample, c_prompt, c_sample, state_ret, state_gla, w_ada, b_ada, g_norm_mix, g_norm_ffn,
             w_in, w_gk_up, b_gk, g_gla_norm, w_ret_o, w_gla_o, w_out, w_router, b_router, w_up, b_up,
             w_down, b_down, g_final, *, tm, tb, gsz, tme):
    b, t, d = x_prompt.shape
    bs, ts, _ = x_sample.shape
    assert w_ada.shape[0] == 1, "single layer only"
    e = N_EXPERTS
    tl = _Tiles(b, t, bs, ts, tm)
    n_tok = tl.n_tok

    ada = _ada(jnp.concatenate([c_prompt, c_sample], axis=0), w_ada[0], b_ada[0])
    ada_p = ada[:, :b].reshape(6, b, 1, d)
    ada_s = ada[:, b:].reshape(6, bs, 1, d)

    w_in0 = w_in[0]
    n_main = 6 * d
    w_main = jnp.concatenate([w_in0[:, :n_main], w_in0[:, n_main + GLA_GATE_RANK:]], axis=1).astype(BF16)
    w_glr = w_in0[:, n_main:n_main + GLA_GATE_RANK].astype(BF16)
    proj, glr = _inproj(tl, x_prompt, x_sample, ada_p, ada_s, g_norm_mix[0], w_main, w_glr)

    w_gk = w_gk_up[0].astype(BF16)
    bgk = b_gk[0].reshape(1, -1)
    ggn = g_gla_norm[0].reshape(1, -1)
    oret_p, ogla_p, sret_p, sgla_p = _mix_prompt(b, t, d, tb, proj, glr, w_gk, bgk, ggn)
    oret_s, ogla_s, sret_s, sgla_s = _mix_sample(bs, ts, d, b * t, gsz, proj, glr, state_ret[0], state_gla[0],
                                                 w_gk, bgk, ggn)

    w_r = w_router[0]
    w_r_hi = w_r.astype(BF16)
    w_r_lo = (w_r - w_r_hi.astype(F32)).astype(BF16)
    h, n2, idx, rank, probs, counts = _outproj(
        tl, oret_p, ogla_p, oret_s, ogla_s, proj, x_prompt, x_sample, ada_p, ada_s, g_norm_ffn[0],
        w_ret_o[0].astype(BF16), w_gla_o[0].astype(BF16), w_out[0].astype(BF16), w_r_hi, w_r_lo, b_router[0])

    counts = counts[0]
    gsize = ((counts + tme - 1) // tme) * tme
    ends = jnp.cumsum(gsize)
    offs = ends - gsize
    experts = jnp.arange(e, dtype=jnp.int32)
    pos = jnp.sum(jnp.where(idx[..., None] == experts, offs, 0), axis=-1) + rank
    max_tiles = (n_tok * TOP_K) // tme + e
    n_active = (ends[-1] // tme).astype(jnp.int32).reshape(1)
    tile_start = jnp.arange(max_tiles, dtype=jnp.int32) * tme
    tile_expert = jnp.minimum(jnp.sum((ends[None, :] <= tile_start[:, None]).astype(jnp.int32), axis=1), e - 1)
    last_row = jnp.sum(jnp.where(tile_expert[:, None] == experts, offs + counts, 0), axis=1)
    tile_rows = jnp.clip(last_row - tile_start, 0, tme).astype(jnp.int32)

    xs = _sc_dispatch(n2, pos, max_tiles * tme)
    ys = _experts(xs, tile_expert, n_active, tile_rows, w_up[0], b_up[0], w_down[0], b_down[0], tme)
    yg = _sc_gather(ys, pos.T.reshape(-1)).reshape(TOP_K, n_tok, d // 2)

    y_p, y_s = _final(tl, h, yg, probs, ada_p, ada_s, g_final, d)
    return (y_p, y_s, sret_p[None], sgla_p[None], sret_s[None], sgla_s[None])


def kernel(x_prompt, x_sample, c_prompt, c_sample, state_ret, state_gla, w_ada, b_ada, g_norm_mix, g_norm_ffn,
           w_in, w_gk_up, b_gk, g_gla_norm, w_ret_o, w_gla_o, w_out, w_router, b_router, w_up, b_up,
           w_down, b_down, g_final):
    t = x_prompt.shape[1]
    bs, ts = x_sample.shape[0], x_sample.shape[1]
    return _forward(x_prompt, x_sample, c_prompt, c_sample, state_ret, state_gla, w_ada, b_ada, g_norm_mix,
                    g_norm_ffn, w_in, w_gk_up, b_gk, g_gla_norm, w_ret_o, w_gla_o, w_out, w_router, b_router,
                    w_up, b_up, w_down, b_down, g_final,
                    tm=_pick(bs * ts, 512), tb=_pick(t, 256), gsz=_pick(bs, 8), tme=512)
```

```python
import functools

import jax
import jax.numpy as jnp
from jax import lax
from jax.experimental import pallas as pl
from jax.experimental.pallas import tpu as pltpu
from jax.experimental.pallas import tpu_sc as plsc

F32 = jnp.float32
BF16 = jnp.bfloat16

N_HEADS = 4
GLA_GATE_RANK = 16
GLA_GATE_NORM = 16.0
GLA_CHUNK = 64
ROPE_BASE = 10000.0
N_EXPERTS = 32
TOP_K = 4
SWIGLU_LIMIT = 7.0
SWIGLU_ALPHA = 1.702
EPS = 1e-6
PAST_LEN = 16384
N_SEG = 8

VMEM_LIMIT = 56 * 1024 * 1024


def _mm(a, b):
    return jnp.dot(a, b, preferred_element_type=F32)


def _mm_nt(a, b):
    return lax.dot_general(a, b, (((1,), (1,)), ((), ())), preferred_element_type=F32)


def _silu(x):
    return x * jax.nn.sigmoid(x)


def _split_hi_lo(x):
    hi = x.astype(BF16)
    lo = (x - hi.astype(F32)).astype(BF16)
    return hi, lo


def _pack_pair(x):
    w = x.shape[1] // 2
    lo = lax.bitcast_convert_type(x[:, :w].astype(BF16).astype(F32), jnp.uint32)
    hi = lax.bitcast_convert_type(x[:, w:].astype(BF16).astype(F32), jnp.uint32)
    return (hi & jnp.uint32(0xFFFF0000)) | (lo >> 16)


def _unpack_pair(p):
    lo = lax.bitcast_convert_type(p << 16, F32)
    hi = lax.bitcast_convert_type(p & jnp.uint32(0xFFFF0000), F32)
    return lo, hi


def _rms_mod(x3, g, sc, sh):
    ms = jnp.mean(x3 * x3, axis=-1, keepdims=True)
    return x3 * lax.rsqrt(ms + EPS) * g * (1.0 + sc) + sh


def _ada_kernel(c_ref, w_ref, b_ref, o_ref):
    cf = _silu(c_ref[...])
    o_ref[0] = _mm(cf.astype(BF16), w_ref[...].astype(BF16)) + b_ref[0]


def _ada(c_all, w_ada, b_ada):
    bc, d = c_all.shape
    n = w_ada.shape[1] // d
    return pl.pallas_call(
        _ada_kernel,
        grid=(n,),
        in_specs=[pl.BlockSpec((bc, d), lambda j: (0, 0)),
                  pl.BlockSpec((d, d), lambda j: (0, j)),
                  pl.BlockSpec((1, 1, d), lambda j: (j, 0, 0))],
        out_specs=pl.BlockSpec((1, bc, d), lambda j: (j, 0, 0)),
        out_shape=jax.ShapeDtypeStruct((n, bc, d), F32),
        compiler_params=pltpu.CompilerParams(dimension_semantics=("arbitrary",), vmem_limit_bytes=VMEM_LIMIT),
        name="ada",
    )(c_all, w_ada, b_ada.reshape(n, 1, d))


class _Tiles:
    def __init__(self, b, t, bs, ts, tm):
        assert t % tm == 0 and (bs * ts) % tm == 0 and tm % ts == 0
        self.b, self.t, self.bs, self.ts, self.tm = b, t, bs, ts, tm
        self.tpb = t // tm
        self.n_pt = b * self.tpb
        self.gs = tm // ts
        self.n_st = (bs * ts) // tm
        self.n = self.n_pt + self.n_st
        self.n_tok = b * t + bs * ts

    def xp_spec(self, d):
        last, tpb = self.n_pt - 1, self.tpb
        return pl.BlockSpec((1, self.tm, d), lambda i: (jnp.minimum(i, last) // tpb, jnp.minimum(i, last) % tpb, 0))

    def xs_spec(self, d):
        n_pt = self.n_pt
        return pl.BlockSpec((self.gs, self.ts, d), lambda i: (jnp.maximum(i - n_pt, 0), 0, 0))

    def adap_spec(self, which, d):
        last, tpb = self.n_pt - 1, self.tpb
        return pl.BlockSpec((1, 1, 1, d), lambda i: (which, jnp.minimum(i, last) // tpb, 0, 0))

    def adas_spec(self, which, d):
        n_pt = self.n_pt
        return pl.BlockSpec((1, self.gs, 1, d), lambda i: (which, jnp.maximum(i - n_pt, 0), 0, 0))

    def tok_spec(self, width):
        return pl.BlockSpec((self.tm, width), lambda i: (i, 0))

    def seg_spec(self, seg, d):
        return pl.BlockSpec((1, self.tm, d), lambda i: (seg, i, 0))


def _resident(shape):
    zeros = (0,) * len(shape)
    return pl.BlockSpec(shape, lambda i: zeros, pipeline_mode=pl.Buffered(1))


def _inproj_kernel(d, xs_ref, shs_ref, scs_ref, g_ref, w_ref, wl_ref, proj_ref, glr_ref):
    n = _rms_mod(xs_ref[...], g_ref[...], scs_ref[0], shs_ref[0]).reshape(-1, d).astype(BF16)
    for s in range(N_SEG):
        proj_ref[s] = _mm(n, w_ref[:, s * d:(s + 1) * d]).astype(BF16)
    glr_ref[...] = _mm(n, wl_ref[...])


def _inproj_sample(tl, x_s, ada_s, g_mix, w_main, w_glr):
    bs, ts, d = x_s.shape
    n_tok = bs * ts

    def ada_spec(which):
        return pl.BlockSpec((1, tl.gs, 1, d), lambda i: (which, i, 0, 0))

    return pl.pallas_call(
        functools.partial(_inproj_kernel, d),
        grid=(tl.n_st,),
        in_specs=[pl.BlockSpec((tl.gs, ts, d), lambda i: (i, 0, 0)), ada_spec(0), ada_spec(1),
                  _resident((1, 1, d)), _resident((d, N_SEG * d)), _resident((d, GLA_GATE_RANK))],
        out_specs=[pl.BlockSpec((N_SEG, tl.tm, d), lambda i: (0, i, 0)),
                   pl.BlockSpec((tl.tm, GLA_GATE_RANK), lambda i: (i, 0))],
        out_shape=[jax.ShapeDtypeStruct((N_SEG, n_tok, d), BF16), jax.ShapeDtypeStruct((n_tok, GLA_GATE_RANK), F32)],
        compiler_params=pltpu.CompilerParams(dimension_semantics=("arbitrary",), vmem_limit_bytes=VMEM_LIMIT),
        name="inproj_sample",
    )(x_s, ada_s, ada_s, g_mix.reshape(1, 1, d), w_main, w_glr)


def _rope_tables(pos0, t, dk):
    half = dk // 2
    inv = ROPE_BASE ** (-jnp.arange(half, dtype=jnp.float32) / half)
    pos = pos0 + jnp.arange(t)
    ang = pos.astype(jnp.float32)[:, None] * inv[None, :]
    cos, sin = jnp.cos(ang), jnp.sin(ang)
    return jnp.concatenate([cos, cos], axis=-1), jnp.concatenate([-sin, sin], axis=-1)


def _ret_tables(c, dk, dv):
    h = N_HEADS
    log_gamma = jnp.log1p(-jnp.exp2(-5.0 - jnp.arange(h, dtype=jnp.float32)))
    idx = jnp.arange(c, dtype=jnp.float32)
    rel = idx[:, None] - idx[None, :]
    dmask = jnp.where(rel >= 0, jnp.exp(log_gamma[:, None, None] * jnp.maximum(rel, 0.0)), 0.0)
    kdec = jnp.exp(log_gamma[:, None] * (c - 1 - idx))
    qdec = jnp.exp(log_gamma[:, None] * (idx + 1.0))
    cdec = jnp.exp(log_gamma * c)
    return (dmask,
            jnp.broadcast_to(qdec[:, :, None], (h, c, dk)),
            jnp.broadcast_to(kdec[:, :, None], (h, c, dk)),
            jnp.broadcast_to(cdec[:, None, None], (h, 1, dv)))


def _rot(x, cos_f, sin_f):
    return x * cos_f + pltpu.roll(x, x.shape[-1] // 2, 1) * sin_f


def _cross_and_update(q_lhs, k_end, vh, states, masks):
    if masks is None:
        (s,) = states
        return _mm(q_lhs, s.astype(BF16)), [_mm(k_end.T.astype(BF16), vh)]
    cross, incs = None, []
    for s, m in zip(states, masks):
        c = _mm(q_lhs, s.astype(BF16))
        cross = c if cross is None else jnp.where(m, c, cross)
        incs.append(_mm(jnp.where(m, k_end, 0.0).T.astype(BF16), vh))
    return cross, incs


def _ret_head(q, k, vh, gh, states, masks, cos_f, sin_f, dmask, qdec, kdec, cdec):
    dk = q.shape[-1]
    q = _rot(q, cos_f, sin_f)
    k = _rot(k, cos_f, sin_f) * (dk ** -0.5)
    scores = _mm_nt(q.astype(BF16), k.astype(BF16)) * dmask
    cross, incs = _cross_and_update((q * qdec).astype(BF16), k * kdec, vh, states, masks)
    o = _mm(scores.astype(BF16), vh) + cross
    new_states = [cdec * s + u for s, u in zip(states, incs)]
    mu = jnp.mean(o, axis=-1, keepdims=True)
    oc = o - mu
    var = jnp.mean(oc * oc, axis=-1, keepdims=True)
    return _silu(gh) * (oc * lax.rsqrt(var + EPS)), new_states


def _gla_head(q, k, vh, gh, b, states, masks, c, gnorm, causal):
    dk = q.shape[-1]
    b_t = b.T
    if masks is None:
        b_last = b[c - 1:c, :]
    else:
        b_last = None
        for g, m in enumerate(masks):
            row = b[g * c + c - 1:g * c + c, :]
            b_last = row if b_last is None else jnp.where(m, row, b_last)
    q_in = (q * (dk ** -0.5) * jnp.exp(b)).astype(BF16)
    k_in = (k * jnp.exp(-b)).astype(BF16)
    scores = jnp.where(causal, _mm_nt(q_in, k_in), 0.0)
    cross, incs = _cross_and_update(q_in, k * jnp.exp(b_last - b), vh, states, masks)
    o = _mm(scores.astype(BF16), vh) + cross
    new_states = [jnp.exp(b_t[:, g * c + c - 1:g * c + c]) * s + u for g, (s, u) in enumerate(zip(states, incs))]
    o = o * lax.rsqrt(jnp.mean(o * o, axis=-1, keepdims=True) + EPS) * gnorm
    return _silu(gh) * o, new_states


def _log_a(glr, wgk, bgk):
    z = _mm(glr.astype(BF16), wgk) + bgk
    return (jnp.minimum(z, 0.0) - jnp.log1p(jnp.exp(-jnp.abs(z)))) / GLA_GATE_NORM


def _causal(c):
    return lax.broadcasted_iota(jnp.int32, (c, c), 0) >= lax.broadcasted_iota(jnp.int32, (c, c), 1)


def _proj_block(d, x3, sh, sc, g, w_ref, wl_ref, proj_s, glr_s):
    n = _rms_mod(x3, g, sc, sh).reshape(-1, d).astype(BF16)
    for seg in range(N_SEG):
        proj_s[:, seg * d:(seg + 1) * d] = _mm(n, w_ref[:, seg * d:(seg + 1) * d]).astype(BF16)
    glr_s[...] = _mm(n, wl_ref[...])


def _mix_block(d, tb, proj_s, glr_s, cos_f, sin_f, dmask_ref, qdec_ref, kdec_ref, cdec_ref, tri, wgk, bgk, gnorm,
               sr_s, sg_s, oret_ref, ogla_ref, mg_ref, r_off):
    dk, dv, hq = d // 8, d // 4, d // 2
    rqk, rv, rg, gqk, gv, gg, mg = (i * d for i in range(7))
    for h in range(N_HEADS):
        o, (s_new,) = _ret_head(proj_s[:, rqk + h * dk:rqk + (h + 1) * dk].astype(F32),
                                proj_s[:, rqk + hq + h * dk:rqk + hq + (h + 1) * dk].astype(F32),
                                proj_s[:, rv + h * dv:rv + (h + 1) * dv],
                                proj_s[:, rg + h * dv:rg + (h + 1) * dv].astype(F32),
                                [sr_s[h]], None, cos_f, sin_f, dmask_ref[h], qdec_ref[h], kdec_ref[h],
                                cdec_ref[h])
        sr_s[h] = s_new
        oret_ref[r_off:r_off + tb, h * dv:(h + 1) * dv] = o.astype(BF16)

    la_hi, la_lo = _split_hi_lo(_log_a(glr_s[...], wgk, bgk))
    b = _mm(tri, la_hi) + _mm(tri, la_lo)
    cg = GLA_CHUNK
    causal = _causal(cg)
    for c in range(tb // cg):
        r0, r1 = c * cg, (c + 1) * cg
        for h in range(N_HEADS):
            o, (s_new,) = _gla_head(proj_s[r0:r1, gqk + h * dk:gqk + (h + 1) * dk].astype(F32),
                                    proj_s[r0:r1, gqk + hq + h * dk:gqk + hq + (h + 1) * dk].astype(F32),
                                    proj_s[r0:r1, gv + h * dv:gv + (h + 1) * dv],
                                    proj_s[r0:r1, gg + h * dv:gg + (h + 1) * dv].astype(F32),
                                    b[r0:r1, h * dk:(h + 1) * dk], [sg_s[h]], None, cg, gnorm, causal)
            sg_s[h] = s_new
            ogla_ref[r_off + r0:r_off + r1, h * dv:(h + 1) * dv] = o.astype(BF16)
    mg_ref[0, r_off:r_off + tb, :] = proj_s[:, mg:mg + d]
    mg_ref[1, r_off:r_off + tb, :] = proj_s[:, mg + d:mg + 2 * d]


def _frontp_kernel(d, tb, ntb, x0_ref, xa_ref, xb_ref, sh0_ref, sc0_ref, sha_ref, sca_ref, shb_ref, scb_ref,
                   g_ref, w_ref, wl_ref, cosa_ref, sina_ref, cosb_ref, sinb_ref,
                   dmask_ref, qdec_ref, kdec_ref, cdec_ref, tri_ref, wgk_ref, bgk_ref, gn_ref,
                   oret_ref, ogla_ref, mg_ref, sret_ref, sgla_ref, pa_s, pb_s, ga_s, gb_s, sr_s, sg_s):
    p = pl.program_id(0)
    blk = 2 * p
    g = g_ref[...]
    proj = functools.partial(_proj_block, d)
    mix = functools.partial(_mix_block, d, tb)
    tables = (dmask_ref, qdec_ref, kdec_ref, cdec_ref, tri_ref[...], wgk_ref[...], bgk_ref[...], gn_ref[...])

    @pl.when(p == 0)
    def _():
        proj(x0_ref[...], sh0_ref[0], sc0_ref[0], g, w_ref, wl_ref, pa_s, ga_s)

    @pl.when(blk % ntb == 0)
    def _():
        sr_s[...] = jnp.zeros_like(sr_s)
        sg_s[...] = jnp.zeros_like(sg_s)

    proj(xa_ref[...], sha_ref[0], sca_ref[0], g, w_ref, wl_ref, pb_s, gb_s)
    mix(pa_s, ga_s, cosa_ref[...], sina_ref[...], *tables, sr_s, sg_s, oret_ref, ogla_ref, mg_ref, 0)
    proj(xb_ref[...], shb_ref[0], scb_ref[0], g, w_ref, wl_ref, pa_s, ga_s)
    mix(pb_s, gb_s, cosb_ref[...], sinb_ref[...], *tables, sr_s, sg_s, oret_ref, ogla_ref, mg_ref, tb)

    @pl.when((blk + 1) % ntb == ntb - 1)
    def _():
        sret_ref[0] = sr_s[...]
        sgla_ref[0] = sg_s[...]


def _chunk_tri(tb, cg):
    i = jnp.arange(tb)
    return ((i[:, None] >= i[None, :]) & (i[:, None] // cg == i[None, :] // cg)).astype(BF16)


def _front_prompt(x_p, ada_p, g_mix, w_main, w_glr, tb, w_gk, b_gk, g_gla):
    b, t, d = x_p.shape
    dk, dv, hq, h = d // 8, d // 4, d // 2, N_HEADS
    ntb = t // tb
    n_blk = b * ntb
    assert ntb % 2 == 0
    cos_f, sin_f = _rope_tables(0, t, dk)
    dmask, qdec, kdec, cdec = _ret_tables(tb, dk, dv)
    tri = _chunk_tri(tb, GLA_CHUNK)

    def x_spec(blk_of):
        return pl.BlockSpec((1, tb, d), lambda p: (blk_of(p) // ntb, blk_of(p) % ntb, 0))

    def ada_spec(which, blk_of):
        return pl.BlockSpec((1, 1, 1, d), lambda p: (which, blk_of(p) // ntb, 0, 0))

    def rope_spec(blk_of):
        return pl.BlockSpec((tb, dk), lambda p: (blk_of(p) % ntb, 0))

    def const(shape):
        zeros = (0,) * len(shape)
        return pl.BlockSpec(shape, lambda p: zeros)

    def first(p):
        return 0 * p

    def even(p):
        return 2 * p

    def odd(p):
        return 2 * p + 1

    def nxt(p):
        return jnp.minimum(2 * p + 2, n_blk - 1)

    state_spec = pl.BlockSpec((1, h, dk, dv), lambda p: ((2 * p) // ntb, 0, 0, 0))
    tok_spec = pl.BlockSpec((2 * tb, d), lambda p: (p, 0))
    n_tok = b * t
    return pl.pallas_call(
        functools.partial(_frontp_kernel, d, tb, ntb),
        grid=(n_blk // 2,),
        in_specs=[x_spec(first), x_spec(odd), x_spec(nxt),
                  ada_spec(0, first), ada_spec(1, first), ada_spec(0, odd), ada_spec(1, odd),
                  ada_spec(0, nxt), ada_spec(1, nxt),
                  _resident((1, 1, d)), _resident((d, N_SEG * d)), _resident((d, GLA_GATE_RANK)),
                  rope_spec(even), rope_spec(even), rope_spec(odd), rope_spec(odd),
                  const((h, tb, tb)), const((h, tb, dk)), const((h, tb, dk)), const((h, 1, dv)),
                  const((tb, tb)), const((GLA_GATE_RANK, hq)), const((1, hq)), const((1, dv))],
        out_specs=[tok_spec, tok_spec, pl.BlockSpec((2, 2 * tb, d), lambda p: (0, p, 0)), state_spec, state_spec],
        out_shape=[jax.ShapeDtypeStruct((n_tok, d), BF16), jax.ShapeDtypeStruct((n_tok, d), BF16),
                   jax.ShapeDtypeStruct((2, n_tok, d), BF16),
                   jax.ShapeDtypeStruct((b, h, dk, dv), F32), jax.ShapeDtypeStruct((b, h, dk, dv), F32)],
        scratch_shapes=[pltpu.VMEM((tb, N_SEG * d), BF16), pltpu.VMEM((tb, N_SEG * d), BF16),
                        pltpu.VMEM((tb, GLA_GATE_RANK), F32), pltpu.VMEM((tb, GLA_GATE_RANK), F32),
                        pltpu.VMEM((h, dk, dv), F32), pltpu.VMEM((h, dk, dv), F32)],
        compiler_params=pltpu.CompilerParams(dimension_semantics=("arbitrary",), vmem_limit_bytes=VMEM_LIMIT),
        name="front_prompt",
    )(x_p, x_p, x_p, ada_p, ada_p, ada_p, ada_p, ada_p, ada_p, g_mix.reshape(1, 1, d), w_main, w_glr,
      cos_f, sin_f, cos_f, sin_f, dmask, qdec, kdec, cdec, tri, w_gk, b_gk, g_gla)


def _mixs_kernel(d, ts, gsz, rqk_ref, rv_ref, rg_ref, gqk_ref, gv_ref, gg_ref, glr_ref, cos_ref, sin_ref,
                 dmask_ref, qdec_ref, kdec_ref, cdec_ref, wgk_ref, bgk_ref, gn_ref, sr_in, sg_in,
                 oret_ref, ogla_ref, sr_out, sg_out):
    dk, dv, hq = d // 8, d // 4, d // 2
    pair = 2 * ts
    cos_f, sin_f = cos_ref[...], sin_ref[...]
    gnorm = gn_ref[...]
    ri = lax.broadcasted_iota(jnp.int32, (pair, pair), 0)
    ci = lax.broadcasted_iota(jnp.int32, (pair, pair), 1)
    causal = jnp.logical_and(ri >= ci, (ri < ts) == (ci < ts))
    tri = causal.astype(F32).astype(BF16)
    first = lax.broadcasted_iota(jnp.int32, (pair, 1), 0) < ts
    masks = [first, jnp.logical_not(first)]

    def body(j, carry):
        rows = pl.ds(pl.multiple_of(j * pair, pair), pair)
        s0, s1 = 2 * j, 2 * j + 1
        la_hi, la_lo = _split_hi_lo(_log_a(glr_ref[rows, :], wgk_ref[...], bgk_ref[...]))
        b = _mm(tri, la_hi) + _mm(tri, la_lo)
        for h in range(N_HEADS):
            o, (n0, n1) = _ret_head(rqk_ref[0, rows, h * dk:(h + 1) * dk].astype(F32),
                                    rqk_ref[0, rows, hq + h * dk:hq + (h + 1) * dk].astype(F32),
                                    rv_ref[0, rows, h * dv:(h + 1) * dv],
                                    rg_ref[0, rows, h * dv:(h + 1) * dv].astype(F32),
                                    [sr_in[s0, h], sr_in[s1, h]], masks, cos_f, sin_f,
                                    dmask_ref[h], qdec_ref[h], kdec_ref[h], cdec_ref[h])
            sr_out[s0, h] = n0
            sr_out[s1, h] = n1
            oret_ref[rows, h * dv:(h + 1) * dv] = o.astype(BF16)
            o, (n0, n1) = _gla_head(gqk_ref[0, rows, h * dk:(h + 1) * dk].astype(F32),
                                    gqk_ref[0, rows, hq + h * dk:hq + (h + 1) * dk].astype(F32),
                                    gv_ref[0, rows, h * dv:(h + 1) * dv],
                                    gg_ref[0, rows, h * dv:(h + 1) * dv].astype(F32),
                                    b[:, h * dk:(h + 1) * dk], [sg_in[s0, h], sg_in[s1, h]], masks, ts,
                                    gnorm, causal)
            sg_out[s0, h] = n0
            sg_out[s1, h] = n1
            ogla_ref[rows, h * dv:(h + 1) * dv] = o.astype(BF16)
        return carry

    lax.fori_loop(0, gsz // 2, body, 0)


def _pair_tables(ts, dk, dv):
    cos_f, sin_f = _rope_tables(PAST_LEN, ts, dk)
    dmask, qdec, kdec, cdec = _ret_tables(ts, dk, dv)
    zero = jnp.zeros_like(dmask)
    dmask2 = jnp.concatenate([jnp.concatenate([dmask, zero], axis=2), jnp.concatenate([zero, dmask], axis=2)], axis=1)

    def twice(a, axis):
        return jnp.concatenate([a, a], axis=axis)

    return twice(cos_f, 0), twice(sin_f, 0), dmask2, twice(qdec, 1), twice(kdec, 1), cdec


def _mix_sample(bs, ts, d, n_prompt_tok, gsz, proj, glr, state_ret, state_gla, w_gk, b_gk, g_gla):
    dk, dv, hq, h = d // 8, d // 4, d // 2, N_HEADS
    assert GLA_CHUNK % ts == 0 and bs % gsz == 0 and gsz % 2 == 0 and n_prompt_tok % (gsz * ts) == 0
    rows = gsz * ts
    pair = 2 * ts
    row0 = n_prompt_tok // rows
    cos_f, sin_f, dmask, qdec, kdec, cdec = _pair_tables(ts, dk, dv)

    def seg(s):
        return pl.BlockSpec((1, rows, d), lambda i: (s, row0 + i, 0))

    def const(shape):
        zeros = (0,) * len(shape)
        return pl.BlockSpec(shape, lambda i: zeros)

    state_spec = pl.BlockSpec((gsz, h, dk, dv), lambda i: (i, 0, 0, 0))
    tok_spec = pl.BlockSpec((rows, d), lambda i: (i, 0))
    return pl.pallas_call(
        functools.partial(_mixs_kernel, d, ts, gsz),
        grid=(bs // gsz,),
        in_specs=[seg(0), seg(1), seg(2), seg(3), seg(4), seg(5),
                  pl.BlockSpec((rows, GLA_GATE_RANK), lambda i: (row0 + i, 0)),
                  const((pair, dk)), const((pair, dk)),
                  const((h, pair, pair)), const((h, pair, dk)), const((h, pair, dk)), const((h, 1, dv)),
                  const((GLA_GATE_RANK, hq)), const((1, hq)), const((1, dv)),
                  state_spec, state_spec],
        out_specs=[tok_spec, tok_spec, state_spec, state_spec],
        out_shape=[jax.ShapeDtypeStruct((bs * ts, d), BF16), jax.ShapeDtypeStruct((bs * ts, d), BF16),
                   jax.ShapeDtypeStruct((bs, h, dk, dv), F32), jax.ShapeDtypeStruct((bs, h, dk, dv), F32)],
        compiler_params=pltpu.CompilerParams(dimension_semantics=("arbitrary",), vmem_limit_bytes=VMEM_LIMIT),
        name="mix_sample",
    )(proj, proj, proj, proj, proj, proj, glr, cos_f, sin_f, dmask, qdec, kdec, cdec, w_gk, b_gk, g_gla,
      state_ret, state_gla)


def _outproj_kernel(n_pt, d, tm, orp_ref, ogp_ref, ors_ref, ogs_ref, mgrp_ref, mggp_ref, mgrs_ref, mggs_ref,
                    xp_ref, xs_ref,
                    gtp_ref, shp_ref, scp_ref, gts_ref, shs_ref, scs_ref, g_ref,
                    wro_ref, wgo_ref, wo_ref, wrh_ref, wrl_ref, br_ref,
                    h_ref, n2_ref, idx_ref, rank_ref, prob_ref, cnt_ref, carry_s):
    i = pl.program_id(0)
    e = N_EXPERTS

    @pl.when(i == 0)
    def _():
        carry_s[...] = jnp.zeros_like(carry_s)

    def body(out_ret, out_gla, mg_ret, mg_gla, x3, gt, sh, sc):
        a = _mm(out_ret, wro_ref[...])
        b = _mm(out_gla, wgo_ref[...])
        merged = jax.nn.sigmoid(mg_ret.astype(F32)) * a + jax.nn.sigmoid(mg_gla.astype(F32)) * b
        mix = _mm(merged.astype(BF16), wo_ref[...])
        h3 = x3 + gt * mix.reshape(x3.shape)
        h_ref[...] = h3.reshape(-1, d)
        n2 = _rms_mod(h3, g_ref[...], sc, sh).reshape(-1, d)
        n2_ref[...] = _pack_pair(n2)

        n_hi, n_lo = _split_hi_lo(n2)
        logits = _mm(n_hi, wrh_ref[...]) + _mm(n_lo, wrh_ref[...]) + _mm(n_hi, wrl_ref[...]) + br_ref[...]
        iota = lax.broadcasted_iota(jnp.int32, (tm, e), 1)
        work = logits
        vals, idxs = [], []
        for _ in range(TOP_K):
            m = jnp.max(work, axis=-1, keepdims=True)
            ik = jnp.min(jnp.where(work == m, iota, e), axis=-1, keepdims=True)
            vals.append(m)
            idxs.append(ik)
            work = jnp.where(iota == ik, -jnp.inf, work)
        ex = [jnp.exp(v - vals[0]) for v in vals]
        den = ex[0] + ex[1] + ex[2] + ex[3]

        onehot = jnp.zeros((tm, e), F32)
        for ik in idxs:
            onehot = onehot + (iota == ik).astype(F32)
        ltri = (lax.broadcasted_iota(jnp.int32, (tm, tm), 0) > lax.broadcasted_iota(jnp.int32, (tm, tm), 1))
        cum = _mm(ltri.astype(F32).astype(BF16), onehot.astype(BF16)) + carry_s[...]
        lane = lax.broadcasted_iota(jnp.int32, (tm, TOP_K), 1)
        idx_o = jnp.zeros((tm, TOP_K), jnp.int32)
        rank_o = jnp.zeros((tm, TOP_K), jnp.int32)
        prob_o = jnp.zeros((tm, TOP_K), F32)
        for k in range(TOP_K):
            rk = jnp.sum(jnp.where(iota == idxs[k], cum, 0.0), axis=-1, keepdims=True).astype(jnp.int32)
            idx_o = jnp.where(lane == k, idxs[k], idx_o)
            rank_o = jnp.where(lane == k, rk, rank_o)
            prob_o = jnp.where(lane == k, ex[k] / den, prob_o)
        idx_ref[...] = idx_o
        rank_ref[...] = rank_o
        prob_ref[...] = prob_o
        carry_s[...] = carry_s[...] + jnp.sum(onehot, axis=0, keepdims=True)

    @pl.when(i < n_pt)
    def _():
        body(orp_ref[...], ogp_ref[...], mgrp_ref[0], mggp_ref[0], xp_ref[...], gtp_ref[0], shp_ref[0], scp_ref[0])

    @pl.when(i >= n_pt)
    def _():
        body(ors_ref[...], ogs_ref[...], mgrs_ref[0], mggs_ref[0], xs_ref[...], gts_ref[0], shs_ref[0], scs_ref[0])

    @pl.when(i == pl.num_programs(0) - 1)
    def _():
        cnt_ref[...] = carry_s[...].astype(jnp.int32)


def _outproj(tl, oret_p, ogla_p, oret_s, ogla_s, mg_p, proj_s, x_p, x_s, ada_p, ada_s, g_ffn,
             w_ret_o, w_gla_o, w_out, w_r_hi, w_r_lo, b_router):
    d = x_p.shape[-1]
    tm, e, n_pt = tl.tm, N_EXPERTS, tl.n_pt
    last = n_pt - 1
    p_spec = pl.BlockSpec((tm, d), lambda i: (jnp.minimum(i, last), 0))
    s_spec = pl.BlockSpec((tm, d), lambda i: (jnp.maximum(i - n_pt, 0), 0))

    def mgp_spec(seg):
        return pl.BlockSpec((1, tm, d), lambda i: (seg, jnp.minimum(i, last), 0))

    def mgs_spec(seg):
        return pl.BlockSpec((1, tm, d), lambda i: (seg, jnp.maximum(i - n_pt, 0), 0))

    return pl.pallas_call(
        functools.partial(_outproj_kernel, n_pt, d, tm),
        grid=(tl.n,),
        in_specs=[p_spec, p_spec, s_spec, s_spec, mgp_spec(0), mgp_spec(1), mgs_spec(6), mgs_spec(7),
                  tl.xp_spec(d), tl.xs_spec(d),
                  tl.adap_spec(2, d), tl.adap_spec(3, d), tl.adap_spec(4, d),
                  tl.adas_spec(2, d), tl.adas_spec(3, d), tl.adas_spec(4, d),
                  _resident((1, 1, d)), _resident((d, d)), _resident((d, d)), _resident((d, d)),
                  _resident((d, e)), _resident((d, e)), _resident((1, e))],
        out_specs=[tl.tok_spec(d), tl.tok_spec(d // 2), tl.tok_spec(TOP_K), tl.tok_spec(TOP_K), tl.tok_spec(TOP_K),
                   pl.BlockSpec((1, e), lambda i: (0, 0))],
        out_shape=[jax.ShapeDtypeStruct((tl.n_tok, d), F32), jax.ShapeDtypeStruct((tl.n_tok, d // 2), jnp.uint32),
                   jax.ShapeDtypeStruct((tl.n_tok, TOP_K), jnp.int32),
                   jax.ShapeDtypeStruct((tl.n_tok, TOP_K), jnp.int32),
                   jax.ShapeDtypeStruct((tl.n_tok, TOP_K), F32),
                   jax.ShapeDtypeStruct((1, e), jnp.int32)],
        scratch_shapes=[pltpu.VMEM((1, e), F32)],
        compiler_params=pltpu.CompilerParams(dimension_semantics=("arbitrary",), vmem_limit_bytes=VMEM_LIMIT),
        name="outproj",
    )(oret_p, ogla_p, oret_s, ogla_s, mg_p, mg_p, proj_s, proj_s, x_p, x_s, ada_p, ada_p, ada_p, ada_s, ada_s, ada_s,
      g_ffn.reshape(1, 1, d), w_ret_o, w_gla_o, w_out, w_r_hi, w_r_lo, b_router.reshape(1, e))


EXPERT_SHORT_DIV = 2


def _expert_kernel(f, te_ref, na_ref, tr_ref, x_ref, wu_ref, bu_ref, wd_ref, bd_ref, y_ref, wu_s, wd_s):
    j = pl.program_id(0)
    active = j < na_ref[0]
    first = jnp.logical_or(j == 0, te_ref[j] != te_ref[jnp.maximum(j - 1, 0)])

    @pl.when(jnp.logical_and(active, first))
    def _():
        wu_s[...] = wu_ref[0].astype(BF16)
        wd_s[...] = wd_ref[0].astype(BF16)

    tme = x_ref.shape[0]
    half = x_ref.shape[1]

    def compute(n_rows):
        x_lo, x_hi = _unpack_pair(x_ref[:n_rows, :])
        gu = _mm(x_lo.astype(BF16), wu_s[:half, :]) + _mm(x_hi.astype(BF16), wu_s[half:, :]) + bu_ref[0]
        gate = jnp.minimum(gu[:, :f], SWIGLU_LIMIT)
        up = jnp.clip(gu[:, f:], -SWIGLU_LIMIT, SWIGLU_LIMIT)
        act = (up + 1.0) * gate * jax.nn.sigmoid(SWIGLU_ALPHA * gate)
        y_ref[:n_rows, :] = _pack_pair(_mm(act.astype(BF16), wd_s[...]) + bd_ref[0])

    short = tme // EXPERT_SHORT_DIV
    is_short = tr_ref[j] <= short

    @pl.when(jnp.logical_and(active, jnp.logical_not(is_short)))
    def _():
        compute(tme)

    @pl.when(jnp.logical_and(active, is_short))
    def _():
        compute(short)


def _experts(xs, tile_expert, n_active, tile_rows, w_up, b_up, w_down, b_down, tme):
    r = xs.shape[0]
    e, d, f2 = w_up.shape
    f = f2 // 2
    n_tiles = r // tme

    def row_map(j, te, na, tr):
        return (jnp.minimum(j, na[0] - 1), 0)

    def w_map(j, te, na, tr):
        return (te[jnp.minimum(j, na[0] - 1)], 0, 0)

    return pl.pallas_call(
        functools.partial(_expert_kernel, f),
        grid_spec=pltpu.PrefetchScalarGridSpec(
            num_scalar_prefetch=3,
            grid=(n_tiles,),
            in_specs=[pl.BlockSpec((tme, d // 2), row_map),
                      pl.BlockSpec((1, d, f2), w_map), pl.BlockSpec((1, 1, f2), w_map),
                      pl.BlockSpec((1, f, d), w_map), pl.BlockSpec((1, 1, d), w_map)],
            out_specs=pl.BlockSpec((tme, d // 2), row_map),
            scratch_shapes=[pltpu.VMEM((d, f2), BF16), pltpu.VMEM((f, d), BF16)]),
        out_shape=jax.ShapeDtypeStruct((r, d // 2), jnp.uint32),
        compiler_params=pltpu.CompilerParams(dimension_semantics=("arbitrary",), vmem_limit_bytes=VMEM_LIMIT),
        name="experts",
    )(tile_expert, n_active, tile_rows, xs, w_up, b_up.reshape(e, 1, f2), w_down, b_down.reshape(e, 1, d))


def _sc_mesh():
    return plsc.VectorSubcoreMesh(core_axis_name="core", subcore_axis_name="subcore")


def _sc_split(n_rows, max_chunk):
    info = plsc.get_sparse_core_info()
    n_workers = info.num_cores * info.num_subcores
    assert n_rows % (8 * n_workers) == 0
    per_w = n_rows // n_workers
    chunk = 8
    while chunk * 2 <= max_chunk and per_w % (chunk * 2) == 0:
        chunk *= 2
    return info.num_cores, n_workers, per_w, chunk


def _sc_dispatch(x, pos, n_rows):
    n, w = x.shape
    nc, nw, per_w, chunk = _sc_split(n, 32)
    n_ch = per_w // chunk
    idx = pos.T.reshape(TOP_K, nw, n_ch, chunk).transpose(1, 0, 2, 3).reshape(nw, TOP_K * n_ch, chunk)

    @functools.partial(
        pl.kernel, out_type=jax.ShapeDtypeStruct((n_rows, w), x.dtype), mesh=_sc_mesh(),
        scratch_types=[pltpu.VMEM((TOP_K * n_ch, chunk), jnp.int32), pltpu.VMEM((2, chunk, w), x.dtype),
                       pltpu.SemaphoreType.DMA((2,)), pltpu.SemaphoreType.DMA((2,))])
    def scatter_rows(x_hbm, i_hbm, o_hbm, idx_v, rows_v, rsem, wsem):
        wid = lax.axis_index("subcore") * nc + lax.axis_index("core")
        base = wid * per_w
        pltpu.sync_copy(i_hbm.at[wid], idx_v)

        def read(j, slot):
            return pltpu.make_async_copy(x_hbm.at[pl.ds(base + j * chunk, chunk)], rows_v.at[slot], rsem.at[slot])

        def write(j, slot, k):
            return pltpu.make_async_copy(rows_v.at[slot], o_hbm.at[idx_v.at[k * n_ch + j]], wsem.at[slot])

        read(0, 0).start()

        @pl.loop(0, n_ch, step=2)
        def _(j0):
            for b in range(2):
                j = j0 + b

                @pl.when(j < n_ch)
                def _():
                    read(j, b).wait()

                    @pl.when(j + 1 < n_ch)
                    def _():
                        @pl.when(j >= 1)
                        def _():
                            for k in range(TOP_K):
                                write(j - 1, 1 - b, k).wait()

                        read(j + 1, 1 - b).start()

                    for k in range(TOP_K):
                        write(j, b, k).start()

        for jj in range(max(n_ch - 2, 0), n_ch):
            for k in range(TOP_K):
                write(jj, jj % 2, k).wait()

    return scatter_rows(x, idx)


def _sc_gather(table, idx):
    m = idx.shape[0]
    w = table.shape[1]
    nc, _, per_w, chunk = _sc_split(m, 64)
    n_ch = per_w // chunk

    @functools.partial(
        pl.kernel, out_type=jax.ShapeDtypeStruct((m, w), table.dtype), mesh=_sc_mesh(),
        scratch_types=[pltpu.VMEM((per_w,), jnp.int32), pltpu.VMEM((2, chunk, w), table.dtype),
                       pltpu.SemaphoreType.DMA((2,)), pltpu.SemaphoreType.DMA((2,))])
    def gather_rows(t_hbm, i_hbm, o_hbm, idx_v, rows_v, gsem, wsem):
        wid = lax.axis_index("subcore") * nc + lax.axis_index("core")
        base = wid * per_w
        pltpu.sync_copy(i_hbm.at[pl.ds(base, per_w)], idx_v)

        def gather(j, slot):
            off = pl.multiple_of(j * chunk, chunk)
            return pltpu.make_async_copy(t_hbm.at[idx_v.at[pl.ds(off, chunk)]], rows_v.at[slot], gsem.at[slot])

        def write(j, slot):
            off = pl.multiple_of(j * chunk, chunk)
            return pltpu.make_async_copy(rows_v.at[slot], o_hbm.at[pl.ds(base + off, chunk)], wsem.at[slot])

        gather(0, 0).start()

        @pl.loop(0, n_ch, step=2)
        def _(j0):
            for b in range(2):
                j = j0 + b

                @pl.when(j < n_ch)
                def _():
                    gather(j, b).wait()

                    @pl.when(j + 1 < n_ch)
                    def _():
                        @pl.when(j >= 1)
                        def _():
                            write(j - 1, 1 - b).wait()

                        gather(j + 1, 1 - b).start()

                    write(j, b).start()

        for jj in range(max(n_ch - 2, 0), n_ch):
            write(jj, jj % 2).wait()

    return gather_rows(table, idx)


def _final_kernel(n_pt, d, h_ref, yg_ref, prob_ref, gtp_ref, gts_ref, g_ref, yp_ref, ys_ref):
    i = pl.program_id(0)
    p = prob_ref[...]
    moe_lo, moe_hi = None, None
    for k in range(TOP_K):
        lo, hi = _unpack_pair(yg_ref[k])
        pk = p[:, k:k + 1]
        moe_lo = pk * lo if moe_lo is None else moe_lo + pk * lo
        moe_hi = pk * hi if moe_hi is None else moe_hi + pk * hi
    moe = jnp.concatenate([moe_lo, moe_hi], axis=1)

    def body(gt, shape):
        h3 = h_ref[...].reshape(shape) + gt * moe.reshape(shape)
        ms = jnp.mean(h3 * h3, axis=-1, keepdims=True)
        return h3 * lax.rsqrt(ms + EPS) * g_ref[...]

    @pl.when(i < n_pt)
    def _():
        yp_ref[...] = body(gtp_ref[0], yp_ref.shape)

    @pl.when(i >= n_pt)
    def _():
        ys_ref[...] = body(gts_ref[0], ys_ref.shape)


def _final(tl, h, yg, probs, ada_p, ada_s, g_final, d):
    return pl.pallas_call(
        functools.partial(_final_kernel, tl.n_pt, d),
        grid=(tl.n,),
        in_specs=[tl.tok_spec(d), pl.BlockSpec((TOP_K, tl.tm, d // 2), lambda i: (0, i, 0)), tl.tok_spec(TOP_K),
                  tl.adap_spec(5, d), tl.adas_spec(5, d), _resident((1, 1, d))],
        out_specs=[tl.xp_spec(d), tl.xs_spec(d)],
        out_shape=[jax.ShapeDtypeStruct((tl.b, tl.t, d), F32), jax.ShapeDtypeStruct((tl.bs, tl.ts, d), F32)],
        compiler_params=pltpu.CompilerParams(dimension_semantics=("arbitrary",), vmem_limit_bytes=VMEM_LIMIT),
        name="final",
    )(h, yg, probs, ada_p, ada_s, g_final.reshape(1, 1, d))


def _pick(n, pref):
    t = min(n, pref)
    while n % t:
        t //= 2
    return t


def _forward(x_prompt, x_sample, c_prompt, c_sample, state_ret, state_gla, w_ada, b_ada, g_norm_mix, g_norm_ffn,
             w_in, w_gk_up, b_gk, g_gla_norm, w_ret_o, w_gla_o, w_out, w_router, b_router, w_up, b_up,
             w_down, b_down, g_final, *, tm, tb, gsz, tme):
    b, t, d = x_prompt.shape
    bs, ts, _ = x_sample.shape
    assert w_ada.shape[0] == 1, "single layer only"
    e = N_EXPERTS
    tl = _Tiles(b, t, bs, ts, tm)
    n_tok = tl.n_tok

    ada = _ada(jnp.concatenate([c_prompt, c_sample], axis=0), w_ada[0], b_ada[0])
    ada_p = ada[:, :b].reshape(6, b, 1, d)
    ada_s = ada[:, b:].reshape(6, bs, 1, d)

    w_in0 = w_in[0]
    n_main = 6 * d
    w_main = jnp.concatenate([w_in0[:, :n_main], w_in0[:, n_main + GLA_GATE_RANK:]], axis=1).astype(BF16)
    w_glr = w_in0[:, n_main:n_main + GLA_GATE_RANK].astype(BF16)
    proj_s, glr_s = _inproj_sample(tl, x_sample, ada_s, g_norm_mix[0], w_main, w_glr)

    w_gk = w_gk_up[0].astype(BF16)
    bgk = b_gk[0].reshape(1, -1)
    ggn = g_gla_norm[0].reshape(1, -1)
    oret_p, ogla_p, mg_p, sret_p, sgla_p = _front_prompt(x_prompt, ada_p, g_norm_mix[0], w_main, w_glr, tb,
                                                         w_gk, bgk, ggn)
    oret_s, ogla_s, sret_s, sgla_s = _mix_sample(bs, ts, d, 0, gsz, proj_s, glr_s, state_ret[0], state_gla[0],
                                                 w_gk, bgk, ggn)

    w_r = w_router[0]
    w_r_hi = w_r.astype(BF16)
    w_r_lo = (w_r - w_r_hi.astype(F32)).astype(BF16)
    h, n2, idx, rank, probs, counts = _outproj(
        tl, oret_p, ogla_p, oret_s, ogla_s, mg_p, proj_s, x_prompt, x_sample, ada_p, ada_s, g_norm_ffn[0],
        w_ret_o[0].astype(BF16), w_gla_o[0].astype(BF16), w_out[0].astype(BF16), w_r_hi, w_r_lo, b_router[0])

    counts = counts[0]
    gsize = ((counts + tme - 1) // tme) * tme
    ends = jnp.cumsum(gsize)
    offs = ends - gsize
    experts = jnp.arange(e, dtype=jnp.int32)
    pos = jnp.sum(jnp.where(idx[..., None] == experts, offs, 0), axis=-1) + rank
    max_tiles = (n_tok * TOP_K) // tme + e
    n_active = (ends[-1] // tme).astype(jnp.int32).reshape(1)
    tile_start = jnp.arange(max_tiles, dtype=jnp.int32) * tme
    tile_expert = jnp.minimum(jnp.sum((ends[None, :] <= tile_start[:, None]).astype(jnp.int32), axis=1), e - 1)
    last_row = jnp.sum(jnp.where(tile_expert[:, None] == experts, offs + counts, 0), axis=1)
    tile_rows = jnp.clip(last_row - tile_start, 0, tme).astype(jnp.int32)

    xs = _sc_dispatch(n2, pos, max_tiles * tme)
    ys = _experts(xs, tile_expert, n_active, tile_rows, w_up[0], b_up[0], w_down[0], b_down[0], tme)
    yg = _sc_gather(ys, pos.T.reshape(-1)).reshape(TOP_K, n_tok, d // 2)

    y_p, y_s = _final(tl, h, yg, probs, ada_p, ada_s, g_final, d)
    return (y_p, y_s, sret_p[None], sgla_p[None], sret_s[None], sgla_s[None])


def kernel(x_prompt, x_sample, c_prompt, c_sample, state_ret, state_gla, w_ada, b_ada, g_norm_mix, g_norm_ffn,
           w_in, w_gk_up, b_gk, g_gla_norm, w_ret_o, w_gla_o, w_out, w_router, b_router, w_up, b_up,
           w_down, b_down, g_final):
    t = x_prompt.shape[1]
    bs, ts = x_sample.shape[0], x_sample.shape[1]
    return _forward(x_prompt, x_sample, c_prompt, c_sample, state_ret, state_gla, w_ada, b_ada, g_norm_mix,
                    g_norm_ffn, w_in, w_gk_up, b_gk, g_gla_norm, w_ret_o, w_gla_o, w_out, w_router, b_router,
                    w_up, b_up, w_down, b_down, g_final,
                    tm=_pick(bs * ts, 512), tb=_pick(t, 256), gsz=_pick(bs, 8), tme=512)
```

```python
import functools

import jax
import jax.numpy as jnp
from jax import lax
from jax.experimental import pallas as pl
from jax.experimental.pallas import tpu as pltpu
from jax.experimental.pallas import tpu_sc as plsc

F32 = jnp.float32
BF16 = jnp.bfloat16

N_HEADS = 4
GLA_GATE_RANK = 16
GLA_GATE_NORM = 16.0
GLA_CHUNK = 64
ROPE_BASE = 10000.0
N_EXPERTS = 32
TOP_K = 4
SWIGLU_LIMIT = 7.0
SWIGLU_ALPHA = 1.702
EPS = 1e-6
PAST_LEN = 16384
N_SEG = 8

VMEM_LIMIT = 56 * 1024 * 1024


def _mm(a, b):
    return jnp.dot(a, b, preferred_element_type=F32)


def _mm_nt(a, b):
    return lax.dot_general(a, b, (((1,), (1,)), ((), ())), preferred_element_type=F32)


def _silu(x):
    return x * jax.nn.sigmoid(x)


def _split_hi_lo(x):
    hi = x.astype(BF16)
    lo = (x - hi.astype(F32)).astype(BF16)
    return hi, lo


def _pack_pair(x):
    w = x.shape[1] // 2
    lo = lax.bitcast_convert_type(x[:, :w].astype(BF16).astype(F32), jnp.uint32)
    hi = lax.bitcast_convert_type(x[:, w:].astype(BF16).astype(F32), jnp.uint32)
    return (hi & jnp.uint32(0xFFFF0000)) | (lo >> 16)


def _unpack_pair(p):
    lo = lax.bitcast_convert_type(p << 16, F32)
    hi = lax.bitcast_convert_type(p & jnp.uint32(0xFFFF0000), F32)
    return lo, hi


def _rms_mod(x3, g, sc, sh):
    ms = jnp.mean(x3 * x3, axis=-1, keepdims=True)
    return x3 * lax.rsqrt(ms + EPS) * g * (1.0 + sc) + sh


def _ada_kernel(c_ref, w_ref, b_ref, o_ref):
    cf = _silu(c_ref[...])
    o_ref[0] = _mm(cf.astype(BF16), w_ref[...].astype(BF16)) + b_ref[0]


def _ada(c_all, w_ada, b_ada):
    bc, d = c_all.shape
    n = w_ada.shape[1] // d
    return pl.pallas_call(
        _ada_kernel,
        grid=(n,),
        in_specs=[pl.BlockSpec((bc, d), lambda j: (0, 0)),
                  pl.BlockSpec((d, d), lambda j: (0, j)),
                  pl.BlockSpec((1, 1, d), lambda j: (j, 0, 0))],
        out_specs=pl.BlockSpec((1, bc, d), lambda j: (j, 0, 0)),
        out_shape=jax.ShapeDtypeStruct((n, bc, d), F32),
        compiler_params=pltpu.CompilerParams(dimension_semantics=("arbitrary",), vmem_limit_bytes=VMEM_LIMIT),
        name="ada",
    )(c_all, w_ada, b_ada.reshape(n, 1, d))


class _Tiles:
    def __init__(self, b, t, bs, ts, tm):
        assert t % tm == 0 and (bs * ts) % tm == 0 and tm % ts == 0
        self.b, self.t, self.bs, self.ts, self.tm = b, t, bs, ts, tm
        self.tpb = t // tm
        self.n_pt = b * self.tpb
        self.gs = tm // ts
        self.n_st = (bs * ts) // tm
        self.n = self.n_pt + self.n_st
        self.n_tok = b * t + bs * ts


    def xp_spec(self, d, off=0):
        last, tpb = self.n_pt - 1, self.tpb
        return pl.BlockSpec((1, self.tm, d),
                            lambda i: (jnp.minimum(i + off, last) // tpb, jnp.minimum(i + off, last) % tpb, 0))

    def xs_spec(self, d, off=0):
        n_pt = self.n_pt
        return pl.BlockSpec((self.gs, self.ts, d), lambda i: (jnp.maximum(i + off - n_pt, 0), 0, 0))

    def adap_spec(self, which, d, off=0):
        last, tpb = self.n_pt - 1, self.tpb
        return pl.BlockSpec((1, 1, 1, d), lambda i: (which, jnp.minimum(i + off, last) // tpb, 0, 0))

    def adas_spec(self, which, d, off=0):
        n_pt = self.n_pt
        return pl.BlockSpec((1, self.gs, 1, d), lambda i: (which, jnp.maximum(i + off - n_pt, 0), 0, 0))

    def tok_spec(self, width, off=0):
        return pl.BlockSpec((self.tm, width), lambda i: (i + off, 0))

    def seg_spec(self, seg, d):
        return pl.BlockSpec((1, self.tm, d), lambda i: (seg, i, 0))


def _resident(shape):
    zeros = (0,) * len(shape)
    return pl.BlockSpec(shape, lambda i: zeros, pipeline_mode=pl.Buffered(1))


def _inproj_kernel(d, xs_ref, shs_ref, scs_ref, g_ref, w_ref, wl_ref, proj_ref, glr_ref):
    n = _rms_mod(xs_ref[...], g_ref[...], scs_ref[0], shs_ref[0]).reshape(-1, d).astype(BF16)
    for s in range(N_SEG):
        proj_ref[s] = _mm(n, w_ref[:, s * d:(s + 1) * d]).astype(BF16)
    glr_ref[...] = _mm(n, wl_ref[...])


def _inproj_sample(tl, x_s, ada_s, g_mix, w_main, w_glr):
    bs, ts, d = x_s.shape
    n_tok = bs * ts

    def ada_spec(which):
        return pl.BlockSpec((1, tl.gs, 1, d), lambda i: (which, i, 0, 0))

    return pl.pallas_call(
        functools.partial(_inproj_kernel, d),
        grid=(tl.n_st,),
        in_specs=[pl.BlockSpec((tl.gs, ts, d), lambda i: (i, 0, 0)), ada_spec(0), ada_spec(1),
                  _resident((1, 1, d)), _resident((d, N_SEG * d)), _resident((d, GLA_GATE_RANK))],
        out_specs=[pl.BlockSpec((N_SEG, tl.tm, d), lambda i: (0, i, 0)),
                   pl.BlockSpec((tl.tm, GLA_GATE_RANK), lambda i: (i, 0))],
        out_shape=[jax.ShapeDtypeStruct((N_SEG, n_tok, d), BF16), jax.ShapeDtypeStruct((n_tok, GLA_GATE_RANK), F32)],
        compiler_params=pltpu.CompilerParams(dimension_semantics=("arbitrary",), vmem_limit_bytes=VMEM_LIMIT),
        name="inproj_sample",
    )(x_s, ada_s, ada_s, g_mix.reshape(1, 1, d), w_main, w_glr)


def _rope_tables(pos0, t, dk):
    half = dk // 2
    inv = ROPE_BASE ** (-jnp.arange(half, dtype=jnp.float32) / half)
    pos = pos0 + jnp.arange(t)
    ang = pos.astype(jnp.float32)[:, None] * inv[None, :]
    cos, sin = jnp.cos(ang), jnp.sin(ang)
    return jnp.concatenate([cos, cos], axis=-1), jnp.concatenate([-sin, sin], axis=-1)


def _ret_tables(c, dk, dv):
    h = N_HEADS
    log_gamma = jnp.log1p(-jnp.exp2(-5.0 - jnp.arange(h, dtype=jnp.float32)))
    idx = jnp.arange(c, dtype=jnp.float32)
    rel = idx[:, None] - idx[None, :]
    dmask = jnp.where(rel >= 0, jnp.exp(log_gamma[:, None, None] * jnp.maximum(rel, 0.0)), 0.0)
    kdec = jnp.exp(log_gamma[:, None] * (c - 1 - idx))
    qdec = jnp.exp(log_gamma[:, None] * (idx + 1.0))
    cdec = jnp.exp(log_gamma * c)
    return (dmask,
            jnp.broadcast_to(qdec[:, :, None], (h, c, dk)),
            jnp.broadcast_to(kdec[:, :, None], (h, c, dk)),
            jnp.broadcast_to(cdec[:, None, None], (h, 1, dv)))


def _rot(x, cos_f, sin_f):
    return x * cos_f + pltpu.roll(x, x.shape[-1] // 2, 1) * sin_f


def _cross_and_update(q_lhs, k_end, vh, states, masks):
    if masks is None:
        (s,) = states
        return _mm(q_lhs, s.astype(BF16)), [_mm(k_end.T.astype(BF16), vh)]
    cross, incs = None, []
    for s, m in zip(states, masks):
        c = _mm(q_lhs, s.astype(BF16))
        cross = c if cross is None else jnp.where(m, c, cross)
        incs.append(_mm(jnp.where(m, k_end, 0.0).T.astype(BF16), vh))
    return cross, incs


def _ret_head(q, k, vh, gh, states, masks, cos_f, sin_f, dmask, qdec, kdec, cdec):
    dk = q.shape[-1]
    q = _rot(q, cos_f, sin_f)
    k = _rot(k, cos_f, sin_f) * (dk ** -0.5)
    scores = _mm_nt(q.astype(BF16), k.astype(BF16)) * dmask
    cross, incs = _cross_and_update((q * qdec).astype(BF16), k * kdec, vh, states, masks)
    o = _mm(scores.astype(BF16), vh) + cross
    new_states = [cdec * s + u for s, u in zip(states, incs)]
    mu = jnp.mean(o, axis=-1, keepdims=True)
    oc = o - mu
    var = jnp.mean(oc * oc, axis=-1, keepdims=True)
    return _silu(gh) * (oc * lax.rsqrt(var + EPS)), new_states


def _gla_head(q, k, vh, gh, b, states, masks, c, gnorm, causal):
    dk = q.shape[-1]
    b_t = b.T
    if masks is None:
        b_last = b[c - 1:c, :]
    else:
        b_last = None
        for g, m in enumerate(masks):
            row = b[g * c + c - 1:g * c + c, :]
            b_last = row if b_last is None else jnp.where(m, row, b_last)
    q_in = (q * (dk ** -0.5) * jnp.exp(b)).astype(BF16)
    k_in = (k * jnp.exp(-b)).astype(BF16)
    scores = jnp.where(causal, _mm_nt(q_in, k_in), 0.0)
    cross, incs = _cross_and_update(q_in, k * jnp.exp(b_last - b), vh, states, masks)
    o = _mm(scores.astype(BF16), vh) + cross
    new_states = [jnp.exp(b_t[:, g * c + c - 1:g * c + c]) * s + u for g, (s, u) in enumerate(zip(states, incs))]
    o = o * lax.rsqrt(jnp.mean(o * o, axis=-1, keepdims=True) + EPS) * gnorm
    return _silu(gh) * o, new_states


def _log_a(glr, wgk, bgk):
    z = _mm(glr.astype(BF16), wgk) + bgk
    return (jnp.minimum(z, 0.0) - jnp.log1p(jnp.exp(-jnp.abs(z)))) / GLA_GATE_NORM


def _causal(c):
    return lax.broadcasted_iota(jnp.int32, (c, c), 0) >= lax.broadcasted_iota(jnp.int32, (c, c), 1)


def _proj_block(d, x3, sh, sc, g, w_ref, wl_ref, proj_s, glr_s):
    n = _rms_mod(x3, g, sc, sh).reshape(-1, d).astype(BF16)
    for seg in range(N_SEG):
        proj_s[:, seg * d:(seg + 1) * d] = _mm(n, w_ref[:, seg * d:(seg + 1) * d]).astype(BF16)
    glr_s[...] = _mm(n, wl_ref[...])


def _mix_block(d, tb, proj_s, glr_s, cos_f, sin_f, dmask_ref, qdec_ref, kdec_ref, cdec_ref, tri, wgk, bgk, gnorm,
               sr_s, sg_s, oret_ref, ogla_ref, mg_ref, r_off):
    dk, dv, hq = d // 8, d // 4, d // 2
    rqk, rv, rg, gqk, gv, gg, mg = (i * d for i in range(7))
    for h in range(N_HEADS):
        o, (s_new,) = _ret_head(proj_s[:, rqk + h * dk:rqk + (h + 1) * dk].astype(F32),
                                proj_s[:, rqk + hq + h * dk:rqk + hq + (h + 1) * dk].astype(F32),
                                proj_s[:, rv + h * dv:rv + (h + 1) * dv],
                                proj_s[:, rg + h * dv:rg + (h + 1) * dv].astype(F32),
                                [sr_s[h]], None, cos_f, sin_f, dmask_ref[h], qdec_ref[h], kdec_ref[h],
                                cdec_ref[h])
        sr_s[h] = s_new
        oret_ref[r_off:r_off + tb, h * dv:(h + 1) * dv] = o.astype(BF16)

    la_hi, la_lo = _split_hi_lo(_log_a(glr_s[...], wgk, bgk))
    b = _mm(tri, la_hi) + _mm(tri, la_lo)
    cg = GLA_CHUNK
    causal = _causal(cg)
    for c in range(tb // cg):
        r0, r1 = c * cg, (c + 1) * cg
        for h in range(N_HEADS):
            o, (s_new,) = _gla_head(proj_s[r0:r1, gqk + h * dk:gqk + (h + 1) * dk].astype(F32),
                                    proj_s[r0:r1, gqk + hq + h * dk:gqk + hq + (h + 1) * dk].astype(F32),
                                    proj_s[r0:r1, gv + h * dv:gv + (h + 1) * dv],
                                    proj_s[r0:r1, gg + h * dv:gg + (h + 1) * dv].astype(F32),
                                    b[r0:r1, h * dk:(h + 1) * dk], [sg_s[h]], None, cg, gnorm, causal)
            sg_s[h] = s_new
            ogla_ref[r_off + r0:r_off + r1, h * dv:(h + 1) * dv] = o.astype(BF16)
    mg_ref[0, r_off:r_off + tb, :] = proj_s[:, mg:mg + d]
    mg_ref[1, r_off:r_off + tb, :] = proj_s[:, mg + d:mg + 2 * d]


def _frontp_kernel(d, tb, ntb, x0_ref, xa_ref, xb_ref, sh0_ref, sc0_ref, sha_ref, sca_ref, shb_ref, scb_ref,
                   g_ref, w_ref, wl_ref, cosa_ref, sina_ref, cosb_ref, sinb_ref,
                   dmask_ref, qdec_ref, kdec_ref, cdec_ref, tri_ref, wgk_ref, bgk_ref, gn_ref,
                   oret_ref, ogla_ref, mg_ref, sret_ref, sgla_ref, pa_s, pb_s, ga_s, gb_s, sr_s, sg_s):
    p = pl.program_id(0)
    blk = 2 * p
    g = g_ref[...]
    proj = functools.partial(_proj_block, d)
    mix = functools.partial(_mix_block, d, tb)
    tables = (dmask_ref, qdec_ref, kdec_ref, cdec_ref, tri_ref[...], wgk_ref[...], bgk_ref[...], gn_ref[...])

    @pl.when(p == 0)
    def _():
        proj(x0_ref[...], sh0_ref[0], sc0_ref[0], g, w_ref, wl_ref, pa_s, ga_s)

    @pl.when(blk % ntb == 0)
    def _():
        sr_s[...] = jnp.zeros_like(sr_s)
        sg_s[...] = jnp.zeros_like(sg_s)

    proj(xa_ref[...], sha_ref[0], sca_ref[0], g, w_ref, wl_ref, pb_s, gb_s)
    mix(pa_s, ga_s, cosa_ref[...], sina_ref[...], *tables, sr_s, sg_s, oret_ref, ogla_ref, mg_ref, 0)
    proj(xb_ref[...], shb_ref[0], scb_ref[0], g, w_ref, wl_ref, pa_s, ga_s)
    mix(pb_s, gb_s, cosb_ref[...], sinb_ref[...], *tables, sr_s, sg_s, oret_ref, ogla_ref, mg_ref, tb)

    @pl.when((blk + 1) % ntb == ntb - 1)
    def _():
        sret_ref[0] = sr_s[...]
        sgla_ref[0] = sg_s[...]


def _chunk_tri(tb, cg):
    i = jnp.arange(tb)
    return ((i[:, None] >= i[None, :]) & (i[:, None] // cg == i[None, :] // cg)).astype(BF16)


def _front_prompt(x_p, ada_p, g_mix, w_main, w_glr, tb, w_gk, b_gk, g_gla):
    b, t, d = x_p.shape
    dk, dv, hq, h = d // 8, d // 4, d // 2, N_HEADS
    ntb = t // tb
    n_blk = b * ntb
    assert ntb % 2 == 0
    cos_f, sin_f = _rope_tables(0, t, dk)
    dmask, qdec, kdec, cdec = _ret_tables(tb, dk, dv)
    tri = _chunk_tri(tb, GLA_CHUNK)

    def x_spec(blk_of):
        return pl.BlockSpec((1, tb, d), lambda p: (blk_of(p) // ntb, blk_of(p) % ntb, 0))

    def ada_spec(which, blk_of):
        return pl.BlockSpec((1, 1, 1, d), lambda p: (which, blk_of(p) // ntb, 0, 0))

    def rope_spec(blk_of):
        return pl.BlockSpec((tb, dk), lambda p: (blk_of(p) % ntb, 0))

    def const(shape):
        zeros = (0,) * len(shape)
        return pl.BlockSpec(shape, lambda p: zeros)

    def first(p):
        return 0 * p

    def even(p):
        return 2 * p

    def odd(p):
        return 2 * p + 1

    def nxt(p):
        return jnp.minimum(2 * p + 2, n_blk - 1)

    state_spec = pl.BlockSpec((1, h, dk, dv), lambda p: ((2 * p) // ntb, 0, 0, 0))
    tok_spec = pl.BlockSpec((2 * tb, d), lambda p: (p, 0))
    n_tok = b * t
    return pl.pallas_call(
        functools.partial(_frontp_kernel, d, tb, ntb),
        grid=(n_blk // 2,),
        in_specs=[x_spec(first), x_spec(odd), x_spec(nxt),
                  ada_spec(0, first), ada_spec(1, first), ada_spec(0, odd), ada_spec(1, odd),
                  ada_spec(0, nxt), ada_spec(1, nxt),
                  _resident((1, 1, d)), _resident((d, N_SEG * d)), _resident((d, GLA_GATE_RANK)),
                  rope_spec(even), rope_spec(even), rope_spec(odd), rope_spec(odd),
                  const((h, tb, tb)), const((h, tb, dk)), const((h, tb, dk)), const((h, 1, dv)),
                  const((tb, tb)), const((GLA_GATE_RANK, hq)), const((1, hq)), const((1, dv))],
        out_specs=[tok_spec, tok_spec, pl.BlockSpec((2, 2 * tb, d), lambda p: (0, p, 0)), state_spec, state_spec],
        out_shape=[jax.ShapeDtypeStruct((n_tok, d), BF16), jax.ShapeDtypeStruct((n_tok, d), BF16),
                   jax.ShapeDtypeStruct((2, n_tok, d), BF16),
                   jax.ShapeDtypeStruct((b, h, dk, dv), F32), jax.ShapeDtypeStruct((b, h, dk, dv), F32)],
        scratch_shapes=[pltpu.VMEM((tb, N_SEG * d), BF16), pltpu.VMEM((tb, N_SEG * d), BF16),
                        pltpu.VMEM((tb, GLA_GATE_RANK), F32), pltpu.VMEM((tb, GLA_GATE_RANK), F32),
                        pltpu.VMEM((h, dk, dv), F32), pltpu.VMEM((h, dk, dv), F32)],
        compiler_params=pltpu.CompilerParams(dimension_semantics=("arbitrary",), vmem_limit_bytes=VMEM_LIMIT),
        name="front_prompt",
    )(x_p, x_p, x_p, ada_p, ada_p, ada_p, ada_p, ada_p, ada_p, g_mix.reshape(1, 1, d), w_main, w_glr,
      cos_f, sin_f, cos_f, sin_f, dmask, qdec, kdec, cdec, tri, w_gk, b_gk, g_gla)


def _mixs_kernel(d, ts, gsz, rqk_ref, rv_ref, rg_ref, gqk_ref, gv_ref, gg_ref, glr_ref, cos_ref, sin_ref,
                 dmask_ref, qdec_ref, kdec_ref, cdec_ref, wgk_ref, bgk_ref, gn_ref, sr_in, sg_in,
                 oret_ref, ogla_ref, sr_out, sg_out):
    dk, dv, hq = d // 8, d // 4, d // 2
    pair = 2 * ts
    cos_f, sin_f = cos_ref[...], sin_ref[...]
    gnorm = gn_ref[...]
    ri = lax.broadcasted_iota(jnp.int32, (pair, pair), 0)
    ci = lax.broadcasted_iota(jnp.int32, (pair, pair), 1)
    causal = jnp.logical_and(ri >= ci, (ri < ts) == (ci < ts))
    tri = causal.astype(F32).astype(BF16)
    first = lax.broadcasted_iota(jnp.int32, (pair, 1), 0) < ts
    masks = [first, jnp.logical_not(first)]

    def body(j, carry):
        rows = pl.ds(pl.multiple_of(j * pair, pair), pair)
        s0, s1 = 2 * j, 2 * j + 1
        la_hi, la_lo = _split_hi_lo(_log_a(glr_ref[rows, :], wgk_ref[...], bgk_ref[...]))
        b = _mm(tri, la_hi) + _mm(tri, la_lo)
        for h in range(N_HEADS):
            o, (n0, n1) = _ret_head(rqk_ref[0, rows, h * dk:(h + 1) * dk].astype(F32),
                                    rqk_ref[0, rows, hq + h * dk:hq + (h + 1) * dk].astype(F32),
                                    rv_ref[0, rows, h * dv:(h + 1) * dv],
                                    rg_ref[0, rows, h * dv:(h + 1) * dv].astype(F32),
                                    [sr_in[s0, h], sr_in[s1, h]], masks, cos_f, sin_f,
                                    dmask_ref[h], qdec_ref[h], kdec_ref[h], cdec_ref[h])
            sr_out[s0, h] = n0
            sr_out[s1, h] = n1
            oret_ref[rows, h * dv:(h + 1) * dv] = o.astype(BF16)
            o, (n0, n1) = _gla_head(gqk_ref[0, rows, h * dk:(h + 1) * dk].astype(F32),
                                    gqk_ref[0, rows, hq + h * dk:hq + (h + 1) * dk].astype(F32),
                                    gv_ref[0, rows, h * dv:(h + 1) * dv],
                                    gg_ref[0, rows, h * dv:(h + 1) * dv].astype(F32),
                                    b[:, h * dk:(h + 1) * dk], [sg_in[s0, h], sg_in[s1, h]], masks, ts,
                                    gnorm, causal)
            sg_out[s0, h] = n0
            sg_out[s1, h] = n1
            ogla_ref[rows, h * dv:(h + 1) * dv] = o.astype(BF16)
        return carry

    lax.fori_loop(0, gsz // 2, body, 0)


def _pair_tables(ts, dk, dv):
    cos_f, sin_f = _rope_tables(PAST_LEN, ts, dk)
    dmask, qdec, kdec, cdec = _ret_tables(ts, dk, dv)
    zero = jnp.zeros_like(dmask)
    dmask2 = jnp.concatenate([jnp.concatenate([dmask, zero], axis=2), jnp.concatenate([zero, dmask], axis=2)], axis=1)

    def twice(a, axis):
        return jnp.concatenate([a, a], axis=axis)

    return twice(cos_f, 0), twice(sin_f, 0), dmask2, twice(qdec, 1), twice(kdec, 1), cdec


def _mix_sample(bs, ts, d, n_prompt_tok, gsz, proj, glr, state_ret, state_gla, w_gk, b_gk, g_gla):
    dk, dv, hq, h = d // 8, d // 4, d // 2, N_HEADS
    assert GLA_CHUNK % ts == 0 and bs % gsz == 0 and gsz % 2 == 0 and n_prompt_tok % (gsz * ts) == 0
    rows = gsz * ts
    pair = 2 * ts
    row0 = n_prompt_tok // rows
    cos_f, sin_f, dmask, qdec, kdec, cdec = _pair_tables(ts, dk, dv)

    def seg(s):
        return pl.BlockSpec((1, rows, d), lambda i: (s, row0 + i, 0))

    def const(shape):
        zeros = (0,) * len(shape)
        return pl.BlockSpec(shape, lambda i: zeros)

    state_spec = pl.BlockSpec((gsz, h, dk, dv), lambda i: (i, 0, 0, 0))
    tok_spec = pl.BlockSpec((rows, d), lambda i: (i, 0))
    return pl.pallas_call(
        functools.partial(_mixs_kernel, d, ts, gsz),
        grid=(bs // gsz,),
        in_specs=[seg(0), seg(1), seg(2), seg(3), seg(4), seg(5),
                  pl.BlockSpec((rows, GLA_GATE_RANK), lambda i: (row0 + i, 0)),
                  const((pair, dk)), const((pair, dk)),
                  const((h, pair, pair)), const((h, pair, dk)), const((h, pair, dk)), const((h, 1, dv)),
                  const((GLA_GATE_RANK, hq)), const((1, hq)), const((1, dv)),
                  state_spec, state_spec],
        out_specs=[tok_spec, tok_spec, state_spec, state_spec],
        out_shape=[jax.ShapeDtypeStruct((bs * ts, d), BF16), jax.ShapeDtypeStruct((bs * ts, d), BF16),
                   jax.ShapeDtypeStruct((bs, h, dk, dv), F32), jax.ShapeDtypeStruct((bs, h, dk, dv), F32)],
        compiler_params=pltpu.CompilerParams(dimension_semantics=("arbitrary",), vmem_limit_bytes=VMEM_LIMIT),
        name="mix_sample",
    )(proj, proj, proj, proj, proj, proj, glr, cos_f, sin_f, dmask, qdec, kdec, cdec, w_gk, b_gk, g_gla,
      state_ret, state_gla)


def _outproj_kernel(n_pt, d, tm, orp_ref, ogp_ref, ors_ref, ogs_ref, mgrp_ref, mggp_ref, mgrs_ref, mggs_ref,
                    xp_ref, xs_ref,
                    gtp_ref, shp_ref, scp_ref, gts_ref, shs_ref, scs_ref, g_ref,
                    wro_ref, wgo_ref, wo_ref, wrh_ref, wrl_ref, br_ref,
                    h_ref, n2_ref, idx_ref, rank_ref, prob_ref, cnt_ref, carry_s):
    i = pl.program_id(0)
    e = N_EXPERTS

    @pl.when(i == 0)
    def _():
        carry_s[...] = jnp.zeros_like(carry_s)

    def body(out_ret, out_gla, mg_ret, mg_gla, x3, gt, sh, sc):
        a = _mm(out_ret, wro_ref[...])
        b = _mm(out_gla, wgo_ref[...])
        merged = jax.nn.sigmoid(mg_ret.astype(F32)) * a + jax.nn.sigmoid(mg_gla.astype(F32)) * b
        mix = _mm(merged.astype(BF16), wo_ref[...])
        h3 = x3 + gt * mix.reshape(x3.shape)
        h_ref[...] = h3.reshape(-1, d)
        n2 = _rms_mod(h3, g_ref[...], sc, sh).reshape(-1, d)
        n2_ref[...] = _pack_pair(n2)

        n_hi, n_lo = _split_hi_lo(n2)
        logits = _mm(n_hi, wrh_ref[...]) + _mm(n_lo, wrh_ref[...]) + _mm(n_hi, wrl_ref[...]) + br_ref[...]
        iota = lax.broadcasted_iota(jnp.int32, (tm, e), 1)
        work = logits
        vals, idxs = [], []
        for _ in range(TOP_K):
            m = jnp.max(work, axis=-1, keepdims=True)
            ik = jnp.min(jnp.where(work == m, iota, e), axis=-1, keepdims=True)
            vals.append(m)
            idxs.append(ik)
            work = jnp.where(iota == ik, -jnp.inf, work)
        ex = [jnp.exp(v - vals[0]) for v in vals]
        den = ex[0] + ex[1] + ex[2] + ex[3]

        onehot = jnp.zeros((tm, e), F32)
        for ik in idxs:
            onehot = onehot + (iota == ik).astype(F32)
        ltri = (lax.broadcasted_iota(jnp.int32, (tm, tm), 0) > lax.broadcasted_iota(jnp.int32, (tm, tm), 1))
        cum = _mm(ltri.astype(F32).astype(BF16), onehot.astype(BF16)) + carry_s[...]
        lane = lax.broadcasted_iota(jnp.int32, (tm, TOP_K), 1)
        idx_o = jnp.zeros((tm, TOP_K), jnp.int32)
        rank_o = jnp.zeros((tm, TOP_K), jnp.int32)
        prob_o = jnp.zeros((tm, TOP_K), F32)
        for k in range(TOP_K):
            rk = jnp.sum(jnp.where(iota == idxs[k], cum, 0.0), axis=-1, keepdims=True).astype(jnp.int32)
            idx_o = jnp.where(lane == k, idxs[k], idx_o)
            rank_o = jnp.where(lane == k, rk, rank_o)
            prob_o = jnp.where(lane == k, ex[k] / den, prob_o)
        idx_ref[...] = idx_o
        rank_ref[...] = rank_o
        prob_ref[...] = prob_o
        carry_s[...] = carry_s[...] + jnp.sum(onehot, axis=0, keepdims=True)

    @pl.when(i < n_pt)
    def _():
        body(orp_ref[...], ogp_ref[...], mgrp_ref[0], mggp_ref[0], xp_ref[...], gtp_ref[0], shp_ref[0], scp_ref[0])

    @pl.when(i >= n_pt)
    def _():
        body(ors_ref[...], ogs_ref[...], mgrs_ref[0], mggs_ref[0], xs_ref[...], gts_ref[0], shs_ref[0], scs_ref[0])

    @pl.when(i == pl.num_programs(0) - 1)
    def _():
        cnt_ref[...] = carry_s[...].astype(jnp.int32)


def _outproj(tl, oret_p, ogla_p, oret_s, ogla_s, mg_p, proj_s, x_p, x_s, ada_p, ada_s, g_ffn,
             w_ret_o, w_gla_o, w_out, w_r_hi, w_r_lo, b_router):
    d = x_p.shape[-1]
    tm, e, n_pt = tl.tm, N_EXPERTS, tl.n_pt
    last = n_pt - 1
    p_spec = pl.BlockSpec((tm, d), lambda i: (jnp.minimum(i, last), 0))
    s_spec = pl.BlockSpec((tm, d), lambda i: (jnp.maximum(i - n_pt, 0), 0))

    def mgp_spec(seg):
        return pl.BlockSpec((1, tm, d), lambda i: (seg, jnp.minimum(i, last), 0))

    def mgs_spec(seg):
        return pl.BlockSpec((1, tm, d), lambda i: (seg, jnp.maximum(i - n_pt, 0), 0))

    return pl.pallas_call(
        functools.partial(_outproj_kernel, n_pt, d, tm),
        grid=(tl.n,),
        in_specs=[p_spec, p_spec, s_spec, s_spec, mgp_spec(0), mgp_spec(1), mgs_spec(6), mgs_spec(7),
                  tl.xp_spec(d), tl.xs_spec(d),
                  tl.adap_spec(2, d), tl.adap_spec(3, d), tl.adap_spec(4, d),
                  tl.adas_spec(2, d), tl.adas_spec(3, d), tl.adas_spec(4, d),
                  _resident((1, 1, d)), _resident((d, d)), _resident((d, d)), _resident((d, d)),
                  _resident((d, e)), _resident((d, e)), _resident((1, e))],
        out_specs=[tl.tok_spec(d), tl.tok_spec(d // 2), tl.tok_spec(TOP_K), tl.tok_spec(TOP_K), tl.tok_spec(TOP_K),
                   pl.BlockSpec((1, e), lambda i: (0, 0))],
        out_shape=[jax.ShapeDtypeStruct((tl.n_tok, d), F32), jax.ShapeDtypeStruct((tl.n_tok, d // 2), jnp.uint32),
                   jax.ShapeDtypeStruct((tl.n_tok, TOP_K), jnp.int32),
                   jax.ShapeDtypeStruct((tl.n_tok, TOP_K), jnp.int32),
                   jax.ShapeDtypeStruct((tl.n_tok, TOP_K), F32),
                   jax.ShapeDtypeStruct((1, e), jnp.int32)],
        scratch_shapes=[pltpu.VMEM((1, e), F32)],
        compiler_params=pltpu.CompilerParams(dimension_semantics=("arbitrary",), vmem_limit_bytes=VMEM_LIMIT),
        name="outproj",
    )(oret_p, ogla_p, oret_s, ogla_s, mg_p, mg_p, proj_s, proj_s, x_p, x_s, ada_p, ada_p, ada_p, ada_s, ada_s, ada_s,
      g_ffn.reshape(1, 1, d), w_ret_o, w_gla_o, w_out, w_r_hi, w_r_lo, b_router.reshape(1, e))


EXPERT_ROW_SLABS = 2
COMBINE_SPLIT = 2


def _expert_kernel(f, te_ref, na_ref, x_ref, wu_ref, bu_ref, wd_ref, bd_ref, y_ref, wu_s, wd_s):
    j = pl.program_id(0)
    active = j < na_ref[0]
    first = jnp.logical_or(j == 0, te_ref[j] != te_ref[jnp.maximum(j - 1, 0)])

    @pl.when(jnp.logical_and(active, first))
    def _():
        wu_s[...] = wu_ref[0].astype(BF16)
        wd_s[...] = wd_ref[0].astype(BF16)

    slab = x_ref.shape[0] // EXPERT_ROW_SLABS
    half = x_ref.shape[1]

    @pl.when(active)
    def _():
        for s in range(EXPERT_ROW_SLABS):
            rows = slice(s * slab, (s + 1) * slab)
            x_lo, x_hi = _unpack_pair(x_ref[rows, :])
            gu = _mm(x_lo.astype(BF16), wu_s[:half, :]) + _mm(x_hi.astype(BF16), wu_s[half:, :]) + bu_ref[0]
            gate = jnp.minimum(gu[:, :f], SWIGLU_LIMIT)
            up = jnp.clip(gu[:, f:], -SWIGLU_LIMIT, SWIGLU_LIMIT)
            act = (up + 1.0) * gate * jax.nn.sigmoid(SWIGLU_ALPHA * gate)
            y_ref[rows, :] = _pack_pair(_mm(act.astype(BF16), wd_s[...]) + bd_ref[0])


def _experts(xs, tile_expert, n_active, w_up, b_up, w_down, b_down, tme):
    r = xs.shape[0]
    e, d, f2 = w_up.shape
    f = f2 // 2
    n_tiles = r // tme

    def row_map(j, te, na):
        return (jnp.minimum(j, na[0] - 1), 0)

    def w_map(j, te, na):
        return (te[jnp.minimum(j, na[0] - 1)], 0, 0)

    return pl.pallas_call(
        functools.partial(_expert_kernel, f),
        grid_spec=pltpu.PrefetchScalarGridSpec(
            num_scalar_prefetch=2,
            grid=(n_tiles,),
            in_specs=[pl.BlockSpec((tme, d // 2), row_map),
                      pl.BlockSpec((1, d, f2), w_map), pl.BlockSpec((1, 1, f2), w_map),
                      pl.BlockSpec((1, f, d), w_map), pl.BlockSpec((1, 1, d), w_map)],
            out_specs=pl.BlockSpec((tme, d // 2), row_map),
            scratch_shapes=[pltpu.VMEM((d, f2), BF16), pltpu.VMEM((f, d), BF16)]),
        out_shape=jax.ShapeDtypeStruct((r, d // 2), jnp.uint32),
        compiler_params=pltpu.CompilerParams(dimension_semantics=("arbitrary",), vmem_limit_bytes=VMEM_LIMIT),
        name="experts",
    )(tile_expert, n_active, xs, w_up, b_up.reshape(e, 1, f2), w_down, b_down.reshape(e, 1, d))


def _sc_mesh():
    return plsc.VectorSubcoreMesh(core_axis_name="core", subcore_axis_name="subcore")


def _sc_split(n_rows, max_chunk):
    info = plsc.get_sparse_core_info()
    n_workers = info.num_cores * info.num_subcores
    assert n_rows % (8 * n_workers) == 0
    per_w = n_rows // n_workers
    chunk = 8
    while chunk * 2 <= max_chunk and per_w % (chunk * 2) == 0:
        chunk *= 2
    return info.num_cores, n_workers, per_w, chunk


def _sc_dispatch(x, pos, n_rows):
    n, w = x.shape
    nc, nw, per_w, chunk = _sc_split(n, 32)
    n_ch = per_w // chunk
    idx = pos.T.reshape(TOP_K, nw, n_ch, chunk).transpose(1, 0, 2, 3).reshape(nw, TOP_K * n_ch, chunk)

    @functools.partial(
        pl.kernel, out_type=jax.ShapeDtypeStruct((n_rows, w), x.dtype), mesh=_sc_mesh(),
        scratch_types=[pltpu.VMEM((TOP_K * n_ch, chunk), jnp.int32), pltpu.VMEM((2, chunk, w), x.dtype),
                       pltpu.SemaphoreType.DMA((2,)), pltpu.SemaphoreType.DMA((2,))])
    def scatter_rows(x_hbm, i_hbm, o_hbm, idx_v, rows_v, rsem, wsem):
        wid = lax.axis_index("subcore") * nc + lax.axis_index("core")
        base = wid * per_w
        pltpu.sync_copy(i_hbm.at[wid], idx_v)

        def read(j, slot):
            return pltpu.make_async_copy(x_hbm.at[pl.ds(base + j * chunk, chunk)], rows_v.at[slot], rsem.at[slot])

        def write(j, slot, k):
            return pltpu.make_async_copy(rows_v.at[slot], o_hbm.at[idx_v.at[k * n_ch + j]], wsem.at[slot])

        read(0, 0).start()

        @pl.loop(0, n_ch, step=2)
        def _(j0):
            for b in range(2):
                j = j0 + b

                @pl.when(j < n_ch)
                def _():
                    read(j, b).wait()

                    @pl.when(j + 1 < n_ch)
                    def _():
                        @pl.when(j >= 1)
                        def _():
                            for k in range(TOP_K):
                                write(j - 1, 1 - b, k).wait()

                        read(j + 1, 1 - b).start()

                    for k in range(TOP_K):
                        write(j, b, k).start()

        for jj in range(max(n_ch - 2, 0), n_ch):
            for k in range(TOP_K):
                write(jj, jj % 2, k).wait()

    return scatter_rows(x, idx)


def _sc_gather(table, idx):
    m = idx.shape[0]
    w = table.shape[1]
    nc, _, per_w, chunk = _sc_split(m, 64)
    n_ch = per_w // chunk

    @functools.partial(
        pl.kernel, out_type=jax.ShapeDtypeStruct((m, w), table.dtype), mesh=_sc_mesh(),
        scratch_types=[pltpu.VMEM((per_w,), jnp.int32), pltpu.VMEM((2, chunk, w), table.dtype),
                       pltpu.SemaphoreType.DMA((2,)), pltpu.SemaphoreType.DMA((2,))])
    def gather_rows(t_hbm, i_hbm, o_hbm, idx_v, rows_v, gsem, wsem):
        wid = lax.axis_index("subcore") * nc + lax.axis_index("core")
        base = wid * per_w
        pltpu.sync_copy(i_hbm.at[pl.ds(base, per_w)], idx_v)

        def gather(j, slot):
            off = pl.multiple_of(j * chunk, chunk)
            return pltpu.make_async_copy(t_hbm.at[idx_v.at[pl.ds(off, chunk)]], rows_v.at[slot], gsem.at[slot])

        def write(j, slot):
            off = pl.multiple_of(j * chunk, chunk)
            return pltpu.make_async_copy(rows_v.at[slot], o_hbm.at[pl.ds(base + off, chunk)], wsem.at[slot])

        gather(0, 0).start()

        @pl.loop(0, n_ch, step=2)
        def _(j0):
            for b in range(2):
                j = j0 + b

                @pl.when(j < n_ch)
                def _():
                    gather(j, b).wait()

                    @pl.when(j + 1 < n_ch)
                    def _():
                        @pl.when(j >= 1)
                        def _():
                            write(j - 1, 1 - b).wait()

                        gather(j + 1, 1 - b).start()

                    write(j, b).start()

        for jj in range(max(n_ch - 2, 0), n_ch):
            write(jj, jj % 2).wait()

    return gather_rows(table, idx)


def _final_kernel(n_pt, off, d, h_ref, yg_ref, prob_ref, gtp_ref, gts_ref, g_ref, *rest):
    yp_ref, ys_ref = rest[-2:]
    i = pl.program_id(0) + off
    p = prob_ref[...]
    moe_lo, moe_hi = None, None
    for k in range(TOP_K):
        lo, hi = _unpack_pair(yg_ref[k])
        pk = p[:, k:k + 1]
        moe_lo = pk * lo if moe_lo is None else moe_lo + pk * lo
        moe_hi = pk * hi if moe_hi is None else moe_hi + pk * hi
    moe = jnp.concatenate([moe_lo, moe_hi], axis=1)

    def body(gt, shape):
        h3 = h_ref[...].reshape(shape) + gt * moe.reshape(shape)
        ms = jnp.mean(h3 * h3, axis=-1, keepdims=True)
        return h3 * lax.rsqrt(ms + EPS) * g_ref[...]

    @pl.when(i < n_pt)
    def _():
        yp_ref[...] = body(gtp_ref[0], yp_ref.shape)

    @pl.when(i >= n_pt)
    def _():
        ys_ref[...] = body(gts_ref[0], ys_ref.shape)


def _final(tl, off, n_tiles, h, yg, probs, ada_p, ada_s, g_final, d, prev=None):
    in_specs = [tl.tok_spec(d, off), pl.BlockSpec((TOP_K, tl.tm, d // 2), lambda i: (0, i, 0)),
                tl.tok_spec(TOP_K, off), tl.adap_spec(5, d, off), tl.adas_spec(5, d, off), _resident((1, 1, d))]
    args = [h, yg, probs, ada_p, ada_s, g_final.reshape(1, 1, d)]
    aliases = {}
    if prev is not None:
        aliases = {len(args): 0, len(args) + 1: 1}
        in_specs += [pl.BlockSpec(memory_space=pl.ANY), pl.BlockSpec(memory_space=pl.ANY)]
        args += list(prev)
    return pl.pallas_call(
        functools.partial(_final_kernel, tl.n_pt, off, d),
        grid=(n_tiles,),
        in_specs=in_specs,
        out_specs=[tl.xp_spec(d, off), tl.xs_spec(d, off)],
        out_shape=[jax.ShapeDtypeStruct((tl.b, tl.t, d), F32), jax.ShapeDtypeStruct((tl.bs, tl.ts, d), F32)],
        input_output_aliases=aliases,
        compiler_params=pltpu.CompilerParams(dimension_semantics=("arbitrary",), vmem_limit_bytes=VMEM_LIMIT),
        name="final",
    )(*args)


def _pick(n, pref):
    t = min(n, pref)
    while n % t:
        t //= 2
    return t


def _forward(x_prompt, x_sample, c_prompt, c_sample, state_ret, state_gla, w_ada, b_ada, g_norm_mix, g_norm_ffn,
             w_in, w_gk_up, b_gk, g_gla_norm, w_ret_o, w_gla_o, w_out, w_router, b_router, w_up, b_up,
             w_down, b_down, g_final, *, tm, tb, gsz, tme):
    b, t, d = x_prompt.shape
    bs, ts, _ = x_sample.shape
    assert w_ada.shape[0] == 1, "single layer only"
    e = N_EXPERTS
    tl = _Tiles(b, t, bs, ts, tm)
    n_tok = tl.n_tok

    ada = _ada(jnp.concatenate([c_prompt, c_sample], axis=0), w_ada[0], b_ada[0])
    ada_p = ada[:, :b].reshape(6, b, 1, d)
    ada_s = ada[:, b:].reshape(6, bs, 1, d)

    w_in0 = w_in[0]
    n_main = 6 * d
    w_main = jnp.concatenate([w_in0[:, :n_main], w_in0[:, n_main + GLA_GATE_RANK:]], axis=1).astype(BF16)
    w_glr = w_in0[:, n_main:n_main + GLA_GATE_RANK].astype(BF16)
    proj_s, glr_s = _inproj_sample(tl, x_sample, ada_s, g_norm_mix[0], w_main, w_glr)

    w_gk = w_gk_up[0].astype(BF16)
    bgk = b_gk[0].reshape(1, -1)
    ggn = g_gla_norm[0].reshape(1, -1)
    oret_p, ogla_p, mg_p, sret_p, sgla_p = _front_prompt(x_prompt, ada_p, g_norm_mix[0], w_main, w_glr, tb,
                                                         w_gk, bgk, ggn)
    oret_s, ogla_s, sret_s, sgla_s = _mix_sample(bs, ts, d, 0, gsz, proj_s, glr_s, state_ret[0], state_gla[0],
                                                 w_gk, bgk, ggn)

    w_r = w_router[0]
    w_r_hi = w_r.astype(BF16)
    w_r_lo = (w_r - w_r_hi.astype(F32)).astype(BF16)
    h, n2, idx, rank, probs, counts = _outproj(
        tl, oret_p, ogla_p, oret_s, ogla_s, mg_p, proj_s, x_prompt, x_sample, ada_p, ada_s, g_norm_ffn[0],
        w_ret_o[0].astype(BF16), w_gla_o[0].astype(BF16), w_out[0].astype(BF16), w_r_hi, w_r_lo, b_router[0])

    counts = counts[0]
    gsize = ((counts + tme - 1) // tme) * tme
    ends = jnp.cumsum(gsize)
    offs = ends - gsize
    experts = jnp.arange(e, dtype=jnp.int32)
    pos = jnp.sum(jnp.where(idx[..., None] == experts, offs, 0), axis=-1) + rank
    max_tiles = (n_tok * TOP_K) // tme + e
    n_active = (ends[-1] // tme).astype(jnp.int32).reshape(1)
    tile_start = jnp.arange(max_tiles, dtype=jnp.int32) * tme
    tile_expert = jnp.minimum(jnp.sum((ends[None, :] <= tile_start[:, None]).astype(jnp.int32), axis=1), e - 1)

    xs = _sc_dispatch(n2, pos, max_tiles * tme)
    ys = _experts(xs, tile_expert, n_active, w_up[0], b_up[0], w_down[0], b_down[0], tme)
    pos_t = pos.T
    step = -(-tl.n // COMBINE_SPLIT)
    outs = None
    for lo in range(0, tl.n, step):
        n_tiles = min(step, tl.n - lo)
        rows = slice(lo * tm, (lo + n_tiles) * tm)
        yg = _sc_gather(ys, pos_t[:, rows].reshape(-1)).reshape(TOP_K, n_tiles * tm, d // 2)
        outs = _final(tl, lo, n_tiles, h, yg, probs, ada_p, ada_s, g_final, d, prev=outs)
    y_p, y_s = outs
    return (y_p, y_s, sret_p[None], sgla_p[None], sret_s[None], sgla_s[None])


def kernel(x_prompt, x_sample, c_prompt, c_sample, state_ret, state_gla, w_ada, b_ada, g_norm_mix, g_norm_ffn,
           w_in, w_gk_up, b_gk, g_gla_norm, w_ret_o, w_gla_o, w_out, w_router, b_router, w_up, b_up,
           w_down, b_down, g_final):
    t = x_prompt.shape[1]
    bs, ts = x_sample.shape[0], x_sample.shape[1]
    return _forward(x_prompt, x_sample, c_prompt, c_sample, state_ret, state_gla, w_ada, b_ada, g_norm_mix,
                    g_norm_ffn, w_in, w_gk_up, b_gk, g_gla_norm, w_ret_o, w_gla_o, w_out, w_router, b_router,
                    w_up, b_up, w_down, b_down, g_final,
                    tm=_pick(bs * ts, 512), tb=_pick(t, 256), gsz=_pick(bs, 8), tme=512)
```

```python
import functools

import jax
import jax.numpy as jnp
from jax import lax
from jax.experimental import pallas as pl
from jax.experimental.pallas import tpu as pltpu
from jax.experimental.pallas import tpu_sc as plsc

F32 = jnp.float32
BF16 = jnp.bfloat16

N_HEADS = 4
GLA_GATE_RANK = 16
GLA_GATE_NORM = 16.0
GLA_CHUNK = 64
ROPE_BASE = 10000.0
N_EXPERTS = 32
TOP_K = 4
SWIGLU_LIMIT = 7.0
SWIGLU_ALPHA = 1.702
EPS = 1e-6
PAST_LEN = 16384
N_SEG = 8
EXPERT_ROW_SLABS = 2

VMEM_LIMIT = 56 * 1024 * 1024
FRONT_VMEM_LIMIT = 60 * 1024 * 1024


def _mm(a, b):
    return jnp.dot(a, b, preferred_element_type=F32)


def _mm_nt(a, b):
    return lax.dot_general(a, b, (((1,), (1,)), ((), ())), preferred_element_type=F32)


def _silu(x):
    return x * jax.nn.sigmoid(x)


def _split_hi_lo(x):
    hi = x.astype(BF16)
    lo = (x - hi.astype(F32)).astype(BF16)
    return hi, lo


def _pack_pair(x):
    w = x.shape[1] // 2
    lo = lax.bitcast_convert_type(x[:, :w].astype(BF16).astype(F32), jnp.uint32)
    hi = lax.bitcast_convert_type(x[:, w:].astype(BF16).astype(F32), jnp.uint32)
    return (hi & jnp.uint32(0xFFFF0000)) | (lo >> 16)


def _unpack_pair(p):
    lo = lax.bitcast_convert_type(p << 16, F32)
    hi = lax.bitcast_convert_type(p & jnp.uint32(0xFFFF0000), F32)
    return lo, hi


def _rms_mod(x3, g, sc, sh):
    ms = jnp.mean(x3 * x3, axis=-1, keepdims=True)
    return x3 * lax.rsqrt(ms + EPS) * g * (1.0 + sc) + sh


def _resident(shape):
    zeros = (0,) * len(shape)
    return pl.BlockSpec(shape, lambda i: zeros, pipeline_mode=pl.Buffered(1))


def _const(shape):
    zeros = (0,) * len(shape)
    return pl.BlockSpec(shape, lambda i: zeros)


def _ada_kernel(c_ref, w_ref, b_ref, o_ref):
    cf = _silu(c_ref[...])
    o_ref[0] = _mm(cf.astype(BF16), w_ref[...].astype(BF16)) + b_ref[0]


def _ada(c_all, w_ada, b_ada):
    bc, d = c_all.shape
    n = w_ada.shape[1] // d
    return pl.pallas_call(
        _ada_kernel,
        grid=(n,),
        in_specs=[pl.BlockSpec((bc, d), lambda j: (0, 0)),
                  pl.BlockSpec((d, d), lambda j: (0, j)),
                  pl.BlockSpec((1, 1, d), lambda j: (j, 0, 0))],
        out_specs=pl.BlockSpec((1, bc, d), lambda j: (j, 0, 0)),
        out_shape=jax.ShapeDtypeStruct((n, bc, d), F32),
        compiler_params=pltpu.CompilerParams(dimension_semantics=("arbitrary",), vmem_limit_bytes=VMEM_LIMIT),
        name="ada",
    )(c_all, w_ada, b_ada.reshape(n, 1, d))


class _Tiles:
    def __init__(self, b, t, bs, ts, tm):
        assert t % tm == 0 and (bs * ts) % tm == 0 and tm % ts == 0
        self.b, self.t, self.bs, self.ts, self.tm = b, t, bs, ts, tm
        self.tpb = t // tm
        self.n_pt = b * self.tpb
        self.gs = tm // ts
        self.n_st = (bs * ts) // tm
        self.n = self.n_pt + self.n_st
        self.n_tok = b * t + bs * ts

    def xp_spec(self, d):
        last, tpb = self.n_pt - 1, self.tpb
        return pl.BlockSpec((1, self.tm, d), lambda i: (jnp.minimum(i, last) // tpb, jnp.minimum(i, last) % tpb, 0))

    def xs_spec(self, d):
        n_pt = self.n_pt
        return pl.BlockSpec((self.gs, self.ts, d), lambda i: (jnp.maximum(i - n_pt, 0), 0, 0))

    def adap_spec(self, which, d):
        last, tpb = self.n_pt - 1, self.tpb
        return pl.BlockSpec((1, 1, 1, d), lambda i: (which, jnp.minimum(i, last) // tpb, 0, 0))

    def adas_spec(self, which, d):
        n_pt = self.n_pt
        return pl.BlockSpec((1, self.gs, 1, d), lambda i: (which, jnp.maximum(i - n_pt, 0), 0, 0))

    def tok_spec(self, width):
        return pl.BlockSpec((self.tm, width), lambda i: (i, 0))

    def s_x_spec(self, d):
        return pl.BlockSpec((self.gs, self.ts, d), lambda i: (i, 0, 0))

    def s_ada_spec(self, which, d):
        return pl.BlockSpec((1, self.gs, 1, d), lambda i: (which, i, 0, 0))

    def s_row_spec(self, width):
        return pl.BlockSpec((self.tm, width), lambda i: (i, 0))

    def s_tok_spec(self, width):
        n_pt = self.n_pt
        return pl.BlockSpec((self.tm, width), lambda i: (n_pt + i, 0))


def _rope_tables(pos0, t, dk):
    half = dk // 2
    inv = ROPE_BASE ** (-jnp.arange(half, dtype=jnp.float32) / half)
    pos = pos0 + jnp.arange(t)
    ang = pos.astype(jnp.float32)[:, None] * inv[None, :]
    cos, sin = jnp.cos(ang), jnp.sin(ang)
    return jnp.concatenate([cos, cos], axis=-1), jnp.concatenate([-sin, sin], axis=-1)


def _ret_tables(c, dk, dv):
    h = N_HEADS
    log_gamma = jnp.log1p(-jnp.exp2(-5.0 - jnp.arange(h, dtype=jnp.float32)))
    idx = jnp.arange(c, dtype=jnp.float32)
    rel = idx[:, None] - idx[None, :]
    dmask = jnp.where(rel >= 0, jnp.exp(log_gamma[:, None, None] * jnp.maximum(rel, 0.0)), 0.0)
    kdec = jnp.exp(log_gamma[:, None] * (c - 1 - idx))
    qdec = jnp.exp(log_gamma[:, None] * (idx + 1.0))
    cdec = jnp.exp(log_gamma * c)
    return (dmask,
            jnp.broadcast_to(qdec[:, :, None], (h, c, dk)),
            jnp.broadcast_to(kdec[:, :, None], (h, c, dk)),
            jnp.broadcast_to(cdec[:, None, None], (h, 1, dv)))


def _rot(x, cos_f, sin_f):
    return x * cos_f + pltpu.roll(x, x.shape[-1] // 2, 1) * sin_f


def _cross_and_update(q_lhs, k_end, vh, states, masks):
    if masks is None:
        (s,) = states
        return _mm(q_lhs, s.astype(BF16)), [_mm(k_end.T.astype(BF16), vh)]
    cross, incs = None, []
    for s, m in zip(states, masks):
        c = _mm(q_lhs, s.astype(BF16))
        cross = c if cross is None else jnp.where(m, c, cross)
        incs.append(_mm(jnp.where(m, k_end, 0.0).T.astype(BF16), vh))
    return cross, incs


def _ret_head(q, k, vh, gh, states, masks, cos_f, sin_f, dmask, qdec, kdec, cdec):
    dk = q.shape[-1]
    q = _rot(q, cos_f, sin_f)
    k = _rot(k, cos_f, sin_f) * (dk ** -0.5)
    scores = _mm_nt(q.astype(BF16), k.astype(BF16)) * dmask
    cross, incs = _cross_and_update((q * qdec).astype(BF16), k * kdec, vh, states, masks)
    o = _mm(scores.astype(BF16), vh) + cross
    new_states = [cdec * s + u for s, u in zip(states, incs)]
    mu = jnp.mean(o, axis=-1, keepdims=True)
    oc = o - mu
    var = jnp.mean(oc * oc, axis=-1, keepdims=True)
    return _silu(gh) * (oc * lax.rsqrt(var + EPS)), new_states


def _gla_head(q, k, vh, gh, b, states, masks, c, gnorm, causal):
    dk = q.shape[-1]
    b_t = b.T
    if masks is None:
        b_last = b[c - 1:c, :]
    else:
        b_last = None
        for g, m in enumerate(masks):
            row = b[g * c + c - 1:g * c + c, :]
            b_last = row if b_last is None else jnp.where(m, row, b_last)
    q_in = (q * (dk ** -0.5) * jnp.exp(b)).astype(BF16)
    k_in = (k * jnp.exp(-b)).astype(BF16)
    scores = jnp.where(causal, _mm_nt(q_in, k_in), 0.0)
    cross, incs = _cross_and_update(q_in, k * jnp.exp(b_last - b), vh, states, masks)
    o = _mm(scores.astype(BF16), vh) + cross
    new_states = [jnp.exp(b_t[:, g * c + c - 1:g * c + c]) * s + u for g, (s, u) in enumerate(zip(states, incs))]
    o = o * lax.rsqrt(jnp.mean(o * o, axis=-1, keepdims=True) + EPS) * gnorm
    return _silu(gh) * o, new_states


def _log_a(glr, wgk, bgk):
    z = _mm(glr.astype(BF16), wgk) + bgk
    return (jnp.minimum(z, 0.0) - jnp.log1p(jnp.exp(-jnp.abs(z)))) / GLA_GATE_NORM


def _causal(c):
    return lax.broadcasted_iota(jnp.int32, (c, c), 0) >= lax.broadcasted_iota(jnp.int32, (c, c), 1)


def _proj_block(d, x3, sh, sc, g, w_ref, wl_ref, proj_s, glr_s):
    n = _rms_mod(x3, g, sc, sh).reshape(-1, d).astype(BF16)
    for seg in range(N_SEG):
        proj_s[:, seg * d:(seg + 1) * d] = _mm(n, w_ref[:, seg * d:(seg + 1) * d]).astype(BF16)
    glr_s[...] = _mm(n, wl_ref[...])


def _mix_block(d, tb, proj_s, glr_s, cos_f, sin_f, dmask_ref, qdec_ref, kdec_ref, cdec_ref, tri, wgk, bgk, gnorm,
               sr_s, sg_s, oret_s, ogla_s, mg_s):
    dk, dv, hq = d // 8, d // 4, d // 2
    rqk, rv, rg, gqk, gv, gg, mg = (i * d for i in range(7))
    for h in range(N_HEADS):
        o, (s_new,) = _ret_head(proj_s[:, rqk + h * dk:rqk + (h + 1) * dk].astype(F32),
                                proj_s[:, rqk + hq + h * dk:rqk + hq + (h + 1) * dk].astype(F32),
                                proj_s[:, rv + h * dv:rv + (h + 1) * dv],
                                proj_s[:, rg + h * dv:rg + (h + 1) * dv].astype(F32),
                                [sr_s[h]], None, cos_f, sin_f, dmask_ref[h], qdec_ref[h], kdec_ref[h],
                                cdec_ref[h])
        sr_s[h] = s_new
        oret_s[:, h * dv:(h + 1) * dv] = o.astype(BF16)

    la_hi, la_lo = _split_hi_lo(_log_a(glr_s[...], wgk, bgk))
    b = _mm(tri, la_hi) + _mm(tri, la_lo)
    cg = GLA_CHUNK
    causal = _causal(cg)
    for c in range(tb // cg):
        r0, r1 = c * cg, (c + 1) * cg
        for h in range(N_HEADS):
            o, (s_new,) = _gla_head(proj_s[r0:r1, gqk + h * dk:gqk + (h + 1) * dk].astype(F32),
                                    proj_s[r0:r1, gqk + hq + h * dk:gqk + hq + (h + 1) * dk].astype(F32),
                                    proj_s[r0:r1, gv + h * dv:gv + (h + 1) * dv],
                                    proj_s[r0:r1, gg + h * dv:gg + (h + 1) * dv].astype(F32),
                                    b[r0:r1, h * dk:(h + 1) * dk], [sg_s[h]], None, cg, gnorm, causal)
            sg_s[h] = s_new
            ogla_s[r0:r1, h * dv:(h + 1) * dv] = o.astype(BF16)
    mg_s[0] = proj_s[:, mg:mg + d]
    mg_s[1] = proj_s[:, mg + d:mg + 2 * d]


def _route_block(d, out_ret, out_gla, mg_ret, mg_gla, x3, gt, sh, sc, g, wro_ref, wgo_ref, wo_ref, wrh_ref, wrl_ref,
                 br, carry_s, h_ref, n2_ref, idx_ref, rank_ref, prob_ref, r_off):
    rows = out_ret.shape[0]
    e = N_EXPERTS
    a = _mm(out_ret, wro_ref[...])
    b = _mm(out_gla, wgo_ref[...])
    merged = jax.nn.sigmoid(mg_ret.astype(F32)) * a + jax.nn.sigmoid(mg_gla.astype(F32)) * b
    mix = _mm(merged.astype(BF16), wo_ref[...])
    h3 = x3 + gt * mix.reshape(x3.shape)
    out_rows = slice(r_off, r_off + rows)
    h_ref[out_rows, :] = h3.reshape(rows, d)
    n2 = _rms_mod(h3, g, sc, sh).reshape(rows, d)
    n2_ref[out_rows, :] = _pack_pair(n2)

    n_hi, n_lo = _split_hi_lo(n2)
    logits = _mm(n_hi, wrh_ref[...]) + _mm(n_lo, wrh_ref[...]) + _mm(n_hi, wrl_ref[...]) + br
    iota = lax.broadcasted_iota(jnp.int32, (rows, e), 1)
    work = logits
    vals, idxs = [], []
    for _ in range(TOP_K):
        m = jnp.max(work, axis=-1, keepdims=True)
        ik = jnp.min(jnp.where(work == m, iota, e), axis=-1, keepdims=True)
        vals.append(m)
        idxs.append(ik)
        work = jnp.where(iota == ik, -jnp.inf, work)
    ex = [jnp.exp(v - vals[0]) for v in vals]
    den = ex[0] + ex[1] + ex[2] + ex[3]

    onehot = jnp.zeros((rows, e), F32)
    for ik in idxs:
        onehot = onehot + (iota == ik).astype(F32)
    ltri = (lax.broadcasted_iota(jnp.int32, (rows, rows), 0) > lax.broadcasted_iota(jnp.int32, (rows, rows), 1))
    cum = _mm(ltri.astype(F32).astype(BF16), onehot.astype(BF16)) + carry_s[...]
    lane = lax.broadcasted_iota(jnp.int32, (rows, TOP_K), 1)
    idx_o = jnp.zeros((rows, TOP_K), jnp.int32)
    rank_o = jnp.zeros((rows, TOP_K), jnp.int32)
    prob_o = jnp.zeros((rows, TOP_K), F32)
    for k in range(TOP_K):
        rk = jnp.sum(jnp.where(iota == idxs[k], cum, 0.0), axis=-1, keepdims=True).astype(jnp.int32)
        idx_o = jnp.where(lane == k, idxs[k], idx_o)
        rank_o = jnp.where(lane == k, rk, rank_o)
        prob_o = jnp.where(lane == k, ex[k] / den, prob_o)
    idx_ref[out_rows, :] = idx_o
    rank_ref[out_rows, :] = rank_o
    prob_ref[out_rows, :] = prob_o
    carry_s[...] = carry_s[...] + jnp.sum(onehot, axis=0, keepdims=True)


def _frontp_kernel(d, tb, ntb, x0_ref, xp_ref, xn_ref, sh0_ref, sc0_ref, shp_ref, scp_ref, gtp_ref, sh2p_ref, sc2p_ref,
                   shn_ref, scn_ref, gmix_ref, w_ref, wl_ref, cosa_ref, sina_ref, cosb_ref, sinb_ref,
                   dmask_ref, qdec_ref, kdec_ref, cdec_ref, tri_ref, wgk_ref, bgk_ref, gn_ref,
                   gffn_ref, wro_ref, wgo_ref, wo_ref, wrh_ref, wrl_ref, br_ref,
                   h_ref, n2_ref, idx_ref, rank_ref, prob_ref, cnt_ref, sret_ref, sgla_ref,
                   pa_s, pb_s, ga_s, gb_s, ora_s, oga_s, mga_s, orb_s, ogb_s, mgb_s, sr_s, sg_s, carry_s):
    p = pl.program_id(0)
    blk = 2 * p
    gmix = gmix_ref[...]
    proj = functools.partial(_proj_block, d)
    mix = functools.partial(_mix_block, d, tb)
    tables = (dmask_ref, qdec_ref, kdec_ref, cdec_ref, tri_ref[...], wgk_ref[...], bgk_ref[...], gn_ref[...])
    route = functools.partial(_route_block, d)
    route_w = (gffn_ref[...], wro_ref, wgo_ref, wo_ref, wrh_ref, wrl_ref, br_ref[...], carry_s,
               h_ref, n2_ref, idx_ref, rank_ref, prob_ref)

    @pl.when(p == 0)
    def _():
        carry_s[...] = jnp.zeros_like(carry_s)
        proj(x0_ref[...], sh0_ref[0], sc0_ref[0], gmix, w_ref, wl_ref, pa_s, ga_s)

    @pl.when(blk % ntb == 0)
    def _():
        sr_s[...] = jnp.zeros_like(sr_s)
        sg_s[...] = jnp.zeros_like(sg_s)

    proj(xp_ref[:, tb:, :], shp_ref[0], scp_ref[0], gmix, w_ref, wl_ref, pb_s, gb_s)
    mix(pa_s, ga_s, cosa_ref[...], sina_ref[...], *tables, sr_s, sg_s, ora_s, oga_s, mga_s)
    proj(xn_ref[...], shn_ref[0], scn_ref[0], gmix, w_ref, wl_ref, pa_s, ga_s)
    route(ora_s[...], oga_s[...], mga_s[0], mga_s[1], xp_ref[:, :tb, :], gtp_ref[0], sh2p_ref[0], sc2p_ref[0],
          *route_w, 0)
    mix(pb_s, gb_s, cosb_ref[...], sinb_ref[...], *tables, sr_s, sg_s, orb_s, ogb_s, mgb_s)
    route(orb_s[...], ogb_s[...], mgb_s[0], mgb_s[1], xp_ref[:, tb:, :], gtp_ref[0], sh2p_ref[0], sc2p_ref[0],
          *route_w, tb)

    @pl.when((blk + 1) % ntb == ntb - 1)
    def _():
        sret_ref[0] = sr_s[...]
        sgla_ref[0] = sg_s[...]

    @pl.when(p == pl.num_programs(0) - 1)
    def _():
        cnt_ref[...] = carry_s[...].astype(jnp.int32)


def _chunk_tri(tb, cg):
    i = jnp.arange(tb)
    return ((i[:, None] >= i[None, :]) & (i[:, None] // cg == i[None, :] // cg)).astype(BF16)


def _front_prompt(x_p, n_tok, ada_p, g_mix, w_main, w_glr, tb, w_gk, b_gk, g_gla, g_ffn, w_ret_o, w_gla_o, w_out,
                  w_r_hi, w_r_lo, b_router):
    b, t, d = x_p.shape
    dk, dv, hq, h, e = d // 8, d // 4, d // 2, N_HEADS, N_EXPERTS
    ntb = t // tb
    n_blk = b * ntb
    assert ntb % 2 == 0
    cos_f, sin_f = _rope_tables(0, t, dk)
    dmask, qdec, kdec, cdec = _ret_tables(tb, dk, dv)
    tri = _chunk_tri(tb, GLA_CHUNK)

    def first(p):
        return 0 * p

    def even(p):
        return 2 * p

    def odd(p):
        return 2 * p + 1

    def nxt(p):
        return jnp.minimum(2 * p + 2, n_blk - 1)

    def x_spec(blk_of):
        return pl.BlockSpec((1, tb, d), lambda p: (blk_of(p) // ntb, blk_of(p) % ntb, 0))

    def ada_spec(which, blk_of):
        return pl.BlockSpec((1, 1, 1, d), lambda p: (which, blk_of(p) // ntb, 0, 0))

    def rope_spec(blk_of):
        return pl.BlockSpec((tb, dk), lambda p: (blk_of(p) % ntb, 0))

    pair_spec = pl.BlockSpec((1, 2 * tb, d), lambda p: ((2 * p) // ntb, ((2 * p) % ntb) // 2, 0))
    state_spec = pl.BlockSpec((1, h, dk, dv), lambda p: ((2 * p) // ntb, 0, 0, 0))

    def tok_spec(width):
        return pl.BlockSpec((2 * tb, width), lambda p: (p, 0))

    return pl.pallas_call(
        functools.partial(_frontp_kernel, d, tb, ntb),
        grid=(n_blk // 2,),
        in_specs=[_resident((1, tb, d)), pair_spec, x_spec(nxt),
                  ada_spec(0, first), ada_spec(1, first),
                  ada_spec(0, even), ada_spec(1, even), ada_spec(2, even), ada_spec(3, even), ada_spec(4, even),
                  ada_spec(0, nxt), ada_spec(1, nxt),
                  _resident((1, 1, d)), _resident((d, N_SEG * d)), _resident((d, GLA_GATE_RANK)),
                  rope_spec(even), rope_spec(even), rope_spec(odd), rope_spec(odd),
                  _resident((h, tb, tb)), _resident((h, tb, dk)), _resident((h, tb, dk)), _resident((h, 1, dv)),
                  _resident((tb, tb)), _resident((GLA_GATE_RANK, hq)), _resident((1, hq)), _resident((1, dv)),
                  _resident((1, 1, d)), _resident((d, d)), _resident((d, d)), _resident((d, d)),
                  _resident((d, e)), _resident((d, e)), _resident((1, e))],
        out_specs=[tok_spec(d), tok_spec(d // 2), tok_spec(TOP_K), tok_spec(TOP_K), tok_spec(TOP_K),
                   pl.BlockSpec((1, e), lambda p: (0, 0)), state_spec, state_spec],
        out_shape=[jax.ShapeDtypeStruct((n_tok, d), F32), jax.ShapeDtypeStruct((n_tok, d // 2), jnp.uint32),
                   jax.ShapeDtypeStruct((n_tok, TOP_K), jnp.int32), jax.ShapeDtypeStruct((n_tok, TOP_K), jnp.int32),
                   jax.ShapeDtypeStruct((n_tok, TOP_K), F32), jax.ShapeDtypeStruct((1, e), jnp.int32),
                   jax.ShapeDtypeStruct((b, h, dk, dv), F32), jax.ShapeDtypeStruct((b, h, dk, dv), F32)],
        scratch_shapes=[pltpu.VMEM((tb, N_SEG * d), BF16), pltpu.VMEM((tb, N_SEG * d), BF16),
                        pltpu.VMEM((tb, GLA_GATE_RANK), F32), pltpu.VMEM((tb, GLA_GATE_RANK), F32),
                        pltpu.VMEM((tb, d), BF16), pltpu.VMEM((tb, d), BF16), pltpu.VMEM((2, tb, d), BF16),
                        pltpu.VMEM((tb, d), BF16), pltpu.VMEM((tb, d), BF16), pltpu.VMEM((2, tb, d), BF16),
                        pltpu.VMEM((h, dk, dv), F32), pltpu.VMEM((h, dk, dv), F32), pltpu.VMEM((1, e), F32)],
        compiler_params=pltpu.CompilerParams(dimension_semantics=("arbitrary",), vmem_limit_bytes=FRONT_VMEM_LIMIT),
        name="front_prompt",
    )(x_p, x_p, x_p, ada_p, ada_p, ada_p, ada_p, ada_p, ada_p, ada_p, ada_p, ada_p,
      g_mix.reshape(1, 1, d), w_main, w_glr, cos_f, sin_f, cos_f, sin_f, dmask, qdec, kdec, cdec, tri,
      w_gk, b_gk, g_gla, g_ffn.reshape(1, 1, d), w_ret_o, w_gla_o, w_out, w_r_hi, w_r_lo, b_router.reshape(1, e))


def _inproj_kernel(d, xs_ref, shs_ref, scs_ref, g_ref, w_ref, wl_ref, proj_ref, glr_ref):
    n = _rms_mod(xs_ref[...], g_ref[...], scs_ref[0], shs_ref[0]).reshape(-1, d).astype(BF16)
    for s in range(N_SEG):
        proj_ref[s] = _mm(n, w_ref[:, s * d:(s + 1) * d]).astype(BF16)
    glr_ref[...] = _mm(n, wl_ref[...])


def _inproj_sample(tl, x_s, ada_s, g_mix, w_main, w_glr):
    bs, ts, d = x_s.shape
    n_tok = bs * ts
    return pl.pallas_call(
        functools.partial(_inproj_kernel, d),
        grid=(tl.n_st,),
        in_specs=[tl.s_x_spec(d), tl.s_ada_spec(0, d), tl.s_ada_spec(1, d),
                  _resident((1, 1, d)), _resident((d, N_SEG * d)), _resident((d, GLA_GATE_RANK))],
        out_specs=[pl.BlockSpec((N_SEG, tl.tm, d), lambda i: (0, i, 0)), tl.s_row_spec(GLA_GATE_RANK)],
        out_shape=[jax.ShapeDtypeStruct((N_SEG, n_tok, d), BF16), jax.ShapeDtypeStruct((n_tok, GLA_GATE_RANK), F32)],
        compiler_params=pltpu.CompilerParams(dimension_semantics=("arbitrary",), vmem_limit_bytes=VMEM_LIMIT),
        name="inproj_sample",
    )(x_s, ada_s, ada_s, g_mix.reshape(1, 1, d), w_main, w_glr)


def _mixs_kernel(d, ts, gsz, rqk_ref, rv_ref, rg_ref, gqk_ref, gv_ref, gg_ref, glr_ref, cos_ref, sin_ref,
                 dmask_ref, qdec_ref, kdec_ref, cdec_ref, wgk_ref, bgk_ref, gn_ref, sr_in, sg_in,
                 oret_ref, ogla_ref, sr_out, sg_out):
    dk, dv, hq = d // 8, d // 4, d // 2
    pair = 2 * ts
    cos_f, sin_f = cos_ref[...], sin_ref[...]
    gnorm = gn_ref[...]
    ri = lax.broadcasted_iota(jnp.int32, (pair, pair), 0)
    ci = lax.broadcasted_iota(jnp.int32, (pair, pair), 1)
    causal = jnp.logical_and(ri >= ci, (ri < ts) == (ci < ts))
    tri = causal.astype(F32).astype(BF16)
    first = lax.broadcasted_iota(jnp.int32, (pair, 1), 0) < ts
    masks = [first, jnp.logical_not(first)]

    def body(j, carry):
        rows = pl.ds(pl.multiple_of(j * pair, pair), pair)
        s0, s1 = 2 * j, 2 * j + 1
        la_hi, la_lo = _split_hi_lo(_log_a(glr_ref[rows, :], wgk_ref[...], bgk_ref[...]))
        b = _mm(tri, la_hi) + _mm(tri, la_lo)
        for h in range(N_HEADS):
            o, (n0, n1) = _ret_head(rqk_ref[0, rows, h * dk:(h + 1) * dk].astype(F32),
                                    rqk_ref[0, rows, hq + h * dk:hq + (h + 1) * dk].astype(F32),
                                    rv_ref[0, rows, h * dv:(h + 1) * dv],
                                    rg_ref[0, rows, h * dv:(h + 1) * dv].astype(F32),
                                    [sr_in[s0, h], sr_in[s1, h]], masks, cos_f, sin_f,
                                    dmask_ref[h], qdec_ref[h], kdec_ref[h], cdec_ref[h])
            sr_out[s0, h] = n0
            sr_out[s1, h] = n1
            oret_ref[rows, h * dv:(h + 1) * dv] = o.astype(BF16)
            o, (n0, n1) = _gla_head(gqk_ref[0, rows, h * dk:(h + 1) * dk].astype(F32),
                                    gqk_ref[0, rows, hq + h * dk:hq + (h + 1) * dk].astype(F32),
                                    gv_ref[0, rows, h * dv:(h + 1) * dv],
                                    gg_ref[0, rows, h * dv:(h + 1) * dv].astype(F32),
                                    b[:, h * dk:(h + 1) * dk], [sg_in[s0, h], sg_in[s1, h]], masks, ts,
                                    gnorm, causal)
            sg_out[s0, h] = n0
            sg_out[s1, h] = n1
            ogla_ref[rows, h * dv:(h + 1) * dv] = o.astype(BF16)
        return carry

    lax.fori_loop(0, gsz // 2, body, 0, unroll=2)


def _pair_tables(ts, dk, dv):
    cos_f, sin_f = _rope_tables(PAST_LEN, ts, dk)
    dmask, qdec, kdec, cdec = _ret_tables(ts, dk, dv)
    zero = jnp.zeros_like(dmask)
    dmask2 = jnp.concatenate([jnp.concatenate([dmask, zero], axis=2), jnp.concatenate([zero, dmask], axis=2)], axis=1)

    def twice(a, axis):
        return jnp.concatenate([a, a], axis=axis)

    return twice(cos_f, 0), twice(sin_f, 0), dmask2, twice(qdec, 1), twice(kdec, 1), cdec


def _mix_sample(bs, ts, d, gsz, proj, glr, state_ret, state_gla, w_gk, b_gk, g_gla):
    dk, dv, hq, h = d // 8, d // 4, d // 2, N_HEADS
    assert GLA_CHUNK % ts == 0 and bs % gsz == 0 and gsz % 4 == 0
    rows = gsz * ts
    pair = 2 * ts
    cos_f, sin_f, dmask, qdec, kdec, cdec = _pair_tables(ts, dk, dv)

    def seg(s):
        return pl.BlockSpec((1, rows, d), lambda i: (s, i, 0))

    state_spec = pl.BlockSpec((gsz, h, dk, dv), lambda i: (i, 0, 0, 0))
    tok_spec = pl.BlockSpec((rows, d), lambda i: (i, 0))
    return pl.pallas_call(
        functools.partial(_mixs_kernel, d, ts, gsz),
        grid=(bs // gsz,),
        in_specs=[seg(0), seg(1), seg(2), seg(3), seg(4), seg(5),
                  pl.BlockSpec((rows, GLA_GATE_RANK), lambda i: (i, 0)),
                  _const((pair, dk)), _const((pair, dk)),
                  _const((h, pair, pair)), _const((h, pair, dk)), _const((h, pair, dk)), _const((h, 1, dv)),
                  _const((GLA_GATE_RANK, hq)), _const((1, hq)), _const((1, dv)),
                  state_spec, state_spec],
        out_specs=[tok_spec, tok_spec, state_spec, state_spec],
        out_shape=[jax.ShapeDtypeStruct((bs * ts, d), BF16), jax.ShapeDtypeStruct((bs * ts, d), BF16),
                   jax.ShapeDtypeStruct((bs, h, dk, dv), F32), jax.ShapeDtypeStruct((bs, h, dk, dv), F32)],
        compiler_params=pltpu.CompilerParams(dimension_semantics=("arbitrary",), vmem_limit_bytes=VMEM_LIMIT),
        name="mix_sample",
    )(proj, proj, proj, proj, proj, proj, glr, cos_f, sin_f, dmask, qdec, kdec, cdec, w_gk, b_gk, g_gla,
      state_ret, state_gla)


N_ROUTE_OUT = 5


def _outproj_sample_kernel(d, or_ref, og_ref, mgr_ref, mgg_ref, x_ref, gt_ref, sh_ref, sc_ref, g_ref,
                           wro_ref, wgo_ref, wo_ref, wrh_ref, wrl_ref, br_ref, cin_ref, *rest):
    h_ref, n2_ref, idx_ref, rank_ref, prob_ref, cnt_ref, carry_s = rest[N_ROUTE_OUT:]
    i = pl.program_id(0)

    @pl.when(i == 0)
    def _():
        carry_s[...] = cin_ref[...].astype(F32)

    _route_block(d, or_ref[...], og_ref[...], mgr_ref[0], mgg_ref[0], x_ref[...], gt_ref[0], sh_ref[0], sc_ref[0],
                 g_ref[...], wro_ref, wgo_ref, wo_ref, wrh_ref, wrl_ref, br_ref[...], carry_s,
                 h_ref, n2_ref, idx_ref, rank_ref, prob_ref, 0)

    @pl.when(i == pl.num_programs(0) - 1)
    def _():
        cnt_ref[...] = carry_s[...].astype(jnp.int32)


def _outproj_sample(tl, route_p, counts_p, oret_s, ogla_s, proj_s, x_s, ada_s, g_ffn, w_ret_o, w_gla_o, w_out,
                    w_r_hi, w_r_lo, b_router):
    d = x_s.shape[-1]
    tm, e = tl.tm, N_EXPERTS

    def mg_spec(seg):
        return pl.BlockSpec((1, tm, d), lambda i: (seg, i, 0))

    any_spec = pl.BlockSpec(memory_space=pl.ANY)
    n_in = 17
    outs = pl.pallas_call(
        functools.partial(_outproj_sample_kernel, d),
        grid=(tl.n_st,),
        in_specs=[tl.s_row_spec(d), tl.s_row_spec(d), mg_spec(6), mg_spec(7), tl.s_x_spec(d),
                  tl.s_ada_spec(2, d), tl.s_ada_spec(3, d), tl.s_ada_spec(4, d),
                  _resident((1, 1, d)), _resident((d, d)), _resident((d, d)), _resident((d, d)),
                  _resident((d, e)), _resident((d, e)), _resident((1, e)), _resident((1, e))]
                 + [any_spec] * N_ROUTE_OUT,
        out_specs=[tl.s_tok_spec(d), tl.s_tok_spec(d // 2), tl.s_tok_spec(TOP_K), tl.s_tok_spec(TOP_K),
                   tl.s_tok_spec(TOP_K), pl.BlockSpec((1, e), lambda i: (0, 0))],
        out_shape=[jax.ShapeDtypeStruct(a.shape, a.dtype) for a in route_p]
                  + [jax.ShapeDtypeStruct((1, e), jnp.int32)],
        scratch_shapes=[pltpu.VMEM((1, e), F32)],
        input_output_aliases={n_in - 1 + k: k for k in range(N_ROUTE_OUT)},
        compiler_params=pltpu.CompilerParams(dimension_semantics=("arbitrary",), vmem_limit_bytes=VMEM_LIMIT),
        name="outproj_sample",
    )(oret_s, ogla_s, proj_s, proj_s, x_s, ada_s, ada_s, ada_s, g_ffn.reshape(1, 1, d), w_ret_o, w_gla_o, w_out,
      w_r_hi, w_r_lo, b_router.reshape(1, e), counts_p, *route_p)
    return outs[:N_ROUTE_OUT], outs[N_ROUTE_OUT]


def _expert_kernel(f, te_ref, na_ref, x_ref, wu_ref, bu_ref, wd_ref, bd_ref, y_ref, wu_s, wd_s):
    j = pl.program_id(0)
    active = j < na_ref[0]
    first = jnp.logical_or(j == 0, te_ref[j] != te_ref[jnp.maximum(j - 1, 0)])

    @pl.when(jnp.logical_and(active, first))
    def _():
        wu_s[...] = wu_ref[0].astype(BF16)
        wd_s[...] = wd_ref[0].astype(BF16)

    slab = x_ref.shape[0] // EXPERT_ROW_SLABS
    half = x_ref.shape[1]

    @pl.when(active)
    def _():
        for s in range(EXPERT_ROW_SLABS):
            rows = slice(s * slab, (s + 1) * slab)
            x_lo, x_hi = _unpack_pair(x_ref[rows, :])
            gu = _mm(x_lo.astype(BF16), wu_s[:half, :]) + _mm(x_hi.astype(BF16), wu_s[half:, :]) + bu_ref[0]
            gate = jnp.minimum(gu[:, :f], SWIGLU_LIMIT)
            up = jnp.clip(gu[:, f:], -SWIGLU_LIMIT, SWIGLU_LIMIT)
            act = (up + 1.0) * gate * jax.nn.sigmoid(SWIGLU_ALPHA * gate)
            y_ref[rows, :] = _pack_pair(_mm(act.astype(BF16), wd_s[...]) + bd_ref[0])


def _experts(xs, tile_expert, n_active, w_up, b_up, w_down, b_down, tme):
    r = xs.shape[0]
    e, d, f2 = w_up.shape
    f = f2 // 2
    n_tiles = r // tme

    def row_map(j, te, na):
        return (jnp.minimum(j, na[0] - 1), 0)

    def w_map(j, te, na):
        return (te[jnp.minimum(j, na[0] - 1)], 0, 0)

    return pl.pallas_call(
        functools.partial(_expert_kernel, f),
        grid_spec=pltpu.PrefetchScalarGridSpec(
            num_scalar_prefetch=2,
            grid=(n_tiles,),
            in_specs=[pl.BlockSpec((tme, d // 2), row_map),
                      pl.BlockSpec((1, d, f2), w_map), pl.BlockSpec((1, 1, f2), w_map),
                      pl.BlockSpec((1, f, d), w_map), pl.BlockSpec((1, 1, d), w_map)],
            out_specs=pl.BlockSpec((tme, d // 2), row_map),
            scratch_shapes=[pltpu.VMEM((d, f2), BF16), pltpu.VMEM((f, d), BF16)]),
        out_shape=jax.ShapeDtypeStruct((r, d // 2), jnp.uint32),
        compiler_params=pltpu.CompilerParams(dimension_semantics=("arbitrary",), vmem_limit_bytes=VMEM_LIMIT),
        name="experts",
    )(tile_expert, n_active, xs, w_up, b_up.reshape(e, 1, f2), w_down, b_down.reshape(e, 1, d))


def _sc_mesh():
    return plsc.VectorSubcoreMesh(core_axis_name="core", subcore_axis_name="subcore")


def _sc_split(n_rows, max_chunk):
    info = plsc.get_sparse_core_info()
    n_workers = info.num_cores * info.num_subcores
    assert n_rows % (8 * n_workers) == 0
    per_w = n_rows // n_workers
    chunk = 8
    while chunk * 2 <= max_chunk and per_w % (chunk * 2) == 0:
        chunk *= 2
    return info.num_cores, n_workers, per_w, chunk


def _sc_dispatch(x, pos, n_rows):
    n, w = x.shape
    nc, nw, per_w, chunk = _sc_split(n, 32)
    n_ch = per_w // chunk
    idx = pos.T.reshape(TOP_K, nw, n_ch, chunk).transpose(1, 0, 2, 3).reshape(nw, TOP_K * n_ch, chunk)

    @functools.partial(
        pl.kernel, out_type=jax.ShapeDtypeStruct((n_rows, w), x.dtype), mesh=_sc_mesh(),
        scratch_types=[pltpu.VMEM((TOP_K * n_ch, chunk), jnp.int32), pltpu.VMEM((2, chunk, w), x.dtype),
                       pltpu.SemaphoreType.DMA((2,)), pltpu.SemaphoreType.DMA((2,))])
    def scatter_rows(x_hbm, i_hbm, o_hbm, idx_v, rows_v, rsem, wsem):
        wid = lax.axis_index("subcore") * nc + lax.axis_index("core")
        base = wid * per_w
        pltpu.sync_copy(i_hbm.at[wid], idx_v)

        def read(j, slot):
            return pltpu.make_async_copy(x_hbm.at[pl.ds(base + j * chunk, chunk)], rows_v.at[slot], rsem.at[slot])

        def write(j, slot, k):
            return pltpu.make_async_copy(rows_v.at[slot], o_hbm.at[idx_v.at[k * n_ch + j]], wsem.at[slot])

        read(0, 0).start()

        @pl.loop(0, n_ch, step=2)
        def _(j0):
            for b in range(2):
                j = j0 + b

                @pl.when(j < n_ch)
                def _():
                    read(j, b).wait()

                    @pl.when(j + 1 < n_ch)
                    def _():
                        @pl.when(j >= 1)
                        def _():
                            for k in range(TOP_K):
                                write(j - 1, 1 - b, k).wait()

                        read(j + 1, 1 - b).start()

                    for k in range(TOP_K):
                        write(j, b, k).start()

        for jj in range(max(n_ch - 2, 0), n_ch):
            for k in range(TOP_K):
                write(jj, jj % 2, k).wait()

    return scatter_rows(x, idx)


def _sc_gather(table, idx):
    m = idx.shape[0]
    w = table.shape[1]
    nc, _, per_w, chunk = _sc_split(m, 64)
    n_ch = per_w // chunk

    @functools.partial(
        pl.kernel, out_type=jax.ShapeDtypeStruct((m, w), table.dtype), mesh=_sc_mesh(),
        scratch_types=[pltpu.VMEM((per_w,), jnp.int32), pltpu.VMEM((2, chunk, w), table.dtype),
                       pltpu.SemaphoreType.DMA((2,)), pltpu.SemaphoreType.DMA((2,))])
    def gather_rows(t_hbm, i_hbm, o_hbm, idx_v, rows_v, gsem, wsem):
        wid = lax.axis_index("subcore") * nc + lax.axis_index("core")
        base = wid * per_w
        pltpu.sync_copy(i_hbm.at[pl.ds(base, per_w)], idx_v)

        def gather(j, slot):
            off = pl.multiple_of(j * chunk, chunk)
            return pltpu.make_async_copy(t_hbm.at[idx_v.at[pl.ds(off, chunk)]], rows_v.at[slot], gsem.at[slot])

        def write(j, slot):
            off = pl.multiple_of(j * chunk, chunk)
            return pltpu.make_async_copy(rows_v.at[slot], o_hbm.at[pl.ds(base + off, chunk)], wsem.at[slot])

        gather(0, 0).start()

        @pl.loop(0, n_ch, step=2)
        def _(j0):
            for b in range(2):
                j = j0 + b

                @pl.when(j < n_ch)
                def _():
                    gather(j, b).wait()

                    @pl.when(j + 1 < n_ch)
                    def _():
                        @pl.when(j >= 1)
                        def _():
                            write(j - 1, 1 - b).wait()

                        gather(j + 1, 1 - b).start()

                    write(j, b).start()

        for jj in range(max(n_ch - 2, 0), n_ch):
            write(jj, jj % 2).wait()

    return gather_rows(table, idx)


def _final_kernel(n_pt, d, h_ref, yg_ref, prob_ref, gtp_ref, gts_ref, g_ref, yp_ref, ys_ref):
    i = pl.program_id(0)
    p = prob_ref[...]
    moe_lo, moe_hi = None, None
    for k in range(TOP_K):
        lo, hi = _unpack_pair(yg_ref[k])
        pk = p[:, k:k + 1]
        moe_lo = pk * lo if moe_lo is None else moe_lo + pk * lo
        moe_hi = pk * hi if moe_hi is None else moe_hi + pk * hi
    moe = jnp.concatenate([moe_lo, moe_hi], axis=1)

    def body(gt, shape):
        h3 = h_ref[...].reshape(shape) + gt * moe.reshape(shape)
        ms = jnp.mean(h3 * h3, axis=-1, keepdims=True)
        return h3 * lax.rsqrt(ms + EPS) * g_ref[...]

    @pl.when(i < n_pt)
    def _():
        yp_ref[...] = body(gtp_ref[0], yp_ref.shape)

    @pl.when(i >= n_pt)
    def _():
        ys_ref[...] = body(gts_ref[0], ys_ref.shape)


def _final(tl, h, yg, probs, ada_p, ada_s, g_final, d):
    return pl.pallas_call(
        functools.partial(_final_kernel, tl.n_pt, d),
        grid=(tl.n,),
        in_specs=[tl.tok_spec(d), pl.BlockSpec((TOP_K, tl.tm, d // 2), lambda i: (0, i, 0)), tl.tok_spec(TOP_K),
                  tl.adap_spec(5, d), tl.adas_spec(5, d), _resident((1, 1, d))],
        out_specs=[tl.xp_spec(d), tl.xs_spec(d)],
        out_shape=[jax.ShapeDtypeStruct((tl.b, tl.t, d), F32), jax.ShapeDtypeStruct((tl.bs, tl.ts, d), F32)],
        compiler_params=pltpu.CompilerParams(dimension_semantics=("arbitrary",), vmem_limit_bytes=VMEM_LIMIT),
        name="final",
    )(h, yg, probs, ada_p, ada_s, g_final.reshape(1, 1, d))


def _pick(n, pref):
    t = min(n, pref)
    while n % t:
        t //= 2
    return t


def _forward(x_prompt, x_sample, c_prompt, c_sample, state_ret, state_gla, w_ada, b_ada, g_norm_mix, g_norm_ffn,
             w_in, w_gk_up, b_gk, g_gla_norm, w_ret_o, w_gla_o, w_out, w_router, b_router, w_up, b_up,
             w_down, b_down, g_final, *, tm, tb, gsz, tme):
    b, t, d = x_prompt.shape
    bs, ts, _ = x_sample.shape
    assert w_ada.shape[0] == 1, "single layer only"
    assert (b * t) % (2 * tb) == 0 and (b * t) % tm == 0
    e = N_EXPERTS
    tl = _Tiles(b, t, bs, ts, tm)
    n_tok = tl.n_tok

    ada = _ada(jnp.concatenate([c_prompt, c_sample], axis=0), w_ada[0], b_ada[0])
    ada_p = ada[:, :b].reshape(6, b, 1, d)
    ada_s = ada[:, b:].reshape(6, bs, 1, d)

    w_in0 = w_in[0]
    n_main = 6 * d
    w_main = jnp.concatenate([w_in0[:, :n_main], w_in0[:, n_main + GLA_GATE_RANK:]], axis=1).astype(BF16)
    w_glr = w_in0[:, n_main:n_main + GLA_GATE_RANK].astype(BF16)
    w_gk = w_gk_up[0].astype(BF16)
    bgk = b_gk[0].reshape(1, -1)
    ggn = g_gla_norm[0].reshape(1, -1)
    w_r = w_router[0]
    w_r_hi = w_r.astype(BF16)
    w_r_lo = (w_r - w_r_hi.astype(F32)).astype(BF16)
    route_w = (g_norm_ffn[0], w_ret_o[0].astype(BF16), w_gla_o[0].astype(BF16), w_out[0].astype(BF16),
               w_r_hi, w_r_lo, b_router[0])

    *route_p, counts_p, sret_p, sgla_p = _front_prompt(x_prompt, n_tok, ada_p, g_norm_mix[0], w_main, w_glr, tb,
                                                       w_gk, bgk, ggn, *route_w)
    proj_s, glr_s = _inproj_sample(tl, x_sample, ada_s, g_norm_mix[0], w_main, w_glr)
    oret_s, ogla_s, sret_s, sgla_s = _mix_sample(bs, ts, d, gsz, proj_s, glr_s, state_ret[0], state_gla[0],
                                                 w_gk, bgk, ggn)
    (h, n2, idx, rank, probs), counts = _outproj_sample(tl, route_p, counts_p, oret_s, ogla_s, proj_s, x_sample,
                                                       ada_s, *route_w)

    counts = counts[0]
    gsize = ((counts + tme - 1) // tme) * tme
    ends = jnp.cumsum(gsize)
    offs = ends - gsize
    experts = jnp.arange(e, dtype=jnp.int32)
    pos = jnp.sum(jnp.where(idx[..., None] == experts, offs, 0), axis=-1) + rank
    max_tiles = (n_tok * TOP_K) // tme + e
    n_active = (ends[-1] // tme).astype(jnp.int32).reshape(1)
    tile_start = jnp.arange(max_tiles, dtype=jnp.int32) * tme
    tile_expert = jnp.minimum(jnp.sum((ends[None, :] <= tile_start[:, None]).astype(jnp.int32), axis=1), e - 1)

    xs = _sc_dispatch(n2, pos, max_tiles * tme)
    ys = _experts(xs, tile_expert, n_active, w_up[0], b_up[0], w_down[0], b_down[0], tme)
    yg = _sc_gather(ys, pos.T.reshape(-1)).reshape(TOP_K, n_tok, d // 2)

    y_p, y_s = _final(tl, h, yg, probs, ada_p, ada_s, g_final, d)
    return (y_p, y_s, sret_p[None], sgla_p[None], sret_s[None], sgla_s[None])


def kernel(x_prompt, x_sample, c_prompt, c_sample, state_ret, state_gla, w_ada, b_ada, g_norm_mix, g_norm_ffn,
           w_in, w_gk_up, b_gk, g_gla_norm, w_ret_o, w_gla_o, w_out, w_router, b_router, w_up, b_up,
           w_down, b_down, g_final):
    t = x_prompt.shape[1]
    bs, ts = x_sample.shape[0], x_sample.shape[1]
    return _forward(x_prompt, x_sample, c_prompt, c_sample, state_ret, state_gla, w_ada, b_ada, g_norm_mix,
                    g_norm_ffn, w_in, w_gk_up, b_gk, g_gla_norm, w_ret_o, w_gla_o, w_out, w_router, b_router,
                    w_up, b_up, w_down, b_down, g_final,
                    tm=_pick(bs * ts, 512), tb=_pick(t, 256), gsz=_pick(bs, 8), tme=512)
```

```python
import functools

import jax
import jax.numpy as jnp
from jax import lax
from jax.experimental import pallas as pl
from jax.experimental.pallas import tpu as pltpu
from jax.experimental.pallas import tpu_sc as plsc

F32 = jnp.float32
BF16 = jnp.bfloat16

N_HEADS = 4
GLA_GATE_RANK = 16
GLA_GATE_NORM = 16.0
GLA_CHUNK = 64
ROPE_BASE = 10000.0
N_EXPERTS = 32
TOP_K = 4
SWIGLU_LIMIT = 7.0
SWIGLU_ALPHA = 1.702
EPS = 1e-6
PAST_LEN = 16384
N_SEG = 8
EXPERT_ROW_SLABS = 2

VMEM_LIMIT = 56 * 1024 * 1024


def _mm(a, b):
    return jnp.dot(a, b, preferred_element_type=F32)


def _mm_nt(a, b):
    return lax.dot_general(a, b, (((1,), (1,)), ((), ())), preferred_element_type=F32)


def _silu(x):
    return x * jax.nn.sigmoid(x)


def _split_hi_lo(x):
    hi = x.astype(BF16)
    lo = (x - hi.astype(F32)).astype(BF16)
    return hi, lo


def _pack_pair(x):
    w = x.shape[1] // 2
    lo = lax.bitcast_convert_type(x[:, :w].astype(BF16).astype(F32), jnp.uint32)
    hi = lax.bitcast_convert_type(x[:, w:].astype(BF16).astype(F32), jnp.uint32)
    return (hi & jnp.uint32(0xFFFF0000)) | (lo >> 16)


def _unpack_pair(p):
    lo = lax.bitcast_convert_type(p << 16, F32)
    hi = lax.bitcast_convert_type(p & jnp.uint32(0xFFFF0000), F32)
    return lo, hi


def _rms_mod(x3, g, sc, sh):
    ms = jnp.mean(x3 * x3, axis=-1, keepdims=True)
    return x3 * lax.rsqrt(ms + EPS) * g * (1.0 + sc) + sh


def _resident(shape):
    zeros = (0,) * len(shape)
    return pl.BlockSpec(shape, lambda i: zeros, pipeline_mode=pl.Buffered(1))


def _const(shape):
    zeros = (0,) * len(shape)
    return pl.BlockSpec(shape, lambda i: zeros)


def _ada_kernel(c_ref, w_ref, b_ref, o_ref):
    cf = _silu(c_ref[...])
    o_ref[0] = _mm(cf.astype(BF16), w_ref[...].astype(BF16)) + b_ref[0]


def _ada(c_all, w_ada, b_ada):
    bc, d = c_all.shape
    n = w_ada.shape[1] // d
    return pl.pallas_call(
        _ada_kernel,
        grid=(n,),
        in_specs=[pl.BlockSpec((bc, d), lambda j: (0, 0)),
                  pl.BlockSpec((d, d), lambda j: (0, j)),
                  pl.BlockSpec((1, 1, d), lambda j: (j, 0, 0))],
        out_specs=pl.BlockSpec((1, bc, d), lambda j: (j, 0, 0)),
        out_shape=jax.ShapeDtypeStruct((n, bc, d), F32),
        compiler_params=pltpu.CompilerParams(dimension_semantics=("arbitrary",), vmem_limit_bytes=VMEM_LIMIT),
        name="ada",
    )(c_all, w_ada, b_ada.reshape(n, 1, d))


class _Tiles:
    def __init__(self, b, t, bs, ts, tm):
        assert t % tm == 0 and (bs * ts) % tm == 0 and tm % ts == 0
        self.b, self.t, self.bs, self.ts, self.tm = b, t, bs, ts, tm
        self.tpb = t // tm
        self.n_pt = b * self.tpb
        self.gs = tm // ts
        self.n_st = (bs * ts) // tm
        self.n = self.n_pt + self.n_st
        self.n_tok = b * t + bs * ts

    def xp_spec(self, d):
        last, tpb = self.n_pt - 1, self.tpb
        return pl.BlockSpec((1, self.tm, d), lambda i: (jnp.minimum(i, last) // tpb, jnp.minimum(i, last) % tpb, 0))

    def xs_spec(self, d):
        n_pt = self.n_pt
        return pl.BlockSpec((self.gs, self.ts, d), lambda i: (jnp.maximum(i - n_pt, 0), 0, 0))

    def adap_spec(self, which, d):
        last, tpb = self.n_pt - 1, self.tpb
        return pl.BlockSpec((1, 1, 1, d), lambda i: (which, jnp.minimum(i, last) // tpb, 0, 0))

    def adas_spec(self, which, d):
        n_pt = self.n_pt
        return pl.BlockSpec((1, self.gs, 1, d), lambda i: (which, jnp.maximum(i - n_pt, 0), 0, 0))

    def tok_spec(self, width):
        return pl.BlockSpec((self.tm, width), lambda i: (i, 0))

    def s_x_spec(self, d):
        return pl.BlockSpec((self.gs, self.ts, d), lambda i: (i, 0, 0))

    def s_ada_spec(self, which, d):
        return pl.BlockSpec((1, self.gs, 1, d), lambda i: (which, i, 0, 0))

    def s_row_spec(self, width):
        return pl.BlockSpec((self.tm, width), lambda i: (i, 0))

    def s_tok_spec(self, width):
        n_pt = self.n_pt
        return pl.BlockSpec((self.tm, width), lambda i: (n_pt + i, 0))


def _rope_tables(pos0, t, dk):
    half = dk // 2
    inv = ROPE_BASE ** (-jnp.arange(half, dtype=jnp.float32) / half)
    pos = pos0 + jnp.arange(t)
    ang = pos.astype(jnp.float32)[:, None] * inv[None, :]
    cos, sin = jnp.cos(ang), jnp.sin(ang)
    return jnp.concatenate([cos, cos], axis=-1), jnp.concatenate([-sin, sin], axis=-1)


def _ret_tables(c, dk, dv):
    h = N_HEADS
    log_gamma = jnp.log1p(-jnp.exp2(-5.0 - jnp.arange(h, dtype=jnp.float32)))
    idx = jnp.arange(c, dtype=jnp.float32)
    rel = idx[:, None] - idx[None, :]
    dmask = jnp.where(rel >= 0, jnp.exp(log_gamma[:, None, None] * jnp.maximum(rel, 0.0)), 0.0)
    kdec = jnp.exp(log_gamma[:, None] * (c - 1 - idx))
    qdec = jnp.exp(log_gamma[:, None] * (idx + 1.0))
    cdec = jnp.exp(log_gamma * c)
    return (dmask,
            jnp.broadcast_to(qdec[:, :, None], (h, c, dk)),
            jnp.broadcast_to(kdec[:, :, None], (h, c, dk)),
            jnp.broadcast_to(cdec[:, None, None], (h, 1, dv)))


def _rot(x, cos_f, sin_f):
    return x * cos_f + pltpu.roll(x, x.shape[-1] // 2, 1) * sin_f


def _cross_and_update(q_lhs, k_end, vh, states, masks):
    if masks is None:
        (s,) = states
        return _mm(q_lhs, s.astype(BF16)), [_mm(k_end.T.astype(BF16), vh)]
    cross, incs = None, []
    for s, m in zip(states, masks):
        c = _mm(q_lhs, s.astype(BF16))
        cross = c if cross is None else jnp.where(m, c, cross)
        incs.append(_mm(jnp.where(m, k_end, 0.0).T.astype(BF16), vh))
    return cross, incs


def _ret_head(q, k, vh, gh, states, masks, cos_f, sin_f, dmask, qdec, kdec, cdec):
    dk = q.shape[-1]
    q = _rot(q, cos_f, sin_f)
    k = _rot(k, cos_f, sin_f) * (dk ** -0.5)
    scores = _mm_nt(q.astype(BF16), k.astype(BF16)) * dmask
    cross, incs = _cross_and_update((q * qdec).astype(BF16), k * kdec, vh, states, masks)
    o = _mm(scores.astype(BF16), vh) + cross
    new_states = [cdec * s + u for s, u in zip(states, incs)]
    mu = jnp.mean(o, axis=-1, keepdims=True)
    oc = o - mu
    var = jnp.mean(oc * oc, axis=-1, keepdims=True)
    return _silu(gh) * (oc * lax.rsqrt(var + EPS)), new_states


def _gla_head(q, k, vh, gh, b, states, masks, c, gnorm, causal):
    dk = q.shape[-1]
    b_t = b.T
    if masks is None:
        b_last = b[c - 1:c, :]
    else:
        b_last = None
        for g, m in enumerate(masks):
            row = b[g * c + c - 1:g * c + c, :]
            b_last = row if b_last is None else jnp.where(m, row, b_last)
    q_in = (q * (dk ** -0.5) * jnp.exp(b)).astype(BF16)
    k_in = (k * jnp.exp(-b)).astype(BF16)
    scores = jnp.where(causal, _mm_nt(q_in, k_in), 0.0)
    cross, incs = _cross_and_update(q_in, k * jnp.exp(b_last - b), vh, states, masks)
    o = _mm(scores.astype(BF16), vh) + cross
    new_states = [jnp.exp(b_t[:, g * c + c - 1:g * c + c]) * s + u for g, (s, u) in enumerate(zip(states, incs))]
    o = o * lax.rsqrt(jnp.mean(o * o, axis=-1, keepdims=True) + EPS) * gnorm
    return _silu(gh) * o, new_states


def _log_a(glr, wgk, bgk):
    z = _mm(glr.astype(BF16), wgk) + bgk
    return (jnp.minimum(z, 0.0) - jnp.log1p(jnp.exp(-jnp.abs(z)))) / GLA_GATE_NORM


def _causal(c):
    return lax.broadcasted_iota(jnp.int32, (c, c), 0) >= lax.broadcasted_iota(jnp.int32, (c, c), 1)


def _proj_block(d, x3, sh, sc, g, w_ref, wl_ref, proj_s, glr_s):
    n = _rms_mod(x3, g, sc, sh).reshape(-1, d).astype(BF16)
    for seg in range(N_SEG):
        proj_s[:, seg * d:(seg + 1) * d] = _mm(n, w_ref[:, seg * d:(seg + 1) * d]).astype(BF16)
    glr_s[...] = _mm(n, wl_ref[...])


def _mix_block(d, tb, proj_s, glr_s, cos_f, sin_f, dmask_ref, qdec_ref, kdec_ref, cdec_ref, tri, wgk, bgk, gnorm,
               sr_s, sg_s, oret_ref, ogla_ref, mg_ref, r_off):
    dk, dv, hq = d // 8, d // 4, d // 2
    rqk, rv, rg, gqk, gv, gg, mg = (i * d for i in range(7))
    for h in range(N_HEADS):
        o, (s_new,) = _ret_head(proj_s[:, rqk + h * dk:rqk + (h + 1) * dk].astype(F32),
                                proj_s[:, rqk + hq + h * dk:rqk + hq + (h + 1) * dk].astype(F32),
                                proj_s[:, rv + h * dv:rv + (h + 1) * dv],
                                proj_s[:, rg + h * dv:rg + (h + 1) * dv].astype(F32),
                                [sr_s[h]], None, cos_f, sin_f, dmask_ref[h], qdec_ref[h], kdec_ref[h],
                                cdec_ref[h])
        sr_s[h] = s_new
        oret_ref[r_off:r_off + tb, h * dv:(h + 1) * dv] = o.astype(BF16)

    la_hi, la_lo = _split_hi_lo(_log_a(glr_s[...], wgk, bgk))
    b = _mm(tri, la_hi) + _mm(tri, la_lo)
    cg = GLA_CHUNK
    causal = _causal(cg)
    for c in range(tb // cg):
        r0, r1 = c * cg, (c + 1) * cg
        for h in range(N_HEADS):
            o, (s_new,) = _gla_head(proj_s[r0:r1, gqk + h * dk:gqk + (h + 1) * dk].astype(F32),
                                    proj_s[r0:r1, gqk + hq + h * dk:gqk + hq + (h + 1) * dk].astype(F32),
                                    proj_s[r0:r1, gv + h * dv:gv + (h + 1) * dv],
                                    proj_s[r0:r1, gg + h * dv:gg + (h + 1) * dv].astype(F32),
                                    b[r0:r1, h * dk:(h + 1) * dk], [sg_s[h]], None, cg, gnorm, causal)
            sg_s[h] = s_new
            ogla_ref[r_off + r0:r_off + r1, h * dv:(h + 1) * dv] = o.astype(BF16)
    mg_ref[0, r_off:r_off + tb, :] = proj_s[:, mg:mg + d]
    mg_ref[1, r_off:r_off + tb, :] = proj_s[:, mg + d:mg + 2 * d]


def _route_block(d, out_ret, out_gla, mg_ret, mg_gla, x3, gt, sh, sc, g, wro_ref, wgo_ref, wo_ref, wrh_ref, wrl_ref,
                 br, carry_s, h_ref, n2_ref, idx_ref, rank_ref, prob_ref, r_off):
    rows = out_ret.shape[0]
    e = N_EXPERTS
    a = _mm(out_ret, wro_ref[...])
    b = _mm(out_gla, wgo_ref[...])
    merged = jax.nn.sigmoid(mg_ret.astype(F32)) * a + jax.nn.sigmoid(mg_gla.astype(F32)) * b
    mix = _mm(merged.astype(BF16), wo_ref[...])
    h3 = x3 + gt * mix.reshape(x3.shape)
    out_rows = slice(r_off, r_off + rows)
    h_ref[out_rows, :] = h3.reshape(rows, d)
    n2 = _rms_mod(h3, g, sc, sh).reshape(rows, d)
    n2_ref[out_rows, :] = _pack_pair(n2)

    n_hi, n_lo = _split_hi_lo(n2)
    logits = _mm(n_hi, wrh_ref[...]) + _mm(n_lo, wrh_ref[...]) + _mm(n_hi, wrl_ref[...]) + br
    iota = lax.broadcasted_iota(jnp.int32, (rows, e), 1)
    work = logits
    vals, idxs = [], []
    for _ in range(TOP_K):
        m = jnp.max(work, axis=-1, keepdims=True)
        ik = jnp.min(jnp.where(work == m, iota, e), axis=-1, keepdims=True)
        vals.append(m)
        idxs.append(ik)
        work = jnp.where(iota == ik, -jnp.inf, work)
    ex = [jnp.exp(v - vals[0]) for v in vals]
    den = ex[0] + ex[1] + ex[2] + ex[3]

    onehot = jnp.zeros((rows, e), F32)
    for ik in idxs:
        onehot = onehot + (iota == ik).astype(F32)
    ltri = (lax.broadcasted_iota(jnp.int32, (rows, rows), 0) > lax.broadcasted_iota(jnp.int32, (rows, rows), 1))
    cum = _mm(ltri.astype(F32).astype(BF16), onehot.astype(BF16)) + carry_s[...]
    lane = lax.broadcasted_iota(jnp.int32, (rows, TOP_K), 1)
    idx_o = jnp.zeros((rows, TOP_K), jnp.int32)
    rank_o = jnp.zeros((rows, TOP_K), jnp.int32)
    prob_o = jnp.zeros((rows, TOP_K), F32)
    for k in range(TOP_K):
        rk = jnp.sum(jnp.where(iota == idxs[k], cum, 0.0), axis=-1, keepdims=True).astype(jnp.int32)
        idx_o = jnp.where(lane == k, idxs[k], idx_o)
        rank_o = jnp.where(lane == k, rk, rank_o)
        prob_o = jnp.where(lane == k, ex[k] / den, prob_o)
    idx_ref[out_rows, :] = idx_o
    rank_ref[out_rows, :] = rank_o
    prob_ref[out_rows, :] = prob_o
    carry_s[...] = carry_s[...] + jnp.sum(onehot, axis=0, keepdims=True)


def _frontp_kernel(d, tb, ntb, x0_ref, xa_ref, xb_ref, sh0_ref, sc0_ref, sha_ref, sca_ref, shb_ref, scb_ref,
                   g_ref, w_ref, wl_ref, cosa_ref, sina_ref, cosb_ref, sinb_ref,
                   dmask_ref, qdec_ref, kdec_ref, cdec_ref, tri_ref, wgk_ref, bgk_ref, gn_ref,
                   oret_ref, ogla_ref, mg_ref, sret_ref, sgla_ref, pa_s, pb_s, ga_s, gb_s, sr_s, sg_s):
    p = pl.program_id(0)
    blk = 2 * p
    g = g_ref[...]
    proj = functools.partial(_proj_block, d)
    mix = functools.partial(_mix_block, d, tb)
    tables = (dmask_ref, qdec_ref, kdec_ref, cdec_ref, tri_ref[...], wgk_ref[...], bgk_ref[...], gn_ref[...])

    @pl.when(p == 0)
    def _():
        proj(x0_ref[...], sh0_ref[0], sc0_ref[0], g, w_ref, wl_ref, pa_s, ga_s)

    @pl.when(blk % ntb == 0)
    def _():
        sr_s[...] = jnp.zeros_like(sr_s)
        sg_s[...] = jnp.zeros_like(sg_s)

    proj(xa_ref[...], sha_ref[0], sca_ref[0], g, w_ref, wl_ref, pb_s, gb_s)
    mix(pa_s, ga_s, cosa_ref[...], sina_ref[...], *tables, sr_s, sg_s, oret_ref, ogla_ref, mg_ref, 0)
    proj(xb_ref[...], shb_ref[0], scb_ref[0], g, w_ref, wl_ref, pa_s, ga_s)
    mix(pb_s, gb_s, cosb_ref[...], sinb_ref[...], *tables, sr_s, sg_s, oret_ref, ogla_ref, mg_ref, tb)

    @pl.when((blk + 1) % ntb == ntb - 1)
    def _():
        sret_ref[0] = sr_s[...]
        sgla_ref[0] = sg_s[...]


def _chunk_tri(tb, cg):
    i = jnp.arange(tb)
    return ((i[:, None] >= i[None, :]) & (i[:, None] // cg == i[None, :] // cg)).astype(BF16)


def _front_prompt(x_p, ada_p, g_mix, w_main, w_glr, tb, w_gk, b_gk, g_gla):
    b, t, d = x_p.shape
    dk, dv, hq, h = d // 8, d // 4, d // 2, N_HEADS
    ntb = t // tb
    n_blk = b * ntb
    n_tok = b * t
    assert ntb % 2 == 0
    cos_f, sin_f = _rope_tables(0, t, dk)
    dmask, qdec, kdec, cdec = _ret_tables(tb, dk, dv)
    tri = _chunk_tri(tb, GLA_CHUNK)

    def first(p):
        return 0 * p

    def even(p):
        return 2 * p

    def odd(p):
        return 2 * p + 1

    def nxt(p):
        return jnp.minimum(2 * p + 2, n_blk - 1)

    def x_spec(blk_of):
        return pl.BlockSpec((1, tb, d), lambda p: (blk_of(p) // ntb, blk_of(p) % ntb, 0))

    def ada_spec(which, blk_of):
        return pl.BlockSpec((1, 1, 1, d), lambda p: (which, blk_of(p) // ntb, 0, 0))

    def rope_spec(blk_of):
        return pl.BlockSpec((tb, dk), lambda p: (blk_of(p) % ntb, 0))

    state_spec = pl.BlockSpec((1, h, dk, dv), lambda p: ((2 * p) // ntb, 0, 0, 0))
    tok_spec = pl.BlockSpec((2 * tb, d), lambda p: (p, 0))
    return pl.pallas_call(
        functools.partial(_frontp_kernel, d, tb, ntb),
        grid=(n_blk // 2,),
        in_specs=[_resident((1, tb, d)), x_spec(odd), x_spec(nxt),
                  ada_spec(0, first), ada_spec(1, first), ada_spec(0, odd), ada_spec(1, odd),
                  ada_spec(0, nxt), ada_spec(1, nxt),
                  _resident((1, 1, d)), _resident((d, N_SEG * d)), _resident((d, GLA_GATE_RANK)),
                  rope_spec(even), rope_spec(even), rope_spec(odd), rope_spec(odd),
                  _resident((h, tb, tb)), _resident((h, tb, dk)), _resident((h, tb, dk)), _resident((h, 1, dv)),
                  _resident((tb, tb)), _resident((GLA_GATE_RANK, hq)), _resident((1, hq)), _resident((1, dv))],
        out_specs=[tok_spec, tok_spec, pl.BlockSpec((2, 2 * tb, d), lambda p: (0, p, 0)), state_spec, state_spec],
        out_shape=[jax.ShapeDtypeStruct((n_tok, d), BF16), jax.ShapeDtypeStruct((n_tok, d), BF16),
                   jax.ShapeDtypeStruct((2, n_tok, d), BF16),
                   jax.ShapeDtypeStruct((b, h, dk, dv), F32), jax.ShapeDtypeStruct((b, h, dk, dv), F32)],
        scratch_shapes=[pltpu.VMEM((tb, N_SEG * d), BF16), pltpu.VMEM((tb, N_SEG * d), BF16),
                        pltpu.VMEM((tb, GLA_GATE_RANK), F32), pltpu.VMEM((tb, GLA_GATE_RANK), F32),
                        pltpu.VMEM((h, dk, dv), F32), pltpu.VMEM((h, dk, dv), F32)],
        compiler_params=pltpu.CompilerParams(dimension_semantics=("arbitrary",), vmem_limit_bytes=VMEM_LIMIT),
        name="front_prompt",
    )(x_p, x_p, x_p, ada_p, ada_p, ada_p, ada_p, ada_p, ada_p, g_mix.reshape(1, 1, d), w_main, w_glr,
      cos_f, sin_f, cos_f, sin_f, dmask, qdec, kdec, cdec, tri, w_gk, b_gk, g_gla)


def _inproj_kernel(d, xs_ref, shs_ref, scs_ref, g_ref, w_ref, wl_ref, proj_ref, glr_ref):
    n = _rms_mod(xs_ref[...], g_ref[...], scs_ref[0], shs_ref[0]).reshape(-1, d).astype(BF16)
    for s in range(N_SEG):
        proj_ref[s] = _mm(n, w_ref[:, s * d:(s + 1) * d]).astype(BF16)
    glr_ref[...] = _mm(n, wl_ref[...])


def _inproj_sample(tl, x_s, ada_s, g_mix, w_main, w_glr):
    bs, ts, d = x_s.shape
    n_tok = bs * ts
    return pl.pallas_call(
        functools.partial(_inproj_kernel, d),
        grid=(tl.n_st,),
        in_specs=[tl.s_x_spec(d), tl.s_ada_spec(0, d), tl.s_ada_spec(1, d),
                  _resident((1, 1, d)), _resident((d, N_SEG * d)), _resident((d, GLA_GATE_RANK))],
        out_specs=[pl.BlockSpec((N_SEG, tl.tm, d), lambda i: (0, i, 0)), tl.s_row_spec(GLA_GATE_RANK)],
        out_shape=[jax.ShapeDtypeStruct((N_SEG, n_tok, d), BF16), jax.ShapeDtypeStruct((n_tok, GLA_GATE_RANK), F32)],
        compiler_params=pltpu.CompilerParams(dimension_semantics=("arbitrary",), vmem_limit_bytes=VMEM_LIMIT),
        name="inproj_sample",
    )(x_s, ada_s, ada_s, g_mix.reshape(1, 1, d), w_main, w_glr)


def _mixs_kernel(d, ts, gsz, rqk_ref, rv_ref, rg_ref, gqk_ref, gv_ref, gg_ref, glr_ref, cos_ref, sin_ref,
                 dmask_ref, qdec_ref, kdec_ref, cdec_ref, wgk_ref, bgk_ref, gn_ref, sr_in, sg_in,
                 oret_ref, ogla_ref, sr_out, sg_out):
    dk, dv, hq = d // 8, d // 4, d // 2
    pair = 2 * ts
    cos_f, sin_f = cos_ref[...], sin_ref[...]
    gnorm = gn_ref[...]
    ri = lax.broadcasted_iota(jnp.int32, (pair, pair), 0)
    ci = lax.broadcasted_iota(jnp.int32, (pair, pair), 1)
    causal = jnp.logical_and(ri >= ci, (ri < ts) == (ci < ts))
    tri = causal.astype(F32).astype(BF16)
    first = lax.broadcasted_iota(jnp.int32, (pair, 1), 0) < ts
    masks = [first, jnp.logical_not(first)]

    def body(j, carry):
        rows = pl.ds(pl.multiple_of(j * pair, pair), pair)
        s0, s1 = 2 * j, 2 * j + 1
        la_hi, la_lo = _split_hi_lo(_log_a(glr_ref[rows, :], wgk_ref[...], bgk_ref[...]))
        b = _mm(tri, la_hi) + _mm(tri, la_lo)
        for h in range(N_HEADS):
            o, (n0, n1) = _ret_head(rqk_ref[0, rows, h * dk:(h + 1) * dk].astype(F32),
                                    rqk_ref[0, rows, hq + h * dk:hq + (h + 1) * dk].astype(F32),
                                    rv_ref[0, rows, h * dv:(h + 1) * dv],
                                    rg_ref[0, rows, h * dv:(h + 1) * dv].astype(F32),
                                    [sr_in[s0, h], sr_in[s1, h]], masks, cos_f, sin_f,
                                    dmask_ref[h], qdec_ref[h], kdec_ref[h], cdec_ref[h])
            sr_out[s0, h] = n0
            sr_out[s1, h] = n1
            oret_ref[rows, h * dv:(h + 1) * dv] = o.astype(BF16)
            o, (n0, n1) = _gla_head(gqk_ref[0, rows, h * dk:(h + 1) * dk].astype(F32),
                                    gqk_ref[0, rows, hq + h * dk:hq + (h + 1) * dk].astype(F32),
                                    gv_ref[0, rows, h * dv:(h + 1) * dv],
                                    gg_ref[0, rows, h * dv:(h + 1) * dv].astype(F32),
                                    b[:, h * dk:(h + 1) * dk], [sg_in[s0, h], sg_in[s1, h]], masks, ts,
                                    gnorm, causal)
            sg_out[s0, h] = n0
            sg_out[s1, h] = n1
            ogla_ref[rows, h * dv:(h + 1) * dv] = o.astype(BF16)
        return carry

    lax.fori_loop(0, gsz // 2, body, 0, unroll=2)


def _pair_tables(ts, dk, dv):
    cos_f, sin_f = _rope_tables(PAST_LEN, ts, dk)
    dmask, qdec, kdec, cdec = _ret_tables(ts, dk, dv)
    zero = jnp.zeros_like(dmask)
    dmask2 = jnp.concatenate([jnp.concatenate([dmask, zero], axis=2), jnp.concatenate([zero, dmask], axis=2)], axis=1)

    def twice(a, axis):
        return jnp.concatenate([a, a], axis=axis)

    return twice(cos_f, 0), twice(sin_f, 0), dmask2, twice(qdec, 1), twice(kdec, 1), cdec


def _mix_sample(bs, ts, d, gsz, proj, glr, state_ret, state_gla, w_gk, b_gk, g_gla):
    dk, dv, hq, h = d // 8, d // 4, d // 2, N_HEADS
    assert GLA_CHUNK % ts == 0 and bs % gsz == 0 and gsz % 4 == 0
    rows = gsz * ts
    pair = 2 * ts
    cos_f, sin_f, dmask, qdec, kdec, cdec = _pair_tables(ts, dk, dv)

    def seg(s):
        return pl.BlockSpec((1, rows, d), lambda i: (s, i, 0))

    state_spec = pl.BlockSpec((gsz, h, dk, dv), lambda i: (i, 0, 0, 0))
    tok_spec = pl.BlockSpec((rows, d), lambda i: (i, 0))
    return pl.pallas_call(
        functools.partial(_mixs_kernel, d, ts, gsz),
        grid=(bs // gsz,),
        in_specs=[seg(0), seg(1), seg(2), seg(3), seg(4), seg(5),
                  pl.BlockSpec((rows, GLA_GATE_RANK), lambda i: (i, 0)),
                  _const((pair, dk)), _const((pair, dk)),
                  _const((h, pair, pair)), _const((h, pair, dk)), _const((h, pair, dk)), _const((h, 1, dv)),
                  _const((GLA_GATE_RANK, hq)), _const((1, hq)), _const((1, dv)),
                  state_spec, state_spec],
        out_specs=[tok_spec, tok_spec, state_spec, state_spec],
        out_shape=[jax.ShapeDtypeStruct((bs * ts, d), BF16), jax.ShapeDtypeStruct((bs * ts, d), BF16),
                   jax.ShapeDtypeStruct((bs, h, dk, dv), F32), jax.ShapeDtypeStruct((bs, h, dk, dv), F32)],
        compiler_params=pltpu.CompilerParams(dimension_semantics=("arbitrary",), vmem_limit_bytes=VMEM_LIMIT),
        name="mix_sample",
    )(proj, proj, proj, proj, proj, proj, glr, cos_f, sin_f, dmask, qdec, kdec, cdec, w_gk, b_gk, g_gla,
      state_ret, state_gla)


def _outproj_kernel(n_pt, d, orp_ref, ogp_ref, ors_ref, ogs_ref, mgrp_ref, mggp_ref, mgrs_ref, mggs_ref,
                    xp_ref, xs_ref, gtp_ref, shp_ref, scp_ref, gts_ref, shs_ref, scs_ref, g_ref,
                    wro_ref, wgo_ref, wo_ref, wrh_ref, wrl_ref, br_ref,
                    h_ref, n2_ref, idx_ref, rank_ref, prob_ref, cnt_ref, carry_s):
    i = pl.program_id(0)

    @pl.when(i == 0)
    def _():
        carry_s[...] = jnp.zeros_like(carry_s)

    route = functools.partial(_route_block, d)
    tail = (g_ref[...], wro_ref, wgo_ref, wo_ref, wrh_ref, wrl_ref, br_ref[...], carry_s,
            h_ref, n2_ref, idx_ref, rank_ref, prob_ref, 0)

    @pl.when(i < n_pt)
    def _():
        route(orp_ref[...], ogp_ref[...], mgrp_ref[0], mggp_ref[0], xp_ref[...], gtp_ref[0], shp_ref[0], scp_ref[0],
              *tail)

    @pl.when(i >= n_pt)
    def _():
        route(ors_ref[...], ogs_ref[...], mgrs_ref[0], mggs_ref[0], xs_ref[...], gts_ref[0], shs_ref[0], scs_ref[0],
              *tail)

    @pl.when(i == pl.num_programs(0) - 1)
    def _():
        cnt_ref[...] = carry_s[...].astype(jnp.int32)


def _outproj(tl, oret_p, ogla_p, oret_s, ogla_s, mg_p, proj_s, x_p, x_s, ada_p, ada_s, g_ffn,
             w_ret_o, w_gla_o, w_out, w_r_hi, w_r_lo, b_router):
    d = x_p.shape[-1]
    tm, e, n_pt = tl.tm, N_EXPERTS, tl.n_pt
    last = n_pt - 1
    p_spec = pl.BlockSpec((tm, d), lambda i: (jnp.minimum(i, last), 0))
    s_spec = pl.BlockSpec((tm, d), lambda i: (jnp.maximum(i - n_pt, 0), 0))

    def mgp_spec(seg):
        return pl.BlockSpec((1, tm, d), lambda i: (seg, jnp.minimum(i, last), 0))

    def mgs_spec(seg):
        return pl.BlockSpec((1, tm, d), lambda i: (seg, jnp.maximum(i - n_pt, 0), 0))

    return pl.pallas_call(
        functools.partial(_outproj_kernel, n_pt, d),
        grid=(tl.n,),
        in_specs=[p_spec, p_spec, s_spec, s_spec, mgp_spec(0), mgp_spec(1), mgs_spec(6), mgs_spec(7),
                  tl.xp_spec(d), tl.xs_spec(d),
                  tl.adap_spec(2, d), tl.adap_spec(3, d), tl.adap_spec(4, d),
                  tl.adas_spec(2, d), tl.adas_spec(3, d), tl.adas_spec(4, d),
                  _resident((1, 1, d)), _resident((d, d)), _resident((d, d)), _resident((d, d)),
                  _resident((d, e)), _resident((d, e)), _resident((1, e))],
        out_specs=[tl.tok_spec(d), tl.tok_spec(d // 2), tl.tok_spec(TOP_K), tl.tok_spec(TOP_K), tl.tok_spec(TOP_K),
                   pl.BlockSpec((1, e), lambda i: (0, 0))],
        out_shape=[jax.ShapeDtypeStruct((tl.n_tok, d), F32), jax.ShapeDtypeStruct((tl.n_tok, d // 2), jnp.uint32),
                   jax.ShapeDtypeStruct((tl.n_tok, TOP_K), jnp.int32),
                   jax.ShapeDtypeStruct((tl.n_tok, TOP_K), jnp.int32),
                   jax.ShapeDtypeStruct((tl.n_tok, TOP_K), F32),
                   jax.ShapeDtypeStruct((1, e), jnp.int32)],
        scratch_shapes=[pltpu.VMEM((1, e), F32)],
        compiler_params=pltpu.CompilerParams(dimension_semantics=("arbitrary",), vmem_limit_bytes=VMEM_LIMIT),
        name="outproj",
    )(oret_p, ogla_p, oret_s, ogla_s, mg_p, mg_p, proj_s, proj_s, x_p, x_s, ada_p, ada_p, ada_p, ada_s, ada_s, ada_s,
      g_ffn.reshape(1, 1, d), w_ret_o, w_gla_o, w_out, w_r_hi, w_r_lo, b_router.reshape(1, e))


def _expert_kernel(f, te_ref, na_ref, grp_ref, nxt_ref, x_ref, wu_hbm, bu_ref, wd_hbm, bd_ref, y_ref,
                   wu_f, wd_f, wu_s, wd_s, sem):
    j = pl.program_id(0)
    active = j < na_ref[0]
    first = jnp.logical_or(j == 0, te_ref[j] != te_ref[jnp.maximum(j - 1, 0)])
    slot = grp_ref[j] % 2

    def fetch(expert, s):
        return (pltpu.make_async_copy(wu_hbm.at[expert], wu_f.at[s], sem.at[0, s]),
                pltpu.make_async_copy(wd_hbm.at[expert], wd_f.at[s], sem.at[1, s]))

    @pl.when(j == 0)
    def _():
        for c in fetch(te_ref[0], 0):
            c.start()

    @pl.when(jnp.logical_and(active, first))
    def _():
        for c in fetch(te_ref[j], slot):
            c.wait()

        @pl.when(nxt_ref[j] >= 0)
        def _():
            for c in fetch(nxt_ref[j], 1 - slot):
                c.start()

        wu_s[...] = wu_f[slot].astype(BF16)
        wd_s[...] = wd_f[slot].astype(BF16)

    slab = x_ref.shape[0] // EXPERT_ROW_SLABS
    half = x_ref.shape[1]

    @pl.when(active)
    def _():
        for s in range(EXPERT_ROW_SLABS):
            rows = slice(s * slab, (s + 1) * slab)
            x_lo, x_hi = _unpack_pair(x_ref[rows, :])
            gu = _mm(x_lo.astype(BF16), wu_s[:half, :]) + _mm(x_hi.astype(BF16), wu_s[half:, :]) + bu_ref[0]
            gate = jnp.minimum(gu[:, :f], SWIGLU_LIMIT)
            up = jnp.clip(gu[:, f:], -SWIGLU_LIMIT, SWIGLU_LIMIT)
            act = (up + 1.0) * gate * jax.nn.sigmoid(SWIGLU_ALPHA * gate)
            y_ref[rows, :] = _pack_pair(_mm(act.astype(BF16), wd_s[...]) + bd_ref[0])


def _experts(xs, tile_expert, n_active, tile_group, next_expert, w_up, b_up, w_down, b_down, tme):
    r = xs.shape[0]
    e, d, f2 = w_up.shape
    f = f2 // 2
    n_tiles = r // tme

    def row_map(j, te, na, grp, nxt):
        return (jnp.minimum(j, na[0] - 1), 0)

    def b_map(j, te, na, grp, nxt):
        return (te[jnp.minimum(j, na[0] - 1)], 0, 0)

    hbm = pl.BlockSpec(memory_space=pl.ANY)
    return pl.pallas_call(
        functools.partial(_expert_kernel, f),
        grid_spec=pltpu.PrefetchScalarGridSpec(
            num_scalar_prefetch=4,
            grid=(n_tiles,),
            in_specs=[pl.BlockSpec((tme, d // 2), row_map),
                      hbm, pl.BlockSpec((1, 1, f2), b_map), hbm, pl.BlockSpec((1, 1, d), b_map)],
            out_specs=pl.BlockSpec((tme, d // 2), row_map),
            scratch_shapes=[pltpu.VMEM((2, d, f2), F32), pltpu.VMEM((2, f, d), F32),
                            pltpu.VMEM((d, f2), BF16), pltpu.VMEM((f, d), BF16),
                            pltpu.SemaphoreType.DMA((2, 2))]),
        out_shape=jax.ShapeDtypeStruct((r, d // 2), jnp.uint32),
        compiler_params=pltpu.CompilerParams(dimension_semantics=("arbitrary",), vmem_limit_bytes=VMEM_LIMIT),
        name="experts",
    )(tile_expert, n_active, tile_group, next_expert, xs, w_up, b_up.reshape(e, 1, f2), w_down,
      b_down.reshape(e, 1, d))


def _sc_mesh():
    return plsc.VectorSubcoreMesh(core_axis_name="core", subcore_axis_name="subcore")


def _sc_split(n_rows, max_chunk):
    info = plsc.get_sparse_core_info()
    n_workers = info.num_cores * info.num_subcores
    assert n_rows % (8 * n_workers) == 0
    per_w = n_rows // n_workers
    chunk = 8
    while chunk * 2 <= max_chunk and per_w % (chunk * 2) == 0:
        chunk *= 2
    return info.num_cores, n_workers, per_w, chunk


def _sc_dispatch(x, pos, n_rows):
    n, w = x.shape
    nc, nw, per_w, chunk = _sc_split(n, 32)
    n_ch = per_w // chunk
    idx = pos.T.reshape(TOP_K, nw, n_ch, chunk).transpose(1, 0, 2, 3).reshape(nw, TOP_K * n_ch, chunk)

    @functools.partial(
        pl.kernel, out_type=jax.ShapeDtypeStruct((n_rows, w), x.dtype), mesh=_sc_mesh(),
        scratch_types=[pltpu.VMEM((TOP_K * n_ch, chunk), jnp.int32), pltpu.VMEM((2, chunk, w), x.dtype),
                       pltpu.SemaphoreType.DMA((2,)), pltpu.SemaphoreType.DMA((2,))])
    def scatter_rows(x_hbm, i_hbm, o_hbm, idx_v, rows_v, rsem, wsem):
        wid = lax.axis_index("subcore") * nc + lax.axis_index("core")
        base = wid * per_w
        pltpu.sync_copy(i_hbm.at[wid], idx_v)

        def read(j, slot):
            return pltpu.make_async_copy(x_hbm.at[pl.ds(base + j * chunk, chunk)], rows_v.at[slot], rsem.at[slot])

        def write(j, slot, k):
            return pltpu.make_async_copy(rows_v.at[slot], o_hbm.at[idx_v.at[k * n_ch + j]], wsem.at[slot])

        read(0, 0).start()

        @pl.loop(0, n_ch, step=2)
        def _(j0):
            for b in range(2):
                j = j0 + b

                @pl.when(j < n_ch)
                def _():
                    read(j, b).wait()

                    @pl.when(j + 1 < n_ch)
                    def _():
                        @pl.when(j >= 1)
                        def _():
                            for k in range(TOP_K):
                                write(j - 1, 1 - b, k).wait()

                        read(j + 1, 1 - b).start()

                    for k in range(TOP_K):
                        write(j, b, k).start()

        for jj in range(max(n_ch - 2, 0), n_ch):
            for k in range(TOP_K):
                write(jj, jj % 2, k).wait()

    return scatter_rows(x, idx)


def _sc_gather(table, idx):
    m = idx.shape[0]
    w = table.shape[1]
    nc, _, per_w, chunk = _sc_split(m, 64)
    n_ch = per_w // chunk

    @functools.partial(
        pl.kernel, out_type=jax.ShapeDtypeStruct((m, w), table.dtype), mesh=_sc_mesh(),
        scratch_types=[pltpu.VMEM((per_w,), jnp.int32), pltpu.VMEM((2, chunk, w), table.dtype),
                       pltpu.SemaphoreType.DMA((2,)), pltpu.SemaphoreType.DMA((2,))])
    def gather_rows(t_hbm, i_hbm, o_hbm, idx_v, rows_v, gsem, wsem):
        wid = lax.axis_index("subcore") * nc + lax.axis_index("core")
        base = wid * per_w
        pltpu.sync_copy(i_hbm.at[pl.ds(base, per_w)], idx_v)

        def gather(j, slot):
            off = pl.multiple_of(j * chunk, chunk)
            return pltpu.make_async_copy(t_hbm.at[idx_v.at[pl.ds(off, chunk)]], rows_v.at[slot], gsem.at[slot])

        def write(j, slot):
            off = pl.multiple_of(j * chunk, chunk)
            return pltpu.make_async_copy(rows_v.at[slot], o_hbm.at[pl.ds(base + off, chunk)], wsem.at[slot])

        gather(0, 0).start()

        @pl.loop(0, n_ch, step=2)
        def _(j0):
            for b in range(2):
                j = j0 + b

                @pl.when(j < n_ch)
                def _():
                    gather(j, b).wait()

                    @pl.when(j + 1 < n_ch)
                    def _():
                        @pl.when(j >= 1)
                        def _():
                            write(j - 1, 1 - b).wait()

                        gather(j + 1, 1 - b).start()

                    write(j, b).start()

        for jj in range(max(n_ch - 2, 0), n_ch):
            write(jj, jj % 2).wait()

    return gather_rows(table, idx)


def _final_kernel(n_pt, d, h_ref, yg_ref, prob_ref, gtp_ref, gts_ref, g_ref, yp_ref, ys_ref):
    i = pl.program_id(0)
    p = prob_ref[...]
    moe_lo, moe_hi = None, None
    for k in range(TOP_K):
        lo, hi = _unpack_pair(yg_ref[k])
        pk = p[:, k:k + 1]
        moe_lo = pk * lo if moe_lo is None else moe_lo + pk * lo
        moe_hi = pk * hi if moe_hi is None else moe_hi + pk * hi
    moe = jnp.concatenate([moe_lo, moe_hi], axis=1)

    def body(gt, shape):
        h3 = h_ref[...].reshape(shape) + gt * moe.reshape(shape)
        ms = jnp.mean(h3 * h3, axis=-1, keepdims=True)
        return h3 * lax.rsqrt(ms + EPS) * g_ref[...]

    @pl.when(i < n_pt)
    def _():
        yp_ref[...] = body(gtp_ref[0], yp_ref.shape)

    @pl.when(i >= n_pt)
    def _():
        ys_ref[...] = body(gts_ref[0], ys_ref.shape)


def _final(tl, h, yg, probs, ada_p, ada_s, g_final, d):
    return pl.pallas_call(
        functools.partial(_final_kernel, tl.n_pt, d),
        grid=(tl.n,),
        in_specs=[tl.tok_spec(d), pl.BlockSpec((TOP_K, tl.tm, d // 2), lambda i: (0, i, 0)), tl.tok_spec(TOP_K),
                  tl.adap_spec(5, d), tl.adas_spec(5, d), _resident((1, 1, d))],
        out_specs=[tl.xp_spec(d), tl.xs_spec(d)],
        out_shape=[jax.ShapeDtypeStruct((tl.b, tl.t, d), F32), jax.ShapeDtypeStruct((tl.bs, tl.ts, d), F32)],
        compiler_params=pltpu.CompilerParams(dimension_semantics=("arbitrary",), vmem_limit_bytes=VMEM_LIMIT),
        name="final",
    )(h, yg, probs, ada_p, ada_s, g_final.reshape(1, 1, d))


def _pick(n, pref):
    t = min(n, pref)
    while n % t:
        t //= 2
    return t


def _forward(x_prompt, x_sample, c_prompt, c_sample, state_ret, state_gla, w_ada, b_ada, g_norm_mix, g_norm_ffn,
             w_in, w_gk_up, b_gk, g_gla_norm, w_ret_o, w_gla_o, w_out, w_router, b_router, w_up, b_up,
             w_down, b_down, g_final, *, tm, tb, gsz, tme):
    b, t, d = x_prompt.shape
    bs, ts, _ = x_sample.shape
    assert w_ada.shape[0] == 1, "single layer only"
    assert (b * t) % (2 * tb) == 0 and (b * t) % tm == 0
    e = N_EXPERTS
    tl = _Tiles(b, t, bs, ts, tm)
    n_tok = tl.n_tok

    ada = _ada(jnp.concatenate([c_prompt, c_sample], axis=0), w_ada[0], b_ada[0])
    ada_p = ada[:, :b].reshape(6, b, 1, d)
    ada_s = ada[:, b:].reshape(6, bs, 1, d)

    w_in0 = w_in[0]
    n_main = 6 * d
    w_main = jnp.concatenate([w_in0[:, :n_main], w_in0[:, n_main + GLA_GATE_RANK:]], axis=1).astype(BF16)
    w_glr = w_in0[:, n_main:n_main + GLA_GATE_RANK].astype(BF16)
    w_gk = w_gk_up[0].astype(BF16)
    bgk = b_gk[0].reshape(1, -1)
    ggn = g_gla_norm[0].reshape(1, -1)
    w_r = w_router[0]
    w_r_hi = w_r.astype(BF16)
    w_r_lo = (w_r - w_r_hi.astype(F32)).astype(BF16)
    route_w = (g_norm_ffn[0], w_ret_o[0].astype(BF16), w_gla_o[0].astype(BF16), w_out[0].astype(BF16),
               w_r_hi, w_r_lo, b_router[0])

    oret_p, ogla_p, mg_p, sret_p, sgla_p = _front_prompt(x_prompt, ada_p, g_norm_mix[0], w_main, w_glr, tb,
                                                         w_gk, bgk, ggn)
    proj_s, glr_s = _inproj_sample(tl, x_sample, ada_s, g_norm_mix[0], w_main, w_glr)
    oret_s, ogla_s, sret_s, sgla_s = _mix_sample(bs, ts, d, gsz, proj_s, glr_s, state_ret[0], state_gla[0],
                                                 w_gk, bgk, ggn)
    h, n2, idx, rank, probs, counts = _outproj(tl, oret_p, ogla_p, oret_s, ogla_s, mg_p, proj_s, x_prompt, x_sample,
                                               ada_p, ada_s, *route_w)

    counts = counts[0]
    gsize = ((counts + tme - 1) // tme) * tme
    ends = jnp.cumsum(gsize)
    offs = ends - gsize
    experts = jnp.arange(e, dtype=jnp.int32)
    pos = jnp.sum(jnp.where(idx[..., None] == experts, offs, 0), axis=-1) + rank
    max_tiles = (n_tok * TOP_K) // tme + e
    n_active = (ends[-1] // tme).astype(jnp.int32).reshape(1)
    tile_start = jnp.arange(max_tiles, dtype=jnp.int32) * tme
    tile_expert = jnp.minimum(jnp.sum((ends[None, :] <= tile_start[:, None]).astype(jnp.int32), axis=1), e - 1)
    is_first = jnp.logical_and(tile_start < ends[-1],
                               jnp.concatenate([jnp.ones((1,), bool), tile_expert[1:] != tile_expert[:-1]]))
    tile_group = jnp.cumsum(is_first.astype(jnp.int32)) - 1
    later = jnp.logical_and(experts[None, :] > experts[:, None], counts[None, :] > 0)
    next_of = jnp.min(jnp.where(later, experts[None, :], e), axis=1)
    next_of = jnp.where(next_of == e, -1, next_of)
    next_expert = jnp.sum(jnp.where(tile_expert[:, None] == experts, next_of, 0), axis=1).astype(jnp.int32)

    xs = _sc_dispatch(n2, pos, max_tiles * tme)
    ys = _experts(xs, tile_expert, n_active, tile_group, next_expert, w_up[0], b_up[0], w_down[0], b_down[0], tme)
    yg = _sc_gather(ys, pos.T.reshape(-1)).reshape(TOP_K, n_tok, d // 2)

    y_p, y_s = _final(tl, h, yg, probs, ada_p, ada_s, g_final, d)
    return (y_p, y_s, sret_p[None], sgla_p[None], sret_s[None], sgla_s[None])


def kernel(x_prompt, x_sample, c_prompt, c_sample, state_ret, state_gla, w_ada, b_ada, g_norm_mix, g_norm_ffn,
           w_in, w_gk_up, b_gk, g_gla_norm, w_ret_o, w_gla_o, w_out, w_router, b_router, w_up, b_up,
           w_down, b_down, g_final):
    t = x_prompt.shape[1]
    bs, ts = x_sample.shape[0], x_sample.shape[1]
    return _forward(x_prompt, x_sample, c_prompt, c_sample, state_ret, state_gla, w_ada, b_ada, g_norm_mix,
                    g_norm_ffn, w_in, w_gk_up, b_gk, g_gla_norm, w_ret_o, w_gla_o, w_out, w_router, b_router,
                    w_up, b_up, w_down, b_down, g_final,
                    tm=_pick(bs * ts, 512), tb=_pick(t, 256), gsz=_pick(bs, 8), tme=512)
```

```python
import functools

import jax
import jax.numpy as jnp
from jax import lax
from jax.experimental import pallas as pl
from jax.experimental.pallas import tpu as pltpu
from jax.experimental.pallas import tpu_sc as plsc

F32 = jnp.float32
BF16 = jnp.bfloat16

N_HEADS = 4
GLA_GATE_RANK = 16
GLA_GATE_NORM = 16.0
GLA_CHUNK = 64
ROPE_BASE = 10000.0
N_EXPERTS = 32
TOP_K = 4
SWIGLU_LIMIT = 7.0
SWIGLU_ALPHA = 1.702
EPS = 1e-6
PAST_LEN = 16384
N_SEG = 8
EXPERT_ROW_SLABS = 2

VMEM_LIMIT = 56 * 1024 * 1024


def _mm(a, b):
    return jnp.dot(a, b, preferred_element_type=F32)


def _mm_nt(a, b):
    return lax.dot_general(a, b, (((1,), (1,)), ((), ())), preferred_element_type=F32)


def _silu(x):
    return x * jax.nn.sigmoid(x)


def _split_hi_lo(x):
    hi = x.astype(BF16)
    lo = (x - hi.astype(F32)).astype(BF16)
    return hi, lo


def _pack_pair(x):
    w = x.shape[1] // 2
    lo = lax.bitcast_convert_type(x[:, :w].astype(BF16).astype(F32), jnp.uint32)
    hi = lax.bitcast_convert_type(x[:, w:].astype(BF16).astype(F32), jnp.uint32)
    return (hi & jnp.uint32(0xFFFF0000)) | (lo >> 16)


def _unpack_pair(p):
    lo = lax.bitcast_convert_type(p << 16, F32)
    hi = lax.bitcast_convert_type(p & jnp.uint32(0xFFFF0000), F32)
    return lo, hi


def _rms_mod(x3, g, sc, sh):
    ms = jnp.mean(x3 * x3, axis=-1, keepdims=True)
    return x3 * lax.rsqrt(ms + EPS) * g * (1.0 + sc) + sh


def _resident(shape):
    zeros = (0,) * len(shape)
    return pl.BlockSpec(shape, lambda i: zeros, pipeline_mode=pl.Buffered(1))


def _const(shape):
    zeros = (0,) * len(shape)
    return pl.BlockSpec(shape, lambda i: zeros)


def _ada_kernel(c_ref, w_ref, b_ref, o_ref):
    cf = _silu(c_ref[...])
    o_ref[0] = _mm(cf.astype(BF16), w_ref[...].astype(BF16)) + b_ref[0]


def _ada(c_all, w_ada, b_ada):
    bc, d = c_all.shape
    n = w_ada.shape[1] // d
    return pl.pallas_call(
        _ada_kernel,
        grid=(n,),
        in_specs=[pl.BlockSpec((bc, d), lambda j: (0, 0)),
                  pl.BlockSpec((d, d), lambda j: (0, j)),
                  pl.BlockSpec((1, 1, d), lambda j: (j, 0, 0))],
        out_specs=pl.BlockSpec((1, bc, d), lambda j: (j, 0, 0)),
        out_shape=jax.ShapeDtypeStruct((n, bc, d), F32),
        compiler_params=pltpu.CompilerParams(dimension_semantics=("arbitrary",), vmem_limit_bytes=VMEM_LIMIT),
        name="ada",
    )(c_all, w_ada, b_ada.reshape(n, 1, d))


class _Tiles:
    def __init__(self, b, t, bs, ts, tm):
        assert t % tm == 0 and (bs * ts) % tm == 0 and tm % ts == 0
        self.b, self.t, self.bs, self.ts, self.tm = b, t, bs, ts, tm
        self.tpb = t // tm
        self.n_pt = b * self.tpb
        self.gs = tm // ts
        self.n_st = (bs * ts) // tm
        self.n = self.n_pt + self.n_st
        self.n_tok = b * t + bs * ts

    def xp_spec(self, d):
        last, tpb = self.n_pt - 1, self.tpb
        return pl.BlockSpec((1, self.tm, d), lambda i: (jnp.minimum(i, last) // tpb, jnp.minimum(i, last) % tpb, 0))

    def xs_spec(self, d):
        n_pt = self.n_pt
        return pl.BlockSpec((self.gs, self.ts, d), lambda i: (jnp.maximum(i - n_pt, 0), 0, 0))

    def adap_spec(self, which, d):
        last, tpb = self.n_pt - 1, self.tpb
        return pl.BlockSpec((1, 1, 1, d), lambda i: (which, jnp.minimum(i, last) // tpb, 0, 0))

    def adas_spec(self, which, d):
        n_pt = self.n_pt
        return pl.BlockSpec((1, self.gs, 1, d), lambda i: (which, jnp.maximum(i - n_pt, 0), 0, 0))

    def tok_spec(self, width):
        return pl.BlockSpec((self.tm, width), lambda i: (i, 0))

    def s_x_spec(self, d):
        return pl.BlockSpec((self.gs, self.ts, d), lambda i: (i, 0, 0))

    def s_ada_spec(self, which, d):
        return pl.BlockSpec((1, self.gs, 1, d), lambda i: (which, i, 0, 0))

    def s_row_spec(self, width):
        return pl.BlockSpec((self.tm, width), lambda i: (i, 0))

    def s_tok_spec(self, width):
        n_pt = self.n_pt
        return pl.BlockSpec((self.tm, width), lambda i: (n_pt + i, 0))


def _rope_tables(pos0, t, dk):
    half = dk // 2
    inv = ROPE_BASE ** (-jnp.arange(half, dtype=jnp.float32) / half)
    pos = pos0 + jnp.arange(t)
    ang = pos.astype(jnp.float32)[:, None] * inv[None, :]
    cos, sin = jnp.cos(ang), jnp.sin(ang)
    return jnp.concatenate([cos, cos], axis=-1), jnp.concatenate([-sin, sin], axis=-1)


def _ret_tables(c, dk, dv):
    h = N_HEADS
    log_gamma = jnp.log1p(-jnp.exp2(-5.0 - jnp.arange(h, dtype=jnp.float32)))
    idx = jnp.arange(c, dtype=jnp.float32)
    rel = idx[:, None] - idx[None, :]
    dmask = jnp.where(rel >= 0, jnp.exp(log_gamma[:, None, None] * jnp.maximum(rel, 0.0)), 0.0)
    kdec = jnp.exp(log_gamma[:, None] * (c - 1 - idx))
    qdec = jnp.exp(log_gamma[:, None] * (idx + 1.0))
    cdec = jnp.exp(log_gamma * c)
    return (dmask,
            jnp.broadcast_to(qdec[:, :, None], (h, c, dk)),
            jnp.broadcast_to(kdec[:, :, None], (h, c, dk)),
            jnp.broadcast_to(cdec[:, None, None], (h, 1, dv)))


def _rot(x, cos_f, sin_f):
    return x * cos_f + pltpu.roll(x, x.shape[-1] // 2, 1) * sin_f


def _cross_and_update(q_lhs, k_end, vh, states, masks):
    if masks is None:
        (s,) = states
        return _mm(q_lhs, s.astype(BF16)), [_mm(k_end.T.astype(BF16), vh)]
    cross, incs = None, []
    for s, m in zip(states, masks):
        c = _mm(q_lhs, s.astype(BF16))
        cross = c if cross is None else jnp.where(m, c, cross)
        incs.append(_mm(jnp.where(m, k_end, 0.0).T.astype(BF16), vh))
    return cross, incs


def _ret_head(q, k, vh, gh, states, masks, cos_f, sin_f, dmask, qdec, kdec, cdec):
    dk = q.shape[-1]
    q = _rot(q, cos_f, sin_f)
    k = _rot(k, cos_f, sin_f) * (dk ** -0.5)
    scores = _mm_nt(q.astype(BF16), k.astype(BF16)) * dmask
    cross, incs = _cross_and_update((q * qdec).astype(BF16), k * kdec, vh, states, masks)
    o = _mm(scores.astype(BF16), vh) + cross
    new_states = [cdec * s + u for s, u in zip(states, incs)]
    mu = jnp.mean(o, axis=-1, keepdims=True)
    oc = o - mu
    var = jnp.mean(oc * oc, axis=-1, keepdims=True)
    return _silu(gh) * (oc * lax.rsqrt(var + EPS)), new_states


def _gla_head(q, k, vh, gh, b, states, masks, c, gnorm, causal):
    dk = q.shape[-1]
    b_t = b.T
    if masks is None:
        b_last = b[c - 1:c, :]
    else:
        b_last = None
        for g, m in enumerate(masks):
            row = b[g * c + c - 1:g * c + c, :]
            b_last = row if b_last is None else jnp.where(m, row, b_last)
    q_in = (q * (dk ** -0.5) * jnp.exp(b)).astype(BF16)
    k_in = (k * jnp.exp(-b)).astype(BF16)
    scores = jnp.where(causal, _mm_nt(q_in, k_in), 0.0)
    cross, incs = _cross_and_update(q_in, k * jnp.exp(b_last - b), vh, states, masks)
    o = _mm(scores.astype(BF16), vh) + cross
    new_states = [jnp.exp(b_t[:, g * c + c - 1:g * c + c]) * s + u for g, (s, u) in enumerate(zip(states, incs))]
    o = o * lax.rsqrt(jnp.mean(o * o, axis=-1, keepdims=True) + EPS) * gnorm
    return _silu(gh) * o, new_states


def _log_a(glr, wgk, bgk):
    z = _mm(glr.astype(BF16), wgk) + bgk
    return (jnp.minimum(z, 0.0) - jnp.log1p(jnp.exp(-jnp.abs(z)))) / GLA_GATE_NORM


def _causal(c):
    return lax.broadcasted_iota(jnp.int32, (c, c), 0) >= lax.broadcasted_iota(jnp.int32, (c, c), 1)


def _w_seg(w_ref, wm_ref, seg, d):
    if seg < N_SEG - 2:
        return w_ref[:, seg * d:(seg + 1) * d]
    return wm_ref[:, (seg - (N_SEG - 2)) * d:(seg - (N_SEG - 3)) * d]


def _proj_block(d, x3, sh, sc, g, w_ref, wm_ref, wl_ref, proj_s, glr_s):
    n = _rms_mod(x3, g, sc, sh).reshape(-1, d).astype(BF16)
    for seg in range(N_SEG):
        proj_s[:, seg * d:(seg + 1) * d] = _mm(n, _w_seg(w_ref, wm_ref, seg, d)).astype(BF16)
    glr_s[...] = _mm(n, wl_ref[...])


def _mix_block(d, tb, proj_s, glr_s, cos_f, sin_f, dmask_ref, qdec_ref, kdec_ref, cdec_ref, tri, wgk, bgk, gnorm,
               sr_s, sg_s, oret_ref, ogla_ref, mg_ref, r_off):
    dk, dv, hq = d // 8, d // 4, d // 2
    rqk, rv, rg, gqk, gv, gg, mg = (i * d for i in range(7))
    for h in range(N_HEADS):
        o, (s_new,) = _ret_head(proj_s[:, rqk + h * dk:rqk + (h + 1) * dk].astype(F32),
                                proj_s[:, rqk + hq + h * dk:rqk + hq + (h + 1) * dk].astype(F32),
                                proj_s[:, rv + h * dv:rv + (h + 1) * dv],
                                proj_s[:, rg + h * dv:rg + (h + 1) * dv].astype(F32),
                                [sr_s[h]], None, cos_f, sin_f, dmask_ref[h], qdec_ref[h], kdec_ref[h],
                                cdec_ref[h])
        sr_s[h] = s_new
        oret_ref[r_off:r_off + tb, h * dv:(h + 1) * dv] = o.astype(BF16)

    la_hi, la_lo = _split_hi_lo(_log_a(glr_s[...], wgk, bgk))
    b = _mm(tri, la_hi) + _mm(tri, la_lo)
    cg = GLA_CHUNK
    causal = _causal(cg)
    for c in range(tb // cg):
        r0, r1 = c * cg, (c + 1) * cg
        for h in range(N_HEADS):
            o, (s_new,) = _gla_head(proj_s[r0:r1, gqk + h * dk:gqk + (h + 1) * dk].astype(F32),
                                    proj_s[r0:r1, gqk + hq + h * dk:gqk + hq + (h + 1) * dk].astype(F32),
                                    proj_s[r0:r1, gv + h * dv:gv + (h + 1) * dv],
                                    proj_s[r0:r1, gg + h * dv:gg + (h + 1) * dv].astype(F32),
                                    b[r0:r1, h * dk:(h + 1) * dk], [sg_s[h]], None, cg, gnorm, causal)
            sg_s[h] = s_new
            ogla_ref[r_off + r0:r_off + r1, h * dv:(h + 1) * dv] = o.astype(BF16)
    mg_ref[0, r_off:r_off + tb, :] = proj_s[:, mg:mg + d]
    mg_ref[1, r_off:r_off + tb, :] = proj_s[:, mg + d:mg + 2 * d]


def _route_block(d, out_ret, out_gla, mg_ret, mg_gla, x3, gt, sh, sc, g, wro_ref, wgo_ref, wo_ref, wrh_ref, wrl_ref,
                 br, carry_s, h_ref, n2_ref, idx_ref, rank_ref, prob_ref, r_off):
    rows = out_ret.shape[0]
    e = N_EXPERTS
    a = _mm(out_ret, wro_ref[...])
    b = _mm(out_gla, wgo_ref[...])
    merged = jax.nn.sigmoid(mg_ret.astype(F32)) * a + jax.nn.sigmoid(mg_gla.astype(F32)) * b
    mix = _mm(merged.astype(BF16), wo_ref[...])
    h3 = x3 + gt * mix.reshape(x3.shape)
    out_rows = slice(r_off, r_off + rows)
    h_ref[out_rows, :] = h3.reshape(rows, d)
    n2 = _rms_mod(h3, g, sc, sh).reshape(rows, d)
    n2_ref[out_rows, :] = _pack_pair(n2)

    n_hi, n_lo = _split_hi_lo(n2)
    logits = _mm(n_hi, wrh_ref[...]) + _mm(n_lo, wrh_ref[...]) + _mm(n_hi, wrl_ref[...]) + br
    iota = lax.broadcasted_iota(jnp.int32, (rows, e), 1)
    work = logits
    vals, idxs = [], []
    for _ in range(TOP_K):
        m = jnp.max(work, axis=-1, keepdims=True)
        ik = jnp.min(jnp.where(work == m, iota, e), axis=-1, keepdims=True)
        vals.append(m)
        idxs.append(ik)
        work = jnp.where(iota == ik, -jnp.inf, work)
    ex = [jnp.exp(v - vals[0]) for v in vals]
    den = ex[0] + ex[1] + ex[2] + ex[3]

    onehot = jnp.zeros((rows, e), F32)
    for ik in idxs:
        onehot = onehot + (iota == ik).astype(F32)
    ltri = (lax.broadcasted_iota(jnp.int32, (rows, rows), 0) > lax.broadcasted_iota(jnp.int32, (rows, rows), 1))
    cum = _mm(ltri.astype(F32).astype(BF16), onehot.astype(BF16)) + carry_s[...]
    lane = lax.broadcasted_iota(jnp.int32, (rows, TOP_K), 1)
    idx_o = jnp.zeros((rows, TOP_K), jnp.int32)
    rank_o = jnp.zeros((rows, TOP_K), jnp.int32)
    prob_o = jnp.zeros((rows, TOP_K), F32)
    for k in range(TOP_K):
        rk = jnp.sum(jnp.where(iota == idxs[k], cum, 0.0), axis=-1, keepdims=True).astype(jnp.int32)
        idx_o = jnp.where(lane == k, idxs[k], idx_o)
        rank_o = jnp.where(lane == k, rk, rank_o)
        prob_o = jnp.where(lane == k, ex[k] / den, prob_o)
    idx_ref[out_rows, :] = idx_o
    rank_ref[out_rows, :] = rank_o
    prob_ref[out_rows, :] = prob_o
    carry_s[...] = carry_s[...] + jnp.sum(onehot, axis=0, keepdims=True)


def _frontp_kernel(d, tb, ntb, x0_ref, xa_ref, xb_ref, sh0_ref, sc0_ref, sha_ref, sca_ref, shb_ref, scb_ref,
                   g_ref, w_ref, wm_ref, wl_ref, cosa_ref, sina_ref, cosb_ref, sinb_ref,
                   dmask_ref, qdec_ref, kdec_ref, cdec_ref, tri_ref, wgk_ref, bgk_ref, gn_ref,
                   oret_ref, ogla_ref, mg_ref, sret_ref, sgla_ref, pa_s, pb_s, ga_s, gb_s, sr_s, sg_s):
    p = pl.program_id(0)
    blk = 2 * p
    g = g_ref[...]
    proj = functools.partial(_proj_block, d)
    mix = functools.partial(_mix_block, d, tb)
    tables = (dmask_ref, qdec_ref, kdec_ref, cdec_ref, tri_ref[...], wgk_ref[...], bgk_ref[...], gn_ref[...])

    @pl.when(p == 0)
    def _():
        proj(x0_ref[...], sh0_ref[0], sc0_ref[0], g, w_ref, wm_ref, wl_ref, pa_s, ga_s)

    @pl.when(blk % ntb == 0)
    def _():
        sr_s[...] = jnp.zeros_like(sr_s)
        sg_s[...] = jnp.zeros_like(sg_s)

    proj(xa_ref[...], sha_ref[0], sca_ref[0], g, w_ref, wm_ref, wl_ref, pb_s, gb_s)
    mix(pa_s, ga_s, cosa_ref[...], sina_ref[...], *tables, sr_s, sg_s, oret_ref, ogla_ref, mg_ref, 0)
    proj(xb_ref[...], shb_ref[0], scb_ref[0], g, w_ref, wm_ref, wl_ref, pa_s, ga_s)
    mix(pb_s, gb_s, cosb_ref[...], sinb_ref[...], *tables, sr_s, sg_s, oret_ref, ogla_ref, mg_ref, tb)

    @pl.when((blk + 1) % ntb == ntb - 1)
    def _():
        sret_ref[0] = sr_s[...]
        sgla_ref[0] = sg_s[...]


def _chunk_tri(tb, cg):
    i = jnp.arange(tb)
    return ((i[:, None] >= i[None, :]) & (i[:, None] // cg == i[None, :] // cg)).astype(BF16)


def _front_prompt(x_p, ada_p, g_mix, w_main, w_mg, w_glr, tb, w_gk, b_gk, g_gla):
    b, t, d = x_p.shape
    dk, dv, hq, h = d // 8, d // 4, d // 2, N_HEADS
    ntb = t // tb
    n_blk = b * ntb
    n_tok = b * t
    assert ntb % 2 == 0
    cos_f, sin_f = _rope_tables(0, t, dk)
    dmask, qdec, kdec, cdec = _ret_tables(tb, dk, dv)
    tri = _chunk_tri(tb, GLA_CHUNK)

    def first(p):
        return 0 * p

    def even(p):
        return 2 * p

    def odd(p):
        return 2 * p + 1

    def nxt(p):
        return jnp.minimum(2 * p + 2, n_blk - 1)

    def x_spec(blk_of):
        return pl.BlockSpec((1, tb, d), lambda p: (blk_of(p) // ntb, blk_of(p) % ntb, 0))

    def ada_spec(which, blk_of):
        return pl.BlockSpec((1, 1, 1, d), lambda p: (which, blk_of(p) // ntb, 0, 0))

    def rope_spec(blk_of):
        return pl.BlockSpec((tb, dk), lambda p: (blk_of(p) % ntb, 0))

    state_spec = pl.BlockSpec((1, h, dk, dv), lambda p: ((2 * p) // ntb, 0, 0, 0))
    tok_spec = pl.BlockSpec((2 * tb, d), lambda p: (p, 0))
    return pl.pallas_call(
        functools.partial(_frontp_kernel, d, tb, ntb),
        grid=(n_blk // 2,),
        in_specs=[_resident((1, tb, d)), x_spec(odd), x_spec(nxt),
                  ada_spec(0, first), ada_spec(1, first), ada_spec(0, odd), ada_spec(1, odd),
                  ada_spec(0, nxt), ada_spec(1, nxt),
                  _resident((1, 1, d)), _resident((d, (N_SEG - 2) * d)), _resident((d, 2 * d)),
                  _resident((d, GLA_GATE_RANK)),
                  rope_spec(even), rope_spec(even), rope_spec(odd), rope_spec(odd),
                  _resident((h, tb, tb)), _resident((h, tb, dk)), _resident((h, tb, dk)), _resident((h, 1, dv)),
                  _resident((tb, tb)), _resident((GLA_GATE_RANK, hq)), _resident((1, hq)), _resident((1, dv))],
        out_specs=[tok_spec, tok_spec, pl.BlockSpec((2, 2 * tb, d), lambda p: (0, p, 0)), state_spec, state_spec],
        out_shape=[jax.ShapeDtypeStruct((n_tok, d), BF16), jax.ShapeDtypeStruct((n_tok, d), BF16),
                   jax.ShapeDtypeStruct((2, n_tok, d), BF16),
                   jax.ShapeDtypeStruct((b, h, dk, dv), F32), jax.ShapeDtypeStruct((b, h, dk, dv), F32)],
        scratch_shapes=[pltpu.VMEM((tb, N_SEG * d), BF16), pltpu.VMEM((tb, N_SEG * d), BF16),
                        pltpu.VMEM((tb, GLA_GATE_RANK), F32), pltpu.VMEM((tb, GLA_GATE_RANK), F32),
                        pltpu.VMEM((h, dk, dv), F32), pltpu.VMEM((h, dk, dv), F32)],
        compiler_params=pltpu.CompilerParams(dimension_semantics=("arbitrary",), vmem_limit_bytes=VMEM_LIMIT),
        name="front_prompt",
    )(x_p, x_p, x_p, ada_p, ada_p, ada_p, ada_p, ada_p, ada_p, g_mix.reshape(1, 1, d), w_main, w_mg, w_glr,
      cos_f, sin_f, cos_f, sin_f, dmask, qdec, kdec, cdec, tri, w_gk, b_gk, g_gla)


def _inproj_kernel(d, xs_ref, shs_ref, scs_ref, g_ref, w_ref, wm_ref, wl_ref, proj_ref, glr_ref):
    n = _rms_mod(xs_ref[...], g_ref[...], scs_ref[0], shs_ref[0]).reshape(-1, d).astype(BF16)
    for s in range(N_SEG):
        proj_ref[s] = _mm(n, _w_seg(w_ref, wm_ref, s, d)).astype(BF16)
    glr_ref[...] = _mm(n, wl_ref[...])


def _inproj_sample(tl, x_s, ada_s, g_mix, w_main, w_mg, w_glr):
    bs, ts, d = x_s.shape
    n_tok = bs * ts
    return pl.pallas_call(
        functools.partial(_inproj_kernel, d),
        grid=(tl.n_st,),
        in_specs=[tl.s_x_spec(d), tl.s_ada_spec(0, d), tl.s_ada_spec(1, d),
                  _resident((1, 1, d)), _resident((d, (N_SEG - 2) * d)), _resident((d, 2 * d)),
                  _resident((d, GLA_GATE_RANK))],
        out_specs=[pl.BlockSpec((N_SEG, tl.tm, d), lambda i: (0, i, 0)), tl.s_row_spec(GLA_GATE_RANK)],
        out_shape=[jax.ShapeDtypeStruct((N_SEG, n_tok, d), BF16), jax.ShapeDtypeStruct((n_tok, GLA_GATE_RANK), F32)],
        compiler_params=pltpu.CompilerParams(dimension_semantics=("arbitrary",), vmem_limit_bytes=VMEM_LIMIT),
        name="inproj_sample",
    )(x_s, ada_s, ada_s, g_mix.reshape(1, 1, d), w_main, w_mg, w_glr)


def _mixs_kernel(d, ts, gsz, rqk_ref, rv_ref, rg_ref, gqk_ref, gv_ref, gg_ref, glr_ref, cos_ref, sin_ref,
                 dmask_ref, qdec_ref, kdec_ref, cdec_ref, wgk_ref, bgk_ref, gn_ref, sr_in, sg_in,
                 oret_ref, ogla_ref, sr_out, sg_out):
    dk, dv, hq = d // 8, d // 4, d // 2
    pair = 2 * ts
    cos_f, sin_f = cos_ref[...], sin_ref[...]
    gnorm = gn_ref[...]
    ri = lax.broadcasted_iota(jnp.int32, (pair, pair), 0)
    ci = lax.broadcasted_iota(jnp.int32, (pair, pair), 1)
    causal = jnp.logical_and(ri >= ci, (ri < ts) == (ci < ts))
    tri = causal.astype(F32).astype(BF16)
    first = lax.broadcasted_iota(jnp.int32, (pair, 1), 0) < ts
    masks = [first, jnp.logical_not(first)]

    def body(j, carry):
        rows = pl.ds(pl.multiple_of(j * pair, pair), pair)
        s0, s1 = 2 * j, 2 * j + 1
        la_hi, la_lo = _split_hi_lo(_log_a(glr_ref[rows, :], wgk_ref[...], bgk_ref[...]))
        b = _mm(tri, la_hi) + _mm(tri, la_lo)
        for h in range(N_HEADS):
            o, (n0, n1) = _ret_head(rqk_ref[0, rows, h * dk:(h + 1) * dk].astype(F32),
                                    rqk_ref[0, rows, hq + h * dk:hq + (h + 1) * dk].astype(F32),
                                    rv_ref[0, rows, h * dv:(h + 1) * dv],
                                    rg_ref[0, rows, h * dv:(h + 1) * dv].astype(F32),
                                    [sr_in[s0, h], sr_in[s1, h]], masks, cos_f, sin_f,
                                    dmask_ref[h], qdec_ref[h], kdec_ref[h], cdec_ref[h])
            sr_out[s0, h] = n0
            sr_out[s1, h] = n1
            oret_ref[rows, h * dv:(h + 1) * dv] = o.astype(BF16)
            o, (n0, n1) = _gla_head(gqk_ref[0, rows, h * dk:(h + 1) * dk].astype(F32),
                                    gqk_ref[0, rows, hq + h * dk:hq + (h + 1) * dk].astype(F32),
                                    gv_ref[0, rows, h * dv:(h + 1) * dv],
                                    gg_ref[0, rows, h * dv:(h + 1) * dv].astype(F32),
                                    b[:, h * dk:(h + 1) * dk], [sg_in[s0, h], sg_in[s1, h]], masks, ts,
                                    gnorm, causal)
            sg_out[s0, h] = n0
            sg_out[s1, h] = n1
            ogla_ref[rows, h * dv:(h + 1) * dv] = o.astype(BF16)
        return carry

    lax.fori_loop(0, gsz // 2, body, 0, unroll=2)


def _pair_tables(ts, dk, dv):
    cos_f, sin_f = _rope_tables(PAST_LEN, ts, dk)
    dmask, qdec, kdec, cdec = _ret_tables(ts, dk, dv)
    zero = jnp.zeros_like(dmask)
    dmask2 = jnp.concatenate([jnp.concatenate([dmask, zero], axis=2), jnp.concatenate([zero, dmask], axis=2)], axis=1)

    def twice(a, axis):
        return jnp.concatenate([a, a], axis=axis)

    return twice(cos_f, 0), twice(sin_f, 0), dmask2, twice(qdec, 1), twice(kdec, 1), cdec


def _mix_sample(bs, ts, d, gsz, proj, glr, state_ret, state_gla, w_gk, b_gk, g_gla):
    dk, dv, hq, h = d // 8, d // 4, d // 2, N_HEADS
    assert GLA_CHUNK % ts == 0 and bs % gsz == 0 and gsz % 4 == 0
    rows = gsz * ts
    pair = 2 * ts
    cos_f, sin_f, dmask, qdec, kdec, cdec = _pair_tables(ts, dk, dv)

    def seg(s):
        return pl.BlockSpec((1, rows, d), lambda i: (s, i, 0))

    state_spec = pl.BlockSpec((gsz, h, dk, dv), lambda i: (i, 0, 0, 0))
    tok_spec = pl.BlockSpec((rows, d), lambda i: (i, 0))
    return pl.pallas_call(
        functools.partial(_mixs_kernel, d, ts, gsz),
        grid=(bs // gsz,),
        in_specs=[seg(0), seg(1), seg(2), seg(3), seg(4), seg(5),
                  pl.BlockSpec((rows, GLA_GATE_RANK), lambda i: (i, 0)),
                  _const((pair, dk)), _const((pair, dk)),
                  _const((h, pair, pair)), _const((h, pair, dk)), _const((h, pair, dk)), _const((h, 1, dv)),
                  _const((GLA_GATE_RANK, hq)), _const((1, hq)), _const((1, dv)),
                  state_spec, state_spec],
        out_specs=[tok_spec, tok_spec, state_spec, state_spec],
        out_shape=[jax.ShapeDtypeStruct((bs * ts, d), BF16), jax.ShapeDtypeStruct((bs * ts, d), BF16),
                   jax.ShapeDtypeStruct((bs, h, dk, dv), F32), jax.ShapeDtypeStruct((bs, h, dk, dv), F32)],
        compiler_params=pltpu.CompilerParams(dimension_semantics=("arbitrary",), vmem_limit_bytes=VMEM_LIMIT),
        name="mix_sample",
    )(proj, proj, proj, proj, proj, proj, glr, cos_f, sin_f, dmask, qdec, kdec, cdec, w_gk, b_gk, g_gla,
      state_ret, state_gla)


def _outproj_kernel(n_pt, d, orp_ref, ogp_ref, ors_ref, ogs_ref, mgrp_ref, mggp_ref, mgrs_ref, mggs_ref,
                    xp_ref, xs_ref, gtp_ref, shp_ref, scp_ref, gts_ref, shs_ref, scs_ref, g_ref,
                    wro_ref, wgo_ref, wo_ref, wrh_ref, wrl_ref, br_ref,
                    h_ref, n2_ref, idx_ref, rank_ref, prob_ref, cnt_ref, carry_s):
    i = pl.program_id(0)

    @pl.when(i == 0)
    def _():
        carry_s[...] = jnp.zeros_like(carry_s)

    route = functools.partial(_route_block, d)
    tail = (g_ref[...], wro_ref, wgo_ref, wo_ref, wrh_ref, wrl_ref, br_ref[...], carry_s,
            h_ref, n2_ref, idx_ref, rank_ref, prob_ref, 0)

    @pl.when(i < n_pt)
    def _():
        route(orp_ref[...], ogp_ref[...], mgrp_ref[0], mggp_ref[0], xp_ref[...], gtp_ref[0], shp_ref[0], scp_ref[0],
              *tail)

    @pl.when(i >= n_pt)
    def _():
        route(ors_ref[...], ogs_ref[...], mgrs_ref[0], mggs_ref[0], xs_ref[...], gts_ref[0], shs_ref[0], scs_ref[0],
              *tail)

    @pl.when(i == pl.num_programs(0) - 1)
    def _():
        cnt_ref[...] = carry_s[...].astype(jnp.int32)


def _outproj(tl, oret_p, ogla_p, oret_s, ogla_s, mg_p, proj_s, x_p, x_s, ada_p, ada_s, g_ffn,
             w_ret_o, w_gla_o, w_out, w_r_hi, w_r_lo, b_router):
    d = x_p.shape[-1]
    tm, e, n_pt = tl.tm, N_EXPERTS, tl.n_pt
    last = n_pt - 1
    p_spec = pl.BlockSpec((tm, d), lambda i: (jnp.minimum(i, last), 0))
    s_spec = pl.BlockSpec((tm, d), lambda i: (jnp.maximum(i - n_pt, 0), 0))

    def mgp_spec(seg):
        return pl.BlockSpec((1, tm, d), lambda i: (seg, jnp.minimum(i, last), 0))

    def mgs_spec(seg):
        return pl.BlockSpec((1, tm, d), lambda i: (seg, jnp.maximum(i - n_pt, 0), 0))

    return pl.pallas_call(
        functools.partial(_outproj_kernel, n_pt, d),
        grid=(tl.n,),
        in_specs=[p_spec, p_spec, s_spec, s_spec, mgp_spec(0), mgp_spec(1), mgs_spec(6), mgs_spec(7),
                  tl.xp_spec(d), tl.xs_spec(d),
                  tl.adap_spec(2, d), tl.adap_spec(3, d), tl.adap_spec(4, d),
                  tl.adas_spec(2, d), tl.adas_spec(3, d), tl.adas_spec(4, d),
                  _resident((1, 1, d)), _resident((d, d)), _resident((d, d)), _resident((d, d)),
                  _resident((d, e)), _resident((d, e)), _resident((1, e))],
        out_specs=[tl.tok_spec(d), tl.tok_spec(d // 2), tl.tok_spec(TOP_K), tl.tok_spec(TOP_K), tl.tok_spec(TOP_K),
                   pl.BlockSpec((1, e), lambda i: (0, 0))],
        out_shape=[jax.ShapeDtypeStruct((tl.n_tok, d), F32), jax.ShapeDtypeStruct((tl.n_tok, d // 2), jnp.uint32),
                   jax.ShapeDtypeStruct((tl.n_tok, TOP_K), jnp.int32),
                   jax.ShapeDtypeStruct((tl.n_tok, TOP_K), jnp.int32),
                   jax.ShapeDtypeStruct((tl.n_tok, TOP_K), F32),
                   jax.ShapeDtypeStruct((1, e), jnp.int32)],
        scratch_shapes=[pltpu.VMEM((1, e), F32)],
        compiler_params=pltpu.CompilerParams(dimension_semantics=("arbitrary",), vmem_limit_bytes=VMEM_LIMIT),
        name="outproj",
    )(oret_p, ogla_p, oret_s, ogla_s, mg_p, mg_p, proj_s, proj_s, x_p, x_s, ada_p, ada_p, ada_p, ada_s, ada_s, ada_s,
      g_ffn.reshape(1, 1, d), w_ret_o, w_gla_o, w_out, w_r_hi, w_r_lo, b_router.reshape(1, e))


def _expert_kernel(f, te_ref, na_ref, grp_ref, nxt_ref, x_ref, wu_hbm, bu_ref, wd_hbm, bd_ref, y_ref,
                   wu_f, wd_f, wu_s, wd_s, sem):
    j = pl.program_id(0)
    active = j < na_ref[0]
    first = jnp.logical_or(j == 0, te_ref[j] != te_ref[jnp.maximum(j - 1, 0)])
    slot = grp_ref[j] % 2

    def fetch(expert, s):
        return (pltpu.make_async_copy(wu_hbm.at[expert], wu_f.at[s], sem.at[0, s]),
                pltpu.make_async_copy(wd_hbm.at[expert], wd_f.at[s], sem.at[1, s]))

    @pl.when(j == 0)
    def _():
        for c in fetch(te_ref[0], 0):
            c.start()

    @pl.when(jnp.logical_and(active, first))
    def _():
        for c in fetch(te_ref[j], slot):
            c.wait()

        @pl.when(nxt_ref[j] >= 0)
        def _():
            for c in fetch(nxt_ref[j], 1 - slot):
                c.start()

        wu_s[...] = wu_f[slot].astype(BF16)
        wd_s[...] = wd_f[slot].astype(BF16)

    slab = x_ref.shape[0] // EXPERT_ROW_SLABS
    half = x_ref.shape[1]

    @pl.when(active)
    def _():
        for s in range(EXPERT_ROW_SLABS):
            rows = slice(s * slab, (s + 1) * slab)
            x_lo, x_hi = _unpack_pair(x_ref[rows, :])
            gu = _mm(x_lo.astype(BF16), wu_s[:half, :]) + _mm(x_hi.astype(BF16), wu_s[half:, :]) + bu_ref[0]
            gate = jnp.minimum(gu[:, :f], SWIGLU_LIMIT)
            up = jnp.clip(gu[:, f:], -SWIGLU_LIMIT, SWIGLU_LIMIT)
            act = (up + 1.0) * gate * jax.nn.sigmoid(SWIGLU_ALPHA * gate)
            y_ref[rows, :] = _pack_pair(_mm(act.astype(BF16), wd_s[...]) + bd_ref[0])


def _experts(xs, tile_expert, n_active, tile_group, next_expert, w_up, b_up, w_down, b_down, tme):
    r = xs.shape[0]
    e, d, f2 = w_up.shape
    f = f2 // 2
    n_tiles = r // tme

    def row_map(j, te, na, grp, nxt):
        return (jnp.minimum(j, na[0] - 1), 0)

    def b_map(j, te, na, grp, nxt):
        return (te[jnp.minimum(j, na[0] - 1)], 0, 0)

    hbm = pl.BlockSpec(memory_space=pl.ANY)
    return pl.pallas_call(
        functools.partial(_expert_kernel, f),
        grid_spec=pltpu.PrefetchScalarGridSpec(
            num_scalar_prefetch=4,
            grid=(n_tiles,),
            in_specs=[pl.BlockSpec((tme, d // 2), row_map),
                      hbm, pl.BlockSpec((1, 1, f2), b_map), hbm, pl.BlockSpec((1, 1, d), b_map)],
            out_specs=pl.BlockSpec((tme, d // 2), row_map),
            scratch_shapes=[pltpu.VMEM((2, d, f2), F32), pltpu.VMEM((2, f, d), F32),
                            pltpu.VMEM((d, f2), BF16), pltpu.VMEM((f, d), BF16),
                            pltpu.SemaphoreType.DMA((2, 2))]),
        out_shape=jax.ShapeDtypeStruct((r, d // 2), jnp.uint32),
        compiler_params=pltpu.CompilerParams(dimension_semantics=("arbitrary",), vmem_limit_bytes=VMEM_LIMIT),
        name="experts",
    )(tile_expert, n_active, tile_group, next_expert, xs, w_up, b_up.reshape(e, 1, f2), w_down,
      b_down.reshape(e, 1, d))


def _sc_mesh():
    return plsc.VectorSubcoreMesh(core_axis_name="core", subcore_axis_name="subcore")


def _sc_split(n_rows, max_chunk):
    info = plsc.get_sparse_core_info()
    n_workers = info.num_cores * info.num_subcores
    assert n_rows % (8 * n_workers) == 0
    per_w = n_rows // n_workers
    chunk = 8
    while chunk * 2 <= max_chunk and per_w % (chunk * 2) == 0:
        chunk *= 2
    return info.num_cores, n_workers, per_w, chunk


def _sc_dispatch(x, pos, n_rows):
    n, w = x.shape
    nc, nw, per_w, chunk = _sc_split(n, 32)
    n_ch = per_w // chunk
    idx = pos.T.reshape(TOP_K, nw, n_ch, chunk).transpose(1, 0, 2, 3).reshape(nw, TOP_K * n_ch, chunk)

    @functools.partial(
        pl.kernel, out_type=jax.ShapeDtypeStruct((n_rows, w), x.dtype), mesh=_sc_mesh(),
        scratch_types=[pltpu.VMEM((TOP_K * n_ch, chunk), jnp.int32), pltpu.VMEM((2, chunk, w), x.dtype),
                       pltpu.SemaphoreType.DMA((2,)), pltpu.SemaphoreType.DMA((2,))])
    def scatter_rows(x_hbm, i_hbm, o_hbm, idx_v, rows_v, rsem, wsem):
        wid = lax.axis_index("subcore") * nc + lax.axis_index("core")
        base = wid * per_w
        pltpu.sync_copy(i_hbm.at[wid], idx_v)

        def read(j, slot):
            return pltpu.make_async_copy(x_hbm.at[pl.ds(base + j * chunk, chunk)], rows_v.at[slot], rsem.at[slot])

        def write(j, slot, k):
            return pltpu.make_async_copy(rows_v.at[slot], o_hbm.at[idx_v.at[k * n_ch + j]], wsem.at[slot])

        read(0, 0).start()

        @pl.loop(0, n_ch, step=2)
        def _(j0):
            for b in range(2):
                j = j0 + b

                @pl.when(j < n_ch)
                def _():
                    read(j, b).wait()

                    @pl.when(j + 1 < n_ch)
                    def _():
                        @pl.when(j >= 1)
                        def _():
                            for k in range(TOP_K):
                                write(j - 1, 1 - b, k).wait()

                        read(j + 1, 1 - b).start()

                    for k in range(TOP_K):
                        write(j, b, k).start()

        for jj in range(max(n_ch - 2, 0), n_ch):
            for k in range(TOP_K):
                write(jj, jj % 2, k).wait()

    return scatter_rows(x, idx)


def _sc_gather(table, idx):
    m = idx.shape[0]
    w = table.shape[1]
    nc, _, per_w, chunk = _sc_split(m, 64)
    n_ch = per_w // chunk

    @functools.partial(
        pl.kernel, out_type=jax.ShapeDtypeStruct((m, w), table.dtype), mesh=_sc_mesh(),
        scratch_types=[pltpu.VMEM((per_w,), jnp.int32), pltpu.VMEM((2, chunk, w), table.dtype),
                       pltpu.SemaphoreType.DMA((2,)), pltpu.SemaphoreType.DMA((2,))])
    def gather_rows(t_hbm, i_hbm, o_hbm, idx_v, rows_v, gsem, wsem):
        wid = lax.axis_index("subcore") * nc + lax.axis_index("core")
        base = wid * per_w
        pltpu.sync_copy(i_hbm.at[pl.ds(base, per_w)], idx_v)

        def gather(j, slot):
            off = pl.multiple_of(j * chunk, chunk)
            return pltpu.make_async_copy(t_hbm.at[idx_v.at[pl.ds(off, chunk)]], rows_v.at[slot], gsem.at[slot])

        def write(j, slot):
            off = pl.multiple_of(j * chunk, chunk)
            return pltpu.make_async_copy(rows_v.at[slot], o_hbm.at[pl.ds(base + off, chunk)], wsem.at[slot])

        gather(0, 0).start()

        @pl.loop(0, n_ch, step=2)
        def _(j0):
            for b in range(2):
                j = j0 + b

                @pl.when(j < n_ch)
                def _():
                    gather(j, b).wait()

                    @pl.when(j + 1 < n_ch)
                    def _():
                        @pl.when(j >= 1)
                        def _():
                            write(j - 1, 1 - b).wait()

                        gather(j + 1, 1 - b).start()

                    write(j, b).start()

        for jj in range(max(n_ch - 2, 0), n_ch):
            write(jj, jj % 2).wait()

    return gather_rows(table, idx)


def _final_kernel(n_pt, d, h_ref, yg_ref, prob_ref, gtp_ref, gts_ref, g_ref, yp_ref, ys_ref):
    i = pl.program_id(0)
    p = prob_ref[...]
    moe_lo, moe_hi = None, None
    for k in range(TOP_K):
        lo, hi = _unpack_pair(yg_ref[k])
        pk = p[:, k:k + 1]
        moe_lo = pk * lo if moe_lo is None else moe_lo + pk * lo
        moe_hi = pk * hi if moe_hi is None else moe_hi + pk * hi
    moe = jnp.concatenate([moe_lo, moe_hi], axis=1)

    def body(gt, shape):
        h3 = h_ref[...].reshape(shape) + gt * moe.reshape(shape)
        ms = jnp.mean(h3 * h3, axis=-1, keepdims=True)
        return h3 * lax.rsqrt(ms + EPS) * g_ref[...]

    @pl.when(i < n_pt)
    def _():
        yp_ref[...] = body(gtp_ref[0], yp_ref.shape)

    @pl.when(i >= n_pt)
    def _():
        ys_ref[...] = body(gts_ref[0], ys_ref.shape)


def _final(tl, h, yg, probs, ada_p, ada_s, g_final, d):
    return pl.pallas_call(
        functools.partial(_final_kernel, tl.n_pt, d),
        grid=(tl.n,),
        in_specs=[tl.tok_spec(d), pl.BlockSpec((TOP_K, tl.tm, d // 2), lambda i: (0, i, 0)), tl.tok_spec(TOP_K),
                  tl.adap_spec(5, d), tl.adas_spec(5, d), _resident((1, 1, d))],
        out_specs=[tl.xp_spec(d), tl.xs_spec(d)],
        out_shape=[jax.ShapeDtypeStruct((tl.b, tl.t, d), F32), jax.ShapeDtypeStruct((tl.bs, tl.ts, d), F32)],
        compiler_params=pltpu.CompilerParams(dimension_semantics=("arbitrary",), vmem_limit_bytes=VMEM_LIMIT),
        name="final",
    )(h, yg, probs, ada_p, ada_s, g_final.reshape(1, 1, d))


def _pick(n, pref):
    t = min(n, pref)
    while n % t:
        t //= 2
    return t


def _forward(x_prompt, x_sample, c_prompt, c_sample, state_ret, state_gla, w_ada, b_ada, g_norm_mix, g_norm_ffn,
             w_in, w_gk_up, b_gk, g_gla_norm, w_ret_o, w_gla_o, w_out, w_router, b_router, w_up, b_up,
             w_down, b_down, g_final, *, tm, tb, gsz, tme):
    b, t, d = x_prompt.shape
    bs, ts, _ = x_sample.shape
    assert w_ada.shape[0] == 1, "single layer only"
    assert (b * t) % (2 * tb) == 0 and (b * t) % tm == 0
    e = N_EXPERTS
    tl = _Tiles(b, t, bs, ts, tm)
    n_tok = tl.n_tok

    ada = _ada(jnp.concatenate([c_prompt, c_sample], axis=0), w_ada[0], b_ada[0])
    ada_p = ada[:, :b].reshape(6, b, 1, d)
    ada_s = ada[:, b:].reshape(6, bs, 1, d)

    w_in0 = w_in[0]
    n_main = 6 * d
    w_main = w_in0[:, :n_main].astype(BF16)
    w_mg = w_in0[:, n_main + GLA_GATE_RANK:].astype(BF16)
    w_glr = w_in0[:, n_main:n_main + GLA_GATE_RANK].astype(BF16)
    w_gk = w_gk_up[0].astype(BF16)
    bgk = b_gk[0].reshape(1, -1)
    ggn = g_gla_norm[0].reshape(1, -1)
    w_r = w_router[0]
    w_r_hi = w_r.astype(BF16)
    w_r_lo = (w_r - w_r_hi.astype(F32)).astype(BF16)
    route_w = (g_norm_ffn[0], w_ret_o[0].astype(BF16), w_gla_o[0].astype(BF16), w_out[0].astype(BF16),
               w_r_hi, w_r_lo, b_router[0])

    oret_p, ogla_p, mg_p, sret_p, sgla_p = _front_prompt(x_prompt, ada_p, g_norm_mix[0], w_main, w_mg, w_glr, tb,
                                                         w_gk, bgk, ggn)
    proj_s, glr_s = _inproj_sample(tl, x_sample, ada_s, g_norm_mix[0], w_main, w_mg, w_glr)
    oret_s, ogla_s, sret_s, sgla_s = _mix_sample(bs, ts, d, gsz, proj_s, glr_s, state_ret[0], state_gla[0],
                                                 w_gk, bgk, ggn)
    h, n2, idx, rank, probs, counts = _outproj(tl, oret_p, ogla_p, oret_s, ogla_s, mg_p, proj_s, x_prompt, x_sample,
                                               ada_p, ada_s, *route_w)

    counts = counts[0]
    gsize = ((counts + tme - 1) // tme) * tme
    ends = jnp.cumsum(gsize)
    offs = ends - gsize
    experts = jnp.arange(e, dtype=jnp.int32)
    pos = jnp.sum(jnp.where(idx[..., None] == experts, offs, 0), axis=-1) + rank
    max_tiles = (n_tok * TOP_K) // tme + e
    n_active = (ends[-1] // tme).astype(jnp.int32).reshape(1)
    tile_start = jnp.arange(max_tiles, dtype=jnp.int32) * tme
    tile_expert = jnp.minimum(jnp.sum((ends[None, :] <= tile_start[:, None]).astype(jnp.int32), axis=1), e - 1)
    is_first = jnp.logical_and(tile_start < ends[-1],
                               jnp.concatenate([jnp.ones((1,), bool), tile_expert[1:] != tile_expert[:-1]]))
    tile_group = jnp.cumsum(is_first.astype(jnp.int32)) - 1
    later = jnp.logical_and(experts[None, :] > experts[:, None], counts[None, :] > 0)
    next_of = jnp.min(jnp.where(later, experts[None, :], e), axis=1)
    next_of = jnp.where(next_of == e, -1, next_of)
    next_expert = jnp.sum(jnp.where(tile_expert[:, None] == experts, next_of, 0), axis=1).astype(jnp.int32)

    xs = _sc_dispatch(n2, pos, max_tiles * tme)
    ys = _experts(xs, tile_expert, n_active, tile_group, next_expert, w_up[0], b_up[0], w_down[0], b_down[0], tme)
    yg = _sc_gather(ys, pos.T.reshape(-1)).reshape(TOP_K, n_tok, d // 2)

    y_p, y_s = _final(tl, h, yg, probs, ada_p, ada_s, g_final, d)
    return (y_p, y_s, sret_p[None], sgla_p[None], sret_s[None], sgla_s[None])


def kernel(x_prompt, x_sample, c_prompt, c_sample, state_ret, state_gla, w_ada, b_ada, g_norm_mix, g_norm_ffn,
           w_in, w_gk_up, b_gk, g_gla_norm, w_ret_o, w_gla_o, w_out, w_router, b_router, w_up, b_up,
           w_down, b_down, g_final):
    t = x_prompt.shape[1]
    bs, ts = x_sample.shape[0], x_sample.shape[1]
    return _forward(x_prompt, x_sample, c_prompt, c_sample, state_ret, state_gla, w_ada, b_ada, g_norm_mix,
                    g_norm_ffn, w_in, w_gk_up, b_gk, g_gla_norm, w_ret_o, w_gla_o, w_out, w_router, b_router,
                    w_up, b_up, w_down, b_down, g_final,
                    tm=_pick(bs * ts, 512), tb=_pick(t, 256), gsz=_pick(bs, 8), tme=512)
```

```python
import functools

import jax
import jax.numpy as jnp
from jax import lax
from jax.experimental import pallas as pl
from jax.experimental.pallas import tpu as pltpu
from jax.experimental.pallas import tpu_sc as plsc

F32 = jnp.float32
BF16 = jnp.bfloat16

N_HEADS = 4
GLA_GATE_RANK = 16
GLA_GATE_NORM = 16.0
GLA_CHUNK = 64
ROPE_BASE = 10000.0
N_EXPERTS = 32
TOP_K = 4
SWIGLU_LIMIT = 7.0
SWIGLU_ALPHA = 1.702
EPS = 1e-6
PAST_LEN = 16384
N_SEG = 8
ROUTE_COLS = 128
EXPERT_ROW_SLABS = 2

VMEM_LIMIT = 56 * 1024 * 1024


def _mm(a, b):
    return jnp.dot(a, b, preferred_element_type=F32)


def _mm_nt(a, b):
    return lax.dot_general(a, b, (((1,), (1,)), ((), ())), preferred_element_type=F32)


def _silu(x):
    return x * jax.nn.sigmoid(x)


def _split_hi_lo(x):
    hi = x.astype(BF16)
    lo = (x - hi.astype(F32)).astype(BF16)
    return hi, lo


def _pack_pair(x):
    w = x.shape[1] // 2
    lo = lax.bitcast_convert_type(x[:, :w].astype(BF16).astype(F32), jnp.uint32)
    hi = lax.bitcast_convert_type(x[:, w:].astype(BF16).astype(F32), jnp.uint32)
    return (hi & jnp.uint32(0xFFFF0000)) | (lo >> 16)


def _unpack_pair(p):
    lo = lax.bitcast_convert_type(p << 16, F32)
    hi = lax.bitcast_convert_type(p & jnp.uint32(0xFFFF0000), F32)
    return lo, hi


def _rms_mod(x3, g, sc, sh):
    ms = jnp.mean(x3 * x3, axis=-1, keepdims=True)
    return x3 * lax.rsqrt(ms + EPS) * g * (1.0 + sc) + sh


def _resident(shape):
    zeros = (0,) * len(shape)
    return pl.BlockSpec(shape, lambda i: zeros, pipeline_mode=pl.Buffered(1))


def _const(shape):
    zeros = (0,) * len(shape)
    return pl.BlockSpec(shape, lambda i: zeros)


def _ada_kernel(c_ref, w_ref, b_ref, o_ref):
    cf = _silu(c_ref[...])
    o_ref[0] = _mm(cf.astype(BF16), w_ref[...].astype(BF16)) + b_ref[0]


def _ada(c_all, w_ada, b_ada):
    bc, d = c_all.shape
    n = w_ada.shape[1] // d
    return pl.pallas_call(
        _ada_kernel,
        grid=(n,),
        in_specs=[pl.BlockSpec((bc, d), lambda j: (0, 0)),
                  pl.BlockSpec((d, d), lambda j: (0, j)),
                  pl.BlockSpec((1, 1, d), lambda j: (j, 0, 0))],
        out_specs=pl.BlockSpec((1, bc, d), lambda j: (j, 0, 0)),
        out_shape=jax.ShapeDtypeStruct((n, bc, d), F32),
        compiler_params=pltpu.CompilerParams(dimension_semantics=("arbitrary",), vmem_limit_bytes=VMEM_LIMIT),
        name="ada",
    )(c_all, w_ada, b_ada.reshape(n, 1, d))


def _cast_kernel(w_ref, o_ref):
    o_ref[...] = w_ref[...].astype(BF16)


def _cast_columns(w, n_cols, blk):
    rows = w.shape[0]
    return pl.pallas_call(
        _cast_kernel,
        grid=(n_cols // blk,),
        in_specs=[pl.BlockSpec((rows, blk), lambda j: (0, j))],
        out_specs=pl.BlockSpec((rows, blk), lambda j: (0, j)),
        out_shape=jax.ShapeDtypeStruct((rows, n_cols), BF16),
        compiler_params=pltpu.CompilerParams(dimension_semantics=("arbitrary",), vmem_limit_bytes=VMEM_LIMIT),
        name="cast_w_in",
    )(w)


class _Tiles:
    def __init__(self, b, t, bs, ts, tm):
        assert t % tm == 0 and (bs * ts) % tm == 0 and tm % ts == 0
        self.b, self.t, self.bs, self.ts, self.tm = b, t, bs, ts, tm
        self.tpb = t // tm
        self.n_pt = b * self.tpb
        self.gs = tm // ts
        self.n_st = (bs * ts) // tm
        self.n = self.n_pt + self.n_st
        self.n_tok = b * t + bs * ts

    def xp_spec(self, d):
        last, tpb = self.n_pt - 1, self.tpb
        return pl.BlockSpec((1, self.tm, d), lambda i: (jnp.minimum(i, last) // tpb, jnp.minimum(i, last) % tpb, 0))

    def xs_spec(self, d):
        n_pt = self.n_pt
        return pl.BlockSpec((self.gs, self.ts, d), lambda i: (jnp.maximum(i - n_pt, 0), 0, 0))

    def adap_spec(self, which, d):
        last, tpb = self.n_pt - 1, self.tpb
        return pl.BlockSpec((1, 1, 1, d), lambda i: (which, jnp.minimum(i, last) // tpb, 0, 0))

    def adas_spec(self, which, d):
        n_pt = self.n_pt
        return pl.BlockSpec((1, self.gs, 1, d), lambda i: (which, jnp.maximum(i - n_pt, 0), 0, 0))

    def tok_spec(self, width):
        return pl.BlockSpec((self.tm, width), lambda i: (i, 0))

    def s_x_spec(self, d):
        return pl.BlockSpec((self.gs, self.ts, d), lambda i: (i, 0, 0))

    def s_ada_spec(self, which, d):
        return pl.BlockSpec((1, self.gs, 1, d), lambda i: (which, i, 0, 0))

    def s_row_spec(self, width):
        return pl.BlockSpec((self.tm, width), lambda i: (i, 0))

    def s_tok_spec(self, width):
        n_pt = self.n_pt
        return pl.BlockSpec((self.tm, width), lambda i: (n_pt + i, 0))


def _rope_tables(pos0, t, dk):
    half = dk // 2
    inv = ROPE_BASE ** (-jnp.arange(half, dtype=jnp.float32) / half)
    pos = pos0 + jnp.arange(t)
    ang = pos.astype(jnp.float32)[:, None] * inv[None, :]
    cos, sin = jnp.cos(ang), jnp.sin(ang)
    return jnp.concatenate([cos, cos], axis=-1), jnp.concatenate([-sin, sin], axis=-1)


def _ret_tables(c, dk, dv):
    h = N_HEADS
    log_gamma = jnp.log1p(-jnp.exp2(-5.0 - jnp.arange(h, dtype=jnp.float32)))
    idx = jnp.arange(c, dtype=jnp.float32)
    rel = idx[:, None] - idx[None, :]
    dmask = jnp.where(rel >= 0, jnp.exp(log_gamma[:, None, None] * jnp.maximum(rel, 0.0)), 0.0)
    kdec = jnp.exp(log_gamma[:, None] * (c - 1 - idx))
    qdec = jnp.exp(log_gamma[:, None] * (idx + 1.0))
    cdec = jnp.exp(log_gamma * c)
    return (dmask,
            jnp.broadcast_to(qdec[:, :, None], (h, c, dk)),
            jnp.broadcast_to(kdec[:, :, None], (h, c, dk)),
            jnp.broadcast_to(cdec[:, None, None], (h, 1, dv)))


def _rot(x, cos_f, sin_f):
    return x * cos_f + pltpu.roll(x, x.shape[-1] // 2, 1) * sin_f


def _cross_and_update(q_lhs, k_end, vh, states, masks):
    if masks is None:
        (s,) = states
        return _mm(q_lhs, s.astype(BF16)), [_mm(k_end.T.astype(BF16), vh)]
    cross, incs = None, []
    for s, m in zip(states, masks):
        c = _mm(q_lhs, s.astype(BF16))
        cross = c if cross is None else jnp.where(m, c, cross)
        incs.append(_mm(jnp.where(m, k_end, 0.0).T.astype(BF16), vh))
    return cross, incs


def _ret_head(q, k, vh, gh, states, masks, cos_f, sin_f, dmask, qdec, kdec, cdec):
    dk = q.shape[-1]
    q = _rot(q, cos_f, sin_f)
    k = _rot(k, cos_f, sin_f) * (dk ** -0.5)
    scores = _mm_nt(q.astype(BF16), k.astype(BF16)) * dmask
    cross, incs = _cross_and_update((q * qdec).astype(BF16), k * kdec, vh, states, masks)
    o = _mm(scores.astype(BF16), vh) + cross
    new_states = [cdec * s + u for s, u in zip(states, incs)]
    mu = jnp.mean(o, axis=-1, keepdims=True)
    oc = o - mu
    var = jnp.mean(oc * oc, axis=-1, keepdims=True)
    return _silu(gh) * (oc * lax.rsqrt(var + EPS)), new_states


def _gla_head(q, k, vh, gh, b, states, masks, c, gnorm, causal):
    dk = q.shape[-1]
    b_t = b.T
    if masks is None:
        b_last = b[c - 1:c, :]
    else:
        b_last = None
        for g, m in enumerate(masks):
            row = b[g * c + c - 1:g * c + c, :]
            b_last = row if b_last is None else jnp.where(m, row, b_last)
    q_in = (q * (dk ** -0.5) * jnp.exp(b)).astype(BF16)
    k_in = (k * jnp.exp(-b)).astype(BF16)
    scores = jnp.where(causal, _mm_nt(q_in, k_in), 0.0)
    cross, incs = _cross_and_update(q_in, k * jnp.exp(b_last - b), vh, states, masks)
    o = _mm(scores.astype(BF16), vh) + cross
    new_states = [jnp.exp(b_t[:, g * c + c - 1:g * c + c]) * s + u for g, (s, u) in enumerate(zip(states, incs))]
    o = o * lax.rsqrt(jnp.mean(o * o, axis=-1, keepdims=True) + EPS) * gnorm
    return _silu(gh) * o, new_states


def _log_a(glr, wgk, bgk):
    z = _mm(glr.astype(BF16), wgk) + bgk
    return (jnp.minimum(z, 0.0) - jnp.log1p(jnp.exp(-jnp.abs(z)))) / GLA_GATE_NORM


def _causal(c):
    return lax.broadcasted_iota(jnp.int32, (c, c), 0) >= lax.broadcasted_iota(jnp.int32, (c, c), 1)


def _w_seg(w_ref, wm_ref, seg, d):
    if seg < N_SEG - 2:
        return w_ref[:, seg * d:(seg + 1) * d]
    return wm_ref[:, (seg - (N_SEG - 2)) * d:(seg - (N_SEG - 3)) * d]


def _proj_block(d, x3, sh, sc, g, w_ref, wm_ref, wl_ref, proj_s, glr_s):
    n = _rms_mod(x3, g, sc, sh).reshape(-1, d).astype(BF16)
    for seg in range(N_SEG):
        proj_s[:, seg * d:(seg + 1) * d] = _mm(n, _w_seg(w_ref, wm_ref, seg, d)).astype(BF16)
    glr_s[...] = _mm(n, wl_ref[...])


def _mix_block(d, tb, proj_s, glr_s, cos_f, sin_f, dmask_ref, qdec_ref, kdec_ref, cdec_ref, tri, wgk, bgk, gnorm,
               sr_s, sg_s, oret_ref, ogla_ref, mg_ref, r_off):
    dk, dv, hq = d // 8, d // 4, d // 2
    rqk, rv, rg, gqk, gv, gg, mg = (i * d for i in range(7))
    for h in range(N_HEADS):
        o, (s_new,) = _ret_head(proj_s[:, rqk + h * dk:rqk + (h + 1) * dk].astype(F32),
                                proj_s[:, rqk + hq + h * dk:rqk + hq + (h + 1) * dk].astype(F32),
                                proj_s[:, rv + h * dv:rv + (h + 1) * dv],
                                proj_s[:, rg + h * dv:rg + (h + 1) * dv].astype(F32),
                                [sr_s[h]], None, cos_f, sin_f, dmask_ref[h], qdec_ref[h], kdec_ref[h],
                                cdec_ref[h])
        sr_s[h] = s_new
        oret_ref[r_off:r_off + tb, h * dv:(h + 1) * dv] = o.astype(BF16)

    la_hi, la_lo = _split_hi_lo(_log_a(glr_s[...], wgk, bgk))
    b = _mm(tri, la_hi) + _mm(tri, la_lo)
    cg = GLA_CHUNK
    causal = _causal(cg)
    for c in range(tb // cg):
        r0, r1 = c * cg, (c + 1) * cg
        for h in range(N_HEADS):
            o, (s_new,) = _gla_head(proj_s[r0:r1, gqk + h * dk:gqk + (h + 1) * dk].astype(F32),
                                    proj_s[r0:r1, gqk + hq + h * dk:gqk + hq + (h + 1) * dk].astype(F32),
                                    proj_s[r0:r1, gv + h * dv:gv + (h + 1) * dv],
                                    proj_s[r0:r1, gg + h * dv:gg + (h + 1) * dv].astype(F32),
                                    b[r0:r1, h * dk:(h + 1) * dk], [sg_s[h]], None, cg, gnorm, causal)
            sg_s[h] = s_new
            ogla_ref[r_off + r0:r_off + r1, h * dv:(h + 1) * dv] = o.astype(BF16)
    mg_ref[0, r_off:r_off + tb, :] = proj_s[:, mg:mg + d]
    mg_ref[1, r_off:r_off + tb, :] = proj_s[:, mg + d:mg + 2 * d]


def _route_block(d, out_ret, out_gla, mg_ret, mg_gla, x3, gt, sh, sc, g, wro_ref, wgo_ref, wo_ref, wrh_ref, wrl_ref,
                 br, carry_s, h_ref, n2_ref, idx_ref, rank_ref, prob_ref, r_off):
    rows = out_ret.shape[0]
    e = N_EXPERTS
    a = _mm(out_ret, wro_ref[...])
    b = _mm(out_gla, wgo_ref[...])
    merged = jax.nn.sigmoid(mg_ret.astype(F32)) * a + jax.nn.sigmoid(mg_gla.astype(F32)) * b
    mix = _mm(merged.astype(BF16), wo_ref[...])
    h3 = x3 + gt * mix.reshape(x3.shape)
    out_rows = slice(r_off, r_off + rows)
    h_ref[out_rows, :] = h3.reshape(rows, d)
    n2 = _rms_mod(h3, g, sc, sh).reshape(rows, d)
    n2_ref[out_rows, :] = _pack_pair(n2)

    n_hi, n_lo = _split_hi_lo(n2)
    logits = _mm(n_hi, wrh_ref[...]) + _mm(n_lo, wrh_ref[...]) + _mm(n_hi, wrl_ref[...]) + br
    iota = lax.broadcasted_iota(jnp.int32, (rows, e), 1)
    work = logits
    vals, idxs = [], []
    for _ in range(TOP_K):
        m = jnp.max(work, axis=-1, keepdims=True)
        ik = jnp.min(jnp.where(work == m, iota, e), axis=-1, keepdims=True)
        vals.append(m)
        idxs.append(ik)
        work = jnp.where(iota == ik, -jnp.inf, work)
    ex = [jnp.exp(v - vals[0]) for v in vals]
    den = ex[0] + ex[1] + ex[2] + ex[3]

    onehot = jnp.zeros((rows, e), F32)
    for ik in idxs:
        onehot = onehot + (iota == ik).astype(F32)
    ltri = (lax.broadcasted_iota(jnp.int32, (rows, rows), 0) > lax.broadcasted_iota(jnp.int32, (rows, rows), 1))
    cum = _mm(ltri.astype(F32).astype(BF16), onehot.astype(BF16)) + carry_s[...]
    lane = lax.broadcasted_iota(jnp.int32, (rows, TOP_K), 1)
    prob_o = jnp.zeros((rows, TOP_K), F32)
    col = lax.broadcasted_iota(jnp.int32, (rows, ROUTE_COLS), 1)
    table = jnp.zeros((rows, ROUTE_COLS), F32)
    for k in range(TOP_K):
        rk = jnp.sum(jnp.where(iota == idxs[k], cum, 0.0), axis=-1, keepdims=True).astype(jnp.int32)
        prob_o = jnp.where(lane == k, ex[k] / den, prob_o)
        for g, piece in enumerate((idxs[k], rk & 255, (rk >> 8) & 255, rk >> 16)):
            table = jnp.where(col == 8 * g + k, piece.astype(F32), table)
    eye = (lax.broadcasted_iota(jnp.int32, (ROUTE_COLS, ROUTE_COLS), 0)
           == lax.broadcasted_iota(jnp.int32, (ROUTE_COLS, ROUTE_COLS), 1)).astype(F32).astype(BF16)
    t = _mm_nt(eye, table.astype(BF16))
    idx_ref[:, out_rows] = t[0:TOP_K].astype(jnp.int32)
    rank_ref[:, out_rows] = (t[8:8 + TOP_K] + 256.0 * t[16:16 + TOP_K] + 65536.0 * t[24:24 + TOP_K]).astype(jnp.int32)
    prob_ref[out_rows, :] = prob_o
    carry_s[...] = carry_s[...] + jnp.sum(onehot, axis=0, keepdims=True)


def _frontp_kernel(d, tb, ntb, x0_ref, xa_ref, xb_ref, sh0_ref, sc0_ref, sha_ref, sca_ref, shb_ref, scb_ref,
                   g_ref, w_ref, wm_ref, wl_ref, cosa_ref, sina_ref, cosb_ref, sinb_ref,
                   dmask_ref, qdec_ref, kdec_ref, cdec_ref, tri_ref, wgk_ref, bgk_ref, gn_ref,
                   oret_ref, ogla_ref, mg_ref, sret_ref, sgla_ref, pa_s, pb_s, ga_s, gb_s, sr_s, sg_s):
    p = pl.program_id(0)
    blk = 2 * p
    g = g_ref[...]
    proj = functools.partial(_proj_block, d)
    mix = functools.partial(_mix_block, d, tb)
    tables = (dmask_ref, qdec_ref, kdec_ref, cdec_ref, tri_ref[...], wgk_ref[...], bgk_ref[...], gn_ref[...])

    @pl.when(p == 0)
    def _():
        proj(x0_ref[...], sh0_ref[0], sc0_ref[0], g, w_ref, wm_ref, wl_ref, pa_s, ga_s)

    @pl.when(blk % ntb == 0)
    def _():
        sr_s[...] = jnp.zeros_like(sr_s)
        sg_s[...] = jnp.zeros_like(sg_s)

    proj(xa_ref[...], sha_ref[0], sca_ref[0], g, w_ref, wm_ref, wl_ref, pb_s, gb_s)
    mix(pa_s, ga_s, cosa_ref[...], sina_ref[...], *tables, sr_s, sg_s, oret_ref, ogla_ref, mg_ref, 0)
    proj(xb_ref[...], shb_ref[0], scb_ref[0], g, w_ref, wm_ref, wl_ref, pa_s, ga_s)
    mix(pb_s, gb_s, cosb_ref[...], sinb_ref[...], *tables, sr_s, sg_s, oret_ref, ogla_ref, mg_ref, tb)

    @pl.when((blk + 1) % ntb == ntb - 1)
    def _():
        sret_ref[0] = sr_s[...]
        sgla_ref[0] = sg_s[...]


def _chunk_tri(tb, cg):
    i = jnp.arange(tb)
    return ((i[:, None] >= i[None, :]) & (i[:, None] // cg == i[None, :] // cg)).astype(BF16)


def _front_prompt(x_p, ada_p, g_mix, w_main, w_mg, w_glr, tb, w_gk, b_gk, g_gla):
    b, t, d = x_p.shape
    dk, dv, hq, h = d // 8, d // 4, d // 2, N_HEADS
    ntb = t // tb
    n_blk = b * ntb
    n_tok = b * t
    assert ntb % 2 == 0
    cos_f, sin_f = _rope_tables(0, t, dk)
    dmask, qdec, kdec, cdec = _ret_tables(tb, dk, dv)
    tri = _chunk_tri(tb, GLA_CHUNK)

    def first(p):
        return 0 * p

    def even(p):
        return 2 * p

    def odd(p):
        return 2 * p + 1

    def nxt(p):
        return jnp.minimum(2 * p + 2, n_blk - 1)

    def x_spec(blk_of):
        return pl.BlockSpec((1, tb, d), lambda p: (blk_of(p) // ntb, blk_of(p) % ntb, 0))

    def ada_spec(which, blk_of):
        return pl.BlockSpec((1, 1, 1, d), lambda p: (which, blk_of(p) // ntb, 0, 0))

    def rope_spec(blk_of):
        return pl.BlockSpec((tb, dk), lambda p: (blk_of(p) % ntb, 0))

    state_spec = pl.BlockSpec((1, h, dk, dv), lambda p: ((2 * p) // ntb, 0, 0, 0))
    tok_spec = pl.BlockSpec((2 * tb, d), lambda p: (p, 0))
    return pl.pallas_call(
        functools.partial(_frontp_kernel, d, tb, ntb),
        grid=(n_blk // 2,),
        in_specs=[_resident((1, tb, d)), x_spec(odd), x_spec(nxt),
                  ada_spec(0, first), ada_spec(1, first), ada_spec(0, odd), ada_spec(1, odd),
                  ada_spec(0, nxt), ada_spec(1, nxt),
                  _resident((1, 1, d)), _resident((d, (N_SEG - 2) * d)), _resident((d, 2 * d)),
                  _resident((d, GLA_GATE_RANK)),
                  rope_spec(even), rope_spec(even), rope_spec(odd), rope_spec(odd),
                  _resident((h, tb, tb)), _resident((h, tb, dk)), _resident((h, tb, dk)), _resident((h, 1, dv)),
                  _resident((tb, tb)), _resident((GLA_GATE_RANK, hq)), _resident((1, hq)), _resident((1, dv))],
        out_specs=[tok_spec, tok_spec, pl.BlockSpec((2, 2 * tb, d), lambda p: (0, p, 0)), state_spec, state_spec],
        out_shape=[jax.ShapeDtypeStruct((n_tok, d), BF16), jax.ShapeDtypeStruct((n_tok, d), BF16),
                   jax.ShapeDtypeStruct((2, n_tok, d), BF16),
                   jax.ShapeDtypeStruct((b, h, dk, dv), F32), jax.ShapeDtypeStruct((b, h, dk, dv), F32)],
        scratch_shapes=[pltpu.VMEM((tb, N_SEG * d), BF16), pltpu.VMEM((tb, N_SEG * d), BF16),
                        pltpu.VMEM((tb, GLA_GATE_RANK), F32), pltpu.VMEM((tb, GLA_GATE_RANK), F32),
                        pltpu.VMEM((h, dk, dv), F32), pltpu.VMEM((h, dk, dv), F32)],
        compiler_params=pltpu.CompilerParams(dimension_semantics=("arbitrary",), vmem_limit_bytes=VMEM_LIMIT),
        name="front_prompt",
    )(x_p, x_p, x_p, ada_p, ada_p, ada_p, ada_p, ada_p, ada_p, g_mix.reshape(1, 1, d), w_main, w_mg, w_glr,
      cos_f, sin_f, cos_f, sin_f, dmask, qdec, kdec, cdec, tri, w_gk, b_gk, g_gla)


def _inproj_kernel(d, xs_ref, shs_ref, scs_ref, g_ref, w_ref, wm_ref, wl_ref, proj_ref, glr_ref):
    n = _rms_mod(xs_ref[...], g_ref[...], scs_ref[0], shs_ref[0]).reshape(-1, d).astype(BF16)
    for s in range(N_SEG):
        proj_ref[s] = _mm(n, _w_seg(w_ref, wm_ref, s, d)).astype(BF16)
    glr_ref[...] = _mm(n, wl_ref[...])


def _inproj_sample(tl, x_s, ada_s, g_mix, w_main, w_mg, w_glr):
    bs, ts, d = x_s.shape
    n_tok = bs * ts
    return pl.pallas_call(
        functools.partial(_inproj_kernel, d),
        grid=(tl.n_st,),
        in_specs=[tl.s_x_spec(d), tl.s_ada_spec(0, d), tl.s_ada_spec(1, d),
                  _resident((1, 1, d)), _resident((d, (N_SEG - 2) * d)), _resident((d, 2 * d)),
                  _resident((d, GLA_GATE_RANK))],
        out_specs=[pl.BlockSpec((N_SEG, tl.tm, d), lambda i: (0, i, 0)), tl.s_row_spec(GLA_GATE_RANK)],
        out_shape=[jax.ShapeDtypeStruct((N_SEG, n_tok, d), BF16), jax.ShapeDtypeStruct((n_tok, GLA_GATE_RANK), F32)],
        compiler_params=pltpu.CompilerParams(dimension_semantics=("arbitrary",), vmem_limit_bytes=VMEM_LIMIT),
        name="inproj_sample",
    )(x_s, ada_s, ada_s, g_mix.reshape(1, 1, d), w_main, w_mg, w_glr)


def _mixs_kernel(d, ts, gsz, rqk_ref, rv_ref, rg_ref, gqk_ref, gv_ref, gg_ref, glr_ref, cos_ref, sin_ref,
                 dmask_ref, qdec_ref, kdec_ref, cdec_ref, wgk_ref, bgk_ref, gn_ref, sr_in, sg_in,
                 oret_ref, ogla_ref, sr_out, sg_out):
    dk, dv, hq = d // 8, d // 4, d // 2
    pair = 2 * ts
    cos_f, sin_f = cos_ref[...], sin_ref[...]
    gnorm = gn_ref[...]
    ri = lax.broadcasted_iota(jnp.int32, (pair, pair), 0)
    ci = lax.broadcasted_iota(jnp.int32, (pair, pair), 1)
    causal = jnp.logical_and(ri >= ci, (ri < ts) == (ci < ts))
    tri = causal.astype(F32).astype(BF16)
    first = lax.broadcasted_iota(jnp.int32, (pair, 1), 0) < ts
    masks = [first, jnp.logical_not(first)]

    def body(j, carry):
        rows = pl.ds(pl.multiple_of(j * pair, pair), pair)
        s0, s1 = 2 * j, 2 * j + 1
        la_hi, la_lo = _split_hi_lo(_log_a(glr_ref[rows, :], wgk_ref[...], bgk_ref[...]))
        b = _mm(tri, la_hi) + _mm(tri, la_lo)
        for h in range(N_HEADS):
            o, (n0, n1) = _ret_head(rqk_ref[0, rows, h * dk:(h + 1) * dk].astype(F32),
                                    rqk_ref[0, rows, hq + h * dk:hq + (h + 1) * dk].astype(F32),
                                    rv_ref[0, rows, h * dv:(h + 1) * dv],
                                    rg_ref[0, rows, h * dv:(h + 1) * dv].astype(F32),
                                    [sr_in[s0, h], sr_in[s1, h]], masks, cos_f, sin_f,
                                    dmask_ref[h], qdec_ref[h], kdec_ref[h], cdec_ref[h])
            sr_out[s0, h] = n0
            sr_out[s1, h] = n1
            oret_ref[rows, h * dv:(h + 1) * dv] = o.astype(BF16)
            o, (n0, n1) = _gla_head(gqk_ref[0, rows, h * dk:(h + 1) * dk].astype(F32),
                                    gqk_ref[0, rows, hq + h * dk:hq + (h + 1) * dk].astype(F32),
                                    gv_ref[0, rows, h * dv:(h + 1) * dv],
                                    gg_ref[0, rows, h * dv:(h + 1) * dv].astype(F32),
                                    b[:, h * dk:(h + 1) * dk], [sg_in[s0, h], sg_in[s1, h]], masks, ts,
                                    gnorm, causal)
            sg_out[s0, h] = n0
            sg_out[s1, h] = n1
            ogla_ref[rows, h * dv:(h + 1) * dv] = o.astype(BF16)
        return carry

    lax.fori_loop(0, gsz // 2, body, 0, unroll=2)


def _pair_tables(ts, dk, dv):
    cos_f, sin_f = _rope_tables(PAST_LEN, ts, dk)
    dmask, qdec, kdec, cdec = _ret_tables(ts, dk, dv)
    zero = jnp.zeros_like(dmask)
    dmask2 = jnp.concatenate([jnp.concatenate([dmask, zero], axis=2), jnp.concatenate([zero, dmask], axis=2)], axis=1)

    def twice(a, axis):
        return jnp.concatenate([a, a], axis=axis)

    return twice(cos_f, 0), twice(sin_f, 0), dmask2, twice(qdec, 1), twice(kdec, 1), cdec


def _mix_sample(bs, ts, d, gsz, proj, glr, state_ret, state_gla, w_gk, b_gk, g_gla):
    dk, dv, hq, h = d // 8, d // 4, d // 2, N_HEADS
    assert GLA_CHUNK % ts == 0 and bs % gsz == 0 and gsz % 4 == 0
    rows = gsz * ts
    pair = 2 * ts
    cos_f, sin_f, dmask, qdec, kdec, cdec = _pair_tables(ts, dk, dv)

    def seg(s):
        return pl.BlockSpec((1, rows, d), lambda i: (s, i, 0))

    state_spec = pl.BlockSpec((gsz, h, dk, dv), lambda i: (i, 0, 0, 0))
    tok_spec = pl.BlockSpec((rows, d), lambda i: (i, 0))
    return pl.pallas_call(
        functools.partial(_mixs_kernel, d, ts, gsz),
        grid=(bs // gsz,),
        in_specs=[seg(0), seg(1), seg(2), seg(3), seg(4), seg(5),
                  pl.BlockSpec((rows, GLA_GATE_RANK), lambda i: (i, 0)),
                  _const((pair, dk)), _const((pair, dk)),
                  _const((h, pair, pair)), _const((h, pair, dk)), _const((h, pair, dk)), _const((h, 1, dv)),
                  _const((GLA_GATE_RANK, hq)), _const((1, hq)), _const((1, dv)),
                  state_spec, state_spec],
        out_specs=[tok_spec, tok_spec, state_spec, state_spec],
        out_shape=[jax.ShapeDtypeStruct((bs * ts, d), BF16), jax.ShapeDtypeStruct((bs * ts, d), BF16),
                   jax.ShapeDtypeStruct((bs, h, dk, dv), F32), jax.ShapeDtypeStruct((bs, h, dk, dv), F32)],
        compiler_params=pltpu.CompilerParams(dimension_semantics=("arbitrary",), vmem_limit_bytes=VMEM_LIMIT),
        name="mix_sample",
    )(proj, proj, proj, proj, proj, proj, glr, cos_f, sin_f, dmask, qdec, kdec, cdec, w_gk, b_gk, g_gla,
      state_ret, state_gla)


def _outproj_kernel(n_pt, d, orp_ref, ogp_ref, ors_ref, ogs_ref, mgrp_ref, mggp_ref, mgrs_ref, mggs_ref,
                    xp_ref, xs_ref, gtp_ref, shp_ref, scp_ref, gts_ref, shs_ref, scs_ref, g_ref,
                    wro_ref, wgo_ref, wo_ref, wrh_ref, wrl_ref, br_ref,
                    h_ref, n2_ref, idx_ref, rank_ref, prob_ref, cnt_ref, carry_s):
    i = pl.program_id(0)

    @pl.when(i == 0)
    def _():
        carry_s[...] = jnp.zeros_like(carry_s)

    route = functools.partial(_route_block, d)
    tail = (g_ref[...], wro_ref, wgo_ref, wo_ref, wrh_ref, wrl_ref, br_ref[...], carry_s,
            h_ref, n2_ref, idx_ref, rank_ref, prob_ref, 0)

    @pl.when(i < n_pt)
    def _():
        route(orp_ref[...], ogp_ref[...], mgrp_ref[0], mggp_ref[0], xp_ref[...], gtp_ref[0], shp_ref[0], scp_ref[0],
              *tail)

    @pl.when(i >= n_pt)
    def _():
        route(ors_ref[...], ogs_ref[...], mgrs_ref[0], mggs_ref[0], xs_ref[...], gts_ref[0], shs_ref[0], scs_ref[0],
              *tail)

    @pl.when(i == pl.num_programs(0) - 1)
    def _():
        cnt_ref[...] = carry_s[...].astype(jnp.int32)


def _outproj(tl, oret_p, ogla_p, oret_s, ogla_s, mg_p, proj_s, x_p, x_s, ada_p, ada_s, g_ffn,
             w_ret_o, w_gla_o, w_out, w_r_hi, w_r_lo, b_router):
    d = x_p.shape[-1]
    tm, e, n_pt = tl.tm, N_EXPERTS, tl.n_pt
    last = n_pt - 1
    p_spec = pl.BlockSpec((tm, d), lambda i: (jnp.minimum(i, last), 0))
    s_spec = pl.BlockSpec((tm, d), lambda i: (jnp.maximum(i - n_pt, 0), 0))

    def mgp_spec(seg):
        return pl.BlockSpec((1, tm, d), lambda i: (seg, jnp.minimum(i, last), 0))

    def mgs_spec(seg):
        return pl.BlockSpec((1, tm, d), lambda i: (seg, jnp.maximum(i - n_pt, 0), 0))

    slot_spec = pl.BlockSpec((TOP_K, tm), lambda i: (0, i))

    return pl.pallas_call(
        functools.partial(_outproj_kernel, n_pt, d),
        grid=(tl.n,),
        in_specs=[p_spec, p_spec, s_spec, s_spec, mgp_spec(0), mgp_spec(1), mgs_spec(6), mgs_spec(7),
                  tl.xp_spec(d), tl.xs_spec(d),
                  tl.adap_spec(2, d), tl.adap_spec(3, d), tl.adap_spec(4, d),
                  tl.adas_spec(2, d), tl.adas_spec(3, d), tl.adas_spec(4, d),
                  _resident((1, 1, d)), _resident((d, d)), _resident((d, d)), _resident((d, d)),
                  _resident((d, e)), _resident((d, e)), _resident((1, e))],
        out_specs=[tl.tok_spec(d), tl.tok_spec(d // 2), slot_spec, slot_spec, tl.tok_spec(TOP_K),
                   pl.BlockSpec((1, e), lambda i: (0, 0))],
        out_shape=[jax.ShapeDtypeStruct((tl.n_tok, d), F32), jax.ShapeDtypeStruct((tl.n_tok, d // 2), jnp.uint32),
                   jax.ShapeDtypeStruct((TOP_K, tl.n_tok), jnp.int32),
                   jax.ShapeDtypeStruct((TOP_K, tl.n_tok), jnp.int32),
                   jax.ShapeDtypeStruct((tl.n_tok, TOP_K), F32),
                   jax.ShapeDtypeStruct((1, e), jnp.int32)],
        scratch_shapes=[pltpu.VMEM((1, e), F32)],
        compiler_params=pltpu.CompilerParams(dimension_semantics=("arbitrary",), vmem_limit_bytes=VMEM_LIMIT),
        name="outproj",
    )(oret_p, ogla_p, oret_s, ogla_s, mg_p, mg_p, proj_s, proj_s, x_p, x_s, ada_p, ada_p, ada_p, ada_s, ada_s, ada_s,
      g_ffn.reshape(1, 1, d), w_ret_o, w_gla_o, w_out, w_r_hi, w_r_lo, b_router.reshape(1, e))


def _expert_kernel(f, te_ref, na_ref, grp_ref, nxt_ref, x_ref, wu_hbm, bu_ref, wd_hbm, bd_ref, y_ref,
                   wu_f, wd_f, wu_s, wd_s, sem):
    j = pl.program_id(0)
    active = j < na_ref[0]
    first = jnp.logical_or(j == 0, te_ref[j] != te_ref[jnp.maximum(j - 1, 0)])
    slot = grp_ref[j] % 2

    def fetch(expert, s):
        return (pltpu.make_async_copy(wu_hbm.at[expert], wu_f.at[s], sem.at[0, s]),
                pltpu.make_async_copy(wd_hbm.at[expert], wd_f.at[s], sem.at[1, s]))

    @pl.when(j == 0)
    def _():
        for c in fetch(te_ref[0], 0):
            c.start()

    @pl.when(jnp.logical_and(active, first))
    def _():
        for c in fetch(te_ref[j], slot):
            c.wait()

        @pl.when(nxt_ref[j] >= 0)
        def _():
            for c in fetch(nxt_ref[j], 1 - slot):
                c.start()

        wu_s[...] = wu_f[slot].astype(BF16)
        wd_s[...] = wd_f[slot].astype(BF16)

    slab = x_ref.shape[0] // EXPERT_ROW_SLABS
    half = x_ref.shape[1]

    @pl.when(active)
    def _():
        for s in range(EXPERT_ROW_SLABS):
            rows = slice(s * slab, (s + 1) * slab)
            x_lo, x_hi = _unpack_pair(x_ref[rows, :])
            gu = _mm(x_lo.astype(BF16), wu_s[:half, :]) + _mm(x_hi.astype(BF16), wu_s[half:, :]) + bu_ref[0]
            gate = jnp.minimum(gu[:, :f], SWIGLU_LIMIT)
            up = jnp.clip(gu[:, f:], -SWIGLU_LIMIT, SWIGLU_LIMIT)
            act = (up + 1.0) * gate * jax.nn.sigmoid(SWIGLU_ALPHA * gate)
            y_ref[rows, :] = _pack_pair(_mm(act.astype(BF16), wd_s[...]) + bd_ref[0])


def _experts(xs, tile_expert, n_active, tile_group, next_expert, w_up, b_up, w_down, b_down, tme):
    r = xs.shape[0]
    e, d, f2 = w_up.shape
    f = f2 // 2
    n_tiles = r // tme

    def row_map(j, te, na, grp, nxt):
        return (jnp.minimum(j, na[0] - 1), 0)

    def b_map(j, te, na, grp, nxt):
        return (te[jnp.minimum(j, na[0] - 1)], 0, 0)

    hbm = pl.BlockSpec(memory_space=pl.ANY)
    return pl.pallas_call(
        functools.partial(_expert_kernel, f),
        grid_spec=pltpu.PrefetchScalarGridSpec(
            num_scalar_prefetch=4,
            grid=(n_tiles,),
            in_specs=[pl.BlockSpec((tme, d // 2), row_map),
                      hbm, pl.BlockSpec((1, 1, f2), b_map), hbm, pl.BlockSpec((1, 1, d), b_map)],
            out_specs=pl.BlockSpec((tme, d // 2), row_map),
            scratch_shapes=[pltpu.VMEM((2, d, f2), F32), pltpu.VMEM((2, f, d), F32),
                            pltpu.VMEM((d, f2), BF16), pltpu.VMEM((f, d), BF16),
                            pltpu.SemaphoreType.DMA((2, 2))]),
        out_shape=jax.ShapeDtypeStruct((r, d // 2), jnp.uint32),
        compiler_params=pltpu.CompilerParams(dimension_semantics=("arbitrary",), vmem_limit_bytes=VMEM_LIMIT),
        name="experts",
    )(tile_expert, n_active, tile_group, next_expert, xs, w_up, b_up.reshape(e, 1, f2), w_down,
      b_down.reshape(e, 1, d))


def _sc_mesh():
    return plsc.VectorSubcoreMesh(core_axis_name="core", subcore_axis_name="subcore")


def _sc_split(n_rows, max_chunk):
    info = plsc.get_sparse_core_info()
    n_workers = info.num_cores * info.num_subcores
    assert n_rows % (8 * n_workers) == 0
    per_w = n_rows // n_workers
    chunk = 8
    while chunk * 2 <= max_chunk and per_w % (chunk * 2) == 0:
        chunk *= 2
    return info.num_cores, n_workers, per_w, chunk


def _sc_dispatch(x, pos_t, n_rows):
    n, w = x.shape
    nc, nw, per_w, chunk = _sc_split(n, 32)
    n_ch = per_w // chunk
    idx = pos_t.reshape(TOP_K, nw, n_ch, chunk).transpose(1, 0, 2, 3).reshape(nw, TOP_K * n_ch, chunk)

    @functools.partial(
        pl.kernel, out_type=jax.ShapeDtypeStruct((n_rows, w), x.dtype), mesh=_sc_mesh(),
        scratch_types=[pltpu.VMEM((TOP_K * n_ch, chunk), jnp.int32), pltpu.VMEM((2, chunk, w), x.dtype),
                       pltpu.SemaphoreType.DMA((2,)), pltpu.SemaphoreType.DMA((2,))])
    def scatter_rows(x_hbm, i_hbm, o_hbm, idx_v, rows_v, rsem, wsem):
        wid = lax.axis_index("subcore") * nc + lax.axis_index("core")
        base = wid * per_w
        pltpu.sync_copy(i_hbm.at[wid], idx_v)

        def read(j, slot):
            return pltpu.make_async_copy(x_hbm.at[pl.ds(base + j * chunk, chunk)], rows_v.at[slot], rsem.at[slot])

        def write(j, slot, k):
            return pltpu.make_async_copy(rows_v.at[slot], o_hbm.at[idx_v.at[k * n_ch + j]], wsem.at[slot])

        read(0, 0).start()

        @pl.loop(0, n_ch, step=2)
        def _(j0):
            for b in range(2):
                j = j0 + b

                @pl.when(j < n_ch)
                def _():
                    read(j, b).wait()

                    @pl.when(j + 1 < n_ch)
                    def _():
                        @pl.when(j >= 1)
                        def _():
                            for k in range(TOP_K):
                                write(j - 1, 1 - b, k).wait()

                        read(j + 1, 1 - b).start()

                    for k in range(TOP_K):
                        write(j, b, k).start()

        for jj in range(max(n_ch - 2, 0), n_ch):
            for k in range(TOP_K):
                write(jj, jj % 2, k).wait()

    return scatter_rows(x, idx)


def _sc_gather(table, idx):
    m = idx.shape[0]
    w = table.shape[1]
    nc, _, per_w, chunk = _sc_split(m, 64)
    n_ch = per_w // chunk

    @functools.partial(
        pl.kernel, out_type=jax.ShapeDtypeStruct((m, w), table.dtype), mesh=_sc_mesh(),
        scratch_types=[pltpu.VMEM((per_w,), jnp.int32), pltpu.VMEM((2, chunk, w), table.dtype),
                       pltpu.SemaphoreType.DMA((2,)), pltpu.SemaphoreType.DMA((2,))])
    def gather_rows(t_hbm, i_hbm, o_hbm, idx_v, rows_v, gsem, wsem):
        wid = lax.axis_index("subcore") * nc + lax.axis_index("core")
        base = wid * per_w
        pltpu.sync_copy(i_hbm.at[pl.ds(base, per_w)], idx_v)

        def gather(j, slot):
            off = pl.multiple_of(j * chunk, chunk)
            return pltpu.make_async_copy(t_hbm.at[idx_v.at[pl.ds(off, chunk)]], rows_v.at[slot], gsem.at[slot])

        def write(j, slot):
            off = pl.multiple_of(j * chunk, chunk)
            return pltpu.make_async_copy(rows_v.at[slot], o_hbm.at[pl.ds(base + off, chunk)], wsem.at[slot])

        gather(0, 0).start()

        @pl.loop(0, n_ch, step=2)
        def _(j0):
            for b in range(2):
                j = j0 + b

                @pl.when(j < n_ch)
                def _():
                    gather(j, b).wait()

                    @pl.when(j + 1 < n_ch)
                    def _():
                        @pl.when(j >= 1)
                        def _():
                            write(j - 1, 1 - b).wait()

                        gather(j + 1, 1 - b).start()

                    write(j, b).start()

        for jj in range(max(n_ch - 2, 0), n_ch):
            write(jj, jj % 2).wait()

    return gather_rows(table, idx)


def _final_kernel(n_pt, d, h_ref, yg_ref, prob_ref, gtp_ref, gts_ref, g_ref, yp_ref, ys_ref):
    i = pl.program_id(0)
    p = prob_ref[...]
    moe_lo, moe_hi = None, None
    for k in range(TOP_K):
        lo, hi = _unpack_pair(yg_ref[k])
        pk = p[:, k:k + 1]
        moe_lo = pk * lo if moe_lo is None else moe_lo + pk * lo
        moe_hi = pk * hi if moe_hi is None else moe_hi + pk * hi
    moe = jnp.concatenate([moe_lo, moe_hi], axis=1)

    def body(gt, shape):
        h3 = h_ref[...].reshape(shape) + gt * moe.reshape(shape)
        ms = jnp.mean(h3 * h3, axis=-1, keepdims=True)
        return h3 * lax.rsqrt(ms + EPS) * g_ref[...]

    @pl.when(i < n_pt)
    def _():
        yp_ref[...] = body(gtp_ref[0], yp_ref.shape)

    @pl.when(i >= n_pt)
    def _():
        ys_ref[...] = body(gts_ref[0], ys_ref.shape)


def _final(tl, h, yg, probs, ada_p, ada_s, g_final, d):
    return pl.pallas_call(
        functools.partial(_final_kernel, tl.n_pt, d),
        grid=(tl.n,),
        in_specs=[tl.tok_spec(d), pl.BlockSpec((TOP_K, tl.tm, d // 2), lambda i: (0, i, 0)), tl.tok_spec(TOP_K),
                  tl.adap_spec(5, d), tl.adas_spec(5, d), _resident((1, 1, d))],
        out_specs=[tl.xp_spec(d), tl.xs_spec(d)],
        out_shape=[jax.ShapeDtypeStruct((tl.b, tl.t, d), F32), jax.ShapeDtypeStruct((tl.bs, tl.ts, d), F32)],
        compiler_params=pltpu.CompilerParams(dimension_semantics=("arbitrary",), vmem_limit_bytes=VMEM_LIMIT),
        name="final",
    )(h, yg, probs, ada_p, ada_s, g_final.reshape(1, 1, d))


def _pick(n, pref):
    t = min(n, pref)
    while n % t:
        t //= 2
    return t


def _forward(x_prompt, x_sample, c_prompt, c_sample, state_ret, state_gla, w_ada, b_ada, g_norm_mix, g_norm_ffn,
             w_in, w_gk_up, b_gk, g_gla_norm, w_ret_o, w_gla_o, w_out, w_router, b_router, w_up, b_up,
             w_down, b_down, g_final, *, tm, tb, gsz, tme):
    b, t, d = x_prompt.shape
    bs, ts, _ = x_sample.shape
    assert w_ada.shape[0] == 1, "single layer only"
    assert (b * t) % (2 * tb) == 0 and (b * t) % tm == 0
    e = N_EXPERTS
    tl = _Tiles(b, t, bs, ts, tm)
    n_tok = tl.n_tok

    ada = _ada(jnp.concatenate([c_prompt, c_sample], axis=0), w_ada[0], b_ada[0])
    ada_p = ada[:, :b].reshape(6, b, 1, d)
    ada_s = ada[:, b:].reshape(6, bs, 1, d)

    w_in0 = w_in[0]
    n_main = 6 * d
    w_main = _cast_columns(w_in0, n_main, d)
    w_mg = w_in0[:, n_main + GLA_GATE_RANK:].astype(BF16)
    w_glr = w_in0[:, n_main:n_main + GLA_GATE_RANK].astype(BF16)
    w_gk = w_gk_up[0].astype(BF16)
    bgk = b_gk[0].reshape(1, -1)
    ggn = g_gla_norm[0].reshape(1, -1)
    w_r = w_router[0]
    w_r_hi = w_r.astype(BF16)
    w_r_lo = (w_r - w_r_hi.astype(F32)).astype(BF16)
    route_w = (g_norm_ffn[0], w_ret_o[0].astype(BF16), w_gla_o[0].astype(BF16), w_out[0].astype(BF16),
               w_r_hi, w_r_lo, b_router[0])

    oret_p, ogla_p, mg_p, sret_p, sgla_p = _front_prompt(x_prompt, ada_p, g_norm_mix[0], w_main, w_mg, w_glr, tb,
                                                         w_gk, bgk, ggn)
    proj_s, glr_s = _inproj_sample(tl, x_sample, ada_s, g_norm_mix[0], w_main, w_mg, w_glr)
    oret_s, ogla_s, sret_s, sgla_s = _mix_sample(bs, ts, d, gsz, proj_s, glr_s, state_ret[0], state_gla[0],
                                                 w_gk, bgk, ggn)
    h, n2, idx_t, rank_t, probs, counts = _outproj(tl, oret_p, ogla_p, oret_s, ogla_s, mg_p, proj_s, x_prompt, x_sample,
                                               ada_p, ada_s, *route_w)

    counts = counts[0]
    gsize = ((counts + tme - 1) // tme) * tme
    ends = jnp.cumsum(gsize)
    offs = ends - gsize
    experts = jnp.arange(e, dtype=jnp.int32)
    pos_t = jnp.sum(jnp.where(idx_t[..., None] == experts, offs, 0), axis=-1) + rank_t
    max_tiles = (n_tok * TOP_K) // tme + e
    n_active = (ends[-1] // tme).astype(jnp.int32).reshape(1)
    tile_start = jnp.arange(max_tiles, dtype=jnp.int32) * tme
    tile_expert = jnp.minimum(jnp.sum((ends[None, :] <= tile_start[:, None]).astype(jnp.int32), axis=1), e - 1)
    is_first = jnp.logical_and(tile_start < ends[-1],
                               jnp.concatenate([jnp.ones((1,), bool), tile_expert[1:] != tile_expert[:-1]]))
    tile_group = jnp.cumsum(is_first.astype(jnp.int32)) - 1
    later = jnp.logical_and(experts[None, :] > experts[:, None], counts[None, :] > 0)
    next_of = jnp.min(jnp.where(later, experts[None, :], e), axis=1)
    next_of = jnp.where(next_of == e, -1, next_of)
    next_expert = jnp.sum(jnp.where(tile_expert[:, None] == experts, next_of, 0), axis=1).astype(jnp.int32)

    xs = _sc_dispatch(n2, pos_t, max_tiles * tme)
    ys = _experts(xs, tile_expert, n_active, tile_group, next_expert, w_up[0], b_up[0], w_down[0], b_down[0], tme)
    yg = _sc_gather(ys, pos_t.reshape(-1)).reshape(TOP_K, n_tok, d // 2)

    y_p, y_s = _final(tl, h, yg, probs, ada_p, ada_s, g_final, d)
    return (y_p, y_s, sret_p[None], sgla_p[None], sret_s[None], sgla_s[None])


def kernel(x_prompt, x_sample, c_prompt, c_sample, state_ret, state_gla, w_ada, b_ada, g_norm_mix, g_norm_ffn,
           w_in, w_gk_up, b_gk, g_gla_norm, w_ret_o, w_gla_o, w_out, w_router, b_router, w_up, b_up,
           w_down, b_down, g_final):
    t = x_prompt.shape[1]
    bs, ts = x_sample.shape[0], x_sample.shape[1]
    return _forward(x_prompt, x_sample, c_prompt, c_sample, state_ret, state_gla, w_ada, b_ada, g_norm_mix,
                    g_norm_ffn, w_in, w_gk_up, b_gk, g_gla_norm, w_ret_o, w_gla_o, w_out, w_router, b_router,
                    w_up, b_up, w_down, b_down, g_final,
                    tm=_pick(bs * ts, 512), tb=_pick(t, 256), gsz=_pick(bs, 8), tme=512)
```

```python
import functools

import jax
import jax.numpy as jnp
from jax import lax
from jax.experimental import pallas as pl
from jax.experimental.pallas import tpu as pltpu
from jax.experimental.pallas import tpu_sc as plsc

F32 = jnp.float32
BF16 = jnp.bfloat16

N_HEADS = 4
GLA_GATE_RANK = 16
GLA_GATE_NORM = 16.0
GLA_CHUNK = 64
ROPE_BASE = 10000.0
N_EXPERTS = 32
TOP_K = 4
SWIGLU_LIMIT = 7.0
SWIGLU_ALPHA = 1.702
EPS = 1e-6
PAST_LEN = 16384
N_SEG = 8
ROUTE_COLS = 128
EXPERT_ROW_SLABS = 2

VMEM_LIMIT = 56 * 1024 * 1024


def _mm(a, b):
    return jnp.dot(a, b, preferred_element_type=F32)


def _mm_nt(a, b):
    return lax.dot_general(a, b, (((1,), (1,)), ((), ())), preferred_element_type=F32)


def _silu(x):
    return x * jax.nn.sigmoid(x)


def _split_hi_lo(x):
    hi = x.astype(BF16)
    lo = (x - hi.astype(F32)).astype(BF16)
    return hi, lo


def _pack_pair(x):
    w = x.shape[1] // 2
    lo = lax.bitcast_convert_type(x[:, :w].astype(BF16).astype(F32), jnp.uint32)
    hi = lax.bitcast_convert_type(x[:, w:].astype(BF16).astype(F32), jnp.uint32)
    return (hi & jnp.uint32(0xFFFF0000)) | (lo >> 16)


def _unpack_pair(p):
    lo = lax.bitcast_convert_type(p << 16, F32)
    hi = lax.bitcast_convert_type(p & jnp.uint32(0xFFFF0000), F32)
    return lo, hi


def _rms_mod(x3, g, sc, sh):
    ms = jnp.mean(x3 * x3, axis=-1, keepdims=True)
    return x3 * lax.rsqrt(ms + EPS) * g * (1.0 + sc) + sh


def _resident(shape):
    zeros = (0,) * len(shape)
    return pl.BlockSpec(shape, lambda i: zeros, pipeline_mode=pl.Buffered(1))


def _const(shape):
    zeros = (0,) * len(shape)
    return pl.BlockSpec(shape, lambda i: zeros)


def _ada_kernel(c_ref, w_ref, b_ref, o_ref):
    cf = _silu(c_ref[...])
    o_ref[0] = _mm(cf.astype(BF16), w_ref[...].astype(BF16)) + b_ref[0]


def _ada(c_all, w_ada, b_ada):
    bc, d = c_all.shape
    n = w_ada.shape[1] // d
    return pl.pallas_call(
        _ada_kernel,
        grid=(n,),
        in_specs=[pl.BlockSpec((bc, d), lambda j: (0, 0)),
                  pl.BlockSpec((d, d), lambda j: (0, j)),
                  pl.BlockSpec((1, 1, d), lambda j: (j, 0, 0))],
        out_specs=pl.BlockSpec((1, bc, d), lambda j: (j, 0, 0)),
        out_shape=jax.ShapeDtypeStruct((n, bc, d), F32),
        compiler_params=pltpu.CompilerParams(dimension_semantics=("arbitrary",), vmem_limit_bytes=VMEM_LIMIT),
        name="ada",
    )(c_all, w_ada, b_ada.reshape(n, 1, d))


class _Tiles:
    def __init__(self, b, t, bs, ts, tm):
        assert t % tm == 0 and (bs * ts) % tm == 0 and tm % ts == 0
        self.b, self.t, self.bs, self.ts, self.tm = b, t, bs, ts, tm
        self.tpb = t // tm
        self.n_pt = b * self.tpb
        self.gs = tm // ts
        self.n_st = (bs * ts) // tm
        self.n = self.n_pt + self.n_st
        self.n_tok = b * t + bs * ts

    def xp_spec(self, d):
        last, tpb = self.n_pt - 1, self.tpb
        return pl.BlockSpec((1, self.tm, d), lambda i: (jnp.minimum(i, last) // tpb, jnp.minimum(i, last) % tpb, 0))

    def xs_spec(self, d):
        n_pt = self.n_pt
        return pl.BlockSpec((self.gs, self.ts, d), lambda i: (jnp.maximum(i - n_pt, 0), 0, 0))

    def adap_spec(self, which, d):
        last, tpb = self.n_pt - 1, self.tpb
        return pl.BlockSpec((1, 1, 1, d), lambda i: (which, jnp.minimum(i, last) // tpb, 0, 0))

    def adas_spec(self, which, d):
        n_pt = self.n_pt
        return pl.BlockSpec((1, self.gs, 1, d), lambda i: (which, jnp.maximum(i - n_pt, 0), 0, 0))

    def tok_spec(self, width):
        return pl.BlockSpec((self.tm, width), lambda i: (i, 0))

    def s_x_spec(self, d):
        return pl.BlockSpec((self.gs, self.ts, d), lambda i: (i, 0, 0))

    def s_ada_spec(self, which, d):
        return pl.BlockSpec((1, self.gs, 1, d), lambda i: (which, i, 0, 0))

    def s_row_spec(self, width):
        return pl.BlockSpec((self.tm, width), lambda i: (i, 0))

    def s_tok_spec(self, width):
        n_pt = self.n_pt
        return pl.BlockSpec((self.tm, width), lambda i: (n_pt + i, 0))


def _rope_tables(pos0, t, dk):
    half = dk // 2
    inv = ROPE_BASE ** (-jnp.arange(half, dtype=jnp.float32) / half)
    pos = pos0 + jnp.arange(t)
    ang = pos.astype(jnp.float32)[:, None] * inv[None, :]
    cos, sin = jnp.cos(ang), jnp.sin(ang)
    return jnp.concatenate([cos, cos], axis=-1), jnp.concatenate([-sin, sin], axis=-1)


def _ret_tables(c, dk, dv):
    h = N_HEADS
    log_gamma = jnp.log1p(-jnp.exp2(-5.0 - jnp.arange(h, dtype=jnp.float32)))
    idx = jnp.arange(c, dtype=jnp.float32)
    rel = idx[:, None] - idx[None, :]
    dmask = jnp.where(rel >= 0, jnp.exp(log_gamma[:, None, None] * jnp.maximum(rel, 0.0)), 0.0)
    kdec = jnp.exp(log_gamma[:, None] * (c - 1 - idx))
    qdec = jnp.exp(log_gamma[:, None] * (idx + 1.0))
    cdec = jnp.exp(log_gamma * c)
    return (dmask,
            jnp.broadcast_to(qdec[:, :, None], (h, c, dk)),
            jnp.broadcast_to(kdec[:, :, None], (h, c, dk)),
            jnp.broadcast_to(cdec[:, None, None], (h, 1, dv)))


def _rot(x, cos_f, sin_f):
    return x * cos_f + pltpu.roll(x, x.shape[-1] // 2, 1) * sin_f


def _cross_and_update(q_lhs, k_end, vh, states, masks):
    if masks is None:
        (s,) = states
        return _mm(q_lhs, s.astype(BF16)), [_mm(k_end.T.astype(BF16), vh)]
    cross, incs = None, []
    for s, m in zip(states, masks):
        c = _mm(q_lhs, s.astype(BF16))
        cross = c if cross is None else jnp.where(m, c, cross)
        incs.append(_mm(jnp.where(m, k_end, 0.0).T.astype(BF16), vh))
    return cross, incs


def _ret_head(q, k, vh, gh, states, masks, cos_f, sin_f, dmask, qdec, kdec, cdec):
    dk = q.shape[-1]
    q = _rot(q, cos_f, sin_f)
    k = _rot(k, cos_f, sin_f) * (dk ** -0.5)
    scores = _mm_nt(q.astype(BF16), k.astype(BF16)) * dmask
    cross, incs = _cross_and_update((q * qdec).astype(BF16), k * kdec, vh, states, masks)
    o = _mm(scores.astype(BF16), vh) + cross
    new_states = [cdec * s + u for s, u in zip(states, incs)]
    mu = jnp.mean(o, axis=-1, keepdims=True)
    oc = o - mu
    var = jnp.mean(oc * oc, axis=-1, keepdims=True)
    return _silu(gh) * (oc * lax.rsqrt(var + EPS)), new_states


def _gla_head(q, k, vh, gh, b, states, masks, c, gnorm, causal):
    dk = q.shape[-1]
    b_t = b.T
    if masks is None:
        b_last = b[c - 1:c, :]
    else:
        b_last = None
        for g, m in enumerate(masks):
            row = b[g * c + c - 1:g * c + c, :]
            b_last = row if b_last is None else jnp.where(m, row, b_last)
    q_in = (q * (dk ** -0.5) * jnp.exp(b)).astype(BF16)
    k_in = (k * jnp.exp(-b)).astype(BF16)
    scores = jnp.where(causal, _mm_nt(q_in, k_in), 0.0)
    cross, incs = _cross_and_update(q_in, k * jnp.exp(b_last - b), vh, states, masks)
    o = _mm(scores.astype(BF16), vh) + cross
    new_states = [jnp.exp(b_t[:, g * c + c - 1:g * c + c]) * s + u for g, (s, u) in enumerate(zip(states, incs))]
    o = o * lax.rsqrt(jnp.mean(o * o, axis=-1, keepdims=True) + EPS) * gnorm
    return _silu(gh) * o, new_states


def _log_a(glr, wgk, bgk):
    z = _mm(glr.astype(BF16), wgk) + bgk
    return (jnp.minimum(z, 0.0) - jnp.log1p(jnp.exp(-jnp.abs(z)))) / GLA_GATE_NORM


def _causal(c):
    return lax.broadcasted_iota(jnp.int32, (c, c), 0) >= lax.broadcasted_iota(jnp.int32, (c, c), 1)


def _w_seg(w_ref, wm_ref, seg, d):
    if seg < N_SEG - 2:
        return w_ref[:, seg * d:(seg + 1) * d]
    return wm_ref[:, (seg - (N_SEG - 2)) * d:(seg - (N_SEG - 3)) * d]


def _proj_block(d, x3, sh, sc, g, w_ref, wm_ref, wl_ref, proj_s, glr_s):
    n = _rms_mod(x3, g, sc, sh).reshape(-1, d).astype(BF16)
    for seg in range(N_SEG):
        proj_s[:, seg * d:(seg + 1) * d] = _mm(n, _w_seg(w_ref, wm_ref, seg, d)).astype(BF16)
    glr_s[...] = _mm(n, wl_ref[...])


def _mix_block(d, tb, proj_s, glr_s, cos_f, sin_f, dmask_ref, qdec_ref, kdec_ref, cdec_ref, tri, wgk, bgk, gnorm,
               sr_s, sg_s, oret_ref, ogla_ref, mg_ref, r_off):
    dk, dv, hq = d // 8, d // 4, d // 2
    rqk, rv, rg, gqk, gv, gg, mg = (i * d for i in range(7))
    for h in range(N_HEADS):
        o, (s_new,) = _ret_head(proj_s[:, rqk + h * dk:rqk + (h + 1) * dk].astype(F32),
                                proj_s[:, rqk + hq + h * dk:rqk + hq + (h + 1) * dk].astype(F32),
                                proj_s[:, rv + h * dv:rv + (h + 1) * dv],
                                proj_s[:, rg + h * dv:rg + (h + 1) * dv].astype(F32),
                                [sr_s[h]], None, cos_f, sin_f, dmask_ref[h], qdec_ref[h], kdec_ref[h],
                                cdec_ref[h])
        sr_s[h] = s_new
        oret_ref[r_off:r_off + tb, h * dv:(h + 1) * dv] = o.astype(BF16)

    la_hi, la_lo = _split_hi_lo(_log_a(glr_s[...], wgk, bgk))
    b = _mm(tri, la_hi) + _mm(tri, la_lo)
    cg = GLA_CHUNK
    n_c = tb // cg
    causal = _causal(cg)
    b_last = jnp.concatenate([jnp.broadcast_to(b[c * cg + cg - 1:c * cg + cg, :], (cg, hq)) for c in range(n_c)],
                             axis=0)
    gq = proj_s[:, gqk:gqk + hq].astype(F32)
    gk = proj_s[:, gqk + hq:gqk + 2 * hq].astype(F32)
    q_in = (gq * (dk ** -0.5) * jnp.exp(b)).astype(BF16)
    k_in = (gk * jnp.exp(-b)).astype(BF16)
    k_end = gk * jnp.exp(b_last - b)
    intra, incs, decs = {}, {}, {}
    for c in range(n_c):
        rows = slice(c * cg, (c + 1) * cg)
        for h in range(N_HEADS):
            cols = slice(h * dk, (h + 1) * dk)
            vh = proj_s[rows, gv + h * dv:gv + (h + 1) * dv]
            scores = jnp.where(causal, _mm_nt(q_in[rows, cols], k_in[rows, cols]), 0.0)
            intra[c, h] = _mm(scores.astype(BF16), vh)
            incs[c, h] = _mm(k_end[rows, cols].T.astype(BF16), vh)
            decs[c, h] = jnp.exp(b[rows, cols].T[:, cg - 1:cg])
    for h in range(N_HEADS):
        cols = slice(h * dk, (h + 1) * dk)
        s = sg_s[h]
        for c in range(n_c):
            rows = slice(c * cg, (c + 1) * cg)
            o = intra[c, h] + _mm(q_in[rows, cols], s.astype(BF16))
            s = decs[c, h] * s + incs[c, h]
            o = o * lax.rsqrt(jnp.mean(o * o, axis=-1, keepdims=True) + EPS) * gnorm
            gh = proj_s[rows, gg + h * dv:gg + (h + 1) * dv].astype(F32)
            ogla_ref[r_off + c * cg:r_off + (c + 1) * cg, h * dv:(h + 1) * dv] = (_silu(gh) * o).astype(BF16)
        sg_s[h] = s
    mg_ref[0, r_off:r_off + tb, :] = proj_s[:, mg:mg + d]
    mg_ref[1, r_off:r_off + tb, :] = proj_s[:, mg + d:mg + 2 * d]


def _route_block(d, out_ret, out_gla, mg_ret, mg_gla, x3, gt, sh, sc, g, wro_ref, wgo_ref, wo_ref, wrh_ref, wrl_ref,
                 br, carry_s, h_ref, n2_ref, idx_ref, rank_ref, prob_ref, r_off):
    rows = out_ret.shape[0]
    e = N_EXPERTS
    a = _mm(out_ret, wro_ref[...])
    b = _mm(out_gla, wgo_ref[...])
    merged = jax.nn.sigmoid(mg_ret.astype(F32)) * a + jax.nn.sigmoid(mg_gla.astype(F32)) * b
    mix = _mm(merged.astype(BF16), wo_ref[...])
    h3 = x3 + gt * mix.reshape(x3.shape)
    out_rows = slice(r_off, r_off + rows)
    h_ref[out_rows, :] = h3.reshape(rows, d)
    n2 = _rms_mod(h3, g, sc, sh).reshape(rows, d)
    n2_ref[out_rows, :] = _pack_pair(n2)

    n_hi, n_lo = _split_hi_lo(n2)
    logits = _mm(n_hi, wrh_ref[...]) + _mm(n_lo, wrh_ref[...]) + _mm(n_hi, wrl_ref[...]) + br
    iota = lax.broadcasted_iota(jnp.int32, (rows, e), 1)
    work = logits
    vals, idxs = [], []
    for _ in range(TOP_K):
        m = jnp.max(work, axis=-1, keepdims=True)
        ik = jnp.min(jnp.where(work == m, iota, e), axis=-1, keepdims=True)
        vals.append(m)
        idxs.append(ik)
        work = jnp.where(iota == ik, -jnp.inf, work)
    ex = [jnp.exp(v - vals[0]) for v in vals]
    den = ex[0] + ex[1] + ex[2] + ex[3]

    onehot = jnp.zeros((rows, e), F32)
    for ik in idxs:
        onehot = onehot + (iota == ik).astype(F32)
    ltri = (lax.broadcasted_iota(jnp.int32, (rows, rows), 0) > lax.broadcasted_iota(jnp.int32, (rows, rows), 1))
    cum = _mm(ltri.astype(F32).astype(BF16), onehot.astype(BF16)) + carry_s[...]
    lane = lax.broadcasted_iota(jnp.int32, (rows, TOP_K), 1)
    prob_o = jnp.zeros((rows, TOP_K), F32)
    col = lax.broadcasted_iota(jnp.int32, (rows, ROUTE_COLS), 1)
    table = jnp.zeros((rows, ROUTE_COLS), F32)
    for k in range(TOP_K):
        rk = jnp.sum(jnp.where(iota == idxs[k], cum, 0.0), axis=-1, keepdims=True).astype(jnp.int32)
        prob_o = jnp.where(lane == k, ex[k] / den, prob_o)
        for g, piece in enumerate((idxs[k], rk & 255, (rk >> 8) & 255, rk >> 16)):
            table = jnp.where(col == 8 * g + k, piece.astype(F32), table)
    eye = (lax.broadcasted_iota(jnp.int32, (ROUTE_COLS, ROUTE_COLS), 0)
           == lax.broadcasted_iota(jnp.int32, (ROUTE_COLS, ROUTE_COLS), 1)).astype(F32).astype(BF16)
    t = _mm_nt(eye, table.astype(BF16))
    idx_ref[:, out_rows] = t[0:TOP_K].astype(jnp.int32)
    rank_ref[:, out_rows] = (t[8:8 + TOP_K] + 256.0 * t[16:16 + TOP_K] + 65536.0 * t[24:24 + TOP_K]).astype(jnp.int32)
    prob_ref[out_rows, :] = prob_o
    carry_s[...] = carry_s[...] + jnp.sum(onehot, axis=0, keepdims=True)


def _frontp_kernel(d, tb, ntb, x0_ref, xa_ref, xb_ref, sh0_ref, sc0_ref, sha_ref, sca_ref, shb_ref, scb_ref,
                   g_ref, w_ref, wm_ref, wl_ref, cosa_ref, sina_ref, cosb_ref, sinb_ref,
                   dmask_ref, qdec_ref, kdec_ref, cdec_ref, tri_ref, wgk_ref, bgk_ref, gn_ref,
                   oret_ref, ogla_ref, mg_ref, sret_ref, sgla_ref, pa_s, pb_s, ga_s, gb_s, sr_s, sg_s):
    p = pl.program_id(0)
    blk = 2 * p
    g = g_ref[...]
    proj = functools.partial(_proj_block, d)
    mix = functools.partial(_mix_block, d, tb)
    tables = (dmask_ref, qdec_ref, kdec_ref, cdec_ref, tri_ref[...], wgk_ref[...], bgk_ref[...], gn_ref[...])

    @pl.when(p == 0)
    def _():
        proj(x0_ref[...], sh0_ref[0], sc0_ref[0], g, w_ref, wm_ref, wl_ref, pa_s, ga_s)

    @pl.when(blk % ntb == 0)
    def _():
        sr_s[...] = jnp.zeros_like(sr_s)
        sg_s[...] = jnp.zeros_like(sg_s)

    proj(xa_ref[...], sha_ref[0], sca_ref[0], g, w_ref, wm_ref, wl_ref, pb_s, gb_s)
    mix(pa_s, ga_s, cosa_ref[...], sina_ref[...], *tables, sr_s, sg_s, oret_ref, ogla_ref, mg_ref, 0)
    proj(xb_ref[...], shb_ref[0], scb_ref[0], g, w_ref, wm_ref, wl_ref, pa_s, ga_s)
    mix(pb_s, gb_s, cosb_ref[...], sinb_ref[...], *tables, sr_s, sg_s, oret_ref, ogla_ref, mg_ref, tb)

    @pl.when((blk + 1) % ntb == ntb - 1)
    def _():
        sret_ref[0] = sr_s[...]
        sgla_ref[0] = sg_s[...]


def _chunk_tri(tb, cg):
    i = jnp.arange(tb)
    return ((i[:, None] >= i[None, :]) & (i[:, None] // cg == i[None, :] // cg)).astype(BF16)


def _front_prompt(x_p, ada_p, g_mix, w_main, w_mg, w_glr, tb, w_gk, b_gk, g_gla):
    b, t, d = x_p.shape
    dk, dv, hq, h = d // 8, d // 4, d // 2, N_HEADS
    ntb = t // tb
    n_blk = b * ntb
    n_tok = b * t
    assert ntb % 2 == 0
    cos_f, sin_f = _rope_tables(0, t, dk)
    dmask, qdec, kdec, cdec = _ret_tables(tb, dk, dv)
    tri = _chunk_tri(tb, GLA_CHUNK)

    def first(p):
        return 0 * p

    def even(p):
        return 2 * p

    def odd(p):
        return 2 * p + 1

    def nxt(p):
        return jnp.minimum(2 * p + 2, n_blk - 1)

    def x_spec(blk_of):
        return pl.BlockSpec((1, tb, d), lambda p: (blk_of(p) // ntb, blk_of(p) % ntb, 0))

    def ada_spec(which, blk_of):
        return pl.BlockSpec((1, 1, 1, d), lambda p: (which, blk_of(p) // ntb, 0, 0))

    def rope_spec(blk_of):
        return pl.BlockSpec((tb, dk), lambda p: (blk_of(p) % ntb, 0))

    state_spec = pl.BlockSpec((1, h, dk, dv), lambda p: ((2 * p) // ntb, 0, 0, 0))
    tok_spec = pl.BlockSpec((2 * tb, d), lambda p: (p, 0))
    return pl.pallas_call(
        functools.partial(_frontp_kernel, d, tb, ntb),
        grid=(n_blk // 2,),
        in_specs=[_resident((1, tb, d)), x_spec(odd), x_spec(nxt),
                  ada_spec(0, first), ada_spec(1, first), ada_spec(0, odd), ada_spec(1, odd),
                  ada_spec(0, nxt), ada_spec(1, nxt),
                  _resident((1, 1, d)), _resident((d, (N_SEG - 2) * d)), _resident((d, 2 * d)),
                  _resident((d, GLA_GATE_RANK)),
                  rope_spec(even), rope_spec(even), rope_spec(odd), rope_spec(odd),
                  _resident((h, tb, tb)), _resident((h, tb, dk)), _resident((h, tb, dk)), _resident((h, 1, dv)),
                  _resident((tb, tb)), _resident((GLA_GATE_RANK, hq)), _resident((1, hq)), _resident((1, dv))],
        out_specs=[tok_spec, tok_spec, pl.BlockSpec((2, 2 * tb, d), lambda p: (0, p, 0)), state_spec, state_spec],
        out_shape=[jax.ShapeDtypeStruct((n_tok, d), BF16), jax.ShapeDtypeStruct((n_tok, d), BF16),
                   jax.ShapeDtypeStruct((2, n_tok, d), BF16),
                   jax.ShapeDtypeStruct((b, h, dk, dv), F32), jax.ShapeDtypeStruct((b, h, dk, dv), F32)],
        scratch_shapes=[pltpu.VMEM((tb, N_SEG * d), BF16), pltpu.VMEM((tb, N_SEG * d), BF16),
                        pltpu.VMEM((tb, GLA_GATE_RANK), F32), pltpu.VMEM((tb, GLA_GATE_RANK), F32),
                        pltpu.VMEM((h, dk, dv), F32), pltpu.VMEM((h, dk, dv), F32)],
        compiler_params=pltpu.CompilerParams(dimension_semantics=("arbitrary",), vmem_limit_bytes=VMEM_LIMIT),
        name="front_prompt",
    )(x_p, x_p, x_p, ada_p, ada_p, ada_p, ada_p, ada_p, ada_p, g_mix.reshape(1, 1, d), w_main, w_mg, w_glr,
      cos_f, sin_f, cos_f, sin_f, dmask, qdec, kdec, cdec, tri, w_gk, b_gk, g_gla)


def _inproj_kernel(d, xs_ref, shs_ref, scs_ref, g_ref, w_ref, wm_ref, wl_ref, proj_ref, glr_ref):
    n = _rms_mod(xs_ref[...], g_ref[...], scs_ref[0], shs_ref[0]).reshape(-1, d).astype(BF16)
    for s in range(N_SEG):
        proj_ref[s] = _mm(n, _w_seg(w_ref, wm_ref, s, d)).astype(BF16)
    glr_ref[...] = _mm(n, wl_ref[...])


def _inproj_sample(tl, x_s, ada_s, g_mix, w_main, w_mg, w_glr):
    bs, ts, d = x_s.shape
    n_tok = bs * ts
    return pl.pallas_call(
        functools.partial(_inproj_kernel, d),
        grid=(tl.n_st,),
        in_specs=[tl.s_x_spec(d), tl.s_ada_spec(0, d), tl.s_ada_spec(1, d),
                  _resident((1, 1, d)), _resident((d, (N_SEG - 2) * d)), _resident((d, 2 * d)),
                  _resident((d, GLA_GATE_RANK))],
        out_specs=[pl.BlockSpec((N_SEG, tl.tm, d), lambda i: (0, i, 0)), tl.s_row_spec(GLA_GATE_RANK)],
        out_shape=[jax.ShapeDtypeStruct((N_SEG, n_tok, d), BF16), jax.ShapeDtypeStruct((n_tok, GLA_GATE_RANK), F32)],
        compiler_params=pltpu.CompilerParams(dimension_semantics=("arbitrary",), vmem_limit_bytes=VMEM_LIMIT),
        name="inproj_sample",
    )(x_s, ada_s, ada_s, g_mix.reshape(1, 1, d), w_main, w_mg, w_glr)


def _mixs_kernel(d, ts, gsz, rqk_ref, rv_ref, rg_ref, gqk_ref, gv_ref, gg_ref, glr_ref, cos_ref, sin_ref,
                 dmask_ref, qdec_ref, kdec_ref, cdec_ref, wgk_ref, bgk_ref, gn_ref, sr_in, sg_in,
                 oret_ref, ogla_ref, sr_out, sg_out):
    dk, dv, hq = d // 8, d // 4, d // 2
    pair = 2 * ts
    cos_f, sin_f = cos_ref[...], sin_ref[...]
    gnorm = gn_ref[...]
    ri = lax.broadcasted_iota(jnp.int32, (pair, pair), 0)
    ci = lax.broadcasted_iota(jnp.int32, (pair, pair), 1)
    causal = jnp.logical_and(ri >= ci, (ri < ts) == (ci < ts))
    tri = causal.astype(F32).astype(BF16)
    first = lax.broadcasted_iota(jnp.int32, (pair, 1), 0) < ts
    masks = [first, jnp.logical_not(first)]

    def body(j, carry):
        rows = pl.ds(pl.multiple_of(j * pair, pair), pair)
        s0, s1 = 2 * j, 2 * j + 1
        la_hi, la_lo = _split_hi_lo(_log_a(glr_ref[rows, :], wgk_ref[...], bgk_ref[...]))
        b = _mm(tri, la_hi) + _mm(tri, la_lo)
        for h in range(N_HEADS):
            o, (n0, n1) = _ret_head(rqk_ref[0, rows, h * dk:(h + 1) * dk].astype(F32),
                                    rqk_ref[0, rows, hq + h * dk:hq + (h + 1) * dk].astype(F32),
                                    rv_ref[0, rows, h * dv:(h + 1) * dv],
                                    rg_ref[0, rows, h * dv:(h + 1) * dv].astype(F32),
                                    [sr_in[s0, h], sr_in[s1, h]], masks, cos_f, sin_f,
                                    dmask_ref[h], qdec_ref[h], kdec_ref[h], cdec_ref[h])
            sr_out[s0, h] = n0
            sr_out[s1, h] = n1
            oret_ref[rows, h * dv:(h + 1) * dv] = o.astype(BF16)
            o, (n0, n1) = _gla_head(gqk_ref[0, rows, h * dk:(h + 1) * dk].astype(F32),
                                    gqk_ref[0, rows, hq + h * dk:hq + (h + 1) * dk].astype(F32),
                                    gv_ref[0, rows, h * dv:(h + 1) * dv],
                                    gg_ref[0, rows, h * dv:(h + 1) * dv].astype(F32),
                                    b[:, h * dk:(h + 1) * dk], [sg_in[s0, h], sg_in[s1, h]], masks, ts,
                                    gnorm, causal)
            sg_out[s0, h] = n0
            sg_out[s1, h] = n1
            ogla_ref[rows, h * dv:(h + 1) * dv] = o.astype(BF16)
        return carry

    lax.fori_loop(0, gsz // 2, body, 0, unroll=2)


def _pair_tables(ts, dk, dv):
    cos_f, sin_f = _rope_tables(PAST_LEN, ts, dk)
    dmask, qdec, kdec, cdec = _ret_tables(ts, dk, dv)
    zero = jnp.zeros_like(dmask)
    dmask2 = jnp.concatenate([jnp.concatenate([dmask, zero], axis=2), jnp.concatenate([zero, dmask], axis=2)], axis=1)

    def twice(a, axis):
        return jnp.concatenate([a, a], axis=axis)

    return twice(cos_f, 0), twice(sin_f, 0), dmask2, twice(qdec, 1), twice(kdec, 1), cdec


def _mix_sample(bs, ts, d, gsz, proj, glr, state_ret, state_gla, w_gk, b_gk, g_gla):
    dk, dv, hq, h = d // 8, d // 4, d // 2, N_HEADS
    assert GLA_CHUNK % ts == 0 and bs % gsz == 0 and gsz % 4 == 0
    rows = gsz * ts
    pair = 2 * ts
    cos_f, sin_f, dmask, qdec, kdec, cdec = _pair_tables(ts, dk, dv)

    def seg(s):
        return pl.BlockSpec((1, rows, d), lambda i: (s, i, 0))

    state_spec = pl.BlockSpec((gsz, h, dk, dv), lambda i: (i, 0, 0, 0))
    tok_spec = pl.BlockSpec((rows, d), lambda i: (i, 0))
    return pl.pallas_call(
        functools.partial(_mixs_kernel, d, ts, gsz),
        grid=(bs // gsz,),
        in_specs=[seg(0), seg(1), seg(2), seg(3), seg(4), seg(5),
                  pl.BlockSpec((rows, GLA_GATE_RANK), lambda i: (i, 0)),
                  _const((pair, dk)), _const((pair, dk)),
                  _const((h, pair, pair)), _const((h, pair, dk)), _const((h, pair, dk)), _const((h, 1, dv)),
                  _const((GLA_GATE_RANK, hq)), _const((1, hq)), _const((1, dv)),
                  state_spec, state_spec],
        out_specs=[tok_spec, tok_spec, state_spec, state_spec],
        out_shape=[jax.ShapeDtypeStruct((bs * ts, d), BF16), jax.ShapeDtypeStruct((bs * ts, d), BF16),
                   jax.ShapeDtypeStruct((bs, h, dk, dv), F32), jax.ShapeDtypeStruct((bs, h, dk, dv), F32)],
        compiler_params=pltpu.CompilerParams(dimension_semantics=("arbitrary",), vmem_limit_bytes=VMEM_LIMIT),
        name="mix_sample",
    )(proj, proj, proj, proj, proj, proj, glr, cos_f, sin_f, dmask, qdec, kdec, cdec, w_gk, b_gk, g_gla,
      state_ret, state_gla)


def _outproj_kernel(n_pt, d, orp_ref, ogp_ref, ors_ref, ogs_ref, mgrp_ref, mggp_ref, mgrs_ref, mggs_ref,
                    xp_ref, xs_ref, gtp_ref, shp_ref, scp_ref, gts_ref, shs_ref, scs_ref, g_ref,
                    wro_ref, wgo_ref, wo_ref, wrh_ref, wrl_ref, br_ref,
                    h_ref, n2_ref, idx_ref, rank_ref, prob_ref, cnt_ref, carry_s):
    i = pl.program_id(0)

    @pl.when(i == 0)
    def _():
        carry_s[...] = jnp.zeros_like(carry_s)

    route = functools.partial(_route_block, d)
    tail = (g_ref[...], wro_ref, wgo_ref, wo_ref, wrh_ref, wrl_ref, br_ref[...], carry_s,
            h_ref, n2_ref, idx_ref, rank_ref, prob_ref, 0)

    @pl.when(i < n_pt)
    def _():
        route(orp_ref[...], ogp_ref[...], mgrp_ref[0], mggp_ref[0], xp_ref[...], gtp_ref[0], shp_ref[0], scp_ref[0],
              *tail)

    @pl.when(i >= n_pt)
    def _():
        route(ors_ref[...], ogs_ref[...], mgrs_ref[0], mggs_ref[0], xs_ref[...], gts_ref[0], shs_ref[0], scs_ref[0],
              *tail)

    @pl.when(i == pl.num_programs(0) - 1)
    def _():
        cnt_ref[...] = carry_s[...].astype(jnp.int32)


def _outproj(tl, oret_p, ogla_p, oret_s, ogla_s, mg_p, proj_s, x_p, x_s, ada_p, ada_s, g_ffn,
             w_ret_o, w_gla_o, w_out, w_r_hi, w_r_lo, b_router):
    d = x_p.shape[-1]
    tm, e, n_pt = tl.tm, N_EXPERTS, tl.n_pt
    last = n_pt - 1
    p_spec = pl.BlockSpec((tm, d), lambda i: (jnp.minimum(i, last), 0))
    s_spec = pl.BlockSpec((tm, d), lambda i: (jnp.maximum(i - n_pt, 0), 0))

    def mgp_spec(seg):
        return pl.BlockSpec((1, tm, d), lambda i: (seg, jnp.minimum(i, last), 0))

    def mgs_spec(seg):
        return pl.BlockSpec((1, tm, d), lambda i: (seg, jnp.maximum(i - n_pt, 0), 0))

    slot_spec = pl.BlockSpec((TOP_K, tm), lambda i: (0, i))

    return pl.pallas_call(
        functools.partial(_outproj_kernel, n_pt, d),
        grid=(tl.n,),
        in_specs=[p_spec, p_spec, s_spec, s_spec, mgp_spec(0), mgp_spec(1), mgs_spec(6), mgs_spec(7),
                  tl.xp_spec(d), tl.xs_spec(d),
                  tl.adap_spec(2, d), tl.adap_spec(3, d), tl.adap_spec(4, d),
                  tl.adas_spec(2, d), tl.adas_spec(3, d), tl.adas_spec(4, d),
                  _resident((1, 1, d)), _resident((d, d)), _resident((d, d)), _resident((d, d)),
                  _resident((d, e)), _resident((d, e)), _resident((1, e))],
        out_specs=[tl.tok_spec(d), tl.tok_spec(d // 2), slot_spec, slot_spec, tl.tok_spec(TOP_K),
                   pl.BlockSpec((1, e), lambda i: (0, 0))],
        out_shape=[jax.ShapeDtypeStruct((tl.n_tok, d), F32), jax.ShapeDtypeStruct((tl.n_tok, d // 2), jnp.uint32),
                   jax.ShapeDtypeStruct((TOP_K, tl.n_tok), jnp.int32),
                   jax.ShapeDtypeStruct((TOP_K, tl.n_tok), jnp.int32),
                   jax.ShapeDtypeStruct((tl.n_tok, TOP_K), F32),
                   jax.ShapeDtypeStruct((1, e), jnp.int32)],
        scratch_shapes=[pltpu.VMEM((1, e), F32)],
        compiler_params=pltpu.CompilerParams(dimension_semantics=("arbitrary",), vmem_limit_bytes=VMEM_LIMIT),
        name="outproj",
    )(oret_p, ogla_p, oret_s, ogla_s, mg_p, mg_p, proj_s, proj_s, x_p, x_s, ada_p, ada_p, ada_p, ada_s, ada_s, ada_s,
      g_ffn.reshape(1, 1, d), w_ret_o, w_gla_o, w_out, w_r_hi, w_r_lo, b_router.reshape(1, e))


def _expert_kernel(f, te_ref, na_ref, grp_ref, nxt_ref, x_ref, wu_hbm, bu_ref, wd_hbm, bd_ref, y_ref,
                   wu_f, wd_f, wu_s, wd_s, sem):
    j = pl.program_id(0)
    active = j < na_ref[0]
    first = jnp.logical_or(j == 0, te_ref[j] != te_ref[jnp.maximum(j - 1, 0)])
    slot = grp_ref[j] % 2

    def fetch(expert, s):
        return (pltpu.make_async_copy(wu_hbm.at[expert], wu_f.at[s], sem.at[0, s]),
                pltpu.make_async_copy(wd_hbm.at[expert], wd_f.at[s], sem.at[1, s]))

    @pl.when(j == 0)
    def _():
        for c in fetch(te_ref[0], 0):
            c.start()

    @pl.when(jnp.logical_and(active, first))
    def _():
        for c in fetch(te_ref[j], slot):
            c.wait()

        @pl.when(nxt_ref[j] >= 0)
        def _():
            for c in fetch(nxt_ref[j], 1 - slot):
                c.start()

        wu_s[...] = wu_f[slot].astype(BF16)
        wd_s[...] = wd_f[slot].astype(BF16)

    slab = x_ref.shape[0] // EXPERT_ROW_SLABS
    half = x_ref.shape[1]

    @pl.when(active)
    def _():
        for s in range(EXPERT_ROW_SLABS):
            rows = slice(s * slab, (s + 1) * slab)
            x_lo, x_hi = _unpack_pair(x_ref[rows, :])
            gu = _mm(x_lo.astype(BF16), wu_s[:half, :]) + _mm(x_hi.astype(BF16), wu_s[half:, :]) + bu_ref[0]
            gate = jnp.minimum(gu[:, :f], SWIGLU_LIMIT)
            up = jnp.clip(gu[:, f:], -SWIGLU_LIMIT, SWIGLU_LIMIT)
            act = (up + 1.0) * gate * jax.nn.sigmoid(SWIGLU_ALPHA * gate)
            y_ref[rows, :] = _pack_pair(_mm(act.astype(BF16), wd_s[...]) + bd_ref[0])


def _experts(xs, tile_expert, n_active, tile_group, next_expert, w_up, b_up, w_down, b_down, tme):
    r = xs.shape[0]
    e, d, f2 = w_up.shape
    f = f2 // 2
    n_tiles = r // tme

    def row_map(j, te, na, grp, nxt):
        return (jnp.minimum(j, na[0] - 1), 0)

    def b_map(j, te, na, grp, nxt):
        return (te[jnp.minimum(j, na[0] - 1)], 0, 0)

    hbm = pl.BlockSpec(memory_space=pl.ANY)
    return pl.pallas_call(
        functools.partial(_expert_kernel, f),
        grid_spec=pltpu.PrefetchScalarGridSpec(
            num_scalar_prefetch=4,
            grid=(n_tiles,),
            in_specs=[pl.BlockSpec((tme, d // 2), row_map),
                      hbm, pl.BlockSpec((1, 1, f2), b_map), hbm, pl.BlockSpec((1, 1, d), b_map)],
            out_specs=pl.BlockSpec((tme, d // 2), row_map),
            scratch_shapes=[pltpu.VMEM((2, d, f2), F32), pltpu.VMEM((2, f, d), F32),
                            pltpu.VMEM((d, f2), BF16), pltpu.VMEM((f, d), BF16),
                            pltpu.SemaphoreType.DMA((2, 2))]),
        out_shape=jax.ShapeDtypeStruct((r, d // 2), jnp.uint32),
        compiler_params=pltpu.CompilerParams(dimension_semantics=("arbitrary",), vmem_limit_bytes=VMEM_LIMIT),
        name="experts",
    )(tile_expert, n_active, tile_group, next_expert, xs, w_up, b_up.reshape(e, 1, f2), w_down,
      b_down.reshape(e, 1, d))


def _sc_mesh():
    return plsc.VectorSubcoreMesh(core_axis_name="core", subcore_axis_name="subcore")


def _sc_split(n_rows, max_chunk):
    info = plsc.get_sparse_core_info()
    n_workers = info.num_cores * info.num_subcores
    assert n_rows % (8 * n_workers) == 0
    per_w = n_rows // n_workers
    chunk = 8
    while chunk * 2 <= max_chunk and per_w % (chunk * 2) == 0:
        chunk *= 2
    return info.num_cores, n_workers, per_w, chunk


def _sc_dispatch(x, pos_t, n_rows):
    n, w = x.shape
    nc, nw, per_w, chunk = _sc_split(n, 32)
    n_ch = per_w // chunk
    idx = pos_t.reshape(TOP_K, nw, n_ch, chunk).transpose(1, 0, 2, 3).reshape(nw, TOP_K * n_ch, chunk)

    @functools.partial(
        pl.kernel, out_type=jax.ShapeDtypeStruct((n_rows, w), x.dtype), mesh=_sc_mesh(),
        scratch_types=[pltpu.VMEM((TOP_K * n_ch, chunk), jnp.int32), pltpu.VMEM((2, chunk, w), x.dtype),
                       pltpu.SemaphoreType.DMA((2,)), pltpu.SemaphoreType.DMA((2,))])
    def scatter_rows(x_hbm, i_hbm, o_hbm, idx_v, rows_v, rsem, wsem):
        wid = lax.axis_index("subcore") * nc + lax.axis_index("core")
        base = wid * per_w
        pltpu.sync_copy(i_hbm.at[wid], idx_v)

        def read(j, slot):
            return pltpu.make_async_copy(x_hbm.at[pl.ds(base + j * chunk, chunk)], rows_v.at[slot], rsem.at[slot])

        def write(j, slot, k):
            return pltpu.make_async_copy(rows_v.at[slot], o_hbm.at[idx_v.at[k * n_ch + j]], wsem.at[slot])

        read(0, 0).start()

        @pl.loop(0, n_ch, step=2)
        def _(j0):
            for b in range(2):
                j = j0 + b

                @pl.when(j < n_ch)
                def _():
                    read(j, b).wait()

                    @pl.when(j + 1 < n_ch)
                    def _():
                        @pl.when(j >= 1)
                        def _():
                            for k in range(TOP_K):
                                write(j - 1, 1 - b, k).wait()

                        read(j + 1, 1 - b).start()

                    for k in range(TOP_K):
                        write(j, b, k).start()

        for jj in range(max(n_ch - 2, 0), n_ch):
            for k in range(TOP_K):
                write(jj, jj % 2, k).wait()

    return scatter_rows(x, idx)


def _sc_gather(table, idx):
    m = idx.shape[0]
    w = table.shape[1]
    nc, _, per_w, chunk = _sc_split(m, 64)
    n_ch = per_w // chunk

    @functools.partial(
        pl.kernel, out_type=jax.ShapeDtypeStruct((m, w), table.dtype), mesh=_sc_mesh(),
        scratch_types=[pltpu.VMEM((per_w,), jnp.int32), pltpu.VMEM((2, chunk, w), table.dtype),
                       pltpu.SemaphoreType.DMA((2,)), pltpu.SemaphoreType.DMA((2,))])
    def gather_rows(t_hbm, i_hbm, o_hbm, idx_v, rows_v, gsem, wsem):
        wid = lax.axis_index("subcore") * nc + lax.axis_index("core")
        base = wid * per_w
        pltpu.sync_copy(i_hbm.at[pl.ds(base, per_w)], idx_v)

        def gather(j, slot):
            off = pl.multiple_of(j * chunk, chunk)
            return pltpu.make_async_copy(t_hbm.at[idx_v.at[pl.ds(off, chunk)]], rows_v.at[slot], gsem.at[slot])

        def write(j, slot):
            off = pl.multiple_of(j * chunk, chunk)
            return pltpu.make_async_copy(rows_v.at[slot], o_hbm.at[pl.ds(base + off, chunk)], wsem.at[slot])

        gather(0, 0).start()

        @pl.loop(0, n_ch, step=2)
        def _(j0):
            for b in range(2):
                j = j0 + b

                @pl.when(j < n_ch)
                def _():
                    gather(j, b).wait()

                    @pl.when(j + 1 < n_ch)
                    def _():
                        @pl.when(j >= 1)
                        def _():
                            write(j - 1, 1 - b).wait()

                        gather(j + 1, 1 - b).start()

                    write(j, b).start()

        for jj in range(max(n_ch - 2, 0), n_ch):
            write(jj, jj % 2).wait()

    return gather_rows(table, idx)


def _final_kernel(n_pt, d, h_ref, yg_ref, prob_ref, gtp_ref, gts_ref, g_ref, yp_ref, ys_ref):
    i = pl.program_id(0)
    p = prob_ref[...]
    moe_lo, moe_hi = None, None
    for k in range(TOP_K):
        lo, hi = _unpack_pair(yg_ref[k])
        pk = p[:, k:k + 1]
        moe_lo = pk * lo if moe_lo is None else moe_lo + pk * lo
        moe_hi = pk * hi if moe_hi is None else moe_hi + pk * hi
    moe = jnp.concatenate([moe_lo, moe_hi], axis=1)

    def body(gt, shape):
        h3 = h_ref[...].reshape(shape) + gt * moe.reshape(shape)
        ms = jnp.mean(h3 * h3, axis=-1, keepdims=True)
        return h3 * lax.rsqrt(ms + EPS) * g_ref[...]

    @pl.when(i < n_pt)
    def _():
        yp_ref[...] = body(gtp_ref[0], yp_ref.shape)

    @pl.when(i >= n_pt)
    def _():
        ys_ref[...] = body(gts_ref[0], ys_ref.shape)


def _final(tl, h, yg, probs, ada_p, ada_s, g_final, d):
    return pl.pallas_call(
        functools.partial(_final_kernel, tl.n_pt, d),
        grid=(tl.n,),
        in_specs=[tl.tok_spec(d), pl.BlockSpec((TOP_K, tl.tm, d // 2), lambda i: (0, i, 0)), tl.tok_spec(TOP_K),
                  tl.adap_spec(5, d), tl.adas_spec(5, d), _resident((1, 1, d))],
        out_specs=[tl.xp_spec(d), tl.xs_spec(d)],
        out_shape=[jax.ShapeDtypeStruct((tl.b, tl.t, d), F32), jax.ShapeDtypeStruct((tl.bs, tl.ts, d), F32)],
        compiler_params=pltpu.CompilerParams(dimension_semantics=("arbitrary",), vmem_limit_bytes=VMEM_LIMIT),
        name="final",
    )(h, yg, probs, ada_p, ada_s, g_final.reshape(1, 1, d))


def _pick(n, pref):
    t = min(n, pref)
    while n % t:
        t //= 2
    return t


def _forward(x_prompt, x_sample, c_prompt, c_sample, state_ret, state_gla, w_ada, b_ada, g_norm_mix, g_norm_ffn,
             w_in, w_gk_up, b_gk, g_gla_norm, w_ret_o, w_gla_o, w_out, w_router, b_router, w_up, b_up,
             w_down, b_down, g_final, *, tm, tb, gsz, tme):
    b, t, d = x_prompt.shape
    bs, ts, _ = x_sample.shape
    assert w_ada.shape[0] == 1, "single layer only"
    assert (b * t) % (2 * tb) == 0 and (b * t) % tm == 0
    e = N_EXPERTS
    tl = _Tiles(b, t, bs, ts, tm)
    n_tok = tl.n_tok

    ada = _ada(jnp.concatenate([c_prompt, c_sample], axis=0), w_ada[0], b_ada[0])
    ada_p = ada[:, :b].reshape(6, b, 1, d)
    ada_s = ada[:, b:].reshape(6, bs, 1, d)

    w_in0 = w_in[0]
    n_main = 6 * d
    w_main = w_in0[:, :n_main].astype(BF16)
    w_mg = w_in0[:, n_main + GLA_GATE_RANK:].astype(BF16)
    w_glr = w_in0[:, n_main:n_main + GLA_GATE_RANK].astype(BF16)
    w_gk = w_gk_up[0].astype(BF16)
    bgk = b_gk[0].reshape(1, -1)
    ggn = g_gla_norm[0].reshape(1, -1)
    w_r = w_router[0]
    w_r_hi = w_r.astype(BF16)
    w_r_lo = (w_r - w_r_hi.astype(F32)).astype(BF16)
    route_w = (g_norm_ffn[0], w_ret_o[0].astype(BF16), w_gla_o[0].astype(BF16), w_out[0].astype(BF16),
               w_r_hi, w_r_lo, b_router[0])

    oret_p, ogla_p, mg_p, sret_p, sgla_p = _front_prompt(x_prompt, ada_p, g_norm_mix[0], w_main, w_mg, w_glr, tb,
                                                         w_gk, bgk, ggn)
    proj_s, glr_s = _inproj_sample(tl, x_sample, ada_s, g_norm_mix[0], w_main, w_mg, w_glr)
    oret_s, ogla_s, sret_s, sgla_s = _mix_sample(bs, ts, d, gsz, proj_s, glr_s, state_ret[0], state_gla[0],
                                                 w_gk, bgk, ggn)
    h, n2, idx_t, rank_t, probs, counts = _outproj(tl, oret_p, ogla_p, oret_s, ogla_s, mg_p, proj_s, x_prompt, x_sample,
                                               ada_p, ada_s, *route_w)

    counts = counts[0]
    gsize = ((counts + tme - 1) // tme) * tme
    ends = jnp.cumsum(gsize)
    offs = ends - gsize
    experts = jnp.arange(e, dtype=jnp.int32)
    pos_t = jnp.sum(jnp.where(idx_t[..., None] == experts, offs, 0), axis=-1) + rank_t
    max_tiles = (n_tok * TOP_K) // tme + e
    n_active = (ends[-1] // tme).astype(jnp.int32).reshape(1)
    tile_start = jnp.arange(max_tiles, dtype=jnp.int32) * tme
    tile_expert = jnp.minimum(jnp.sum((ends[None, :] <= tile_start[:, None]).astype(jnp.int32), axis=1), e - 1)
    is_first = jnp.logical_and(tile_start < ends[-1],
                               jnp.concatenate([jnp.ones((1,), bool), tile_expert[1:] != tile_expert[:-1]]))
    tile_group = jnp.cumsum(is_first.astype(jnp.int32)) - 1
    later = jnp.logical_and(experts[None, :] > experts[:, None], counts[None, :] > 0)
    next_of = jnp.min(jnp.where(later, experts[None, :], e), axis=1)
    next_of = jnp.where(next_of == e, -1, next_of)
    next_expert = jnp.sum(jnp.where(tile_expert[:, None] == experts, next_of, 0), axis=1).astype(jnp.int32)

    xs = _sc_dispatch(n2, pos_t, max_tiles * tme)
    ys = _experts(xs, tile_expert, n_active, tile_group, next_expert, w_up[0], b_up[0], w_down[0], b_down[0], tme)
    yg = _sc_gather(ys, pos_t.reshape(-1)).reshape(TOP_K, n_tok, d // 2)

    y_p, y_s = _final(tl, h, yg, probs, ada_p, ada_s, g_final, d)
    return (y_p, y_s, sret_p[None], sgla_p[None], sret_s[None], sgla_s[None])


def kernel(x_prompt, x_sample, c_prompt, c_sample, state_ret, state_gla, w_ada, b_ada, g_norm_mix, g_norm_ffn,
           w_in, w_gk_up, b_gk, g_gla_norm, w_ret_o, w_gla_o, w_out, w_router, b_router, w_up, b_up,
           w_down, b_down, g_final):
    t = x_prompt.shape[1]
    bs, ts = x_sample.shape[0], x_sample.shape[1]
    return _forward(x_prompt, x_sample, c_prompt, c_sample, state_ret, state_gla, w_ada, b_ada, g_norm_mix,
                    g_norm_ffn, w_in, w_gk_up, b_gk, g_gla_norm, w_ret_o, w_gla_o, w_out, w_router, b_router,
                    w_up, b_up, w_down, b_down, g_final,
                    tm=_pick(bs * ts, 512), tb=_pick(t, 256), gsz=_pick(bs, 8), tme=512)
```

```python
import functools

import jax
import jax.numpy as jnp
from jax import lax
from jax.experimental import pallas as pl
from jax.experimental.pallas import tpu as pltpu
from jax.experimental.pallas import tpu_sc as plsc

F32 = jnp.float32
BF16 = jnp.bfloat16

N_HEADS = 4
GLA_GATE_RANK = 16
GLA_GATE_NORM = 16.0
GLA_CHUNK = 64
ROPE_BASE = 10000.0
N_EXPERTS = 32
TOP_K = 4
SWIGLU_LIMIT = 7.0
SWIGLU_ALPHA = 1.702
EPS = 1e-6
PAST_LEN = 16384
N_SEG = 8
EXPERT_ROW_SLABS = 2

VMEM_LIMIT = 56 * 1024 * 1024


def _mm(a, b):
    return jnp.dot(a, b, preferred_element_type=F32)


def _mm_nt(a, b):
    return lax.dot_general(a, b, (((1,), (1,)), ((), ())), preferred_element_type=F32)


def _silu(x):
    return x * jax.nn.sigmoid(x)


def _split_hi_lo(x):
    hi = x.astype(BF16)
    lo = (x - hi.astype(F32)).astype(BF16)
    return hi, lo


def _pack_pair(x):
    w = x.shape[1] // 2
    lo = lax.bitcast_convert_type(x[:, :w].astype(BF16).astype(F32), jnp.uint32)
    hi = lax.bitcast_convert_type(x[:, w:].astype(BF16).astype(F32), jnp.uint32)
    return (hi & jnp.uint32(0xFFFF0000)) | (lo >> 16)


def _unpack_pair(p):
    lo = lax.bitcast_convert_type(p << 16, F32)
    hi = lax.bitcast_convert_type(p & jnp.uint32(0xFFFF0000), F32)
    return lo, hi


def _rms_mod(x3, g, sc, sh):
    ms = jnp.mean(x3 * x3, axis=-1, keepdims=True)
    return x3 * lax.rsqrt(ms + EPS) * g * (1.0 + sc) + sh


def _resident(shape):
    zeros = (0,) * len(shape)
    return pl.BlockSpec(shape, lambda i: zeros, pipeline_mode=pl.Buffered(1))


def _const(shape):
    zeros = (0,) * len(shape)
    return pl.BlockSpec(shape, lambda i: zeros)


def _ada_kernel(c_ref, w_ref, b_ref, o_ref):
    cf = _silu(c_ref[...])
    o_ref[0] = _mm(cf.astype(BF16), w_ref[...].astype(BF16)) + b_ref[0]


def _ada(c_all, w_ada, b_ada):
    bc, d = c_all.shape
    n = w_ada.shape[1] // d
    return pl.pallas_call(
        _ada_kernel,
        grid=(n,),
        in_specs=[pl.BlockSpec((bc, d), lambda j: (0, 0)),
                  pl.BlockSpec((d, d), lambda j: (0, j)),
                  pl.BlockSpec((1, 1, d), lambda j: (j, 0, 0))],
        out_specs=pl.BlockSpec((1, bc, d), lambda j: (j, 0, 0)),
        out_shape=jax.ShapeDtypeStruct((n, bc, d), F32),
        compiler_params=pltpu.CompilerParams(dimension_semantics=("arbitrary",), vmem_limit_bytes=VMEM_LIMIT),
        name="ada",
    )(c_all, w_ada, b_ada.reshape(n, 1, d))


class _Tiles:
    def __init__(self, b, t, bs, ts, tm):
        assert t % tm == 0 and (bs * ts) % tm == 0 and tm % ts == 0
        self.b, self.t, self.bs, self.ts, self.tm = b, t, bs, ts, tm
        self.tpb = t // tm
        self.n_pt = b * self.tpb
        self.gs = tm // ts
        self.n_st = (bs * ts) // tm
        self.n = self.n_pt + self.n_st
        self.n_tok = b * t + bs * ts

    def xp_spec(self, d):
        last, tpb = self.n_pt - 1, self.tpb
        return pl.BlockSpec((1, self.tm, d), lambda i: (jnp.minimum(i, last) // tpb, jnp.minimum(i, last) % tpb, 0))

    def xs_spec(self, d):
        n_pt = self.n_pt
        return pl.BlockSpec((self.gs, self.ts, d), lambda i: (jnp.maximum(i - n_pt, 0), 0, 0))

    def adap_spec(self, which, d):
        last, tpb = self.n_pt - 1, self.tpb
        return pl.BlockSpec((1, 1, 1, d), lambda i: (which, jnp.minimum(i, last) // tpb, 0, 0))

    def adas_spec(self, which, d):
        n_pt = self.n_pt
        return pl.BlockSpec((1, self.gs, 1, d), lambda i: (which, jnp.maximum(i - n_pt, 0), 0, 0))

    def tok_spec(self, width):
        return pl.BlockSpec((self.tm, width), lambda i: (i, 0))

    def s_x_spec(self, d):
        return pl.BlockSpec((self.gs, self.ts, d), lambda i: (i, 0, 0))

    def s_ada_spec(self, which, d):
        return pl.BlockSpec((1, self.gs, 1, d), lambda i: (which, i, 0, 0))

    def s_row_spec(self, width):
        return pl.BlockSpec((self.tm, width), lambda i: (i, 0))

    def s_tok_spec(self, width):
        n_pt = self.n_pt
        return pl.BlockSpec((self.tm, width), lambda i: (n_pt + i, 0))


def _rope_tables(pos0, t, dk):
    half = dk // 2
    inv = ROPE_BASE ** (-jnp.arange(half, dtype=jnp.float32) / half)
    pos = pos0 + jnp.arange(t)
    ang = pos.astype(jnp.float32)[:, None] * inv[None, :]
    cos, sin = jnp.cos(ang), jnp.sin(ang)
    return jnp.concatenate([cos, cos], axis=-1), jnp.concatenate([-sin, sin], axis=-1)


def _ret_tables(c, dk, dv):
    h = N_HEADS
    log_gamma = jnp.log1p(-jnp.exp2(-5.0 - jnp.arange(h, dtype=jnp.float32)))
    idx = jnp.arange(c, dtype=jnp.float32)
    rel = idx[:, None] - idx[None, :]
    dmask = jnp.where(rel >= 0, jnp.exp(log_gamma[:, None, None] * jnp.maximum(rel, 0.0)), 0.0)
    kdec = jnp.exp(log_gamma[:, None] * (c - 1 - idx))
    qdec = jnp.exp(log_gamma[:, None] * (idx + 1.0))
    cdec = jnp.exp(log_gamma * c)
    return (dmask,
            jnp.broadcast_to(qdec[:, :, None], (h, c, dk)),
            jnp.broadcast_to(kdec[:, :, None], (h, c, dk)),
            jnp.broadcast_to(cdec[:, None, None], (h, 1, dv)))


def _rot(x, cos_f, sin_f):
    return x * cos_f + pltpu.roll(x, x.shape[-1] // 2, 1) * sin_f


def _cross_and_update(q_lhs, k_end, vh, states, masks):
    if masks is None:
        (s,) = states
        return _mm(q_lhs, s.astype(BF16)), [_mm(k_end.T.astype(BF16), vh)]
    cross, incs = None, []
    for s, m in zip(states, masks):
        c = _mm(q_lhs, s.astype(BF16))
        cross = c if cross is None else jnp.where(m, c, cross)
        incs.append(_mm(jnp.where(m, k_end, 0.0).T.astype(BF16), vh))
    return cross, incs


def _ret_head(q, k, vh, gh, states, masks, cos_f, sin_f, dmask, qdec, kdec, cdec):
    dk = q.shape[-1]
    q = _rot(q, cos_f, sin_f)
    k = _rot(k, cos_f, sin_f) * (dk ** -0.5)
    scores = _mm_nt(q.astype(BF16), k.astype(BF16)) * dmask
    cross, incs = _cross_and_update((q * qdec).astype(BF16), k * kdec, vh, states, masks)
    o = _mm(scores.astype(BF16), vh) + cross
    new_states = [cdec * s + u for s, u in zip(states, incs)]
    mu = jnp.mean(o, axis=-1, keepdims=True)
    oc = o - mu
    var = jnp.mean(oc * oc, axis=-1, keepdims=True)
    return _silu(gh) * (oc * lax.rsqrt(var + EPS)), new_states


def _gla_head(q, k, vh, gh, b, states, masks, c, gnorm, causal):
    dk = q.shape[-1]
    b_t = b.T
    if masks is None:
        b_last = b[c - 1:c, :]
    else:
        b_last = None
        for g, m in enumerate(masks):
            row = b[g * c + c - 1:g * c + c, :]
            b_last = row if b_last is None else jnp.where(m, row, b_last)
    q_in = (q * (dk ** -0.5) * jnp.exp(b)).astype(BF16)
    k_in = (k * jnp.exp(-b)).astype(BF16)
    scores = jnp.where(causal, _mm_nt(q_in, k_in), 0.0)
    cross, incs = _cross_and_update(q_in, k * jnp.exp(b_last - b), vh, states, masks)
    o = _mm(scores.astype(BF16), vh) + cross
    new_states = [jnp.exp(b_t[:, g * c + c - 1:g * c + c]) * s + u for g, (s, u) in enumerate(zip(states, incs))]
    o = o * lax.rsqrt(jnp.mean(o * o, axis=-1, keepdims=True) + EPS) * gnorm
    return _silu(gh) * o, new_states


def _log_a(glr, wgk, bgk):
    z = _mm(glr.astype(BF16), wgk) + bgk
    return (jnp.minimum(z, 0.0) - jnp.log1p(jnp.exp(-jnp.abs(z)))) / GLA_GATE_NORM


def _causal(c):
    return lax.broadcasted_iota(jnp.int32, (c, c), 0) >= lax.broadcasted_iota(jnp.int32, (c, c), 1)


def _w_seg(w_ref, wm_ref, seg, d):
    if seg < N_SEG - 2:
        return w_ref[:, seg * d:(seg + 1) * d]
    return wm_ref[:, (seg - (N_SEG - 2)) * d:(seg - (N_SEG - 3)) * d]


def _proj_block(d, x3, sh, sc, g, w_ref, wm_ref, wl_ref, proj_s, glr_s):
    n = _rms_mod(x3, g, sc, sh).reshape(-1, d).astype(BF16)
    for seg in range(N_SEG):
        proj_s[:, seg * d:(seg + 1) * d] = _mm(n, _w_seg(w_ref, wm_ref, seg, d)).astype(BF16)
    glr_s[...] = _mm(n, wl_ref[...])


def _mix_block(d, tb, proj_s, glr_s, cos_f, sin_f, dmask_ref, qdec_ref, kdec_ref, cdec_ref, tri, wgk, bgk, gnorm,
               sr_s, sg_s, oret_ref, ogla_ref, mg_ref, r_off):
    dk, dv, hq = d // 8, d // 4, d // 2
    rqk, rv, rg, gqk, gv, gg, mg = (i * d for i in range(7))
    for h in range(N_HEADS):
        o, (s_new,) = _ret_head(proj_s[:, rqk + h * dk:rqk + (h + 1) * dk].astype(F32),
                                proj_s[:, rqk + hq + h * dk:rqk + hq + (h + 1) * dk].astype(F32),
                                proj_s[:, rv + h * dv:rv + (h + 1) * dv],
                                proj_s[:, rg + h * dv:rg + (h + 1) * dv].astype(F32),
                                [sr_s[h]], None, cos_f, sin_f, dmask_ref[h], qdec_ref[h], kdec_ref[h],
                                cdec_ref[h])
        sr_s[h] = s_new
        oret_ref[r_off:r_off + tb, h * dv:(h + 1) * dv] = o.astype(BF16)

    la_hi, la_lo = _split_hi_lo(_log_a(glr_s[...], wgk, bgk))
    b = _mm(tri, la_hi) + _mm(tri, la_lo)
    cg = GLA_CHUNK
    n_c = tb // cg
    causal = _causal(cg)
    b_last = jnp.concatenate([jnp.broadcast_to(b[c * cg + cg - 1:c * cg + cg, :], (cg, hq)) for c in range(n_c)],
                             axis=0)
    gq = proj_s[:, gqk:gqk + hq].astype(F32)
    gk = proj_s[:, gqk + hq:gqk + 2 * hq].astype(F32)
    q_in = (gq * (dk ** -0.5) * jnp.exp(b)).astype(BF16)
    k_in = (gk * jnp.exp(-b)).astype(BF16)
    k_end = gk * jnp.exp(b_last - b)
    intra, incs, decs = {}, {}, {}
    for c in range(n_c):
        rows = slice(c * cg, (c + 1) * cg)
        for h in range(N_HEADS):
            cols = slice(h * dk, (h + 1) * dk)
            vh = proj_s[rows, gv + h * dv:gv + (h + 1) * dv]
            scores = jnp.where(causal, _mm_nt(q_in[rows, cols], k_in[rows, cols]), 0.0)
            intra[c, h] = _mm(scores.astype(BF16), vh)
            incs[c, h] = _mm(k_end[rows, cols].T.astype(BF16), vh)
            decs[c, h] = jnp.exp(b[rows, cols].T[:, cg - 1:cg])
    for h in range(N_HEADS):
        cols = slice(h * dk, (h + 1) * dk)
        s = sg_s[h]
        for c in range(n_c):
            rows = slice(c * cg, (c + 1) * cg)
            o = intra[c, h] + _mm(q_in[rows, cols], s.astype(BF16))
            s = decs[c, h] * s + incs[c, h]
            o = o * lax.rsqrt(jnp.mean(o * o, axis=-1, keepdims=True) + EPS) * gnorm
            gh = proj_s[rows, gg + h * dv:gg + (h + 1) * dv].astype(F32)
            ogla_ref[r_off + c * cg:r_off + (c + 1) * cg, h * dv:(h + 1) * dv] = (_silu(gh) * o).astype(BF16)
        sg_s[h] = s
    mg_ref[0, r_off:r_off + tb, :] = proj_s[:, mg:mg + d]
    mg_ref[1, r_off:r_off + tb, :] = proj_s[:, mg + d:mg + 2 * d]


ROUTE_SLABS = 2


def _rows2d(a3, tm, d):
    if a3.shape[0] == 1:
        return a3.reshape(1, d)
    return jnp.broadcast_to(a3, (a3.shape[0], tm // a3.shape[0], d)).reshape(tm, d)


def _route_block(d, out_ret, out_gla, mg_ret, mg_gla, x3, gt, sh, sc, g, wro_ref, wgo_ref, wo_ref, wrh_ref, wrl_ref,
                 br, utri_ref, eye_ref, carry_s, h_ref, n2_ref, idx_ref, rank_ref, prob_ref):
    rows = out_ret.shape[0]
    e = N_EXPERTS
    half = rows // ROUTE_SLABS
    slabs = [slice(s * half, (s + 1) * half) for s in range(ROUTE_SLABS)]
    x2 = x3.reshape(rows, d)
    gt2, sh2, sc2 = (_rows2d(v, rows, d) for v in (gt, sh, sc))
    g2 = g.reshape(1, d)

    def rows_of(v, sl):
        return v if v.shape[0] == 1 else v[sl]

    ab = [(_mm(out_ret[sl], wro_ref[...]), _mm(out_gla[sl], wgo_ref[...])) for sl in slabs]
    mix = [_mm((jax.nn.sigmoid(mg_ret[sl].astype(F32)) * a + jax.nn.sigmoid(mg_gla[sl].astype(F32)) * b).astype(BF16),
               wo_ref[...]) for sl, (a, b) in zip(slabs, ab)]
    logits = []
    for sl, m in zip(slabs, mix):
        hs = x2[sl] + rows_of(gt2, sl) * m
        h_ref[sl, :] = hs
        ms = jnp.mean(hs * hs, axis=-1, keepdims=True)
        n2 = hs * lax.rsqrt(ms + EPS) * g2 * (1.0 + rows_of(sc2, sl)) + rows_of(sh2, sl)
        n2_ref[sl, :] = _pack_pair(n2)
        n_hi, n_lo = _split_hi_lo(n2)
        logits.append(_mm_nt(wrh_ref[...], n_hi) + _mm_nt(wrh_ref[...], n_lo) + _mm_nt(wrl_ref[...], n_hi) + br)

    iota_e = lax.broadcasted_iota(jnp.int32, (e, half), 0)
    slot = lax.broadcasted_iota(jnp.int32, (TOP_K, half), 0)
    carry = carry_s[...]
    for sl, work in zip(slabs, logits):
        vals, idxs = [], []
        for _ in range(TOP_K):
            m = jnp.max(work, axis=0, keepdims=True)
            ik = jnp.min(jnp.where(work == m, iota_e, e), axis=0, keepdims=True)
            vals.append(m)
            idxs.append(ik)
            work = jnp.where(iota_e == ik, -jnp.inf, work)
        ex = [jnp.exp(v - vals[0]) for v in vals]
        den = ex[0] + ex[1] + ex[2] + ex[3]
        onehot = jnp.zeros((e, half), F32)
        for ik in idxs:
            onehot = onehot + (iota_e == ik).astype(F32)
        cum = _mm(onehot.astype(BF16), utri_ref[...]) + carry
        carry = carry + jnp.sum(onehot, axis=1, keepdims=True)
        idx_o = jnp.zeros((TOP_K, half), jnp.int32)
        rank_o = jnp.zeros((TOP_K, half), jnp.int32)
        prob_t = jnp.zeros((TOP_K, half), F32)
        for k in range(TOP_K):
            rk = jnp.sum(jnp.where(iota_e == idxs[k], cum, 0.0), axis=0, keepdims=True).astype(jnp.int32)
            idx_o = jnp.where(slot == k, idxs[k], idx_o)
            rank_o = jnp.where(slot == k, rk, rank_o)
            prob_t = jnp.where(slot == k, ex[k] / den, prob_t)
        idx_ref[:, sl] = idx_o
        rank_ref[:, sl] = rank_o
        p1 = prob_t.astype(BF16)
        r1 = prob_t - p1.astype(F32)
        p2 = r1.astype(BF16)
        p3 = (r1 - p2.astype(F32)).astype(BF16)
        pieces = jnp.concatenate([p1, p2, p3, jnp.zeros_like(p1)], axis=0)
        t = _mm_nt(eye_ref[...], pieces)
        prob_ref[sl, :] = t[:, 0:TOP_K] + t[:, TOP_K:2 * TOP_K] + t[:, 2 * TOP_K:3 * TOP_K]
    carry_s[...] = carry


def _frontp_kernel(d, tb, ntb, x0_ref, xa_ref, xb_ref, sh0_ref, sc0_ref, sha_ref, sca_ref, shb_ref, scb_ref,
                   g_ref, w_ref, wm_ref, wl_ref, cosa_ref, sina_ref, cosb_ref, sinb_ref,
                   dmask_ref, qdec_ref, kdec_ref, cdec_ref, tri_ref, wgk_ref, bgk_ref, gn_ref,
                   oret_ref, ogla_ref, mg_ref, sret_ref, sgla_ref, pa_s, pb_s, ga_s, gb_s, sr_s, sg_s):
    p = pl.program_id(0)
    blk = 2 * p
    g = g_ref[...]
    proj = functools.partial(_proj_block, d)
    mix = functools.partial(_mix_block, d, tb)
    tables = (dmask_ref, qdec_ref, kdec_ref, cdec_ref, tri_ref[...], wgk_ref[...], bgk_ref[...], gn_ref[...])

    @pl.when(p == 0)
    def _():
        proj(x0_ref[...], sh0_ref[0], sc0_ref[0], g, w_ref, wm_ref, wl_ref, pa_s, ga_s)

    @pl.when(blk % ntb == 0)
    def _():
        sr_s[...] = jnp.zeros_like(sr_s)
        sg_s[...] = jnp.zeros_like(sg_s)

    proj(xa_ref[...], sha_ref[0], sca_ref[0], g, w_ref, wm_ref, wl_ref, pb_s, gb_s)
    mix(pa_s, ga_s, cosa_ref[...], sina_ref[...], *tables, sr_s, sg_s, oret_ref, ogla_ref, mg_ref, 0)
    proj(xb_ref[...], shb_ref[0], scb_ref[0], g, w_ref, wm_ref, wl_ref, pa_s, ga_s)
    mix(pb_s, gb_s, cosb_ref[...], sinb_ref[...], *tables, sr_s, sg_s, oret_ref, ogla_ref, mg_ref, tb)

    @pl.when((blk + 1) % ntb == ntb - 1)
    def _():
        sret_ref[0] = sr_s[...]
        sgla_ref[0] = sg_s[...]


def _chunk_tri(tb, cg):
    i = jnp.arange(tb)
    return ((i[:, None] >= i[None, :]) & (i[:, None] // cg == i[None, :] // cg)).astype(BF16)


def _front_prompt(x_p, ada_p, g_mix, w_main, w_mg, w_glr, tb, w_gk, b_gk, g_gla):
    b, t, d = x_p.shape
    dk, dv, hq, h = d // 8, d // 4, d // 2, N_HEADS
    ntb = t // tb
    n_blk = b * ntb
    n_tok = b * t
    assert ntb % 2 == 0
    cos_f, sin_f = _rope_tables(0, t, dk)
    dmask, qdec, kdec, cdec = _ret_tables(tb, dk, dv)
    tri = _chunk_tri(tb, GLA_CHUNK)

    def first(p):
        return 0 * p

    def even(p):
        return 2 * p

    def odd(p):
        return 2 * p + 1

    def nxt(p):
        return jnp.minimum(2 * p + 2, n_blk - 1)

    def x_spec(blk_of):
        return pl.BlockSpec((1, tb, d), lambda p: (blk_of(p) // ntb, blk_of(p) % ntb, 0))

    def ada_spec(which, blk_of):
        return pl.BlockSpec((1, 1, 1, d), lambda p: (which, blk_of(p) // ntb, 0, 0))

    def rope_spec(blk_of):
        return pl.BlockSpec((tb, dk), lambda p: (blk_of(p) % ntb, 0))

    state_spec = pl.BlockSpec((1, h, dk, dv), lambda p: ((2 * p) // ntb, 0, 0, 0))
    tok_spec = pl.BlockSpec((2 * tb, d), lambda p: (p, 0))
    return pl.pallas_call(
        functools.partial(_frontp_kernel, d, tb, ntb),
        grid=(n_blk // 2,),
        in_specs=[_resident((1, tb, d)), x_spec(odd), x_spec(nxt),
                  ada_spec(0, first), ada_spec(1, first), ada_spec(0, odd), ada_spec(1, odd),
                  ada_spec(0, nxt), ada_spec(1, nxt),
                  _resident((1, 1, d)), _resident((d, (N_SEG - 2) * d)), _resident((d, 2 * d)),
                  _resident((d, GLA_GATE_RANK)),
                  rope_spec(even), rope_spec(even), rope_spec(odd), rope_spec(odd),
                  _resident((h, tb, tb)), _resident((h, tb, dk)), _resident((h, tb, dk)), _resident((h, 1, dv)),
                  _resident((tb, tb)), _resident((GLA_GATE_RANK, hq)), _resident((1, hq)), _resident((1, dv))],
        out_specs=[tok_spec, tok_spec, pl.BlockSpec((2, 2 * tb, d), lambda p: (0, p, 0)), state_spec, state_spec],
        out_shape=[jax.ShapeDtypeStruct((n_tok, d), BF16), jax.ShapeDtypeStruct((n_tok, d), BF16),
                   jax.ShapeDtypeStruct((2, n_tok, d), BF16),
                   jax.ShapeDtypeStruct((b, h, dk, dv), F32), jax.ShapeDtypeStruct((b, h, dk, dv), F32)],
        scratch_shapes=[pltpu.VMEM((tb, N_SEG * d), BF16), pltpu.VMEM((tb, N_SEG * d), BF16),
                        pltpu.VMEM((tb, GLA_GATE_RANK), F32), pltpu.VMEM((tb, GLA_GATE_RANK), F32),
                        pltpu.VMEM((h, dk, dv), F32), pltpu.VMEM((h, dk, dv), F32)],
        compiler_params=pltpu.CompilerParams(dimension_semantics=("arbitrary",), vmem_limit_bytes=VMEM_LIMIT),
        name="front_prompt",
    )(x_p, x_p, x_p, ada_p, ada_p, ada_p, ada_p, ada_p, ada_p, g_mix.reshape(1, 1, d), w_main, w_mg, w_glr,
      cos_f, sin_f, cos_f, sin_f, dmask, qdec, kdec, cdec, tri, w_gk, b_gk, g_gla)


def _inproj_kernel(d, xs_ref, shs_ref, scs_ref, g_ref, w_ref, wm_ref, wl_ref, proj_ref, glr_ref):
    n = _rms_mod(xs_ref[...], g_ref[...], scs_ref[0], shs_ref[0]).reshape(-1, d).astype(BF16)
    for s in range(N_SEG):
        proj_ref[s] = _mm(n, _w_seg(w_ref, wm_ref, s, d)).astype(BF16)
    glr_ref[...] = _mm(n, wl_ref[...])


def _inproj_sample(tl, x_s, ada_s, g_mix, w_main, w_mg, w_glr):
    bs, ts, d = x_s.shape
    n_tok = bs * ts
    return pl.pallas_call(
        functools.partial(_inproj_kernel, d),
        grid=(tl.n_st,),
        in_specs=[tl.s_x_spec(d), tl.s_ada_spec(0, d), tl.s_ada_spec(1, d),
                  _resident((1, 1, d)), _resident((d, (N_SEG - 2) * d)), _resident((d, 2 * d)),
                  _resident((d, GLA_GATE_RANK))],
        out_specs=[pl.BlockSpec((N_SEG, tl.tm, d), lambda i: (0, i, 0)), tl.s_row_spec(GLA_GATE_RANK)],
        out_shape=[jax.ShapeDtypeStruct((N_SEG, n_tok, d), BF16), jax.ShapeDtypeStruct((n_tok, GLA_GATE_RANK), F32)],
        compiler_params=pltpu.CompilerParams(dimension_semantics=("arbitrary",), vmem_limit_bytes=VMEM_LIMIT),
        name="inproj_sample",
    )(x_s, ada_s, ada_s, g_mix.reshape(1, 1, d), w_main, w_mg, w_glr)


def _mixs_kernel(d, ts, gsz, rqk_ref, rv_ref, rg_ref, gqk_ref, gv_ref, gg_ref, glr_ref, cos_ref, sin_ref,
                 dmask_ref, qdec_ref, kdec_ref, cdec_ref, wgk_ref, bgk_ref, gn_ref, sr_in, sg_in,
                 oret_ref, ogla_ref, sr_out, sg_out):
    dk, dv, hq = d // 8, d // 4, d // 2
    pair = 2 * ts
    cos_f, sin_f = cos_ref[...], sin_ref[...]
    gnorm = gn_ref[...]
    ri = lax.broadcasted_iota(jnp.int32, (pair, pair), 0)
    ci = lax.broadcasted_iota(jnp.int32, (pair, pair), 1)
    causal = jnp.logical_and(ri >= ci, (ri < ts) == (ci < ts))
    tri = causal.astype(F32).astype(BF16)
    first = lax.broadcasted_iota(jnp.int32, (pair, 1), 0) < ts
    masks = [first, jnp.logical_not(first)]

    def body(j, carry):
        rows = pl.ds(pl.multiple_of(j * pair, pair), pair)
        s0, s1 = 2 * j, 2 * j + 1
        la_hi, la_lo = _split_hi_lo(_log_a(glr_ref[rows, :], wgk_ref[...], bgk_ref[...]))
        b = _mm(tri, la_hi) + _mm(tri, la_lo)
        for h in range(N_HEADS):
            o, (n0, n1) = _ret_head(rqk_ref[0, rows, h * dk:(h + 1) * dk].astype(F32),
                                    rqk_ref[0, rows, hq + h * dk:hq + (h + 1) * dk].astype(F32),
                                    rv_ref[0, rows, h * dv:(h + 1) * dv],
                                    rg_ref[0, rows, h * dv:(h + 1) * dv].astype(F32),
                                    [sr_in[s0, h], sr_in[s1, h]], masks, cos_f, sin_f,
                                    dmask_ref[h], qdec_ref[h], kdec_ref[h], cdec_ref[h])
            sr_out[s0, h] = n0
            sr_out[s1, h] = n1
            oret_ref[rows, h * dv:(h + 1) * dv] = o.astype(BF16)
            o, (n0, n1) = _gla_head(gqk_ref[0, rows, h * dk:(h + 1) * dk].astype(F32),
                                    gqk_ref[0, rows, hq + h * dk:hq + (h + 1) * dk].astype(F32),
                                    gv_ref[0, rows, h * dv:(h + 1) * dv],
                                    gg_ref[0, rows, h * dv:(h + 1) * dv].astype(F32),
                                    b[:, h * dk:(h + 1) * dk], [sg_in[s0, h], sg_in[s1, h]], masks, ts,
                                    gnorm, causal)
            sg_out[s0, h] = n0
            sg_out[s1, h] = n1
            ogla_ref[rows, h * dv:(h + 1) * dv] = o.astype(BF16)
        return carry

    lax.fori_loop(0, gsz // 2, body, 0, unroll=2)


def _pair_tables(ts, dk, dv):
    cos_f, sin_f = _rope_tables(PAST_LEN, ts, dk)
    dmask, qdec, kdec, cdec = _ret_tables(ts, dk, dv)
    zero = jnp.zeros_like(dmask)
    dmask2 = jnp.concatenate([jnp.concatenate([dmask, zero], axis=2), jnp.concatenate([zero, dmask], axis=2)], axis=1)

    def twice(a, axis):
        return jnp.concatenate([a, a], axis=axis)

    return twice(cos_f, 0), twice(sin_f, 0), dmask2, twice(qdec, 1), twice(kdec, 1), cdec


def _mix_sample(bs, ts, d, gsz, proj, glr, state_ret, state_gla, w_gk, b_gk, g_gla):
    dk, dv, hq, h = d // 8, d // 4, d // 2, N_HEADS
    assert GLA_CHUNK % ts == 0 and bs % gsz == 0 and gsz % 4 == 0
    rows = gsz * ts
    pair = 2 * ts
    cos_f, sin_f, dmask, qdec, kdec, cdec = _pair_tables(ts, dk, dv)

    def seg(s):
        return pl.BlockSpec((1, rows, d), lambda i: (s, i, 0))

    state_spec = pl.BlockSpec((gsz, h, dk, dv), lambda i: (i, 0, 0, 0))
    tok_spec = pl.BlockSpec((rows, d), lambda i: (i, 0))
    return pl.pallas_call(
        functools.partial(_mixs_kernel, d, ts, gsz),
        grid=(bs // gsz,),
        in_specs=[seg(0), seg(1), seg(2), seg(3), seg(4), seg(5),
                  pl.BlockSpec((rows, GLA_GATE_RANK), lambda i: (i, 0)),
                  _const((pair, dk)), _const((pair, dk)),
                  _const((h, pair, pair)), _const((h, pair, dk)), _const((h, pair, dk)), _const((h, 1, dv)),
                  _const((GLA_GATE_RANK, hq)), _const((1, hq)), _const((1, dv)),
                  state_spec, state_spec],
        out_specs=[tok_spec, tok_spec, state_spec, state_spec],
        out_shape=[jax.ShapeDtypeStruct((bs * ts, d), BF16), jax.ShapeDtypeStruct((bs * ts, d), BF16),
                   jax.ShapeDtypeStruct((bs, h, dk, dv), F32), jax.ShapeDtypeStruct((bs, h, dk, dv), F32)],
        compiler_params=pltpu.CompilerParams(dimension_semantics=("arbitrary",), vmem_limit_bytes=VMEM_LIMIT),
        name="mix_sample",
    )(proj, proj, proj, proj, proj, proj, glr, cos_f, sin_f, dmask, qdec, kdec, cdec, w_gk, b_gk, g_gla,
      state_ret, state_gla)


def _outproj_kernel(n_pt, d, orp_ref, ogp_ref, ors_ref, ogs_ref, mgrp_ref, mggp_ref, mgrs_ref, mggs_ref,
                    xp_ref, xs_ref, gtp_ref, shp_ref, scp_ref, gts_ref, shs_ref, scs_ref, g_ref,
                    wro_ref, wgo_ref, wo_ref, wrh_ref, wrl_ref, br_ref, utri_ref, eye_ref,
                    h_ref, n2_ref, idx_ref, rank_ref, prob_ref, cnt_ref, carry_s):
    i = pl.program_id(0)

    @pl.when(i == 0)
    def _():
        carry_s[...] = jnp.zeros_like(carry_s)

    route = functools.partial(_route_block, d)
    tail = (g_ref[...], wro_ref, wgo_ref, wo_ref, wrh_ref, wrl_ref, br_ref[...], utri_ref, eye_ref, carry_s,
            h_ref, n2_ref, idx_ref, rank_ref, prob_ref)

    @pl.when(i < n_pt)
    def _():
        route(orp_ref[...], ogp_ref[...], mgrp_ref[0], mggp_ref[0], xp_ref[...], gtp_ref[0], shp_ref[0], scp_ref[0],
              *tail)

    @pl.when(i >= n_pt)
    def _():
        route(ors_ref[...], ogs_ref[...], mgrs_ref[0], mggs_ref[0], xs_ref[...], gts_ref[0], shs_ref[0], scs_ref[0],
              *tail)

    @pl.when(i == pl.num_programs(0) - 1)
    def _():
        cnt_ref[...] = carry_s[...].astype(jnp.int32)


def _outproj(tl, oret_p, ogla_p, oret_s, ogla_s, mg_p, proj_s, x_p, x_s, ada_p, ada_s, g_ffn,
             w_ret_o, w_gla_o, w_out, w_r_hi, w_r_lo, b_router):
    d = x_p.shape[-1]
    tm, e, n_pt = tl.tm, N_EXPERTS, tl.n_pt
    last = n_pt - 1
    p_spec = pl.BlockSpec((tm, d), lambda i: (jnp.minimum(i, last), 0))
    s_spec = pl.BlockSpec((tm, d), lambda i: (jnp.maximum(i - n_pt, 0), 0))

    def mgp_spec(seg):
        return pl.BlockSpec((1, tm, d), lambda i: (seg, jnp.minimum(i, last), 0))

    def mgs_spec(seg):
        return pl.BlockSpec((1, tm, d), lambda i: (seg, jnp.maximum(i - n_pt, 0), 0))

    slot_spec = pl.BlockSpec((TOP_K, tm), lambda i: (0, i))
    half = tm // ROUTE_SLABS
    token = jnp.arange(half)
    utri = (token[:, None] < token[None, :]).astype(BF16)
    eye = jnp.eye(half, dtype=BF16)

    return pl.pallas_call(
        functools.partial(_outproj_kernel, n_pt, d),
        grid=(tl.n,),
        in_specs=[p_spec, p_spec, s_spec, s_spec, mgp_spec(0), mgp_spec(1), mgs_spec(6), mgs_spec(7),
                  tl.xp_spec(d), tl.xs_spec(d),
                  tl.adap_spec(2, d), tl.adap_spec(3, d), tl.adap_spec(4, d),
                  tl.adas_spec(2, d), tl.adas_spec(3, d), tl.adas_spec(4, d),
                  _resident((1, 1, d)), _resident((d, d)), _resident((d, d)), _resident((d, d)),
                  _resident((e, d)), _resident((e, d)), _resident((e, 1)),
                  _resident((half, half)), _resident((half, half))],
        out_specs=[tl.tok_spec(d), tl.tok_spec(d // 2), slot_spec, slot_spec, tl.tok_spec(TOP_K),
                   pl.BlockSpec((e, 1), lambda i: (0, 0))],
        out_shape=[jax.ShapeDtypeStruct((tl.n_tok, d), F32), jax.ShapeDtypeStruct((tl.n_tok, d // 2), jnp.uint32),
                   jax.ShapeDtypeStruct((TOP_K, tl.n_tok), jnp.int32),
                   jax.ShapeDtypeStruct((TOP_K, tl.n_tok), jnp.int32),
                   jax.ShapeDtypeStruct((tl.n_tok, TOP_K), F32),
                   jax.ShapeDtypeStruct((e, 1), jnp.int32)],
        scratch_shapes=[pltpu.VMEM((e, 1), F32)],
        compiler_params=pltpu.CompilerParams(dimension_semantics=("arbitrary",), vmem_limit_bytes=VMEM_LIMIT),
        name="outproj",
    )(oret_p, ogla_p, oret_s, ogla_s, mg_p, mg_p, proj_s, proj_s, x_p, x_s, ada_p, ada_p, ada_p, ada_s, ada_s, ada_s,
      g_ffn.reshape(1, 1, d), w_ret_o, w_gla_o, w_out, w_r_hi, w_r_lo, b_router.reshape(e, 1), utri, eye)


def _expert_kernel(f, te_ref, na_ref, grp_ref, nxt_ref, x_ref, wu_hbm, bu_ref, wd_hbm, bd_ref, y_ref,
                   wu_f, wd_f, wu_s, wd_s, sem):
    j = pl.program_id(0)
    active = j < na_ref[0]
    first = jnp.logical_or(j == 0, te_ref[j] != te_ref[jnp.maximum(j - 1, 0)])
    slot = grp_ref[j] % 2

    def fetch(expert, s):
        return (pltpu.make_async_copy(wu_hbm.at[expert], wu_f.at[s], sem.at[0, s]),
                pltpu.make_async_copy(wd_hbm.at[expert], wd_f.at[s], sem.at[1, s]))

    @pl.when(j == 0)
    def _():
        for c in fetch(te_ref[0], 0):
            c.start()

    @pl.when(jnp.logical_and(active, first))
    def _():
        for c in fetch(te_ref[j], slot):
            c.wait()

        @pl.when(nxt_ref[j] >= 0)
        def _():
            for c in fetch(nxt_ref[j], 1 - slot):
                c.start()

        wu_s[...] = wu_f[slot].astype(BF16)
        wd_s[...] = wd_f[slot].astype(BF16)

    slab = x_ref.shape[0] // EXPERT_ROW_SLABS
    half = x_ref.shape[1]

    @pl.when(active)
    def _():
        for s in range(EXPERT_ROW_SLABS):
            rows = slice(s * slab, (s + 1) * slab)
            x_lo, x_hi = _unpack_pair(x_ref[rows, :])
            gu = _mm(x_lo.astype(BF16), wu_s[:half, :]) + _mm(x_hi.astype(BF16), wu_s[half:, :]) + bu_ref[0]
            gate = jnp.minimum(gu[:, :f], SWIGLU_LIMIT)
            up = jnp.clip(gu[:, f:], -SWIGLU_LIMIT, SWIGLU_LIMIT)
            act = (up + 1.0) * gate * jax.nn.sigmoid(SWIGLU_ALPHA * gate)
            y_ref[rows, :] = _pack_pair(_mm(act.astype(BF16), wd_s[...]) + bd_ref[0])


def _experts(xs, tile_expert, n_active, tile_group, next_expert, w_up, b_up, w_down, b_down, tme):
    r = xs.shape[0]
    e, d, f2 = w_up.shape
    f = f2 // 2
    n_tiles = r // tme

    def row_map(j, te, na, grp, nxt):
        return (jnp.minimum(j, na[0] - 1), 0)

    def b_map(j, te, na, grp, nxt):
        return (te[jnp.minimum(j, na[0] - 1)], 0, 0)

    hbm = pl.BlockSpec(memory_space=pl.ANY)
    return pl.pallas_call(
        functools.partial(_expert_kernel, f),
        grid_spec=pltpu.PrefetchScalarGridSpec(
            num_scalar_prefetch=4,
            grid=(n_tiles,),
            in_specs=[pl.BlockSpec((tme, d // 2), row_map),
                      hbm, pl.BlockSpec((1, 1, f2), b_map), hbm, pl.BlockSpec((1, 1, d), b_map)],
            out_specs=pl.BlockSpec((tme, d // 2), row_map),
            scratch_shapes=[pltpu.VMEM((2, d, f2), F32), pltpu.VMEM((2, f, d), F32),
                            pltpu.VMEM((d, f2), BF16), pltpu.VMEM((f, d), BF16),
                            pltpu.SemaphoreType.DMA((2, 2))]),
        out_shape=jax.ShapeDtypeStruct((r, d // 2), jnp.uint32),
        compiler_params=pltpu.CompilerParams(dimension_semantics=("arbitrary",), vmem_limit_bytes=VMEM_LIMIT),
        name="experts",
    )(tile_expert, n_active, tile_group, next_expert, xs, w_up, b_up.reshape(e, 1, f2), w_down,
      b_down.reshape(e, 1, d))


def _sc_mesh():
    return plsc.VectorSubcoreMesh(core_axis_name="core", subcore_axis_name="subcore")


def _sc_split(n_rows, max_chunk):
    info = plsc.get_sparse_core_info()
    n_workers = info.num_cores * info.num_subcores
    assert n_rows % (8 * n_workers) == 0
    per_w = n_rows // n_workers
    chunk = 8
    while chunk * 2 <= max_chunk and per_w % (chunk * 2) == 0:
        chunk *= 2
    return info.num_cores, n_workers, per_w, chunk


def _sc_dispatch(x, pos_t, n_rows):
    n, w = x.shape
    nc, nw, per_w, chunk = _sc_split(n, 32)
    n_ch = per_w // chunk
    idx = pos_t.reshape(TOP_K, nw, n_ch, chunk).transpose(1, 0, 2, 3).reshape(nw, TOP_K * n_ch, chunk)

    @functools.partial(
        pl.kernel, out_type=jax.ShapeDtypeStruct((n_rows, w), x.dtype), mesh=_sc_mesh(),
        scratch_types=[pltpu.VMEM((TOP_K * n_ch, chunk), jnp.int32), pltpu.VMEM((2, chunk, w), x.dtype),
                       pltpu.SemaphoreType.DMA((2,)), pltpu.SemaphoreType.DMA((2,))])
    def scatter_rows(x_hbm, i_hbm, o_hbm, idx_v, rows_v, rsem, wsem):
        wid = lax.axis_index("subcore") * nc + lax.axis_index("core")
        base = wid * per_w
        pltpu.sync_copy(i_hbm.at[wid], idx_v)

        def read(j, slot):
            return pltpu.make_async_copy(x_hbm.at[pl.ds(base + j * chunk, chunk)], rows_v.at[slot], rsem.at[slot])

        def write(j, slot, k):
            return pltpu.make_async_copy(rows_v.at[slot], o_hbm.at[idx_v.at[k * n_ch + j]], wsem.at[slot])

        read(0, 0).start()

        @pl.loop(0, n_ch, step=2)
        def _(j0):
            for b in range(2):
                j = j0 + b

                @pl.when(j < n_ch)
                def _():
                    read(j, b).wait()

                    @pl.when(j + 1 < n_ch)
                    def _():
                        @pl.when(j >= 1)
                        def _():
                            for k in range(TOP_K):
                                write(j - 1, 1 - b, k).wait()

                        read(j + 1, 1 - b).start()

                    for k in range(TOP_K):
                        write(j, b, k).start()

        for jj in range(max(n_ch - 2, 0), n_ch):
            for k in range(TOP_K):
                write(jj, jj % 2, k).wait()

    return scatter_rows(x, idx)


def _sc_gather(table, idx):
    m = idx.shape[0]
    w = table.shape[1]
    nc, _, per_w, chunk = _sc_split(m, 64)
    n_ch = per_w // chunk

    @functools.partial(
        pl.kernel, out_type=jax.ShapeDtypeStruct((m, w), table.dtype), mesh=_sc_mesh(),
        scratch_types=[pltpu.VMEM((per_w,), jnp.int32), pltpu.VMEM((2, chunk, w), table.dtype),
                       pltpu.SemaphoreType.DMA((2,)), pltpu.SemaphoreType.DMA((2,))])
    def gather_rows(t_hbm, i_hbm, o_hbm, idx_v, rows_v, gsem, wsem):
        wid = lax.axis_index("subcore") * nc + lax.axis_index("core")
        base = wid * per_w
        pltpu.sync_copy(i_hbm.at[pl.ds(base, per_w)], idx_v)

        def gather(j, slot):
            off = pl.multiple_of(j * chunk, chunk)
            return pltpu.make_async_copy(t_hbm.at[idx_v.at[pl.ds(off, chunk)]], rows_v.at[slot], gsem.at[slot])

        def write(j, slot):
            off = pl.multiple_of(j * chunk, chunk)
            return pltpu.make_async_copy(rows_v.at[slot], o_hbm.at[pl.ds(base + off, chunk)], wsem.at[slot])

        gather(0, 0).start()

        @pl.loop(0, n_ch, step=2)
        def _(j0):
            for b in range(2):
                j = j0 + b

                @pl.when(j < n_ch)
                def _():
                    gather(j, b).wait()

                    @pl.when(j + 1 < n_ch)
                    def _():
                        @pl.when(j >= 1)
                        def _():
                            write(j - 1, 1 - b).wait()

                        gather(j + 1, 1 - b).start()

                    write(j, b).start()

        for jj in range(max(n_ch - 2, 0), n_ch):
            write(jj, jj % 2).wait()

    return gather_rows(table, idx)


def _final_kernel(n_pt, d, h_ref, yg_ref, prob_ref, gtp_ref, gts_ref, g_ref, yp_ref, ys_ref):
    i = pl.program_id(0)
    p = prob_ref[...]
    moe_lo, moe_hi = None, None
    for k in range(TOP_K):
        lo, hi = _unpack_pair(yg_ref[k])
        pk = p[:, k:k + 1]
        moe_lo = pk * lo if moe_lo is None else moe_lo + pk * lo
        moe_hi = pk * hi if moe_hi is None else moe_hi + pk * hi
    moe = jnp.concatenate([moe_lo, moe_hi], axis=1)

    def body(gt, shape):
        h3 = h_ref[...].reshape(shape) + gt * moe.reshape(shape)
        ms = jnp.mean(h3 * h3, axis=-1, keepdims=True)
        return h3 * lax.rsqrt(ms + EPS) * g_ref[...]

    @pl.when(i < n_pt)
    def _():
        yp_ref[...] = body(gtp_ref[0], yp_ref.shape)

    @pl.when(i >= n_pt)
    def _():
        ys_ref[...] = body(gts_ref[0], ys_ref.shape)


def _final(tl, h, yg, probs, ada_p, ada_s, g_final, d):
    return pl.pallas_call(
        functools.partial(_final_kernel, tl.n_pt, d),
        grid=(tl.n,),
        in_specs=[tl.tok_spec(d), pl.BlockSpec((TOP_K, tl.tm, d // 2), lambda i: (0, i, 0)), tl.tok_spec(TOP_K),
                  tl.adap_spec(5, d), tl.adas_spec(5, d), _resident((1, 1, d))],
        out_specs=[tl.xp_spec(d), tl.xs_spec(d)],
        out_shape=[jax.ShapeDtypeStruct((tl.b, tl.t, d), F32), jax.ShapeDtypeStruct((tl.bs, tl.ts, d), F32)],
        compiler_params=pltpu.CompilerParams(dimension_semantics=("arbitrary",), vmem_limit_bytes=VMEM_LIMIT),
        name="final",
    )(h, yg, probs, ada_p, ada_s, g_final.reshape(1, 1, d))


def _pick(n, pref):
    t = min(n, pref)
    while n % t:
        t //= 2
    return t


def _forward(x_prompt, x_sample, c_prompt, c_sample, state_ret, state_gla, w_ada, b_ada, g_norm_mix, g_norm_ffn,
             w_in, w_gk_up, b_gk, g_gla_norm, w_ret_o, w_gla_o, w_out, w_router, b_router, w_up, b_up,
             w_down, b_down, g_final, *, tm, tb, gsz, tme):
    b, t, d = x_prompt.shape
    bs, ts, _ = x_sample.shape
    assert w_ada.shape[0] == 1, "single layer only"
    assert (b * t) % (2 * tb) == 0 and (b * t) % tm == 0
    e = N_EXPERTS
    tl = _Tiles(b, t, bs, ts, tm)
    n_tok = tl.n_tok

    ada = _ada(jnp.concatenate([c_prompt, c_sample], axis=0), w_ada[0], b_ada[0])
    ada_p = ada[:, :b].reshape(6, b, 1, d)
    ada_s = ada[:, b:].reshape(6, bs, 1, d)

    w_in0 = w_in[0]
    n_main = 6 * d
    w_main = w_in0[:, :n_main].astype(BF16)
    w_mg = w_in0[:, n_main + GLA_GATE_RANK:].astype(BF16)
    w_glr = w_in0[:, n_main:n_main + GLA_GATE_RANK].astype(BF16)
    w_gk = w_gk_up[0].astype(BF16)
    bgk = b_gk[0].reshape(1, -1)
    ggn = g_gla_norm[0].reshape(1, -1)
    w_r = w_router[0].T
    w_r_hi = w_r.astype(BF16)
    w_r_lo = (w_r - w_r_hi.astype(F32)).astype(BF16)
    route_w = (g_norm_ffn[0], w_ret_o[0].astype(BF16), w_gla_o[0].astype(BF16), w_out[0].astype(BF16),
               w_r_hi, w_r_lo, b_router[0])

    oret_p, ogla_p, mg_p, sret_p, sgla_p = _front_prompt(x_prompt, ada_p, g_norm_mix[0], w_main, w_mg, w_glr, tb,
                                                         w_gk, bgk, ggn)
    proj_s, glr_s = _inproj_sample(tl, x_sample, ada_s, g_norm_mix[0], w_main, w_mg, w_glr)
    oret_s, ogla_s, sret_s, sgla_s = _mix_sample(bs, ts, d, gsz, proj_s, glr_s, state_ret[0], state_gla[0],
                                                 w_gk, bgk, ggn)
    h, n2, idx_t, rank_t, probs, counts = _outproj(tl, oret_p, ogla_p, oret_s, ogla_s, mg_p, proj_s, x_prompt, x_sample,
                                               ada_p, ada_s, *route_w)

    counts = counts[:, 0]
    gsize = ((counts + tme - 1) // tme) * tme
    ends = jnp.cumsum(gsize)
    offs = ends - gsize
    experts = jnp.arange(e, dtype=jnp.int32)
    pos_t = jnp.sum(jnp.where(idx_t[..., None] == experts, offs, 0), axis=-1) + rank_t
    max_tiles = (n_tok * TOP_K) // tme + e
    n_active = (ends[-1] // tme).astype(jnp.int32).reshape(1)
    tile_start = jnp.arange(max_tiles, dtype=jnp.int32) * tme
    tile_expert = jnp.minimum(jnp.sum((ends[None, :] <= tile_start[:, None]).astype(jnp.int32), axis=1), e - 1)
    is_first = jnp.logical_and(tile_start < ends[-1],
                               jnp.concatenate([jnp.ones((1,), bool), tile_expert[1:] != tile_expert[:-1]]))
    tile_group = jnp.cumsum(is_first.astype(jnp.int32)) - 1
    later = jnp.logical_and(experts[None, :] > experts[:, None], counts[None, :] > 0)
    next_of = jnp.min(jnp.where(later, experts[None, :], e), axis=1)
    next_of = jnp.where(next_of == e, -1, next_of)
    next_expert = jnp.sum(jnp.where(tile_expert[:, None] == experts, next_of, 0), axis=1).astype(jnp.int32)

    xs = _sc_dispatch(n2, pos_t, max_tiles * tme)
    ys = _experts(xs, tile_expert, n_active, tile_group, next_expert, w_up[0], b_up[0], w_down[0], b_down[0], tme)
    yg = _sc_gather(ys, pos_t.reshape(-1)).reshape(TOP_K, n_tok, d // 2)

    y_p, y_s = _final(tl, h, yg, probs, ada_p, ada_s, g_final, d)
    return (y_p, y_s, sret_p[None], sgla_p[None], sret_s[None], sgla_s[None])


def kernel(x_prompt, x_sample, c_prompt, c_sample, state_ret, state_gla, w_ada, b_ada, g_norm_mix, g_norm_ffn,
           w_in, w_gk_up, b_gk, g_gla_norm, w_ret_o, w_gla_o, w_out, w_router, b_router, w_up, b_up,
           w_down, b_down, g_final):
    t = x_prompt.shape[1]
    bs, ts = x_sample.shape[0], x_sample.shape[1]
    return _forward(x_prompt, x_sample, c_prompt, c_sample, state_ret, state_gla, w_ada, b_ada, g_norm_mix,
                    g_norm_ffn, w_in, w_gk_up, b_gk, g_gla_norm, w_ret_o, w_gla_o, w_out, w_router, b_router,
                    w_up, b_up, w_down, b_down, g_final,
                    tm=_pick(bs * ts, 512), tb=_pick(t, 256), gsz=_pick(bs, 8), tme=512)
```

```python
import functools

import jax
import jax.numpy as jnp
from jax import lax
from jax.experimental import pallas as pl
from jax.experimental.pallas import tpu as pltpu
from jax.experimental.pallas import tpu_sc as plsc

F32 = jnp.float32
BF16 = jnp.bfloat16

N_HEADS = 4
GLA_GATE_RANK = 16
GLA_GATE_NORM = 16.0
GLA_CHUNK = 64
ROPE_BASE = 10000.0
N_EXPERTS = 32
TOP_K = 4
SWIGLU_LIMIT = 7.0
SWIGLU_ALPHA = 1.702
EPS = 1e-6
PAST_LEN = 16384
N_SEG = 8
EXPERT_ROW_SLABS = 2

VMEM_LIMIT = 56 * 1024 * 1024


def _mm(a, b):
    return jnp.dot(a, b, preferred_element_type=F32)


def _mm_nt(a, b):
    return lax.dot_general(a, b, (((1,), (1,)), ((), ())), preferred_element_type=F32)


def _silu(x):
    return x * jax.nn.sigmoid(x)


def _split_hi_lo(x):
    hi = x.astype(BF16)
    lo = (x - hi.astype(F32)).astype(BF16)
    return hi, lo


def _pack_pair(x):
    w = x.shape[1] // 2
    lo = lax.bitcast_convert_type(x[:, :w].astype(BF16).astype(F32), jnp.uint32)
    hi = lax.bitcast_convert_type(x[:, w:].astype(BF16).astype(F32), jnp.uint32)
    return (hi & jnp.uint32(0xFFFF0000)) | (lo >> 16)


def _unpack_pair(p):
    lo = lax.bitcast_convert_type(p << 16, F32)
    hi = lax.bitcast_convert_type(p & jnp.uint32(0xFFFF0000), F32)
    return lo, hi


def _rms_mod(x3, g, sc, sh):
    ms = jnp.mean(x3 * x3, axis=-1, keepdims=True)
    return x3 * lax.rsqrt(ms + EPS) * g * (1.0 + sc) + sh


def _resident(shape):
    zeros = (0,) * len(shape)
    return pl.BlockSpec(shape, lambda i: zeros, pipeline_mode=pl.Buffered(1))


def _const(shape):
    zeros = (0,) * len(shape)
    return pl.BlockSpec(shape, lambda i: zeros)


def _ada_kernel(c_ref, w_ref, b_ref, o_ref):
    cf = _silu(c_ref[...])
    o_ref[0] = _mm(cf.astype(BF16), w_ref[...].astype(BF16)) + b_ref[0]


def _ada(c_all, w_ada, b_ada):
    bc, d = c_all.shape
    n = w_ada.shape[1] // d
    return pl.pallas_call(
        _ada_kernel,
        grid=(n,),
        in_specs=[pl.BlockSpec((bc, d), lambda j: (0, 0)),
                  pl.BlockSpec((d, d), lambda j: (0, j)),
                  pl.BlockSpec((1, 1, d), lambda j: (j, 0, 0))],
        out_specs=pl.BlockSpec((1, bc, d), lambda j: (j, 0, 0)),
        out_shape=jax.ShapeDtypeStruct((n, bc, d), F32),
        compiler_params=pltpu.CompilerParams(dimension_semantics=("arbitrary",), vmem_limit_bytes=VMEM_LIMIT),
        name="ada",
    )(c_all, w_ada, b_ada.reshape(n, 1, d))


class _Tiles:
    def __init__(self, b, t, bs, ts, tm):
        assert t % tm == 0 and (bs * ts) % tm == 0 and tm % ts == 0
        self.b, self.t, self.bs, self.ts, self.tm = b, t, bs, ts, tm
        self.tpb = t // tm
        self.n_pt = b * self.tpb
        self.gs = tm // ts
        self.n_st = (bs * ts) // tm
        self.n = self.n_pt + self.n_st
        self.n_tok = b * t + bs * ts

    def xp_spec(self, d):
        last, tpb = self.n_pt - 1, self.tpb
        return pl.BlockSpec((1, self.tm, d), lambda i: (jnp.minimum(i, last) // tpb, jnp.minimum(i, last) % tpb, 0))

    def xs_spec(self, d):
        n_pt = self.n_pt
        return pl.BlockSpec((self.gs, self.ts, d), lambda i: (jnp.maximum(i - n_pt, 0), 0, 0))

    def adap_spec(self, which, d):
        last, tpb = self.n_pt - 1, self.tpb
        return pl.BlockSpec((1, 1, 1, d), lambda i: (which, jnp.minimum(i, last) // tpb, 0, 0))

    def adas_spec(self, which, d):
        n_pt = self.n_pt
        return pl.BlockSpec((1, self.gs, 1, d), lambda i: (which, jnp.maximum(i - n_pt, 0), 0, 0))

    def tok_spec(self, width):
        return pl.BlockSpec((self.tm, width), lambda i: (i, 0))

    def s_x_spec(self, d):
        return pl.BlockSpec((self.gs, self.ts, d), lambda i: (i, 0, 0))

    def s_ada_spec(self, which, d):
        return pl.BlockSpec((1, self.gs, 1, d), lambda i: (which, i, 0, 0))

    def s_row_spec(self, width):
        return pl.BlockSpec((self.tm, width), lambda i: (i, 0))

    def s_tok_spec(self, width):
        n_pt = self.n_pt
        return pl.BlockSpec((self.tm, width), lambda i: (n_pt + i, 0))


def _rope_tables(pos0, t, dk):
    half = dk // 2
    inv = ROPE_BASE ** (-jnp.arange(half, dtype=jnp.float32) / half)
    pos = pos0 + jnp.arange(t)
    ang = pos.astype(jnp.float32)[:, None] * inv[None, :]
    cos, sin = jnp.cos(ang), jnp.sin(ang)
    return jnp.concatenate([cos, cos], axis=-1), jnp.concatenate([-sin, sin], axis=-1)


def _ret_tables(c, dk, dv):
    h = N_HEADS
    log_gamma = jnp.log1p(-jnp.exp2(-5.0 - jnp.arange(h, dtype=jnp.float32)))
    idx = jnp.arange(c, dtype=jnp.float32)
    rel = idx[:, None] - idx[None, :]
    dmask = jnp.where(rel >= 0, jnp.exp(log_gamma[:, None, None] * jnp.maximum(rel, 0.0)), 0.0)
    kdec = jnp.exp(log_gamma[:, None] * (c - 1 - idx))
    qdec = jnp.exp(log_gamma[:, None] * (idx + 1.0))
    cdec = jnp.exp(log_gamma * c)
    return (dmask,
            jnp.broadcast_to(qdec[:, :, None], (h, c, dk)),
            jnp.broadcast_to(kdec[:, :, None], (h, c, dk)),
            jnp.broadcast_to(cdec[:, None, None], (h, 1, dv)))


def _rot(x, cos_f, sin_f):
    return x * cos_f + pltpu.roll(x, x.shape[-1] // 2, 1) * sin_f


def _cross_and_update(q_lhs, k_end, vh, states, masks):
    if masks is None:
        (s,) = states
        return _mm(q_lhs, s.astype(BF16)), [_mm(k_end.T.astype(BF16), vh)]
    cross, incs = None, []
    for s, m in zip(states, masks):
        c = _mm(q_lhs, s.astype(BF16))
        cross = c if cross is None else jnp.where(m, c, cross)
        incs.append(_mm(jnp.where(m, k_end, 0.0).T.astype(BF16), vh))
    return cross, incs


def _ret_head(q, k, vh, gh, states, masks, cos_f, sin_f, dmask, qdec, kdec, cdec):
    dk = q.shape[-1]
    q = _rot(q, cos_f, sin_f)
    k = _rot(k, cos_f, sin_f) * (dk ** -0.5)
    scores = _mm_nt(q.astype(BF16), k.astype(BF16)) * dmask
    cross, incs = _cross_and_update((q * qdec).astype(BF16), k * kdec, vh, states, masks)
    o = _mm(scores.astype(BF16), vh) + cross
    new_states = [cdec * s + u for s, u in zip(states, incs)]
    mu = jnp.mean(o, axis=-1, keepdims=True)
    oc = o - mu
    var = jnp.mean(oc * oc, axis=-1, keepdims=True)
    return _silu(gh) * (oc * lax.rsqrt(var + EPS)), new_states


def _gla_head(q, k, vh, gh, b, states, masks, c, gnorm, causal):
    dk = q.shape[-1]
    b_t = b.T
    if masks is None:
        b_last = b[c - 1:c, :]
    else:
        b_last = None
        for g, m in enumerate(masks):
            row = b[g * c + c - 1:g * c + c, :]
            b_last = row if b_last is None else jnp.where(m, row, b_last)
    q_in = (q * (dk ** -0.5) * jnp.exp(b)).astype(BF16)
    k_in = (k * jnp.exp(-b)).astype(BF16)
    scores = jnp.where(causal, _mm_nt(q_in, k_in), 0.0)
    cross, incs = _cross_and_update(q_in, k * jnp.exp(b_last - b), vh, states, masks)
    o = _mm(scores.astype(BF16), vh) + cross
    new_states = [jnp.exp(b_t[:, g * c + c - 1:g * c + c]) * s + u for g, (s, u) in enumerate(zip(states, incs))]
    o = o * lax.rsqrt(jnp.mean(o * o, axis=-1, keepdims=True) + EPS) * gnorm
    return _silu(gh) * o, new_states


def _log_a(glr, wgk, bgk):
    z = _mm(glr.astype(BF16), wgk) + bgk
    return (jnp.minimum(z, 0.0) - jnp.log1p(jnp.exp(-jnp.abs(z)))) / GLA_GATE_NORM


def _causal(c):
    return lax.broadcasted_iota(jnp.int32, (c, c), 0) >= lax.broadcasted_iota(jnp.int32, (c, c), 1)


def _w_seg(w_ref, wm_ref, seg, d):
    if seg < N_SEG - 2:
        return w_ref[:, seg * d:(seg + 1) * d]
    return wm_ref[:, (seg - (N_SEG - 2)) * d:(seg - (N_SEG - 3)) * d]


def _proj_block(d, x3, sh, sc, g, w_ref, wm_ref, wl_ref, proj_s, glr_s):
    n = _rms_mod(x3, g, sc, sh).reshape(-1, d).astype(BF16)
    for seg in range(N_SEG):
        proj_s[:, seg * d:(seg + 1) * d] = _mm(n, _w_seg(w_ref, wm_ref, seg, d)).astype(BF16)
    glr_s[...] = _mm(n, wl_ref[...])


def _mix_block(d, tb, proj_s, glr_s, cos_f, sin_f, dmask_ref, qdec_ref, kdec_ref, cdec_ref, tri, wgk, bgk, gnorm,
               sr_s, sg_s, oret_ref, ogla_ref, mg_ref, r_off):
    dk, dv, hq = d // 8, d // 4, d // 2
    rqk, rv, rg, gqk, gv, gg, mg = (i * d for i in range(7))
    for h in range(N_HEADS):
        o, (s_new,) = _ret_head(proj_s[:, rqk + h * dk:rqk + (h + 1) * dk].astype(F32),
                                proj_s[:, rqk + hq + h * dk:rqk + hq + (h + 1) * dk].astype(F32),
                                proj_s[:, rv + h * dv:rv + (h + 1) * dv],
                                proj_s[:, rg + h * dv:rg + (h + 1) * dv].astype(F32),
                                [sr_s[h]], None, cos_f, sin_f, dmask_ref[h], qdec_ref[h], kdec_ref[h],
                                cdec_ref[h])
        sr_s[h] = s_new
        oret_ref[r_off:r_off + tb, h * dv:(h + 1) * dv] = o.astype(BF16)

    la_hi, la_lo = _split_hi_lo(_log_a(glr_s[...], wgk, bgk))
    b = _mm(tri, la_hi) + _mm(tri, la_lo)
    cg = GLA_CHUNK
    n_c = tb // cg
    causal = _causal(cg)
    b_last = jnp.concatenate([jnp.broadcast_to(b[c * cg + cg - 1:c * cg + cg, :], (cg, hq)) for c in range(n_c)],
                             axis=0)
    gq = proj_s[:, gqk:gqk + hq].astype(F32)
    gk = proj_s[:, gqk + hq:gqk + 2 * hq].astype(F32)
    q_in = (gq * (dk ** -0.5) * jnp.exp(b)).astype(BF16)
    k_in = (gk * jnp.exp(-b)).astype(BF16)
    k_end = gk * jnp.exp(b_last - b)
    intra, incs, decs = {}, {}, {}
    for c in range(n_c):
        rows = slice(c * cg, (c + 1) * cg)
        for h in range(N_HEADS):
            cols = slice(h * dk, (h + 1) * dk)
            vh = proj_s[rows, gv + h * dv:gv + (h + 1) * dv]
            scores = jnp.where(causal, _mm_nt(q_in[rows, cols], k_in[rows, cols]), 0.0)
            intra[c, h] = _mm(scores.astype(BF16), vh)
            incs[c, h] = _mm(k_end[rows, cols].T.astype(BF16), vh)
            decs[c, h] = jnp.exp(b[rows, cols].T[:, cg - 1:cg])
    for h in range(N_HEADS):
        cols = slice(h * dk, (h + 1) * dk)
        s = sg_s[h]
        for c in range(n_c):
            rows = slice(c * cg, (c + 1) * cg)
            o = intra[c, h] + _mm(q_in[rows, cols], s.astype(BF16))
            s = decs[c, h] * s + incs[c, h]
            o = o * lax.rsqrt(jnp.mean(o * o, axis=-1, keepdims=True) + EPS) * gnorm
            gh = proj_s[rows, gg + h * dv:gg + (h + 1) * dv].astype(F32)
            ogla_ref[r_off + c * cg:r_off + (c + 1) * cg, h * dv:(h + 1) * dv] = (_silu(gh) * o).astype(BF16)
        sg_s[h] = s
    mg_ref[0, r_off:r_off + tb, :] = proj_s[:, mg:mg + d]
    mg_ref[1, r_off:r_off + tb, :] = proj_s[:, mg + d:mg + 2 * d]


ROUTE_SLABS = 2


def _rows2d(a3, tm, d):
    if a3.shape[0] == 1:
        return a3.reshape(1, d)
    return jnp.broadcast_to(a3, (a3.shape[0], tm // a3.shape[0], d)).reshape(tm, d)


def _route_block(d, out_ret, out_gla, mg_ret, mg_gla, x3, gt, sh, sc, g, wro_ref, wgo_ref, wo_ref, wrh_ref, wrl_ref,
                 br, utri_ref, eye_ref, carry_s, h_ref, n2_ref, idx_ref, rank_ref, prob_ref):
    rows = out_ret.shape[0]
    e = N_EXPERTS
    half = rows // ROUTE_SLABS
    slabs = [slice(s * half, (s + 1) * half) for s in range(ROUTE_SLABS)]
    x2 = x3.reshape(rows, d)
    gt2, sh2, sc2 = (_rows2d(v, rows, d) for v in (gt, sh, sc))
    g2 = g.reshape(1, d)

    def rows_of(v, sl):
        return v if v.shape[0] == 1 else v[sl]

    ab = [(_mm(out_ret[sl], wro_ref[...]), _mm(out_gla[sl], wgo_ref[...])) for sl in slabs]
    mix = [_mm((jax.nn.sigmoid(mg_ret[sl].astype(F32)) * a + jax.nn.sigmoid(mg_gla[sl].astype(F32)) * b).astype(BF16),
               wo_ref[...]) for sl, (a, b) in zip(slabs, ab)]
    logits = []
    for sl, m in zip(slabs, mix):
        hs = x2[sl] + rows_of(gt2, sl) * m
        h_ref[sl, :] = hs
        ms = jnp.mean(hs * hs, axis=-1, keepdims=True)
        n2 = hs * lax.rsqrt(ms + EPS) * g2 * (1.0 + rows_of(sc2, sl)) + rows_of(sh2, sl)
        n2_ref[sl, :] = _pack_pair(n2)
        n_hi, n_lo = _split_hi_lo(n2)
        logits.append(_mm_nt(wrh_ref[...], n_hi) + _mm_nt(wrh_ref[...], n_lo) + _mm_nt(wrl_ref[...], n_hi) + br)

    iota_e = lax.broadcasted_iota(jnp.int32, (e, half), 0)
    slot = lax.broadcasted_iota(jnp.int32, (TOP_K, half), 0)
    carry = carry_s[...]
    for sl, work in zip(slabs, logits):
        vals, idxs = [], []
        for _ in range(TOP_K):
            m = jnp.max(work, axis=0, keepdims=True)
            ik = jnp.min(jnp.where(work == m, iota_e, e), axis=0, keepdims=True)
            vals.append(m)
            idxs.append(ik)
            work = jnp.where(iota_e == ik, -jnp.inf, work)
        ex = [jnp.exp(v - vals[0]) for v in vals]
        den = ex[0] + ex[1] + ex[2] + ex[3]
        onehot = jnp.zeros((e, half), F32)
        for ik in idxs:
            onehot = onehot + (iota_e == ik).astype(F32)
        cum = _mm(onehot.astype(BF16), utri_ref[...]) + carry
        carry = carry + jnp.sum(onehot, axis=1, keepdims=True)
        idx_o = jnp.zeros((TOP_K, half), jnp.int32)
        rank_o = jnp.zeros((TOP_K, half), jnp.int32)
        prob_t = jnp.zeros((TOP_K, half), F32)
        for k in range(TOP_K):
            rk = jnp.sum(jnp.where(iota_e == idxs[k], cum, 0.0), axis=0, keepdims=True).astype(jnp.int32)
            idx_o = jnp.where(slot == k, idxs[k], idx_o)
            rank_o = jnp.where(slot == k, rk, rank_o)
            prob_t = jnp.where(slot == k, ex[k] / den, prob_t)
        idx_ref[:, sl] = idx_o
        rank_ref[:, sl] = rank_o
        p1 = prob_t.astype(BF16)
        r1 = prob_t - p1.astype(F32)
        p2 = r1.astype(BF16)
        p3 = (r1 - p2.astype(F32)).astype(BF16)
        pieces = jnp.concatenate([p1, p2, p3, jnp.zeros_like(p1)], axis=0)
        t = _mm_nt(eye_ref[...], pieces)
        prob_ref[sl, :] = t[:, 0:TOP_K] + t[:, TOP_K:2 * TOP_K] + t[:, 2 * TOP_K:3 * TOP_K]
    carry_s[...] = carry


def _frontp_kernel(d, tb, ntb, x0_ref, xa_ref, xb_ref, sh0_ref, sc0_ref, sha_ref, sca_ref, shb_ref, scb_ref,
                   g_ref, w_ref, wm_ref, wl_ref, cosa_ref, sina_ref, cosb_ref, sinb_ref,
                   dmask_ref, qdec_ref, kdec_ref, cdec_ref, tri_ref, wgk_ref, bgk_ref, gn_ref,
                   oret_ref, ogla_ref, mg_ref, sret_ref, sgla_ref, pa_s, pb_s, ga_s, gb_s, sr_s, sg_s):
    p = pl.program_id(0)
    blk = 2 * p
    g = g_ref[...]
    proj = functools.partial(_proj_block, d)
    mix = functools.partial(_mix_block, d, tb)
    tables = (dmask_ref, qdec_ref, kdec_ref, cdec_ref, tri_ref[...], wgk_ref[...], bgk_ref[...], gn_ref[...])

    @pl.when(p == 0)
    def _():
        proj(x0_ref[...], sh0_ref[0], sc0_ref[0], g, w_ref, wm_ref, wl_ref, pa_s, ga_s)

    @pl.when(blk % ntb == 0)
    def _():
        sr_s[...] = jnp.zeros_like(sr_s)
        sg_s[...] = jnp.zeros_like(sg_s)

    proj(xa_ref[...], sha_ref[0], sca_ref[0], g, w_ref, wm_ref, wl_ref, pb_s, gb_s)
    mix(pa_s, ga_s, cosa_ref[...], sina_ref[...], *tables, sr_s, sg_s, oret_ref, ogla_ref, mg_ref, 0)
    proj(xb_ref[...], shb_ref[0], scb_ref[0], g, w_ref, wm_ref, wl_ref, pa_s, ga_s)
    mix(pb_s, gb_s, cosb_ref[...], sinb_ref[...], *tables, sr_s, sg_s, oret_ref, ogla_ref, mg_ref, tb)

    @pl.when((blk + 1) % ntb == ntb - 1)
    def _():
        sret_ref[0] = sr_s[...]
        sgla_ref[0] = sg_s[...]


def _chunk_tri(tb, cg):
    i = jnp.arange(tb)
    return ((i[:, None] >= i[None, :]) & (i[:, None] // cg == i[None, :] // cg)).astype(BF16)


def _front_prompt(x_p, ada_p, g_mix, w_main, w_mg, w_glr, tb, w_gk, b_gk, g_gla):
    b, t, d = x_p.shape
    dk, dv, hq, h = d // 8, d // 4, d // 2, N_HEADS
    ntb = t // tb
    n_blk = b * ntb
    n_tok = b * t
    assert ntb % 2 == 0
    cos_f, sin_f = _rope_tables(0, t, dk)
    dmask, qdec, kdec, cdec = _ret_tables(tb, dk, dv)
    tri = _chunk_tri(tb, GLA_CHUNK)

    def first(p):
        return 0 * p

    def even(p):
        return 2 * p

    def odd(p):
        return 2 * p + 1

    def nxt(p):
        return jnp.minimum(2 * p + 2, n_blk - 1)

    def x_spec(blk_of):
        return pl.BlockSpec((1, tb, d), lambda p: (blk_of(p) // ntb, blk_of(p) % ntb, 0))

    def ada_spec(which, blk_of):
        return pl.BlockSpec((1, 1, 1, d), lambda p: (which, blk_of(p) // ntb, 0, 0))

    def rope_spec(blk_of):
        return pl.BlockSpec((tb, dk), lambda p: (blk_of(p) % ntb, 0))

    state_spec = pl.BlockSpec((1, h, dk, dv), lambda p: ((2 * p) // ntb, 0, 0, 0))
    tok_spec = pl.BlockSpec((2 * tb, d), lambda p: (p, 0))
    return pl.pallas_call(
        functools.partial(_frontp_kernel, d, tb, ntb),
        grid=(n_blk // 2,),
        in_specs=[_resident((1, tb, d)), x_spec(odd), x_spec(nxt),
                  ada_spec(0, first), ada_spec(1, first), ada_spec(0, odd), ada_spec(1, odd),
                  ada_spec(0, nxt), ada_spec(1, nxt),
                  _resident((1, 1, d)), _resident((d, (N_SEG - 2) * d)), _resident((d, 2 * d)),
                  _resident((d, GLA_GATE_RANK)),
                  rope_spec(even), rope_spec(even), rope_spec(odd), rope_spec(odd),
                  _resident((h, tb, tb)), _resident((h, tb, dk)), _resident((h, tb, dk)), _resident((h, 1, dv)),
                  _resident((tb, tb)), _resident((GLA_GATE_RANK, hq)), _resident((1, hq)), _resident((1, dv))],
        out_specs=[tok_spec, tok_spec, pl.BlockSpec((2, 2 * tb, d), lambda p: (0, p, 0)), state_spec, state_spec],
        out_shape=[jax.ShapeDtypeStruct((n_tok, d), BF16), jax.ShapeDtypeStruct((n_tok, d), BF16),
                   jax.ShapeDtypeStruct((2, n_tok, d), BF16),
                   jax.ShapeDtypeStruct((b, h, dk, dv), F32), jax.ShapeDtypeStruct((b, h, dk, dv), F32)],
        scratch_shapes=[pltpu.VMEM((tb, N_SEG * d), BF16), pltpu.VMEM((tb, N_SEG * d), BF16),
                        pltpu.VMEM((tb, GLA_GATE_RANK), F32), pltpu.VMEM((tb, GLA_GATE_RANK), F32),
                        pltpu.VMEM((h, dk, dv), F32), pltpu.VMEM((h, dk, dv), F32)],
        compiler_params=pltpu.CompilerParams(dimension_semantics=("arbitrary",), vmem_limit_bytes=VMEM_LIMIT),
        name="front_prompt",
    )(x_p, x_p, x_p, ada_p, ada_p, ada_p, ada_p, ada_p, ada_p, g_mix.reshape(1, 1, d), w_main, w_mg, w_glr,
      cos_f, sin_f, cos_f, sin_f, dmask, qdec, kdec, cdec, tri, w_gk, b_gk, g_gla)


def _inproj_kernel(d, xs_ref, shs_ref, scs_ref, g_ref, w_ref, wm_ref, wl_ref, proj_ref, glr_ref):
    n = _rms_mod(xs_ref[...], g_ref[...], scs_ref[0], shs_ref[0]).reshape(-1, d).astype(BF16)
    for s in range(N_SEG):
        proj_ref[s] = _mm(n, _w_seg(w_ref, wm_ref, s, d)).astype(BF16)
    glr_ref[...] = _mm(n, wl_ref[...])


def _inproj_sample(tl, x_s, ada_s, g_mix, w_main, w_mg, w_glr):
    bs, ts, d = x_s.shape
    n_tok = bs * ts
    return pl.pallas_call(
        functools.partial(_inproj_kernel, d),
        grid=(tl.n_st,),
        in_specs=[tl.s_x_spec(d), tl.s_ada_spec(0, d), tl.s_ada_spec(1, d),
                  _resident((1, 1, d)), _resident((d, (N_SEG - 2) * d)), _resident((d, 2 * d)),
                  _resident((d, GLA_GATE_RANK))],
        out_specs=[pl.BlockSpec((N_SEG, tl.tm, d), lambda i: (0, i, 0)), tl.s_row_spec(GLA_GATE_RANK)],
        out_shape=[jax.ShapeDtypeStruct((N_SEG, n_tok, d), BF16), jax.ShapeDtypeStruct((n_tok, GLA_GATE_RANK), F32)],
        compiler_params=pltpu.CompilerParams(dimension_semantics=("arbitrary",), vmem_limit_bytes=VMEM_LIMIT),
        name="inproj_sample",
    )(x_s, ada_s, ada_s, g_mix.reshape(1, 1, d), w_main, w_mg, w_glr)


def _mixs_kernel(d, ts, gsz, rqk_ref, rv_ref, rg_ref, gqk_ref, gv_ref, gg_ref, glr_ref, cos_ref, sin_ref,
                 dmask_ref, qdec_ref, kdec_ref, cdec_ref, wgk_ref, bgk_ref, gn_ref, sr_in, sg_in,
                 oret_ref, ogla_ref, sr_out, sg_out):
    dk, dv, hq = d // 8, d // 4, d // 2
    pair = 2 * ts
    cos_f, sin_f = cos_ref[...], sin_ref[...]
    gnorm = gn_ref[...]
    ri = lax.broadcasted_iota(jnp.int32, (pair, pair), 0)
    ci = lax.broadcasted_iota(jnp.int32, (pair, pair), 1)
    causal = jnp.logical_and(ri >= ci, (ri < ts) == (ci < ts))
    tri = causal.astype(F32).astype(BF16)
    first = lax.broadcasted_iota(jnp.int32, (pair, 1), 0) < ts
    masks = [first, jnp.logical_not(first)]

    def body(j, carry):
        rows = pl.ds(pl.multiple_of(j * pair, pair), pair)
        s0, s1 = 2 * j, 2 * j + 1
        la_hi, la_lo = _split_hi_lo(_log_a(glr_ref[rows, :], wgk_ref[...], bgk_ref[...]))
        b = _mm(tri, la_hi) + _mm(tri, la_lo)
        for h in range(N_HEADS):
            o, (n0, n1) = _ret_head(rqk_ref[0, rows, h * dk:(h + 1) * dk].astype(F32),
                                    rqk_ref[0, rows, hq + h * dk:hq + (h + 1) * dk].astype(F32),
                                    rv_ref[0, rows, h * dv:(h + 1) * dv],
                                    rg_ref[0, rows, h * dv:(h + 1) * dv].astype(F32),
                                    [sr_in[s0, h], sr_in[s1, h]], masks, cos_f, sin_f,
                                    dmask_ref[h], qdec_ref[h], kdec_ref[h], cdec_ref[h])
            sr_out[s0, h] = n0
            sr_out[s1, h] = n1
            oret_ref[rows, h * dv:(h + 1) * dv] = o.astype(BF16)
            o, (n0, n1) = _gla_head(gqk_ref[0, rows, h * dk:(h + 1) * dk].astype(F32),
                                    gqk_ref[0, rows, hq + h * dk:hq + (h + 1) * dk].astype(F32),
                                    gv_ref[0, rows, h * dv:(h + 1) * dv],
                                    gg_ref[0, rows, h * dv:(h + 1) * dv].astype(F32),
                                    b[:, h * dk:(h + 1) * dk], [sg_in[s0, h], sg_in[s1, h]], masks, ts,
                                    gnorm, causal)
            sg_out[s0, h] = n0
            sg_out[s1, h] = n1
            ogla_ref[rows, h * dv:(h + 1) * dv] = o.astype(BF16)
        return carry

    lax.fori_loop(0, gsz // 2, body, 0, unroll=2)


def _pair_tables(ts, dk, dv):
    cos_f, sin_f = _rope_tables(PAST_LEN, ts, dk)
    dmask, qdec, kdec, cdec = _ret_tables(ts, dk, dv)
    zero = jnp.zeros_like(dmask)
    dmask2 = jnp.concatenate([jnp.concatenate([dmask, zero], axis=2), jnp.concatenate([zero, dmask], axis=2)], axis=1)

    def twice(a, axis):
        return jnp.concatenate([a, a], axis=axis)

    return twice(cos_f, 0), twice(sin_f, 0), dmask2, twice(qdec, 1), twice(kdec, 1), cdec


def _mix_sample(bs, ts, d, gsz, proj, glr, state_ret, state_gla, w_gk, b_gk, g_gla):
    dk, dv, hq, h = d // 8, d // 4, d // 2, N_HEADS
    assert GLA_CHUNK % ts == 0 and bs % gsz == 0 and gsz % 4 == 0
    rows = gsz * ts
    pair = 2 * ts
    cos_f, sin_f, dmask, qdec, kdec, cdec = _pair_tables(ts, dk, dv)

    def seg(s):
        return pl.BlockSpec((1, rows, d), lambda i: (s, i, 0))

    state_spec = pl.BlockSpec((gsz, h, dk, dv), lambda i: (i, 0, 0, 0))
    tok_spec = pl.BlockSpec((rows, d), lambda i: (i, 0))
    return pl.pallas_call(
        functools.partial(_mixs_kernel, d, ts, gsz),
        grid=(bs // gsz,),
        in_specs=[seg(0), seg(1), seg(2), seg(3), seg(4), seg(5),
                  pl.BlockSpec((rows, GLA_GATE_RANK), lambda i: (i, 0)),
                  _const((pair, dk)), _const((pair, dk)),
                  _const((h, pair, pair)), _const((h, pair, dk)), _const((h, pair, dk)), _const((h, 1, dv)),
                  _const((GLA_GATE_RANK, hq)), _const((1, hq)), _const((1, dv)),
                  state_spec, state_spec],
        out_specs=[tok_spec, tok_spec, state_spec, state_spec],
        out_shape=[jax.ShapeDtypeStruct((bs * ts, d), BF16), jax.ShapeDtypeStruct((bs * ts, d), BF16),
                   jax.ShapeDtypeStruct((bs, h, dk, dv), F32), jax.ShapeDtypeStruct((bs, h, dk, dv), F32)],
        compiler_params=pltpu.CompilerParams(dimension_semantics=("arbitrary",), vmem_limit_bytes=VMEM_LIMIT),
        name="mix_sample",
    )(proj, proj, proj, proj, proj, proj, glr, cos_f, sin_f, dmask, qdec, kdec, cdec, w_gk, b_gk, g_gla,
      state_ret, state_gla)


def _outproj_kernel(n_pt, d, orp_ref, ogp_ref, ors_ref, ogs_ref, mgrp_ref, mggp_ref, mgrs_ref, mggs_ref,
                    xp_ref, xs_ref, gtp_ref, shp_ref, scp_ref, gts_ref, shs_ref, scs_ref, g_ref,
                    wro_ref, wgo_ref, wo_ref, wrh_ref, wrl_ref, br_ref, utri_ref, eye_ref,
                    h_ref, n2_ref, idx_ref, rank_ref, prob_ref, cnt_ref, carry_s):
    i = pl.program_id(0)

    @pl.when(i == 0)
    def _():
        carry_s[...] = jnp.zeros_like(carry_s)

    route = functools.partial(_route_block, d)
    tail = (g_ref[...], wro_ref, wgo_ref, wo_ref, wrh_ref, wrl_ref, br_ref[...], utri_ref, eye_ref, carry_s,
            h_ref, n2_ref, idx_ref, rank_ref, prob_ref)

    @pl.when(i < n_pt)
    def _():
        route(orp_ref[...], ogp_ref[...], mgrp_ref[0], mggp_ref[0], xp_ref[...], gtp_ref[0], shp_ref[0], scp_ref[0],
              *tail)

    @pl.when(i >= n_pt)
    def _():
        route(ors_ref[...], ogs_ref[...], mgrs_ref[0], mggs_ref[0], xs_ref[...], gts_ref[0], shs_ref[0], scs_ref[0],
              *tail)

    @pl.when(i == pl.num_programs(0) - 1)
    def _():
        cnt_ref[...] = carry_s[...].astype(jnp.int32)


def _outproj(tl, oret_p, ogla_p, oret_s, ogla_s, mg_p, proj_s, x_p, x_s, ada_p, ada_s, g_ffn,
             w_ret_o, w_gla_o, w_out, w_r_hi, w_r_lo, b_router):
    d = x_p.shape[-1]
    tm, e, n_pt = tl.tm, N_EXPERTS, tl.n_pt
    last = n_pt - 1
    p_spec = pl.BlockSpec((tm, d), lambda i: (jnp.minimum(i, last), 0))
    s_spec = pl.BlockSpec((tm, d), lambda i: (jnp.maximum(i - n_pt, 0), 0))

    def mgp_spec(seg):
        return pl.BlockSpec((1, tm, d), lambda i: (seg, jnp.minimum(i, last), 0))

    def mgs_spec(seg):
        return pl.BlockSpec((1, tm, d), lambda i: (seg, jnp.maximum(i - n_pt, 0), 0))

    slot_spec = pl.BlockSpec((TOP_K, tm), lambda i: (0, i))
    half = tm // ROUTE_SLABS
    token = jnp.arange(half)
    utri = (token[:, None] < token[None, :]).astype(BF16)
    eye = jnp.eye(half, dtype=BF16)

    return pl.pallas_call(
        functools.partial(_outproj_kernel, n_pt, d),
        grid=(tl.n,),
        in_specs=[p_spec, p_spec, s_spec, s_spec, mgp_spec(0), mgp_spec(1), mgs_spec(6), mgs_spec(7),
                  tl.xp_spec(d), tl.xs_spec(d),
                  tl.adap_spec(2, d), tl.adap_spec(3, d), tl.adap_spec(4, d),
                  tl.adas_spec(2, d), tl.adas_spec(3, d), tl.adas_spec(4, d),
                  _resident((1, 1, d)), _resident((d, d)), _resident((d, d)), _resident((d, d)),
                  _resident((e, d)), _resident((e, d)), _resident((e, 1)),
                  _resident((half, half)), _resident((half, half))],
        out_specs=[tl.tok_spec(d), tl.tok_spec(d // 2), slot_spec, slot_spec, tl.tok_spec(TOP_K),
                   pl.BlockSpec((e, 1), lambda i: (0, 0))],
        out_shape=[jax.ShapeDtypeStruct((tl.n_tok, d), F32), jax.ShapeDtypeStruct((tl.n_tok, d // 2), jnp.uint32),
                   jax.ShapeDtypeStruct((TOP_K, tl.n_tok), jnp.int32),
                   jax.ShapeDtypeStruct((TOP_K, tl.n_tok), jnp.int32),
                   jax.ShapeDtypeStruct((tl.n_tok, TOP_K), F32),
                   jax.ShapeDtypeStruct((e, 1), jnp.int32)],
        scratch_shapes=[pltpu.VMEM((e, 1), F32)],
        compiler_params=pltpu.CompilerParams(dimension_semantics=("arbitrary",), vmem_limit_bytes=VMEM_LIMIT),
        name="outproj",
    )(oret_p, ogla_p, oret_s, ogla_s, mg_p, mg_p, proj_s, proj_s, x_p, x_s, ada_p, ada_p, ada_p, ada_s, ada_s, ada_s,
      g_ffn.reshape(1, 1, d), w_ret_o, w_gla_o, w_out, w_r_hi, w_r_lo, b_router.reshape(e, 1), utri, eye)


EXPERT_TILES_PER_STEP = 2


def _expert_kernel(f, tme, te_ref, na_ref, grp_ref, nxt_ref, x_ref, wu_hbm, wd_hbm, *rest):
    n_t = EXPERT_TILES_PER_STEP
    bu_refs, bd_refs = rest[:n_t], rest[n_t:2 * n_t]
    y_ref, wu_f, wd_f, wu_s, wd_s, sem = rest[2 * n_t:]
    step = pl.program_id(0)

    def fetch(expert, s):
        return (pltpu.make_async_copy(wu_hbm.at[expert], wu_f.at[s], sem.at[0, s]),
                pltpu.make_async_copy(wd_hbm.at[expert], wd_f.at[s], sem.at[1, s]))

    @pl.when(step == 0)
    def _():
        for c in fetch(te_ref[0], 0):
            c.start()

    slab = tme // EXPERT_ROW_SLABS
    half = x_ref.shape[1]
    for t in range(n_t):
        j = step * n_t + t
        active = j < na_ref[0]
        first = jnp.logical_or(j == 0, te_ref[j] != te_ref[jnp.maximum(j - 1, 0)])
        slot = grp_ref[j] % 2

        @pl.when(jnp.logical_and(active, first))
        def _(j=j, slot=slot):
            for c in fetch(te_ref[j], slot):
                c.wait()

            @pl.when(nxt_ref[j] >= 0)
            def _():
                for c in fetch(nxt_ref[j], 1 - slot):
                    c.start()

            wu_s[...] = wu_f[slot].astype(BF16)
            wd_s[...] = wd_f[slot].astype(BF16)

        @pl.when(active)
        def _(t=t):
            for s in range(EXPERT_ROW_SLABS):
                rows = slice(t * tme + s * slab, t * tme + (s + 1) * slab)
                x_lo, x_hi = _unpack_pair(x_ref[rows, :])
                gu = (_mm(x_lo.astype(BF16), wu_s[:half, :]) + _mm(x_hi.astype(BF16), wu_s[half:, :])
                      + bu_refs[t][0])
                gate = jnp.minimum(gu[:, :f], SWIGLU_LIMIT)
                up = jnp.clip(gu[:, f:], -SWIGLU_LIMIT, SWIGLU_LIMIT)
                act = (up + 1.0) * gate * jax.nn.sigmoid(SWIGLU_ALPHA * gate)
                y_ref[rows, :] = _pack_pair(_mm(act.astype(BF16), wd_s[...]) + bd_refs[t][0])


def _experts(xs, tile_expert, n_active, tile_group, next_expert, w_up, b_up, w_down, b_down, tme):
    r = xs.shape[0]
    e, d, f2 = w_up.shape
    f = f2 // 2
    n_t = EXPERT_TILES_PER_STEP
    n_tiles = r // tme
    assert n_tiles % n_t == 0

    def row_map(s, te, na, grp, nxt):
        return (jnp.minimum(s, (na[0] - 1) // n_t), 0)

    def bias_map(t):
        return lambda s, te, na, grp, nxt: (te[jnp.minimum(s * n_t + t, na[0] - 1)], 0, 0)

    hbm = pl.BlockSpec(memory_space=pl.ANY)
    return pl.pallas_call(
        functools.partial(_expert_kernel, f, tme),
        grid_spec=pltpu.PrefetchScalarGridSpec(
            num_scalar_prefetch=4,
            grid=(n_tiles // n_t,),
            in_specs=[pl.BlockSpec((n_t * tme, d // 2), row_map), hbm, hbm]
                     + [pl.BlockSpec((1, 1, f2), bias_map(t)) for t in range(n_t)]
                     + [pl.BlockSpec((1, 1, d), bias_map(t)) for t in range(n_t)],
            out_specs=pl.BlockSpec((n_t * tme, d // 2), row_map),
            scratch_shapes=[pltpu.VMEM((2, d, f2), F32), pltpu.VMEM((2, f, d), F32),
                            pltpu.VMEM((d, f2), BF16), pltpu.VMEM((f, d), BF16),
                            pltpu.SemaphoreType.DMA((2, 2))]),
        out_shape=jax.ShapeDtypeStruct((r, d // 2), jnp.uint32),
        compiler_params=pltpu.CompilerParams(dimension_semantics=("arbitrary",), vmem_limit_bytes=VMEM_LIMIT),
        name="experts",
    )(tile_expert, n_active, tile_group, next_expert, xs, w_up, w_down,
      *([b_up.reshape(e, 1, f2)] * n_t), *([b_down.reshape(e, 1, d)] * n_t))


def _sc_mesh():
    return plsc.VectorSubcoreMesh(core_axis_name="core", subcore_axis_name="subcore")


def _sc_split(n_rows, max_chunk):
    info = plsc.get_sparse_core_info()
    n_workers = info.num_cores * info.num_subcores
    assert n_rows % (8 * n_workers) == 0
    per_w = n_rows // n_workers
    chunk = 8
    while chunk * 2 <= max_chunk and per_w % (chunk * 2) == 0:
        chunk *= 2
    return info.num_cores, n_workers, per_w, chunk


def _sc_dispatch(x, pos_t, n_rows):
    n, w = x.shape
    nc, nw, per_w, chunk = _sc_split(n, 32)
    n_ch = per_w // chunk
    idx = pos_t.reshape(TOP_K, nw, n_ch, chunk).transpose(1, 0, 2, 3).reshape(nw, TOP_K * n_ch, chunk)

    @functools.partial(
        pl.kernel, out_type=jax.ShapeDtypeStruct((n_rows, w), x.dtype), mesh=_sc_mesh(),
        scratch_types=[pltpu.VMEM((TOP_K * n_ch, chunk), jnp.int32), pltpu.VMEM((2, chunk, w), x.dtype),
                       pltpu.SemaphoreType.DMA((2,)), pltpu.SemaphoreType.DMA((2,))])
    def scatter_rows(x_hbm, i_hbm, o_hbm, idx_v, rows_v, rsem, wsem):
        wid = lax.axis_index("subcore") * nc + lax.axis_index("core")
        base = wid * per_w
        pltpu.sync_copy(i_hbm.at[wid], idx_v)

        def read(j, slot):
            return pltpu.make_async_copy(x_hbm.at[pl.ds(base + j * chunk, chunk)], rows_v.at[slot], rsem.at[slot])

        def write(j, slot, k):
            return pltpu.make_async_copy(rows_v.at[slot], o_hbm.at[idx_v.at[k * n_ch + j]], wsem.at[slot])

        read(0, 0).start()

        @pl.loop(0, n_ch, step=2)
        def _(j0):
            for b in range(2):
                j = j0 + b

                @pl.when(j < n_ch)
                def _():
                    read(j, b).wait()

                    @pl.when(j + 1 < n_ch)
                    def _():
                        @pl.when(j >= 1)
                        def _():
                            for k in range(TOP_K):
                                write(j - 1, 1 - b, k).wait()

                        read(j + 1, 1 - b).start()

                    for k in range(TOP_K):
                        write(j, b, k).start()

        for jj in range(max(n_ch - 2, 0), n_ch):
            for k in range(TOP_K):
                write(jj, jj % 2, k).wait()

    return scatter_rows(x, idx)


def _sc_gather(table, idx):
    m = idx.shape[0]
    w = table.shape[1]
    nc, _, per_w, chunk = _sc_split(m, 64)
    n_ch = per_w // chunk

    @functools.partial(
        pl.kernel, out_type=jax.ShapeDtypeStruct((m, w), table.dtype), mesh=_sc_mesh(),
        scratch_types=[pltpu.VMEM((per_w,), jnp.int32), pltpu.VMEM((2, chunk, w), table.dtype),
                       pltpu.SemaphoreType.DMA((2,)), pltpu.SemaphoreType.DMA((2,))])
    def gather_rows(t_hbm, i_hbm, o_hbm, idx_v, rows_v, gsem, wsem):
        wid = lax.axis_index("subcore") * nc + lax.axis_index("core")
        base = wid * per_w
        pltpu.sync_copy(i_hbm.at[pl.ds(base, per_w)], idx_v)

        def gather(j, slot):
            off = pl.multiple_of(j * chunk, chunk)
            return pltpu.make_async_copy(t_hbm.at[idx_v.at[pl.ds(off, chunk)]], rows_v.at[slot], gsem.at[slot])

        def write(j, slot):
            off = pl.multiple_of(j * chunk, chunk)
            return pltpu.make_async_copy(rows_v.at[slot], o_hbm.at[pl.ds(base + off, chunk)], wsem.at[slot])

        gather(0, 0).start()

        @pl.loop(0, n_ch, step=2)
        def _(j0):
            for b in range(2):
                j = j0 + b

                @pl.when(j < n_ch)
                def _():
                    gather(j, b).wait()

                    @pl.when(j + 1 < n_ch)
                    def _():
                        @pl.when(j >= 1)
                        def _():
                            write(j - 1, 1 - b).wait()

                        gather(j + 1, 1 - b).start()

                    write(j, b).start()

        for jj in range(max(n_ch - 2, 0), n_ch):
            write(jj, jj % 2).wait()

    return gather_rows(table, idx)


def _final_kernel(n_pt, d, h_ref, yg_ref, prob_ref, gtp_ref, gts_ref, g_ref, yp_ref, ys_ref):
    i = pl.program_id(0)
    p = prob_ref[...]
    moe_lo, moe_hi = None, None
    for k in range(TOP_K):
        lo, hi = _unpack_pair(yg_ref[k])
        pk = p[:, k:k + 1]
        moe_lo = pk * lo if moe_lo is None else moe_lo + pk * lo
        moe_hi = pk * hi if moe_hi is None else moe_hi + pk * hi
    moe = jnp.concatenate([moe_lo, moe_hi], axis=1)

    def body(gt, shape):
        h3 = h_ref[...].reshape(shape) + gt * moe.reshape(shape)
        ms = jnp.mean(h3 * h3, axis=-1, keepdims=True)
        return h3 * lax.rsqrt(ms + EPS) * g_ref[...]

    @pl.when(i < n_pt)
    def _():
        yp_ref[...] = body(gtp_ref[0], yp_ref.shape)

    @pl.when(i >= n_pt)
    def _():
        ys_ref[...] = body(gts_ref[0], ys_ref.shape)


def _final(tl, h, yg, probs, ada_p, ada_s, g_final, d):
    return pl.pallas_call(
        functools.partial(_final_kernel, tl.n_pt, d),
        grid=(tl.n,),
        in_specs=[tl.tok_spec(d), pl.BlockSpec((TOP_K, tl.tm, d // 2), lambda i: (0, i, 0)), tl.tok_spec(TOP_K),
                  tl.adap_spec(5, d), tl.adas_spec(5, d), _resident((1, 1, d))],
        out_specs=[tl.xp_spec(d), tl.xs_spec(d)],
        out_shape=[jax.ShapeDtypeStruct((tl.b, tl.t, d), F32), jax.ShapeDtypeStruct((tl.bs, tl.ts, d), F32)],
        compiler_params=pltpu.CompilerParams(dimension_semantics=("arbitrary",), vmem_limit_bytes=VMEM_LIMIT),
        name="final",
    )(h, yg, probs, ada_p, ada_s, g_final.reshape(1, 1, d))


def _pick(n, pref):
    t = min(n, pref)
    while n % t:
        t //= 2
    return t


def _forward(x_prompt, x_sample, c_prompt, c_sample, state_ret, state_gla, w_ada, b_ada, g_norm_mix, g_norm_ffn,
             w_in, w_gk_up, b_gk, g_gla_norm, w_ret_o, w_gla_o, w_out, w_router, b_router, w_up, b_up,
             w_down, b_down, g_final, *, tm, tb, gsz, tme):
    b, t, d = x_prompt.shape
    bs, ts, _ = x_sample.shape
    assert w_ada.shape[0] == 1, "single layer only"
    assert (b * t) % (2 * tb) == 0 and (b * t) % tm == 0
    e = N_EXPERTS
    tl = _Tiles(b, t, bs, ts, tm)
    n_tok = tl.n_tok

    ada = _ada(jnp.concatenate([c_prompt, c_sample], axis=0), w_ada[0], b_ada[0])
    ada_p = ada[:, :b].reshape(6, b, 1, d)
    ada_s = ada[:, b:].reshape(6, bs, 1, d)

    w_in0 = w_in[0]
    n_main = 6 * d
    w_main = w_in0[:, :n_main].astype(BF16)
    w_mg = w_in0[:, n_main + GLA_GATE_RANK:].astype(BF16)
    w_glr = w_in0[:, n_main:n_main + GLA_GATE_RANK].astype(BF16)
    w_gk = w_gk_up[0].astype(BF16)
    bgk = b_gk[0].reshape(1, -1)
    ggn = g_gla_norm[0].reshape(1, -1)
    w_r = w_router[0].T
    w_r_hi = w_r.astype(BF16)
    w_r_lo = (w_r - w_r_hi.astype(F32)).astype(BF16)
    route_w = (g_norm_ffn[0], w_ret_o[0].astype(BF16), w_gla_o[0].astype(BF16), w_out[0].astype(BF16),
               w_r_hi, w_r_lo, b_router[0])

    oret_p, ogla_p, mg_p, sret_p, sgla_p = _front_prompt(x_prompt, ada_p, g_norm_mix[0], w_main, w_mg, w_glr, tb,
                                                         w_gk, bgk, ggn)
    proj_s, glr_s = _inproj_sample(tl, x_sample, ada_s, g_norm_mix[0], w_main, w_mg, w_glr)
    oret_s, ogla_s, sret_s, sgla_s = _mix_sample(bs, ts, d, gsz, proj_s, glr_s, state_ret[0], state_gla[0],
                                                 w_gk, bgk, ggn)
    h, n2, idx_t, rank_t, probs, counts = _outproj(tl, oret_p, ogla_p, oret_s, ogla_s, mg_p, proj_s, x_prompt, x_sample,
                                               ada_p, ada_s, *route_w)

    counts = counts[:, 0]
    gsize = ((counts + tme - 1) // tme) * tme
    ends = jnp.cumsum(gsize)
    offs = ends - gsize
    experts = jnp.arange(e, dtype=jnp.int32)
    pos_t = jnp.sum(jnp.where(idx_t[..., None] == experts, offs, 0), axis=-1) + rank_t
    max_tiles = (n_tok * TOP_K) // tme + e
    n_active = (ends[-1] // tme).astype(jnp.int32).reshape(1)
    tile_start = jnp.arange(max_tiles, dtype=jnp.int32) * tme
    tile_expert = jnp.minimum(jnp.sum((ends[None, :] <= tile_start[:, None]).astype(jnp.int32), axis=1), e - 1)
    is_first = jnp.logical_and(tile_start < ends[-1],
                               jnp.concatenate([jnp.ones((1,), bool), tile_expert[1:] != tile_expert[:-1]]))
    tile_group = jnp.cumsum(is_first.astype(jnp.int32)) - 1
    later = jnp.logical_and(experts[None, :] > experts[:, None], counts[None, :] > 0)
    next_of = jnp.min(jnp.where(later, experts[None, :], e), axis=1)
    next_of = jnp.where(next_of == e, -1, next_of)
    next_expert = jnp.sum(jnp.where(tile_expert[:, None] == experts, next_of, 0), axis=1).astype(jnp.int32)

    xs = _sc_dispatch(n2, pos_t, max_tiles * tme)
    ys = _experts(xs, tile_expert, n_active, tile_group, next_expert, w_up[0], b_up[0], w_down[0], b_down[0], tme)
    yg = _sc_gather(ys, pos_t.reshape(-1)).reshape(TOP_K, n_tok, d // 2)

    y_p, y_s = _final(tl, h, yg, probs, ada_p, ada_s, g_final, d)
    return (y_p, y_s, sret_p[None], sgla_p[None], sret_s[None], sgla_s[None])


def kernel(x_prompt, x_sample, c_prompt, c_sample, state_ret, state_gla, w_ada, b_ada, g_norm_mix, g_norm_ffn,
           w_in, w_gk_up, b_gk, g_gla_norm, w_ret_o, w_gla_o, w_out, w_router, b_router, w_up, b_up,
           w_down, b_down, g_final):
    t = x_prompt.shape[1]
    bs, ts = x_sample.shape[0], x_sample.shape[1]
    return _forward(x_prompt, x_sample, c_prompt, c_sample, state_ret, state_gla, w_ada, b_ada, g_norm_mix,
                    g_norm_ffn, w_in, w_gk_up, b_gk, g_gla_norm, w_ret_o, w_gla_o, w_out, w_router, b_router,
                    w_up, b_up, w_down, b_down, g_final,
                    tm=_pick(bs * ts, 512), tb=_pick(t, 256), gsz=_pick(bs, 8), tme=512)
```

```python
import functools

import jax
import jax.numpy as jnp
from jax import lax
from jax.experimental import pallas as pl
from jax.experimental.pallas import tpu as pltpu
from jax.experimental.pallas import tpu_sc as plsc

F32 = jnp.float32
BF16 = jnp.bfloat16

N_HEADS = 4
GLA_GATE_RANK = 16
GLA_GATE_NORM = 16.0
GLA_CHUNK = 64
ROPE_BASE = 10000.0
N_EXPERTS = 32
TOP_K = 4
SWIGLU_LIMIT = 7.0
SWIGLU_ALPHA = 1.702
EPS = 1e-6
PAST_LEN = 16384
N_SEG = 8
EXPERT_ROW_SLABS = 2

VMEM_LIMIT = 56 * 1024 * 1024


def _mm(a, b):
    return jnp.dot(a, b, preferred_element_type=F32)


def _mm_nt(a, b):
    return lax.dot_general(a, b, (((1,), (1,)), ((), ())), preferred_element_type=F32)


def _silu(x):
    return x * jax.nn.sigmoid(x)


def _split_hi_lo(x):
    hi = x.astype(BF16)
    lo = (x - hi.astype(F32)).astype(BF16)
    return hi, lo


def _pack_pair(x):
    w = x.shape[1] // 2
    lo = lax.bitcast_convert_type(x[:, :w].astype(BF16).astype(F32), jnp.uint32)
    hi = lax.bitcast_convert_type(x[:, w:].astype(BF16).astype(F32), jnp.uint32)
    return (hi & jnp.uint32(0xFFFF0000)) | (lo >> 16)


def _unpack_pair(p):
    lo = lax.bitcast_convert_type(p << 16, F32)
    hi = lax.bitcast_convert_type(p & jnp.uint32(0xFFFF0000), F32)
    return lo, hi


def _rms_mod(x3, g, sc, sh):
    ms = jnp.mean(x3 * x3, axis=-1, keepdims=True)
    return x3 * lax.rsqrt(ms + EPS) * g * (1.0 + sc) + sh


def _resident(shape):
    zeros = (0,) * len(shape)
    return pl.BlockSpec(shape, lambda i: zeros, pipeline_mode=pl.Buffered(1))


def _const(shape):
    zeros = (0,) * len(shape)
    return pl.BlockSpec(shape, lambda i: zeros)


def _ada_kernel(c_ref, w_ref, b_ref, o_ref):
    cf = _silu(c_ref[...])
    o_ref[0] = _mm(cf.astype(BF16), w_ref[...].astype(BF16)) + b_ref[0]


def _ada(c_all, w_ada, b_ada):
    bc, d = c_all.shape
    n = w_ada.shape[1] // d
    return pl.pallas_call(
        _ada_kernel,
        grid=(n,),
        in_specs=[pl.BlockSpec((bc, d), lambda j: (0, 0)),
                  pl.BlockSpec((d, d), lambda j: (0, j)),
                  pl.BlockSpec((1, 1, d), lambda j: (j, 0, 0))],
        out_specs=pl.BlockSpec((1, bc, d), lambda j: (j, 0, 0)),
        out_shape=jax.ShapeDtypeStruct((n, bc, d), F32),
        compiler_params=pltpu.CompilerParams(dimension_semantics=("arbitrary",), vmem_limit_bytes=VMEM_LIMIT),
        name="ada",
    )(c_all, w_ada, b_ada.reshape(n, 1, d))


class _Tiles:
    def __init__(self, b, t, bs, ts, tm):
        assert t % tm == 0 and (bs * ts) % tm == 0 and tm % ts == 0
        self.b, self.t, self.bs, self.ts, self.tm = b, t, bs, ts, tm
        self.tpb = t // tm
        self.n_pt = b * self.tpb
        self.gs = tm // ts
        self.n_st = (bs * ts) // tm
        self.n = self.n_pt + self.n_st
        self.n_tok = b * t + bs * ts

    def xp_spec(self, d):
        last, tpb = self.n_pt - 1, self.tpb
        return pl.BlockSpec((1, self.tm, d), lambda i: (jnp.minimum(i, last) // tpb, jnp.minimum(i, last) % tpb, 0))

    def xs_spec(self, d):
        n_pt = self.n_pt
        return pl.BlockSpec((self.gs, self.ts, d), lambda i: (jnp.maximum(i - n_pt, 0), 0, 0))

    def adap_spec(self, which, d):
        last, tpb = self.n_pt - 1, self.tpb
        return pl.BlockSpec((1, 1, 1, d), lambda i: (which, jnp.minimum(i, last) // tpb, 0, 0))

    def adas_spec(self, which, d):
        n_pt = self.n_pt
        return pl.BlockSpec((1, self.gs, 1, d), lambda i: (which, jnp.maximum(i - n_pt, 0), 0, 0))

    def tok_spec(self, width):
        return pl.BlockSpec((self.tm, width), lambda i: (i, 0))

    def s_x_spec(self, d):
        return pl.BlockSpec((self.gs, self.ts, d), lambda i: (i, 0, 0))

    def s_ada_spec(self, which, d):
        return pl.BlockSpec((1, self.gs, 1, d), lambda i: (which, i, 0, 0))

    def s_row_spec(self, width):
        return pl.BlockSpec((self.tm, width), lambda i: (i, 0))

    def s_tok_spec(self, width):
        n_pt = self.n_pt
        return pl.BlockSpec((self.tm, width), lambda i: (n_pt + i, 0))


def _rope_tables(pos0, t, dk):
    half = dk // 2
    inv = ROPE_BASE ** (-jnp.arange(half, dtype=jnp.float32) / half)
    pos = pos0 + jnp.arange(t)
    ang = pos.astype(jnp.float32)[:, None] * inv[None, :]
    cos, sin = jnp.cos(ang), jnp.sin(ang)
    return jnp.concatenate([cos, cos], axis=-1), jnp.concatenate([-sin, sin], axis=-1)


def _ret_tables(c, dk, dv):
    h = N_HEADS
    log_gamma = jnp.log1p(-jnp.exp2(-5.0 - jnp.arange(h, dtype=jnp.float32)))
    idx = jnp.arange(c, dtype=jnp.float32)
    rel = idx[:, None] - idx[None, :]
    dmask = jnp.where(rel >= 0, jnp.exp(log_gamma[:, None, None] * jnp.maximum(rel, 0.0)), 0.0)
    kdec = jnp.exp(log_gamma[:, None] * (c - 1 - idx))
    qdec = jnp.exp(log_gamma[:, None] * (idx + 1.0))
    cdec = jnp.exp(log_gamma * c)
    return (dmask,
            jnp.broadcast_to(qdec[:, :, None], (h, c, dk)),
            jnp.broadcast_to(kdec[:, :, None], (h, c, dk)),
            jnp.broadcast_to(cdec[:, None, None], (h, 1, dv)))


def _rot(x, cos_f, sin_f):
    return x * cos_f + pltpu.roll(x, x.shape[-1] // 2, 1) * sin_f


def _cross_and_update(q_lhs, k_end, vh, states, masks):
    if masks is None:
        (s,) = states
        return _mm(q_lhs, s.astype(BF16)), [_mm(k_end.T.astype(BF16), vh)]
    cross, incs = None, []
    for s, m in zip(states, masks):
        c = _mm(q_lhs, s.astype(BF16))
        cross = c if cross is None else jnp.where(m, c, cross)
        incs.append(_mm(jnp.where(m, k_end, 0.0).T.astype(BF16), vh))
    return cross, incs


def _ret_head(q, k, vh, gh, states, masks, cos_f, sin_f, dmask, qdec, kdec, cdec):
    dk = q.shape[-1]
    q = _rot(q, cos_f, sin_f)
    k = _rot(k, cos_f, sin_f) * (dk ** -0.5)
    scores = _mm_nt(q.astype(BF16), k.astype(BF16)) * dmask
    cross, incs = _cross_and_update((q * qdec).astype(BF16), k * kdec, vh, states, masks)
    o = _mm(scores.astype(BF16), vh) + cross
    new_states = [cdec * s + u for s, u in zip(states, incs)]
    mu = jnp.mean(o, axis=-1, keepdims=True)
    oc = o - mu
    var = jnp.mean(oc * oc, axis=-1, keepdims=True)
    return _silu(gh) * (oc * lax.rsqrt(var + EPS)), new_states


def _gla_head(q, k, vh, gh, b, states, masks, c, gnorm, causal):
    dk = q.shape[-1]
    b_t = b.T
    if masks is None:
        b_last = b[c - 1:c, :]
    else:
        b_last = None
        for g, m in enumerate(masks):
            row = b[g * c + c - 1:g * c + c, :]
            b_last = row if b_last is None else jnp.where(m, row, b_last)
    q_in = (q * (dk ** -0.5) * jnp.exp(b)).astype(BF16)
    k_in = (k * jnp.exp(-b)).astype(BF16)
    scores = jnp.where(causal, _mm_nt(q_in, k_in), 0.0)
    cross, incs = _cross_and_update(q_in, k * jnp.exp(b_last - b), vh, states, masks)
    o = _mm(scores.astype(BF16), vh) + cross
    new_states = [jnp.exp(b_t[:, g * c + c - 1:g * c + c]) * s + u for g, (s, u) in enumerate(zip(states, incs))]
    o = o * lax.rsqrt(jnp.mean(o * o, axis=-1, keepdims=True) + EPS) * gnorm
    return _silu(gh) * o, new_states


def _log_a(glr, wgk, bgk):
    z = _mm(glr.astype(BF16), wgk) + bgk
    return (jnp.minimum(z, 0.0) - jnp.log1p(jnp.exp(-jnp.abs(z)))) / GLA_GATE_NORM


def _causal(c):
    return lax.broadcasted_iota(jnp.int32, (c, c), 0) >= lax.broadcasted_iota(jnp.int32, (c, c), 1)


def _gate_rank_columns(w_ref, d):
    start = (N_SEG - 2) * d
    return w_ref.at[:, start:start + GLA_GATE_RANK]


def _merge_gate_columns(w_ref, d):
    start = (N_SEG - 2) * d + GLA_GATE_RANK
    return w_ref[:, start:start + 2 * d]


def _w_seg(w_ref, wm_ref, seg, d):
    if seg < N_SEG - 2:
        return w_ref[:, seg * d:(seg + 1) * d]
    return wm_ref[:, (seg - (N_SEG - 2)) * d:(seg - (N_SEG - 3)) * d]


def _proj_block(d, x3, sh, sc, g, w_ref, wm_ref, wl_ref, proj_s, glr_s):
    n = _rms_mod(x3, g, sc, sh).reshape(-1, d).astype(BF16)
    for seg in range(N_SEG):
        proj_s[:, seg * d:(seg + 1) * d] = _mm(n, _w_seg(w_ref, wm_ref, seg, d)).astype(BF16)
    glr_s[...] = _mm(n, wl_ref[...])


def _mix_block(d, tb, proj_s, glr_s, cos_f, sin_f, dmask_ref, qdec_ref, kdec_ref, cdec_ref, tri, wgk, bgk, gnorm,
               sr_s, sg_s, oret_ref, ogla_ref, mg_ref, r_off):
    dk, dv, hq = d // 8, d // 4, d // 2
    rqk, rv, rg, gqk, gv, gg, mg = (i * d for i in range(7))
    for h in range(N_HEADS):
        o, (s_new,) = _ret_head(proj_s[:, rqk + h * dk:rqk + (h + 1) * dk].astype(F32),
                                proj_s[:, rqk + hq + h * dk:rqk + hq + (h + 1) * dk].astype(F32),
                                proj_s[:, rv + h * dv:rv + (h + 1) * dv],
                                proj_s[:, rg + h * dv:rg + (h + 1) * dv].astype(F32),
                                [sr_s[h]], None, cos_f, sin_f, dmask_ref[h], qdec_ref[h], kdec_ref[h],
                                cdec_ref[h])
        sr_s[h] = s_new
        oret_ref[r_off:r_off + tb, h * dv:(h + 1) * dv] = o.astype(BF16)

    la_hi, la_lo = _split_hi_lo(_log_a(glr_s[...], wgk, bgk))
    b = _mm(tri, la_hi) + _mm(tri, la_lo)
    cg = GLA_CHUNK
    n_c = tb // cg
    causal = _causal(cg)
    b_last = jnp.concatenate([jnp.broadcast_to(b[c * cg + cg - 1:c * cg + cg, :], (cg, hq)) for c in range(n_c)],
                             axis=0)
    gq = proj_s[:, gqk:gqk + hq].astype(F32)
    gk = proj_s[:, gqk + hq:gqk + 2 * hq].astype(F32)
    q_in = (gq * (dk ** -0.5) * jnp.exp(b)).astype(BF16)
    k_in = (gk * jnp.exp(-b)).astype(BF16)
    k_end = gk * jnp.exp(b_last - b)
    intra, incs, decs = {}, {}, {}
    for c in range(n_c):
        rows = slice(c * cg, (c + 1) * cg)
        for h in range(N_HEADS):
            cols = slice(h * dk, (h + 1) * dk)
            vh = proj_s[rows, gv + h * dv:gv + (h + 1) * dv]
            scores = jnp.where(causal, _mm_nt(q_in[rows, cols], k_in[rows, cols]), 0.0)
            intra[c, h] = _mm(scores.astype(BF16), vh)
            incs[c, h] = _mm(k_end[rows, cols].T.astype(BF16), vh)
            decs[c, h] = jnp.exp(b[rows, cols].T[:, cg - 1:cg])
    for h in range(N_HEADS):
        cols = slice(h * dk, (h + 1) * dk)
        s = sg_s[h]
        for c in range(n_c):
            rows = slice(c * cg, (c + 1) * cg)
            o = intra[c, h] + _mm(q_in[rows, cols], s.astype(BF16))
            s = decs[c, h] * s + incs[c, h]
            o = o * lax.rsqrt(jnp.mean(o * o, axis=-1, keepdims=True) + EPS) * gnorm
            gh = proj_s[rows, gg + h * dv:gg + (h + 1) * dv].astype(F32)
            ogla_ref[r_off + c * cg:r_off + (c + 1) * cg, h * dv:(h + 1) * dv] = (_silu(gh) * o).astype(BF16)
        sg_s[h] = s
    mg_ref[0, r_off:r_off + tb, :] = proj_s[:, mg:mg + d]
    mg_ref[1, r_off:r_off + tb, :] = proj_s[:, mg + d:mg + 2 * d]


ROUTE_SLABS = 2


def _rows2d(a3, tm, d):
    if a3.shape[0] == 1:
        return a3.reshape(1, d)
    return jnp.broadcast_to(a3, (a3.shape[0], tm // a3.shape[0], d)).reshape(tm, d)


def _route_block(d, out_ret, out_gla, mg_ret, mg_gla, x3, gt, sh, sc, g, wro_ref, wgo_ref, wo_ref, wrh_ref, wrl_ref,
                 br, utri_ref, eye_ref, carry_s, h_ref, n2_ref, idx_ref, rank_ref, prob_ref):
    rows = out_ret.shape[0]
    e = N_EXPERTS
    half = rows // ROUTE_SLABS
    slabs = [slice(s * half, (s + 1) * half) for s in range(ROUTE_SLABS)]
    x2 = x3.reshape(rows, d)
    gt2, sh2, sc2 = (_rows2d(v, rows, d) for v in (gt, sh, sc))
    g2 = g.reshape(1, d)

    def rows_of(v, sl):
        return v if v.shape[0] == 1 else v[sl]

    ab = [(_mm(out_ret[sl], wro_ref[...]), _mm(out_gla[sl], wgo_ref[...])) for sl in slabs]
    mix = [_mm((jax.nn.sigmoid(mg_ret[sl].astype(F32)) * a + jax.nn.sigmoid(mg_gla[sl].astype(F32)) * b).astype(BF16),
               wo_ref[...]) for sl, (a, b) in zip(slabs, ab)]
    logits = []
    for sl, m in zip(slabs, mix):
        hs = x2[sl] + rows_of(gt2, sl) * m
        h_ref[sl, :] = hs
        ms = jnp.mean(hs * hs, axis=-1, keepdims=True)
        n2 = hs * lax.rsqrt(ms + EPS) * g2 * (1.0 + rows_of(sc2, sl)) + rows_of(sh2, sl)
        n2_ref[sl, :] = _pack_pair(n2)
        n_hi, n_lo = _split_hi_lo(n2)
        logits.append(_mm_nt(wrh_ref[...], n_hi) + _mm_nt(wrh_ref[...], n_lo) + _mm_nt(wrl_ref[...], n_hi) + br)

    iota_e = lax.broadcasted_iota(jnp.int32, (e, half), 0)
    slot = lax.broadcasted_iota(jnp.int32, (TOP_K, half), 0)
    carry = carry_s[...]
    for sl, work in zip(slabs, logits):
        vals, idxs = [], []
        for _ in range(TOP_K):
            m = jnp.max(work, axis=0, keepdims=True)
            ik = jnp.min(jnp.where(work == m, iota_e, e), axis=0, keepdims=True)
            vals.append(m)
            idxs.append(ik)
            work = jnp.where(iota_e == ik, -jnp.inf, work)
        ex = [jnp.exp(v - vals[0]) for v in vals]
        den = ex[0] + ex[1] + ex[2] + ex[3]
        onehot = jnp.zeros((e, half), F32)
        for ik in idxs:
            onehot = onehot + (iota_e == ik).astype(F32)
        cum = _mm(onehot.astype(BF16), utri_ref[...]) + carry
        carry = carry + jnp.sum(onehot, axis=1, keepdims=True)
        idx_o = jnp.zeros((TOP_K, half), jnp.int32)
        rank_o = jnp.zeros((TOP_K, half), jnp.int32)
        prob_t = jnp.zeros((TOP_K, half), F32)
        for k in range(TOP_K):
            rk = jnp.sum(jnp.where(iota_e == idxs[k], cum, 0.0), axis=0, keepdims=True).astype(jnp.int32)
            idx_o = jnp.where(slot == k, idxs[k], idx_o)
            rank_o = jnp.where(slot == k, rk, rank_o)
            prob_t = jnp.where(slot == k, ex[k] / den, prob_t)
        idx_ref[:, sl] = idx_o
        rank_ref[:, sl] = rank_o
        p1 = prob_t.astype(BF16)
        r1 = prob_t - p1.astype(F32)
        p2 = r1.astype(BF16)
        p3 = (r1 - p2.astype(F32)).astype(BF16)
        pieces = jnp.concatenate([p1, p2, p3, jnp.zeros_like(p1)], axis=0)
        t = _mm_nt(eye_ref[...], pieces)
        prob_ref[sl, :] = t[:, 0:TOP_K] + t[:, TOP_K:2 * TOP_K] + t[:, 2 * TOP_K:3 * TOP_K]
    carry_s[...] = carry


def _frontp_kernel(d, tb, ntb, x0_ref, xa_ref, xb_ref, sh0_ref, sc0_ref, sha_ref, sca_ref, shb_ref, scb_ref,
                   g_ref, w_ref, cosa_ref, sina_ref, cosb_ref, sinb_ref,
                   dmask_ref, qdec_ref, kdec_ref, cdec_ref, tri_ref, wgk_ref, bgk_ref, gn_ref,
                   oret_ref, ogla_ref, mg_ref, sret_ref, sgla_ref, pa_s, pb_s, ga_s, gb_s, sr_s, sg_s, wm_ref):
    p = pl.program_id(0)
    blk = 2 * p
    g = g_ref[...]
    proj = functools.partial(_proj_block, d)
    mix = functools.partial(_mix_block, d, tb)
    tables = (dmask_ref, qdec_ref, kdec_ref, cdec_ref, tri_ref[...], wgk_ref[...], bgk_ref[...], gn_ref[...])

    wl_ref = _gate_rank_columns(w_ref, d)

    @pl.when(p == 0)
    def _():
        wm_ref[...] = _merge_gate_columns(w_ref, d)
        proj(x0_ref[...], sh0_ref[0], sc0_ref[0], g, w_ref, wm_ref, wl_ref, pa_s, ga_s)

    @pl.when(blk % ntb == 0)
    def _():
        sr_s[...] = jnp.zeros_like(sr_s)
        sg_s[...] = jnp.zeros_like(sg_s)

    proj(xa_ref[...], sha_ref[0], sca_ref[0], g, w_ref, wm_ref, wl_ref, pb_s, gb_s)
    mix(pa_s, ga_s, cosa_ref[...], sina_ref[...], *tables, sr_s, sg_s, oret_ref, ogla_ref, mg_ref, 0)
    proj(xb_ref[...], shb_ref[0], scb_ref[0], g, w_ref, wm_ref, wl_ref, pa_s, ga_s)
    mix(pb_s, gb_s, cosb_ref[...], sinb_ref[...], *tables, sr_s, sg_s, oret_ref, ogla_ref, mg_ref, tb)

    @pl.when((blk + 1) % ntb == ntb - 1)
    def _():
        sret_ref[0] = sr_s[...]
        sgla_ref[0] = sg_s[...]


def _chunk_tri(tb, cg):
    i = jnp.arange(tb)
    return ((i[:, None] >= i[None, :]) & (i[:, None] // cg == i[None, :] // cg)).astype(BF16)


def _front_prompt(x_p, ada_p, g_mix, w_all, tb, w_gk, b_gk, g_gla):
    b, t, d = x_p.shape
    dk, dv, hq, h = d // 8, d // 4, d // 2, N_HEADS
    ntb = t // tb
    n_blk = b * ntb
    n_tok = b * t
    assert ntb % 2 == 0
    cos_f, sin_f = _rope_tables(0, t, dk)
    dmask, qdec, kdec, cdec = _ret_tables(tb, dk, dv)
    tri = _chunk_tri(tb, GLA_CHUNK)

    def first(p):
        return 0 * p

    def even(p):
        return 2 * p

    def odd(p):
        return 2 * p + 1

    def nxt(p):
        return jnp.minimum(2 * p + 2, n_blk - 1)

    def x_spec(blk_of):
        return pl.BlockSpec((1, tb, d), lambda p: (blk_of(p) // ntb, blk_of(p) % ntb, 0))

    def ada_spec(which, blk_of):
        return pl.BlockSpec((1, 1, 1, d), lambda p: (which, blk_of(p) // ntb, 0, 0))

    def rope_spec(blk_of):
        return pl.BlockSpec((tb, dk), lambda p: (blk_of(p) % ntb, 0))

    state_spec = pl.BlockSpec((1, h, dk, dv), lambda p: ((2 * p) // ntb, 0, 0, 0))
    tok_spec = pl.BlockSpec((2 * tb, d), lambda p: (p, 0))
    return pl.pallas_call(
        functools.partial(_frontp_kernel, d, tb, ntb),
        grid=(n_blk // 2,),
        in_specs=[_resident((1, tb, d)), x_spec(odd), x_spec(nxt),
                  ada_spec(0, first), ada_spec(1, first), ada_spec(0, odd), ada_spec(1, odd),
                  ada_spec(0, nxt), ada_spec(1, nxt),
                  _resident((1, 1, d)), _resident(w_all.shape),
                  rope_spec(even), rope_spec(even), rope_spec(odd), rope_spec(odd),
                  _resident((h, tb, tb)), _resident((h, tb, dk)), _resident((h, tb, dk)), _resident((h, 1, dv)),
                  _resident((tb, tb)), _resident((GLA_GATE_RANK, hq)), _resident((1, hq)), _resident((1, dv))],
        out_specs=[tok_spec, tok_spec, pl.BlockSpec((2, 2 * tb, d), lambda p: (0, p, 0)), state_spec, state_spec],
        out_shape=[jax.ShapeDtypeStruct((n_tok, d), BF16), jax.ShapeDtypeStruct((n_tok, d), BF16),
                   jax.ShapeDtypeStruct((2, n_tok, d), BF16),
                   jax.ShapeDtypeStruct((b, h, dk, dv), F32), jax.ShapeDtypeStruct((b, h, dk, dv), F32)],
        scratch_shapes=[pltpu.VMEM((tb, N_SEG * d), BF16), pltpu.VMEM((tb, N_SEG * d), BF16),
                        pltpu.VMEM((tb, GLA_GATE_RANK), F32), pltpu.VMEM((tb, GLA_GATE_RANK), F32),
                        pltpu.VMEM((h, dk, dv), F32), pltpu.VMEM((h, dk, dv), F32), pltpu.VMEM((d, 2 * d), BF16)],
        compiler_params=pltpu.CompilerParams(dimension_semantics=("arbitrary",), vmem_limit_bytes=VMEM_LIMIT),
        name="front_prompt",
    )(x_p, x_p, x_p, ada_p, ada_p, ada_p, ada_p, ada_p, ada_p, g_mix.reshape(1, 1, d), w_all,
      cos_f, sin_f, cos_f, sin_f, dmask, qdec, kdec, cdec, tri, w_gk, b_gk, g_gla)


def _inproj_kernel(d, xs_ref, shs_ref, scs_ref, g_ref, w_ref, proj_ref, glr_ref, wm_ref):
    @pl.when(pl.program_id(0) == 0)
    def _():
        wm_ref[...] = _merge_gate_columns(w_ref, d)

    n = _rms_mod(xs_ref[...], g_ref[...], scs_ref[0], shs_ref[0]).reshape(-1, d).astype(BF16)
    for s in range(N_SEG):
        proj_ref[s] = _mm(n, _w_seg(w_ref, wm_ref, s, d)).astype(BF16)
    glr_ref[...] = _mm(n, _gate_rank_columns(w_ref, d)[...])


def _inproj_sample(tl, x_s, ada_s, g_mix, w_all):
    bs, ts, d = x_s.shape
    n_tok = bs * ts
    return pl.pallas_call(
        functools.partial(_inproj_kernel, d),
        grid=(tl.n_st,),
        in_specs=[tl.s_x_spec(d), tl.s_ada_spec(0, d), tl.s_ada_spec(1, d),
                  _resident((1, 1, d)), _resident(w_all.shape)],
        out_specs=[pl.BlockSpec((N_SEG, tl.tm, d), lambda i: (0, i, 0)), tl.s_row_spec(GLA_GATE_RANK)],
        out_shape=[jax.ShapeDtypeStruct((N_SEG, n_tok, d), BF16), jax.ShapeDtypeStruct((n_tok, GLA_GATE_RANK), F32)],
        scratch_shapes=[pltpu.VMEM((d, 2 * d), BF16)],
        compiler_params=pltpu.CompilerParams(dimension_semantics=("arbitrary",), vmem_limit_bytes=VMEM_LIMIT),
        name="inproj_sample",
    )(x_s, ada_s, ada_s, g_mix.reshape(1, 1, d), w_all)


def _mixs_kernel(d, ts, gsz, rqk_ref, rv_ref, rg_ref, gqk_ref, gv_ref, gg_ref, glr_ref, cos_ref, sin_ref,
                 dmask_ref, qdec_ref, kdec_ref, cdec_ref, wgk_ref, bgk_ref, gn_ref, sr_in, sg_in,
                 oret_ref, ogla_ref, sr_out, sg_out):
    dk, dv, hq = d // 8, d // 4, d // 2
    pair = 2 * ts
    cos_f, sin_f = cos_ref[...], sin_ref[...]
    gnorm = gn_ref[...]
    ri = lax.broadcasted_iota(jnp.int32, (pair, pair), 0)
    ci = lax.broadcasted_iota(jnp.int32, (pair, pair), 1)
    causal = jnp.logical_and(ri >= ci, (ri < ts) == (ci < ts))
    tri = causal.astype(F32).astype(BF16)
    first = lax.broadcasted_iota(jnp.int32, (pair, 1), 0) < ts
    masks = [first, jnp.logical_not(first)]

    def body(j, carry):
        rows = pl.ds(pl.multiple_of(j * pair, pair), pair)
        s0, s1 = 2 * j, 2 * j + 1
        la_hi, la_lo = _split_hi_lo(_log_a(glr_ref[rows, :], wgk_ref[...], bgk_ref[...]))
        b = _mm(tri, la_hi) + _mm(tri, la_lo)
        for h in range(N_HEADS):
            o, (n0, n1) = _ret_head(rqk_ref[0, rows, h * dk:(h + 1) * dk].astype(F32),
                                    rqk_ref[0, rows, hq + h * dk:hq + (h + 1) * dk].astype(F32),
                                    rv_ref[0, rows, h * dv:(h + 1) * dv],
                                    rg_ref[0, rows, h * dv:(h + 1) * dv].astype(F32),
                                    [sr_in[s0, h], sr_in[s1, h]], masks, cos_f, sin_f,
                                    dmask_ref[h], qdec_ref[h], kdec_ref[h], cdec_ref[h])
            sr_out[s0, h] = n0
            sr_out[s1, h] = n1
            oret_ref[rows, h * dv:(h + 1) * dv] = o.astype(BF16)
            o, (n0, n1) = _gla_head(gqk_ref[0, rows, h * dk:(h + 1) * dk].astype(F32),
                                    gqk_ref[0, rows, hq + h * dk:hq + (h + 1) * dk].astype(F32),
                                    gv_ref[0, rows, h * dv:(h + 1) * dv],
                                    gg_ref[0, rows, h * dv:(h + 1) * dv].astype(F32),
                                    b[:, h * dk:(h + 1) * dk], [sg_in[s0, h], sg_in[s1, h]], masks, ts,
                                    gnorm, causal)
            sg_out[s0, h] = n0
            sg_out[s1, h] = n1
            ogla_ref[rows, h * dv:(h + 1) * dv] = o.astype(BF16)
        return carry

    lax.fori_loop(0, gsz // 2, body, 0, unroll=2)


def _pair_tables(ts, dk, dv):
    cos_f, sin_f = _rope_tables(PAST_LEN, ts, dk)
    dmask, qdec, kdec, cdec = _ret_tables(ts, dk, dv)
    zero = jnp.zeros_like(dmask)
    dmask2 = jnp.concatenate([jnp.concatenate([dmask, zero], axis=2), jnp.concatenate([zero, dmask], axis=2)], axis=1)

    def twice(a, axis):
        return jnp.concatenate([a, a], axis=axis)

    return twice(cos_f, 0), twice(sin_f, 0), dmask2, twice(qdec, 1), twice(kdec, 1), cdec


def _mix_sample(bs, ts, d, gsz, proj, glr, state_ret, state_gla, w_gk, b_gk, g_gla):
    dk, dv, hq, h = d // 8, d // 4, d // 2, N_HEADS
    assert GLA_CHUNK % ts == 0 and bs % gsz == 0 and gsz % 4 == 0
    rows = gsz * ts
    pair = 2 * ts
    cos_f, sin_f, dmask, qdec, kdec, cdec = _pair_tables(ts, dk, dv)

    def seg(s):
        return pl.BlockSpec((1, rows, d), lambda i: (s, i, 0))

    state_spec = pl.BlockSpec((gsz, h, dk, dv), lambda i: (i, 0, 0, 0))
    tok_spec = pl.BlockSpec((rows, d), lambda i: (i, 0))
    return pl.pallas_call(
        functools.partial(_mixs_kernel, d, ts, gsz),
        grid=(bs // gsz,),
        in_specs=[seg(0), seg(1), seg(2), seg(3), seg(4), seg(5),
                  pl.BlockSpec((rows, GLA_GATE_RANK), lambda i: (i, 0)),
                  _const((pair, dk)), _const((pair, dk)),
                  _const((h, pair, pair)), _const((h, pair, dk)), _const((h, pair, dk)), _const((h, 1, dv)),
                  _const((GLA_GATE_RANK, hq)), _const((1, hq)), _const((1, dv)),
                  state_spec, state_spec],
        out_specs=[tok_spec, tok_spec, state_spec, state_spec],
        out_shape=[jax.ShapeDtypeStruct((bs * ts, d), BF16), jax.ShapeDtypeStruct((bs * ts, d), BF16),
                   jax.ShapeDtypeStruct((bs, h, dk, dv), F32), jax.ShapeDtypeStruct((bs, h, dk, dv), F32)],
        compiler_params=pltpu.CompilerParams(dimension_semantics=("arbitrary",), vmem_limit_bytes=VMEM_LIMIT),
        name="mix_sample",
    )(proj, proj, proj, proj, proj, proj, glr, cos_f, sin_f, dmask, qdec, kdec, cdec, w_gk, b_gk, g_gla,
      state_ret, state_gla)


def _outproj_kernel(n_pt, d, orp_ref, ogp_ref, ors_ref, ogs_ref, mgrp_ref, mggp_ref, mgrs_ref, mggs_ref,
                    xp_ref, xs_ref, gtp_ref, shp_ref, scp_ref, gts_ref, shs_ref, scs_ref, g_ref,
                    wro_ref, wgo_ref, wo_ref, wrh_ref, wrl_ref, br_ref, utri_ref, eye_ref,
                    h_ref, n2_ref, idx_ref, rank_ref, prob_ref, cnt_ref, carry_s):
    i = pl.program_id(0)

    @pl.when(i == 0)
    def _():
        carry_s[...] = jnp.zeros_like(carry_s)

    route = functools.partial(_route_block, d)
    tail = (g_ref[...], wro_ref, wgo_ref, wo_ref, wrh_ref, wrl_ref, br_ref[...], utri_ref, eye_ref, carry_s,
            h_ref, n2_ref, idx_ref, rank_ref, prob_ref)

    @pl.when(i < n_pt)
    def _():
        route(orp_ref[...], ogp_ref[...], mgrp_ref[0], mggp_ref[0], xp_ref[...], gtp_ref[0], shp_ref[0], scp_ref[0],
              *tail)

    @pl.when(i >= n_pt)
    def _():
        route(ors_ref[...], ogs_ref[...], mgrs_ref[0], mggs_ref[0], xs_ref[...], gts_ref[0], shs_ref[0], scs_ref[0],
              *tail)

    @pl.when(i == pl.num_programs(0) - 1)
    def _():
        cnt_ref[...] = carry_s[...].astype(jnp.int32)


def _outproj(tl, oret_p, ogla_p, oret_s, ogla_s, mg_p, proj_s, x_p, x_s, ada_p, ada_s, g_ffn,
             w_ret_o, w_gla_o, w_out, w_r_hi, w_r_lo, b_router):
    d = x_p.shape[-1]
    tm, e, n_pt = tl.tm, N_EXPERTS, tl.n_pt
    last = n_pt - 1
    p_spec = pl.BlockSpec((tm, d), lambda i: (jnp.minimum(i, last), 0))
    s_spec = pl.BlockSpec((tm, d), lambda i: (jnp.maximum(i - n_pt, 0), 0))

    def mgp_spec(seg):
        return pl.BlockSpec((1, tm, d), lambda i: (seg, jnp.minimum(i, last), 0))

    def mgs_spec(seg):
        return pl.BlockSpec((1, tm, d), lambda i: (seg, jnp.maximum(i - n_pt, 0), 0))

    slot_spec = pl.BlockSpec((TOP_K, tm), lambda i: (0, i))
    half = tm // ROUTE_SLABS
    token = jnp.arange(half)
    utri = (token[:, None] < token[None, :]).astype(BF16)
    eye = jnp.eye(half, dtype=BF16)

    return pl.pallas_call(
        functools.partial(_outproj_kernel, n_pt, d),
        grid=(tl.n,),
        in_specs=[p_spec, p_spec, s_spec, s_spec, mgp_spec(0), mgp_spec(1), mgs_spec(6), mgs_spec(7),
                  tl.xp_spec(d), tl.xs_spec(d),
                  tl.adap_spec(2, d), tl.adap_spec(3, d), tl.adap_spec(4, d),
                  tl.adas_spec(2, d), tl.adas_spec(3, d), tl.adas_spec(4, d),
                  _resident((1, 1, d)), _resident((d, d)), _resident((d, d)), _resident((d, d)),
                  _resident((e, d)), _resident((e, d)), _resident((e, 1)),
                  _resident((half, half)), _resident((half, half))],
        out_specs=[tl.tok_spec(d), tl.tok_spec(d // 2), slot_spec, slot_spec, tl.tok_spec(TOP_K),
                   pl.BlockSpec((e, 1), lambda i: (0, 0))],
        out_shape=[jax.ShapeDtypeStruct((tl.n_tok, d), F32), jax.ShapeDtypeStruct((tl.n_tok, d // 2), jnp.uint32),
                   jax.ShapeDtypeStruct((TOP_K, tl.n_tok), jnp.int32),
                   jax.ShapeDtypeStruct((TOP_K, tl.n_tok), jnp.int32),
                   jax.ShapeDtypeStruct((tl.n_tok, TOP_K), F32),
                   jax.ShapeDtypeStruct((e, 1), jnp.int32)],
        scratch_shapes=[pltpu.VMEM((e, 1), F32)],
        compiler_params=pltpu.CompilerParams(dimension_semantics=("arbitrary",), vmem_limit_bytes=VMEM_LIMIT),
        name="outproj",
    )(oret_p, ogla_p, oret_s, ogla_s, mg_p, mg_p, proj_s, proj_s, x_p, x_s, ada_p, ada_p, ada_p, ada_s, ada_s, ada_s,
      g_ffn.reshape(1, 1, d), w_ret_o, w_gla_o, w_out, w_r_hi, w_r_lo, b_router.reshape(e, 1), utri, eye)


EXPERT_TILES_PER_STEP = 2


def _expert_kernel(f, tme, te_ref, na_ref, grp_ref, nxt_ref, x_ref, wu_hbm, wd_hbm, *rest):
    n_t = EXPERT_TILES_PER_STEP
    bu_refs, bd_refs = rest[:n_t], rest[n_t:2 * n_t]
    y_ref, wu_f, wd_f, wu_s, wd_s, sem = rest[2 * n_t:]
    step = pl.program_id(0)

    def fetch(expert, s):
        return (pltpu.make_async_copy(wu_hbm.at[expert], wu_f.at[s], sem.at[0, s]),
                pltpu.make_async_copy(wd_hbm.at[expert], wd_f.at[s], sem.at[1, s]))

    @pl.when(step == 0)
    def _():
        for c in fetch(te_ref[0], 0):
            c.start()

    slab = tme // EXPERT_ROW_SLABS
    half = x_ref.shape[1]
    for t in range(n_t):
        j = step * n_t + t
        active = j < na_ref[0]
        first = jnp.logical_or(j == 0, te_ref[j] != te_ref[jnp.maximum(j - 1, 0)])
        slot = grp_ref[j] % 2

        @pl.when(jnp.logical_and(active, first))
        def _(j=j, slot=slot):
            for c in fetch(te_ref[j], slot):
                c.wait()

            @pl.when(nxt_ref[j] >= 0)
            def _():
                for c in fetch(nxt_ref[j], 1 - slot):
                    c.start()

            wu_s[...] = wu_f[slot].astype(BF16)
            wd_s[...] = wd_f[slot].astype(BF16)

        @pl.when(active)
        def _(t=t):
            for s in range(EXPERT_ROW_SLABS):
                rows = slice(t * tme + s * slab, t * tme + (s + 1) * slab)
                x_lo, x_hi = _unpack_pair(x_ref[rows, :])
                gu = (_mm(x_lo.astype(BF16), wu_s[:half, :]) + _mm(x_hi.astype(BF16), wu_s[half:, :])
                      + bu_refs[t][0])
                gate = jnp.minimum(gu[:, :f], SWIGLU_LIMIT)
                up = jnp.clip(gu[:, f:], -SWIGLU_LIMIT, SWIGLU_LIMIT)
                act = (up + 1.0) * gate * jax.nn.sigmoid(SWIGLU_ALPHA * gate)
                y_ref[rows, :] = _pack_pair(_mm(act.astype(BF16), wd_s[...]) + bd_refs[t][0])


def _experts(xs, tile_expert, n_active, tile_group, next_expert, w_up, b_up, w_down, b_down, tme):
    r = xs.shape[0]
    e, d, f2 = w_up.shape
    f = f2 // 2
    n_t = EXPERT_TILES_PER_STEP
    n_tiles = r // tme
    assert n_tiles % n_t == 0

    def row_map(s, te, na, grp, nxt):
        return (jnp.minimum(s, (na[0] - 1) // n_t), 0)

    def bias_map(t):
        return lambda s, te, na, grp, nxt: (te[jnp.minimum(s * n_t + t, na[0] - 1)], 0, 0)

    hbm = pl.BlockSpec(memory_space=pl.ANY)
    return pl.pallas_call(
        functools.partial(_expert_kernel, f, tme),
        grid_spec=pltpu.PrefetchScalarGridSpec(
            num_scalar_prefetch=4,
            grid=(n_tiles // n_t,),
            in_specs=[pl.BlockSpec((n_t * tme, d // 2), row_map), hbm, hbm]
                     + [pl.BlockSpec((1, 1, f2), bias_map(t)) for t in range(n_t)]
                     + [pl.BlockSpec((1, 1, d), bias_map(t)) for t in range(n_t)],
            out_specs=pl.BlockSpec((n_t * tme, d // 2), row_map),
            scratch_shapes=[pltpu.VMEM((2, d, f2), F32), pltpu.VMEM((2, f, d), F32),
                            pltpu.VMEM((d, f2), BF16), pltpu.VMEM((f, d), BF16),
                            pltpu.SemaphoreType.DMA((2, 2))]),
        out_shape=jax.ShapeDtypeStruct((r, d // 2), jnp.uint32),
        compiler_params=pltpu.CompilerParams(dimension_semantics=("arbitrary",), vmem_limit_bytes=VMEM_LIMIT),
        name="experts",
    )(tile_expert, n_active, tile_group, next_expert, xs, w_up, w_down,
      *([b_up.reshape(e, 1, f2)] * n_t), *([b_down.reshape(e, 1, d)] * n_t))


def _sc_mesh():
    return plsc.VectorSubcoreMesh(core_axis_name="core", subcore_axis_name="subcore")


def _sc_split(n_rows, max_chunk):
    info = plsc.get_sparse_core_info()
    n_workers = info.num_cores * info.num_subcores
    assert n_rows % (8 * n_workers) == 0
    per_w = n_rows // n_workers
    chunk = 8
    while chunk * 2 <= max_chunk and per_w % (chunk * 2) == 0:
        chunk *= 2
    return info.num_cores, n_workers, per_w, chunk


def _sc_dispatch(x, pos_t, n_rows):
    n, w = x.shape
    nc, nw, per_w, chunk = _sc_split(n, 32)
    n_ch = per_w // chunk
    idx = pos_t.reshape(TOP_K, nw, n_ch, chunk).transpose(1, 0, 2, 3).reshape(nw, TOP_K * n_ch, chunk)

    @functools.partial(
        pl.kernel, out_type=jax.ShapeDtypeStruct((n_rows, w), x.dtype), mesh=_sc_mesh(),
        scratch_types=[pltpu.VMEM((TOP_K * n_ch, chunk), jnp.int32), pltpu.VMEM((2, chunk, w), x.dtype),
                       pltpu.SemaphoreType.DMA((2,)), pltpu.SemaphoreType.DMA((2,))])
    def scatter_rows(x_hbm, i_hbm, o_hbm, idx_v, rows_v, rsem, wsem):
        wid = lax.axis_index("subcore") * nc + lax.axis_index("core")
        base = wid * per_w
        pltpu.sync_copy(i_hbm.at[wid], idx_v)

        def read(j, slot):
            return pltpu.make_async_copy(x_hbm.at[pl.ds(base + j * chunk, chunk)], rows_v.at[slot], rsem.at[slot])

        def write(j, slot, k):
            return pltpu.make_async_copy(rows_v.at[slot], o_hbm.at[idx_v.at[k * n_ch + j]], wsem.at[slot])

        read(0, 0).start()

        @pl.loop(0, n_ch, step=2)
        def _(j0):
            for b in range(2):
                j = j0 + b

                @pl.when(j < n_ch)
                def _():
                    read(j, b).wait()

                    @pl.when(j + 1 < n_ch)
                    def _():
                        @pl.when(j >= 1)
                        def _():
                            for k in range(TOP_K):
                                write(j - 1, 1 - b, k).wait()

                        read(j + 1, 1 - b).start()

                    for k in range(TOP_K):
                        write(j, b, k).start()

        for jj in range(max(n_ch - 2, 0), n_ch):
            for k in range(TOP_K):
                write(jj, jj % 2, k).wait()

    return scatter_rows(x, idx)


def _sc_gather(table, idx):
    m = idx.shape[0]
    w = table.shape[1]
    nc, _, per_w, chunk = _sc_split(m, 64)
    n_ch = per_w // chunk

    @functools.partial(
        pl.kernel, out_type=jax.ShapeDtypeStruct((m, w), table.dtype), mesh=_sc_mesh(),
        scratch_types=[pltpu.VMEM((per_w,), jnp.int32), pltpu.VMEM((2, chunk, w), table.dtype),
                       pltpu.SemaphoreType.DMA((2,)), pltpu.SemaphoreType.DMA((2,))])
    def gather_rows(t_hbm, i_hbm, o_hbm, idx_v, rows_v, gsem, wsem):
        wid = lax.axis_index("subcore") * nc + lax.axis_index("core")
        base = wid * per_w
        pltpu.sync_copy(i_hbm.at[pl.ds(base, per_w)], idx_v)

        def gather(j, slot):
            off = pl.multiple_of(j * chunk, chunk)
            return pltpu.make_async_copy(t_hbm.at[idx_v.at[pl.ds(off, chunk)]], rows_v.at[slot], gsem.at[slot])

        def write(j, slot):
            off = pl.multiple_of(j * chunk, chunk)
            return pltpu.make_async_copy(rows_v.at[slot], o_hbm.at[pl.ds(base + off, chunk)], wsem.at[slot])

        gather(0, 0).start()

        @pl.loop(0, n_ch, step=2)
        def _(j0):
            for b in range(2):
                j = j0 + b

                @pl.when(j < n_ch)
                def _():
                    gather(j, b).wait()

                    @pl.when(j + 1 < n_ch)
                    def _():
                        @pl.when(j >= 1)
                        def _():
                            write(j - 1, 1 - b).wait()

                        gather(j + 1, 1 - b).start()

                    write(j, b).start()

        for jj in range(max(n_ch - 2, 0), n_ch):
            write(jj, jj % 2).wait()

    return gather_rows(table, idx)


def _final_kernel(n_pt, d, h_ref, yg_ref, prob_ref, gtp_ref, gts_ref, g_ref, yp_ref, ys_ref):
    i = pl.program_id(0)
    p = prob_ref[...]
    moe_lo, moe_hi = None, None
    for k in range(TOP_K):
        lo, hi = _unpack_pair(yg_ref[k])
        pk = p[:, k:k + 1]
        moe_lo = pk * lo if moe_lo is None else moe_lo + pk * lo
        moe_hi = pk * hi if moe_hi is None else moe_hi + pk * hi
    moe = jnp.concatenate([moe_lo, moe_hi], axis=1)

    def body(gt, shape):
        h3 = h_ref[...].reshape(shape) + gt * moe.reshape(shape)
        ms = jnp.mean(h3 * h3, axis=-1, keepdims=True)
        return h3 * lax.rsqrt(ms + EPS) * g_ref[...]

    @pl.when(i < n_pt)
    def _():
        yp_ref[...] = body(gtp_ref[0], yp_ref.shape)

    @pl.when(i >= n_pt)
    def _():
        ys_ref[...] = body(gts_ref[0], ys_ref.shape)


def _final(tl, h, yg, probs, ada_p, ada_s, g_final, d):
    return pl.pallas_call(
        functools.partial(_final_kernel, tl.n_pt, d),
        grid=(tl.n,),
        in_specs=[tl.tok_spec(d), pl.BlockSpec((TOP_K, tl.tm, d // 2), lambda i: (0, i, 0)), tl.tok_spec(TOP_K),
                  tl.adap_spec(5, d), tl.adas_spec(5, d), _resident((1, 1, d))],
        out_specs=[tl.xp_spec(d), tl.xs_spec(d)],
        out_shape=[jax.ShapeDtypeStruct((tl.b, tl.t, d), F32), jax.ShapeDtypeStruct((tl.bs, tl.ts, d), F32)],
        compiler_params=pltpu.CompilerParams(dimension_semantics=("arbitrary",), vmem_limit_bytes=VMEM_LIMIT),
        name="final",
    )(h, yg, probs, ada_p, ada_s, g_final.reshape(1, 1, d))


def _pick(n, pref):
    t = min(n, pref)
    while n % t:
        t //= 2
    return t


def _forward(x_prompt, x_sample, c_prompt, c_sample, state_ret, state_gla, w_ada, b_ada, g_norm_mix, g_norm_ffn,
             w_in, w_gk_up, b_gk, g_gla_norm, w_ret_o, w_gla_o, w_out, w_router, b_router, w_up, b_up,
             w_down, b_down, g_final, *, tm, tb, gsz, tme):
    b, t, d = x_prompt.shape
    bs, ts, _ = x_sample.shape
    assert w_ada.shape[0] == 1, "single layer only"
    assert (b * t) % (2 * tb) == 0 and (b * t) % tm == 0
    e = N_EXPERTS
    tl = _Tiles(b, t, bs, ts, tm)
    n_tok = tl.n_tok

    ada = _ada(jnp.concatenate([c_prompt, c_sample], axis=0), w_ada[0], b_ada[0])
    ada_p = ada[:, :b].reshape(6, b, 1, d)
    ada_s = ada[:, b:].reshape(6, bs, 1, d)

    w_all = w_in[0].astype(BF16)
    w_gk = w_gk_up[0].astype(BF16)
    bgk = b_gk[0].reshape(1, -1)
    ggn = g_gla_norm[0].reshape(1, -1)
    w_r = w_router[0].T
    w_r_hi = w_r.astype(BF16)
    w_r_lo = (w_r - w_r_hi.astype(F32)).astype(BF16)
    route_w = (g_norm_ffn[0], w_ret_o[0].astype(BF16), w_gla_o[0].astype(BF16), w_out[0].astype(BF16),
               w_r_hi, w_r_lo, b_router[0])

    oret_p, ogla_p, mg_p, sret_p, sgla_p = _front_prompt(x_prompt, ada_p, g_norm_mix[0], w_all, tb,
                                                         w_gk, bgk, ggn)
    proj_s, glr_s = _inproj_sample(tl, x_sample, ada_s, g_norm_mix[0], w_all)
    oret_s, ogla_s, sret_s, sgla_s = _mix_sample(bs, ts, d, gsz, proj_s, glr_s, state_ret[0], state_gla[0],
                                                 w_gk, bgk, ggn)
    h, n2, idx_t, rank_t, probs, counts = _outproj(tl, oret_p, ogla_p, oret_s, ogla_s, mg_p, proj_s, x_prompt, x_sample,
                                               ada_p, ada_s, *route_w)

    counts = counts[:, 0]
    gsize = ((counts + tme - 1) // tme) * tme
    ends = jnp.cumsum(gsize)
    offs = ends - gsize
    experts = jnp.arange(e, dtype=jnp.int32)
    pos_t = jnp.sum(jnp.where(idx_t[..., None] == experts, offs, 0), axis=-1) + rank_t
    max_tiles = (n_tok * TOP_K) // tme + e
    n_active = (ends[-1] // tme).astype(jnp.int32).reshape(1)
    tile_start = jnp.arange(max_tiles, dtype=jnp.int32) * tme
    tile_expert = jnp.minimum(jnp.sum((ends[None, :] <= tile_start[:, None]).astype(jnp.int32), axis=1), e - 1)
    is_first = jnp.logical_and(tile_start < ends[-1],
                               jnp.concatenate([jnp.ones((1,), bool), tile_expert[1:] != tile_expert[:-1]]))
    tile_group = jnp.cumsum(is_first.astype(jnp.int32)) - 1
    later = jnp.logical_and(experts[None, :] > experts[:, None], counts[None, :] > 0)
    next_of = jnp.min(jnp.where(later, experts[None, :], e), axis=1)
    next_of = jnp.where(next_of == e, -1, next_of)
    next_expert = jnp.sum(jnp.where(tile_expert[:, None] == experts, next_of, 0), axis=1).astype(jnp.int32)

    xs = _sc_dispatch(n2, pos_t, max_tiles * tme)
    ys = _experts(xs, tile_expert, n_active, tile_group, next_expert, w_up[0], b_up[0], w_down[0], b_down[0], tme)
    yg = _sc_gather(ys, pos_t.reshape(-1)).reshape(TOP_K, n_tok, d // 2)

    y_p, y_s = _final(tl, h, yg, probs, ada_p, ada_s, g_final, d)
    return (y_p, y_s, sret_p[None], sgla_p[None], sret_s[None], sgla_s[None])


def kernel(x_prompt, x_sample, c_prompt, c_sample, state_ret, state_gla, w_ada, b_ada, g_norm_mix, g_norm_ffn,
           w_in, w_gk_up, b_gk, g_gla_norm, w_ret_o, w_gla_o, w_out, w_router, b_router, w_up, b_up,
           w_down, b_down, g_final):
    t = x_prompt.shape[1]
    bs, ts = x_sample.shape[0], x_sample.shape[1]
    return _forward(x_prompt, x_sample, c_prompt, c_sample, state_ret, state_gla, w_ada, b_ada, g_norm_mix,
                    g_norm_ffn, w_in, w_gk_up, b_gk, g_gla_norm, w_ret_o, w_gla_o, w_out, w_router, b_router,
                    w_up, b_up, w_down, b_down, g_final,
                    tm=_pick(bs * ts, 512), tb=_pick(t, 256), gsz=_pick(bs, 8), tme=512)
```

```python
import functools

import jax
import jax.numpy as jnp
from jax import lax
from jax.experimental import pallas as pl
from jax.experimental.pallas import tpu as pltpu
from jax.experimental.pallas import tpu_sc as plsc

F32 = jnp.float32
BF16 = jnp.bfloat16

N_HEADS = 4
GLA_GATE_RANK = 16
GLA_GATE_NORM = 16.0
GLA_CHUNK = 64
ROPE_BASE = 10000.0
N_EXPERTS = 32
TOP_K = 4
SWIGLU_LIMIT = 7.0
SWIGLU_ALPHA = 1.702
EPS = 1e-6
PAST_LEN = 16384
N_SEG = 8
EXPERT_ROW_SLABS = 2

VMEM_LIMIT = 56 * 1024 * 1024


def _mm(a, b):
    return jnp.dot(a, b, preferred_element_type=F32)


def _mm_nt(a, b):
    return lax.dot_general(a, b, (((1,), (1,)), ((), ())), preferred_element_type=F32)


def _silu(x):
    return x * jax.nn.sigmoid(x)


def _split_hi_lo(x):
    hi = x.astype(BF16)
    lo = (x - hi.astype(F32)).astype(BF16)
    return hi, lo


def _pack_pair(x):
    w = x.shape[1] // 2
    lo = lax.bitcast_convert_type(x[:, :w].astype(BF16).astype(F32), jnp.uint32)
    hi = lax.bitcast_convert_type(x[:, w:].astype(BF16).astype(F32), jnp.uint32)
    return (hi & jnp.uint32(0xFFFF0000)) | (lo >> 16)


def _unpack_pair(p):
    lo = lax.bitcast_convert_type(p << 16, F32)
    hi = lax.bitcast_convert_type(p & jnp.uint32(0xFFFF0000), F32)
    return lo, hi


def _rms_mod(x3, g, sc, sh):
    ms = jnp.mean(x3 * x3, axis=-1, keepdims=True)
    return x3 * lax.rsqrt(ms + EPS) * g * (1.0 + sc) + sh


def _resident(shape):
    zeros = (0,) * len(shape)
    return pl.BlockSpec(shape, lambda i: zeros, pipeline_mode=pl.Buffered(1))


def _const(shape):
    zeros = (0,) * len(shape)
    return pl.BlockSpec(shape, lambda i: zeros)


def _ada_kernel(c_ref, w_ref, b_ref, o_ref):
    cf = _silu(c_ref[...])
    o_ref[0] = _mm(cf.astype(BF16), w_ref[...].astype(BF16)) + b_ref[0]


def _ada(c_all, w_ada, b_ada):
    bc, d = c_all.shape
    n = w_ada.shape[1] // d
    return pl.pallas_call(
        _ada_kernel,
        grid=(n,),
        in_specs=[pl.BlockSpec((bc, d), lambda j: (0, 0)),
                  pl.BlockSpec((d, d), lambda j: (0, j)),
                  pl.BlockSpec((1, 1, d), lambda j: (j, 0, 0))],
        out_specs=pl.BlockSpec((1, bc, d), lambda j: (j, 0, 0)),
        out_shape=jax.ShapeDtypeStruct((n, bc, d), F32),
        compiler_params=pltpu.CompilerParams(dimension_semantics=("arbitrary",), vmem_limit_bytes=VMEM_LIMIT),
        name="ada",
    )(c_all, w_ada, b_ada.reshape(n, 1, d))


class _Tiles:
    def __init__(self, b, t, bs, ts, tm):
        assert t % tm == 0 and (bs * ts) % tm == 0 and tm % ts == 0
        self.b, self.t, self.bs, self.ts, self.tm = b, t, bs, ts, tm
        self.tpb = t // tm
        self.n_pt = b * self.tpb
        self.gs = tm // ts
        self.n_st = (bs * ts) // tm
        self.n = self.n_pt + self.n_st
        self.n_tok = b * t + bs * ts

    def xp_spec(self, d):
        last, tpb = self.n_pt - 1, self.tpb
        return pl.BlockSpec((1, self.tm, d), lambda i: (jnp.minimum(i, last) // tpb, jnp.minimum(i, last) % tpb, 0))

    def xs_spec(self, d):
        n_pt = self.n_pt
        return pl.BlockSpec((self.gs, self.ts, d), lambda i: (jnp.maximum(i - n_pt, 0), 0, 0))

    def adap_spec(self, which, d):
        last, tpb = self.n_pt - 1, self.tpb
        return pl.BlockSpec((1, 1, 1, d), lambda i: (which, jnp.minimum(i, last) // tpb, 0, 0))

    def adas_spec(self, which, d):
        n_pt = self.n_pt
        return pl.BlockSpec((1, self.gs, 1, d), lambda i: (which, jnp.maximum(i - n_pt, 0), 0, 0))

    def tok_spec(self, width):
        return pl.BlockSpec((self.tm, width), lambda i: (i, 0))

    def s_x_spec(self, d):
        return pl.BlockSpec((self.gs, self.ts, d), lambda i: (i, 0, 0))

    def s_ada_spec(self, which, d):
        return pl.BlockSpec((1, self.gs, 1, d), lambda i: (which, i, 0, 0))

    def s_row_spec(self, width):
        return pl.BlockSpec((self.tm, width), lambda i: (i, 0))

    def s_tok_spec(self, width):
        n_pt = self.n_pt
        return pl.BlockSpec((self.tm, width), lambda i: (n_pt + i, 0))


def _rope_tables(pos0, t, dk):
    half = dk // 2
    inv = ROPE_BASE ** (-jnp.arange(half, dtype=jnp.float32) / half)
    pos = pos0 + jnp.arange(t)
    ang = pos.astype(jnp.float32)[:, None] * inv[None, :]
    cos, sin = jnp.cos(ang), jnp.sin(ang)
    return jnp.concatenate([cos, cos], axis=-1), jnp.concatenate([-sin, sin], axis=-1)


def _ret_tables(c, dk, dv):
    h = N_HEADS
    log_gamma = jnp.log1p(-jnp.exp2(-5.0 - jnp.arange(h, dtype=jnp.float32)))
    idx = jnp.arange(c, dtype=jnp.float32)
    rel = idx[:, None] - idx[None, :]
    dmask = jnp.where(rel >= 0, jnp.exp(log_gamma[:, None, None] * jnp.maximum(rel, 0.0)), 0.0)
    kdec = jnp.exp(log_gamma[:, None] * (c - 1 - idx))
    qdec = jnp.exp(log_gamma[:, None] * (idx + 1.0))
    cdec = jnp.exp(log_gamma * c)
    return (dmask,
            jnp.broadcast_to(qdec[:, :, None], (h, c, dk)),
            jnp.broadcast_to(kdec[:, :, None], (h, c, dk)),
            jnp.broadcast_to(cdec[:, None, None], (h, 1, dv)))


def _rot(x, cos_f, sin_f):
    return x * cos_f + pltpu.roll(x, x.shape[-1] // 2, 1) * sin_f


def _cross_and_update(q_lhs, k_end, vh, states, masks):
    if masks is None:
        (s,) = states
        return _mm(q_lhs, s.astype(BF16)), [_mm(k_end.T.astype(BF16), vh)]
    cross, incs = None, []
    for s, m in zip(states, masks):
        c = _mm(q_lhs, s.astype(BF16))
        cross = c if cross is None else jnp.where(m, c, cross)
        incs.append(_mm(jnp.where(m, k_end, 0.0).T.astype(BF16), vh))
    return cross, incs


def _ret_head(q, k, vh, gh, states, masks, cos_f, sin_f, dmask, qdec, kdec, cdec):
    dk = q.shape[-1]
    q = _rot(q, cos_f, sin_f)
    k = _rot(k, cos_f, sin_f) * (dk ** -0.5)
    scores = _mm_nt(q.astype(BF16), k.astype(BF16)) * dmask
    cross, incs = _cross_and_update((q * qdec).astype(BF16), k * kdec, vh, states, masks)
    o = _mm(scores.astype(BF16), vh) + cross
    new_states = [cdec * s + u for s, u in zip(states, incs)]
    mu = jnp.mean(o, axis=-1, keepdims=True)
    oc = o - mu
    var = jnp.mean(oc * oc, axis=-1, keepdims=True)
    return _silu(gh) * (oc * lax.rsqrt(var + EPS)), new_states


def _gla_head(q, k, vh, gh, b, states, masks, c, gnorm, causal):
    dk = q.shape[-1]
    b_t = b.T
    if masks is None:
        b_last = b[c - 1:c, :]
    else:
        b_last = None
        for g, m in enumerate(masks):
            row = b[g * c + c - 1:g * c + c, :]
            b_last = row if b_last is None else jnp.where(m, row, b_last)
    q_in = (q * (dk ** -0.5) * jnp.exp(b)).astype(BF16)
    k_in = (k * jnp.exp(-b)).astype(BF16)
    scores = jnp.where(causal, _mm_nt(q_in, k_in), 0.0)
    cross, incs = _cross_and_update(q_in, k * jnp.exp(b_last - b), vh, states, masks)
    o = _mm(scores.astype(BF16), vh) + cross
    new_states = [jnp.exp(b_t[:, g * c + c - 1:g * c + c]) * s + u for g, (s, u) in enumerate(zip(states, incs))]
    o = o * lax.rsqrt(jnp.mean(o * o, axis=-1, keepdims=True) + EPS) * gnorm
    return _silu(gh) * o, new_states


def _log_a(glr, wgk, bgk):
    z = _mm(glr.astype(BF16), wgk) + bgk
    return (jnp.minimum(z, 0.0) - jnp.log1p(jnp.exp(-jnp.abs(z)))) / GLA_GATE_NORM


def _causal(c):
    return lax.broadcasted_iota(jnp.int32, (c, c), 0) >= lax.broadcasted_iota(jnp.int32, (c, c), 1)


def _gate_rank_columns(w_ref, d):
    start = (N_SEG - 2) * d
    return w_ref.at[:, start:start + GLA_GATE_RANK]


def _merge_gate_columns(w_ref, d):
    start = (N_SEG - 2) * d + GLA_GATE_RANK
    return w_ref[:, start:start + 2 * d]


def _w_seg(w_ref, wm_ref, seg, d):
    if seg < N_SEG - 2:
        return w_ref[:, seg * d:(seg + 1) * d]
    return wm_ref[:, (seg - (N_SEG - 2)) * d:(seg - (N_SEG - 3)) * d]


def _proj_block(d, x3, sh, sc, g, w_ref, wm_ref, wl_ref, proj_s, glr_s):
    n = _rms_mod(x3, g, sc, sh).reshape(-1, d).astype(BF16)
    for seg in range(N_SEG):
        proj_s[:, seg * d:(seg + 1) * d] = _mm(n, _w_seg(w_ref, wm_ref, seg, d)).astype(BF16)
    glr_s[...] = _mm(n, wl_ref[...])


def _mix_block(d, tb, proj_s, glr_s, cos_f, sin_f, dmask_ref, qdec_ref, kdec_ref, cdec_ref, tri, wgk, bgk, gnorm,
               sr_s, sg_s, oret_ref, ogla_ref, mg_ref, r_off):
    dk, dv, hq = d // 8, d // 4, d // 2
    rqk, rv, rg, gqk, gv, gg, mg = (i * d for i in range(7))
    for h in range(N_HEADS):
        o, (s_new,) = _ret_head(proj_s[:, rqk + h * dk:rqk + (h + 1) * dk].astype(F32),
                                proj_s[:, rqk + hq + h * dk:rqk + hq + (h + 1) * dk].astype(F32),
                                proj_s[:, rv + h * dv:rv + (h + 1) * dv],
                                proj_s[:, rg + h * dv:rg + (h + 1) * dv].astype(F32),
                                [sr_s[h]], None, cos_f, sin_f, dmask_ref[h], qdec_ref[h], kdec_ref[h],
                                cdec_ref[h])
        sr_s[h] = s_new
        oret_ref[r_off:r_off + tb, h * dv:(h + 1) * dv] = o.astype(BF16)

    la_hi, la_lo = _split_hi_lo(_log_a(glr_s[...], wgk, bgk))
    b = _mm(tri, la_hi) + _mm(tri, la_lo)
    cg = GLA_CHUNK
    n_c = tb // cg
    causal = _causal(cg)
    b_last = jnp.concatenate([jnp.broadcast_to(b[c * cg + cg - 1:c * cg + cg, :], (cg, hq)) for c in range(n_c)],
                             axis=0)
    gq = proj_s[:, gqk:gqk + hq].astype(F32)
    gk = proj_s[:, gqk + hq:gqk + 2 * hq].astype(F32)
    q_in = (gq * (dk ** -0.5) * jnp.exp(b)).astype(BF16)
    k_in = (gk * jnp.exp(-b)).astype(BF16)
    k_end = gk * jnp.exp(b_last - b)
    intra, incs, decs = {}, {}, {}
    for c in range(n_c):
        rows = slice(c * cg, (c + 1) * cg)
        for h in range(N_HEADS):
            cols = slice(h * dk, (h + 1) * dk)
            vh = proj_s[rows, gv + h * dv:gv + (h + 1) * dv]
            scores = jnp.where(causal, _mm_nt(q_in[rows, cols], k_in[rows, cols]), 0.0)
            intra[c, h] = _mm(scores.astype(BF16), vh)
            incs[c, h] = _mm(k_end[rows, cols].T.astype(BF16), vh)
            decs[c, h] = jnp.exp(b[rows, cols].T[:, cg - 1:cg])
    for h in range(N_HEADS):
        cols = slice(h * dk, (h + 1) * dk)
        s = sg_s[h]
        for c in range(n_c):
            rows = slice(c * cg, (c + 1) * cg)
            o = intra[c, h] + _mm(q_in[rows, cols], s.astype(BF16))
            s = decs[c, h] * s + incs[c, h]
            o = o * lax.rsqrt(jnp.mean(o * o, axis=-1, keepdims=True) + EPS) * gnorm
            gh = proj_s[rows, gg + h * dv:gg + (h + 1) * dv].astype(F32)
            ogla_ref[r_off + c * cg:r_off + (c + 1) * cg, h * dv:(h + 1) * dv] = (_silu(gh) * o).astype(BF16)
        sg_s[h] = s
    mg_ref[0, r_off:r_off + tb, :] = proj_s[:, mg:mg + d]
    mg_ref[1, r_off:r_off + tb, :] = proj_s[:, mg + d:mg + 2 * d]


ROUTE_SLABS = 4


def _rows2d(a3, tm, d):
    if a3.shape[0] == 1:
        return a3.reshape(1, d)
    return jnp.broadcast_to(a3, (a3.shape[0], tm // a3.shape[0], d)).reshape(tm, d)


def _route_block(d, out_ret, out_gla, mg_ret, mg_gla, x3, gt, sh, sc, g, wro_ref, wgo_ref, wo_ref, wrh_ref, wrl_ref,
                 br, utri_ref, eye_ref, carry_s, h_ref, n2_ref, idx_ref, rank_ref, prob_ref):
    rows = out_ret.shape[0]
    e = N_EXPERTS
    half = rows // ROUTE_SLABS
    slabs = [slice(s * half, (s + 1) * half) for s in range(ROUTE_SLABS)]
    x2 = x3.reshape(rows, d)
    gt2, sh2, sc2 = (_rows2d(v, rows, d) for v in (gt, sh, sc))
    g2 = g.reshape(1, d)

    def rows_of(v, sl):
        return v if v.shape[0] == 1 else v[sl]

    ab = [(_mm(out_ret[sl], wro_ref[...]), _mm(out_gla[sl], wgo_ref[...])) for sl in slabs]
    mix = [_mm((jax.nn.sigmoid(mg_ret[sl].astype(F32)) * a + jax.nn.sigmoid(mg_gla[sl].astype(F32)) * b).astype(BF16),
               wo_ref[...]) for sl, (a, b) in zip(slabs, ab)]
    logits = []
    for sl, m in zip(slabs, mix):
        hs = x2[sl] + rows_of(gt2, sl) * m
        h_ref[sl, :] = hs
        ms = jnp.mean(hs * hs, axis=-1, keepdims=True)
        n2 = hs * lax.rsqrt(ms + EPS) * g2 * (1.0 + rows_of(sc2, sl)) + rows_of(sh2, sl)
        n2_ref[sl, :] = _pack_pair(n2)
        n_hi, n_lo = _split_hi_lo(n2)
        logits.append(_mm_nt(wrh_ref[...], n_hi) + _mm_nt(wrh_ref[...], n_lo) + _mm_nt(wrl_ref[...], n_hi) + br)

    iota_e = lax.broadcasted_iota(jnp.int32, (e, half), 0)
    slot = lax.broadcasted_iota(jnp.int32, (TOP_K, half), 0)
    carry = carry_s[...]
    for sl, work in zip(slabs, logits):
        vals, idxs = [], []
        for _ in range(TOP_K):
            m = jnp.max(work, axis=0, keepdims=True)
            ik = jnp.min(jnp.where(work == m, iota_e, e), axis=0, keepdims=True)
            vals.append(m)
            idxs.append(ik)
            work = jnp.where(iota_e == ik, -jnp.inf, work)
        ex = [jnp.exp(v - vals[0]) for v in vals]
        den = ex[0] + ex[1] + ex[2] + ex[3]
        onehot = jnp.zeros((e, half), F32)
        for ik in idxs:
            onehot = onehot + (iota_e == ik).astype(F32)
        cum = _mm(onehot.astype(BF16), utri_ref[...]) + carry
        carry = carry + jnp.sum(onehot, axis=1, keepdims=True)
        idx_o = jnp.zeros((TOP_K, half), jnp.int32)
        rank_o = jnp.zeros((TOP_K, half), jnp.int32)
        prob_t = jnp.zeros((TOP_K, half), F32)
        for k in range(TOP_K):
            rk = jnp.sum(jnp.where(iota_e == idxs[k], cum, 0.0), axis=0, keepdims=True).astype(jnp.int32)
            idx_o = jnp.where(slot == k, idxs[k], idx_o)
            rank_o = jnp.where(slot == k, rk, rank_o)
            prob_t = jnp.where(slot == k, ex[k] / den, prob_t)
        idx_ref[:, sl] = idx_o
        rank_ref[:, sl] = rank_o
        p1 = prob_t.astype(BF16)
        r1 = prob_t - p1.astype(F32)
        p2 = r1.astype(BF16)
        p3 = (r1 - p2.astype(F32)).astype(BF16)
        pieces = jnp.concatenate([p1, p2, p3, jnp.zeros_like(p1)], axis=0)
        t = _mm_nt(eye_ref[...], pieces)
        prob_ref[sl, :] = t[:, 0:TOP_K] + t[:, TOP_K:2 * TOP_K] + t[:, 2 * TOP_K:3 * TOP_K]
    carry_s[...] = carry


def _frontp_kernel(d, tb, ntb, x0_ref, xa_ref, xb_ref, sh0_ref, sc0_ref, sha_ref, sca_ref, shb_ref, scb_ref,
                   g_ref, w_ref, cosa_ref, sina_ref, cosb_ref, sinb_ref,
                   dmask_ref, qdec_ref, kdec_ref, cdec_ref, tri_ref, wgk_ref, bgk_ref, gn_ref,
                   oret_ref, ogla_ref, mg_ref, sret_ref, sgla_ref, pa_s, pb_s, ga_s, gb_s, sr_s, sg_s, wm_ref):
    p = pl.program_id(0)
    blk = 2 * p
    g = g_ref[...]
    proj = functools.partial(_proj_block, d)
    mix = functools.partial(_mix_block, d, tb)
    tables = (dmask_ref, qdec_ref, kdec_ref, cdec_ref, tri_ref[...], wgk_ref[...], bgk_ref[...], gn_ref[...])

    wl_ref = _gate_rank_columns(w_ref, d)

    @pl.when(p == 0)
    def _():
        wm_ref[...] = _merge_gate_columns(w_ref, d)
        proj(x0_ref[...], sh0_ref[0], sc0_ref[0], g, w_ref, wm_ref, wl_ref, pa_s, ga_s)

    @pl.when(blk % ntb == 0)
    def _():
        sr_s[...] = jnp.zeros_like(sr_s)
        sg_s[...] = jnp.zeros_like(sg_s)

    proj(xa_ref[...], sha_ref[0], sca_ref[0], g, w_ref, wm_ref, wl_ref, pb_s, gb_s)
    mix(pa_s, ga_s, cosa_ref[...], sina_ref[...], *tables, sr_s, sg_s, oret_ref, ogla_ref, mg_ref, 0)
    proj(xb_ref[...], shb_ref[0], scb_ref[0], g, w_ref, wm_ref, wl_ref, pa_s, ga_s)
    mix(pb_s, gb_s, cosb_ref[...], sinb_ref[...], *tables, sr_s, sg_s, oret_ref, ogla_ref, mg_ref, tb)

    @pl.when((blk + 1) % ntb == ntb - 1)
    def _():
        sret_ref[0] = sr_s[...]
        sgla_ref[0] = sg_s[...]


def _chunk_tri(tb, cg):
    i = jnp.arange(tb)
    return ((i[:, None] >= i[None, :]) & (i[:, None] // cg == i[None, :] // cg)).astype(BF16)


def _front_prompt(x_p, ada_p, g_mix, w_all, tb, w_gk, b_gk, g_gla):
    b, t, d = x_p.shape
    dk, dv, hq, h = d // 8, d // 4, d // 2, N_HEADS
    ntb = t // tb
    n_blk = b * ntb
    n_tok = b * t
    assert ntb % 2 == 0
    cos_f, sin_f = _rope_tables(0, t, dk)
    dmask, qdec, kdec, cdec = _ret_tables(tb, dk, dv)
    tri = _chunk_tri(tb, GLA_CHUNK)

    def first(p):
        return 0 * p

    def even(p):
        return 2 * p

    def odd(p):
        return 2 * p + 1

    def nxt(p):
        return jnp.minimum(2 * p + 2, n_blk - 1)

    def x_spec(blk_of):
        return pl.BlockSpec((1, tb, d), lambda p: (blk_of(p) // ntb, blk_of(p) % ntb, 0))

    def ada_spec(which, blk_of):
        return pl.BlockSpec((1, 1, 1, d), lambda p: (which, blk_of(p) // ntb, 0, 0))

    def rope_spec(blk_of):
        return pl.BlockSpec((tb, dk), lambda p: (blk_of(p) % ntb, 0))

    state_spec = pl.BlockSpec((1, h, dk, dv), lambda p: ((2 * p) // ntb, 0, 0, 0))
    tok_spec = pl.BlockSpec((2 * tb, d), lambda p: (p, 0))
    return pl.pallas_call(
        functools.partial(_frontp_kernel, d, tb, ntb),
        grid=(n_blk // 2,),
        in_specs=[_resident((1, tb, d)), x_spec(odd), x_spec(nxt),
                  ada_spec(0, first), ada_spec(1, first), ada_spec(0, odd), ada_spec(1, odd),
                  ada_spec(0, nxt), ada_spec(1, nxt),
                  _resident((1, 1, d)), _resident(w_all.shape),
                  rope_spec(even), rope_spec(even), rope_spec(odd), rope_spec(odd),
                  _resident((h, tb, tb)), _resident((h, tb, dk)), _resident((h, tb, dk)), _resident((h, 1, dv)),
                  _resident((tb, tb)), _resident((GLA_GATE_RANK, hq)), _resident((1, hq)), _resident((1, dv))],
        out_specs=[tok_spec, tok_spec, pl.BlockSpec((2, 2 * tb, d), lambda p: (0, p, 0)), state_spec, state_spec],
        out_shape=[jax.ShapeDtypeStruct((n_tok, d), BF16), jax.ShapeDtypeStruct((n_tok, d), BF16),
                   jax.ShapeDtypeStruct((2, n_tok, d), BF16),
                   jax.ShapeDtypeStruct((b, h, dk, dv), F32), jax.ShapeDtypeStruct((b, h, dk, dv), F32)],
        scratch_shapes=[pltpu.VMEM((tb, N_SEG * d), BF16), pltpu.VMEM((tb, N_SEG * d), BF16),
                        pltpu.VMEM((tb, GLA_GATE_RANK), F32), pltpu.VMEM((tb, GLA_GATE_RANK), F32),
                        pltpu.VMEM((h, dk, dv), F32), pltpu.VMEM((h, dk, dv), F32), pltpu.VMEM((d, 2 * d), BF16)],
        compiler_params=pltpu.CompilerParams(dimension_semantics=("arbitrary",), vmem_limit_bytes=VMEM_LIMIT),
        name="front_prompt",
    )(x_p, x_p, x_p, ada_p, ada_p, ada_p, ada_p, ada_p, ada_p, g_mix.reshape(1, 1, d), w_all,
      cos_f, sin_f, cos_f, sin_f, dmask, qdec, kdec, cdec, tri, w_gk, b_gk, g_gla)


def _inproj_kernel(d, xs_ref, shs_ref, scs_ref, g_ref, w_ref, proj_ref, glr_ref, wm_ref):
    @pl.when(pl.program_id(0) == 0)
    def _():
        wm_ref[...] = _merge_gate_columns(w_ref, d)

    n = _rms_mod(xs_ref[...], g_ref[...], scs_ref[0], shs_ref[0]).reshape(-1, d).astype(BF16)
    for s in range(N_SEG):
        proj_ref[s] = _mm(n, _w_seg(w_ref, wm_ref, s, d)).astype(BF16)
    glr_ref[...] = _mm(n, _gate_rank_columns(w_ref, d)[...])


def _inproj_sample(tl, x_s, ada_s, g_mix, w_all):
    bs, ts, d = x_s.shape
    n_tok = bs * ts
    return pl.pallas_call(
        functools.partial(_inproj_kernel, d),
        grid=(tl.n_st,),
        in_specs=[tl.s_x_spec(d), tl.s_ada_spec(0, d), tl.s_ada_spec(1, d),
                  _resident((1, 1, d)), _resident(w_all.shape)],
        out_specs=[pl.BlockSpec((N_SEG, tl.tm, d), lambda i: (0, i, 0)), tl.s_row_spec(GLA_GATE_RANK)],
        out_shape=[jax.ShapeDtypeStruct((N_SEG, n_tok, d), BF16), jax.ShapeDtypeStruct((n_tok, GLA_GATE_RANK), F32)],
        scratch_shapes=[pltpu.VMEM((d, 2 * d), BF16)],
        compiler_params=pltpu.CompilerParams(dimension_semantics=("arbitrary",), vmem_limit_bytes=VMEM_LIMIT),
        name="inproj_sample",
    )(x_s, ada_s, ada_s, g_mix.reshape(1, 1, d), w_all)


def _mixs_kernel(d, ts, gsz, rqk_ref, rv_ref, rg_ref, gqk_ref, gv_ref, gg_ref, glr_ref, cos_ref, sin_ref,
                 dmask_ref, qdec_ref, kdec_ref, cdec_ref, wgk_ref, bgk_ref, gn_ref, sr_in, sg_in,
                 oret_ref, ogla_ref, sr_out, sg_out):
    dk, dv, hq = d // 8, d // 4, d // 2
    pair = 2 * ts
    cos_f, sin_f = cos_ref[...], sin_ref[...]
    gnorm = gn_ref[...]
    ri = lax.broadcasted_iota(jnp.int32, (pair, pair), 0)
    ci = lax.broadcasted_iota(jnp.int32, (pair, pair), 1)
    causal = jnp.logical_and(ri >= ci, (ri < ts) == (ci < ts))
    tri = causal.astype(F32).astype(BF16)
    first = lax.broadcasted_iota(jnp.int32, (pair, 1), 0) < ts
    masks = [first, jnp.logical_not(first)]

    def body(j, carry):
        rows = pl.ds(pl.multiple_of(j * pair, pair), pair)
        s0, s1 = 2 * j, 2 * j + 1
        la_hi, la_lo = _split_hi_lo(_log_a(glr_ref[rows, :], wgk_ref[...], bgk_ref[...]))
        b = _mm(tri, la_hi) + _mm(tri, la_lo)
        for h in range(N_HEADS):
            o, (n0, n1) = _ret_head(rqk_ref[0, rows, h * dk:(h + 1) * dk].astype(F32),
                                    rqk_ref[0, rows, hq + h * dk:hq + (h + 1) * dk].astype(F32),
                                    rv_ref[0, rows, h * dv:(h + 1) * dv],
                                    rg_ref[0, rows, h * dv:(h + 1) * dv].astype(F32),
                                    [sr_in[s0, h], sr_in[s1, h]], masks, cos_f, sin_f,
                                    dmask_ref[h], qdec_ref[h], kdec_ref[h], cdec_ref[h])
            sr_out[s0, h] = n0
            sr_out[s1, h] = n1
            oret_ref[rows, h * dv:(h + 1) * dv] = o.astype(BF16)
            o, (n0, n1) = _gla_head(gqk_ref[0, rows, h * dk:(h + 1) * dk].astype(F32),
                                    gqk_ref[0, rows, hq + h * dk:hq + (h + 1) * dk].astype(F32),
                                    gv_ref[0, rows, h * dv:(h + 1) * dv],
                                    gg_ref[0, rows, h * dv:(h + 1) * dv].astype(F32),
                                    b[:, h * dk:(h + 1) * dk], [sg_in[s0, h], sg_in[s1, h]], masks, ts,
                                    gnorm, causal)
            sg_out[s0, h] = n0
            sg_out[s1, h] = n1
            ogla_ref[rows, h * dv:(h + 1) * dv] = o.astype(BF16)
        return carry

    lax.fori_loop(0, gsz // 2, body, 0, unroll=2)


def _pair_tables(ts, dk, dv):
    cos_f, sin_f = _rope_tables(PAST_LEN, ts, dk)
    dmask, qdec, kdec, cdec = _ret_tables(ts, dk, dv)
    zero = jnp.zeros_like(dmask)
    dmask2 = jnp.concatenate([jnp.concatenate([dmask, zero], axis=2), jnp.concatenate([zero, dmask], axis=2)], axis=1)

    def twice(a, axis):
        return jnp.concatenate([a, a], axis=axis)

    return twice(cos_f, 0), twice(sin_f, 0), dmask2, twice(qdec, 1), twice(kdec, 1), cdec


def _mix_sample(bs, ts, d, gsz, proj, glr, state_ret, state_gla, w_gk, b_gk, g_gla):
    dk, dv, hq, h = d // 8, d // 4, d // 2, N_HEADS
    assert GLA_CHUNK % ts == 0 and bs % gsz == 0 and gsz % 4 == 0
    rows = gsz * ts
    pair = 2 * ts
    cos_f, sin_f, dmask, qdec, kdec, cdec = _pair_tables(ts, dk, dv)

    def seg(s):
        return pl.BlockSpec((1, rows, d), lambda i: (s, i, 0))

    state_spec = pl.BlockSpec((gsz, h, dk, dv), lambda i: (i, 0, 0, 0))
    tok_spec = pl.BlockSpec((rows, d), lambda i: (i, 0))
    return pl.pallas_call(
        functools.partial(_mixs_kernel, d, ts, gsz),
        grid=(bs // gsz,),
        in_specs=[seg(0), seg(1), seg(2), seg(3), seg(4), seg(5),
                  pl.BlockSpec((rows, GLA_GATE_RANK), lambda i: (i, 0)),
                  _const((pair, dk)), _const((pair, dk)),
                  _const((h, pair, pair)), _const((h, pair, dk)), _const((h, pair, dk)), _const((h, 1, dv)),
                  _const((GLA_GATE_RANK, hq)), _const((1, hq)), _const((1, dv)),
                  state_spec, state_spec],
        out_specs=[tok_spec, tok_spec, state_spec, state_spec],
        out_shape=[jax.ShapeDtypeStruct((bs * ts, d), BF16), jax.ShapeDtypeStruct((bs * ts, d), BF16),
                   jax.ShapeDtypeStruct((bs, h, dk, dv), F32), jax.ShapeDtypeStruct((bs, h, dk, dv), F32)],
        compiler_params=pltpu.CompilerParams(dimension_semantics=("arbitrary",), vmem_limit_bytes=VMEM_LIMIT),
        name="mix_sample",
    )(proj, proj, proj, proj, proj, proj, glr, cos_f, sin_f, dmask, qdec, kdec, cdec, w_gk, b_gk, g_gla,
      state_ret, state_gla)


def _outproj_kernel(n_pt, d, orp_ref, ogp_ref, ors_ref, ogs_ref, mgrp_ref, mggp_ref, mgrs_ref, mggs_ref,
                    xp_ref, xs_ref, gtp_ref, shp_ref, scp_ref, gts_ref, shs_ref, scs_ref, g_ref,
                    wro_ref, wgo_ref, wo_ref, wrh_ref, wrl_ref, br_ref, utri_ref, eye_ref,
                    h_ref, n2_ref, idx_ref, rank_ref, prob_ref, cnt_ref, carry_s):
    i = pl.program_id(0)

    @pl.when(i == 0)
    def _():
        carry_s[...] = jnp.zeros_like(carry_s)

    route = functools.partial(_route_block, d)
    tail = (g_ref[...], wro_ref, wgo_ref, wo_ref, wrh_ref, wrl_ref, br_ref[...], utri_ref, eye_ref, carry_s,
            h_ref, n2_ref, idx_ref, rank_ref, prob_ref)

    @pl.when(i < n_pt)
    def _():
        route(orp_ref[...], ogp_ref[...], mgrp_ref[0], mggp_ref[0], xp_ref[...], gtp_ref[0], shp_ref[0], scp_ref[0],
              *tail)

    @pl.when(i >= n_pt)
    def _():
        route(ors_ref[...], ogs_ref[...], mgrs_ref[0], mggs_ref[0], xs_ref[...], gts_ref[0], shs_ref[0], scs_ref[0],
              *tail)

    @pl.when(i == pl.num_programs(0) - 1)
    def _():
        cnt_ref[...] = carry_s[...].astype(jnp.int32)


def _outproj(tl, oret_p, ogla_p, oret_s, ogla_s, mg_p, proj_s, x_p, x_s, ada_p, ada_s, g_ffn,
             w_ret_o, w_gla_o, w_out, w_r_hi, w_r_lo, b_router):
    d = x_p.shape[-1]
    tm, e, n_pt = tl.tm, N_EXPERTS, tl.n_pt
    last = n_pt - 1
    p_spec = pl.BlockSpec((tm, d), lambda i: (jnp.minimum(i, last), 0))
    s_spec = pl.BlockSpec((tm, d), lambda i: (jnp.maximum(i - n_pt, 0), 0))

    def mgp_spec(seg):
        return pl.BlockSpec((1, tm, d), lambda i: (seg, jnp.minimum(i, last), 0))

    def mgs_spec(seg):
        return pl.BlockSpec((1, tm, d), lambda i: (seg, jnp.maximum(i - n_pt, 0), 0))

    slot_spec = pl.BlockSpec((TOP_K, tm), lambda i: (0, i))
    half = tm // ROUTE_SLABS
    token = jnp.arange(half)
    utri = (token[:, None] < token[None, :]).astype(BF16)
    eye = jnp.eye(half, dtype=BF16)

    return pl.pallas_call(
        functools.partial(_outproj_kernel, n_pt, d),
        grid=(tl.n,),
        in_specs=[p_spec, p_spec, s_spec, s_spec, mgp_spec(0), mgp_spec(1), mgs_spec(6), mgs_spec(7),
                  tl.xp_spec(d), tl.xs_spec(d),
                  tl.adap_spec(2, d), tl.adap_spec(3, d), tl.adap_spec(4, d),
                  tl.adas_spec(2, d), tl.adas_spec(3, d), tl.adas_spec(4, d),
                  _resident((1, 1, d)), _resident((d, d)), _resident((d, d)), _resident((d, d)),
                  _resident((e, d)), _resident((e, d)), _resident((e, 1)),
                  _resident((half, half)), _resident((half, half))],
        out_specs=[tl.tok_spec(d), tl.tok_spec(d // 2), slot_spec, slot_spec, tl.tok_spec(TOP_K),
                   pl.BlockSpec((e, 1), lambda i: (0, 0))],
        out_shape=[jax.ShapeDtypeStruct((tl.n_tok, d), F32), jax.ShapeDtypeStruct((tl.n_tok, d // 2), jnp.uint32),
                   jax.ShapeDtypeStruct((TOP_K, tl.n_tok), jnp.int32),
                   jax.ShapeDtypeStruct((TOP_K, tl.n_tok), jnp.int32),
                   jax.ShapeDtypeStruct((tl.n_tok, TOP_K), F32),
                   jax.ShapeDtypeStruct((e, 1), jnp.int32)],
        scratch_shapes=[pltpu.VMEM((e, 1), F32)],
        compiler_params=pltpu.CompilerParams(dimension_semantics=("arbitrary",), vmem_limit_bytes=VMEM_LIMIT),
        name="outproj",
    )(oret_p, ogla_p, oret_s, ogla_s, mg_p, mg_p, proj_s, proj_s, x_p, x_s, ada_p, ada_p, ada_p, ada_s, ada_s, ada_s,
      g_ffn.reshape(1, 1, d), w_ret_o, w_gla_o, w_out, w_r_hi, w_r_lo, b_router.reshape(e, 1), utri, eye)


EXPERT_TILES_PER_STEP = 2


def _expert_kernel(f, tme, te_ref, na_ref, grp_ref, nxt_ref, x_ref, wu_hbm, wd_hbm, *rest):
    n_t = EXPERT_TILES_PER_STEP
    bu_refs, bd_refs = rest[:n_t], rest[n_t:2 * n_t]
    y_ref, wu_f, wd_f, wu_s, wd_s, sem = rest[2 * n_t:]
    step = pl.program_id(0)

    def fetch(expert, s):
        return (pltpu.make_async_copy(wu_hbm.at[expert], wu_f.at[s], sem.at[0, s]),
                pltpu.make_async_copy(wd_hbm.at[expert], wd_f.at[s], sem.at[1, s]))

    @pl.when(step == 0)
    def _():
        for c in fetch(te_ref[0], 0):
            c.start()

    slab = tme // EXPERT_ROW_SLABS
    half = x_ref.shape[1]
    for t in range(n_t):
        j = step * n_t + t
        active = j < na_ref[0]
        first = jnp.logical_or(j == 0, te_ref[j] != te_ref[jnp.maximum(j - 1, 0)])
        slot = grp_ref[j] % 2

        @pl.when(jnp.logical_and(active, first))
        def _(j=j, slot=slot):
            for c in fetch(te_ref[j], slot):
                c.wait()

            @pl.when(nxt_ref[j] >= 0)
            def _():
                for c in fetch(nxt_ref[j], 1 - slot):
                    c.start()

            wu_s[...] = wu_f[slot].astype(BF16)
            wd_s[...] = wd_f[slot].astype(BF16)

        @pl.when(active)
        def _(t=t):
            for s in range(EXPERT_ROW_SLABS):
                rows = slice(t * tme + s * slab, t * tme + (s + 1) * slab)
                x_lo, x_hi = _unpack_pair(x_ref[rows, :])
                gu = (_mm(x_lo.astype(BF16), wu_s[:half, :]) + _mm(x_hi.astype(BF16), wu_s[half:, :])
                      + bu_refs[t][0])
                gate = jnp.minimum(gu[:, :f], SWIGLU_LIMIT)
                up = jnp.clip(gu[:, f:], -SWIGLU_LIMIT, SWIGLU_LIMIT)
                act = (up + 1.0) * gate * jax.nn.sigmoid(SWIGLU_ALPHA * gate)
                y_ref[rows, :] = _pack_pair(_mm(act.astype(BF16), wd_s[...]) + bd_refs[t][0])


def _experts(xs, tile_expert, n_active, tile_group, next_expert, w_up, b_up, w_down, b_down, tme):
    r = xs.shape[0]
    e, d, f2 = w_up.shape
    f = f2 // 2
    n_t = EXPERT_TILES_PER_STEP
    n_tiles = r // tme
    assert n_tiles % n_t == 0

    def row_map(s, te, na, grp, nxt):
        return (jnp.minimum(s, (na[0] - 1) // n_t), 0)

    def bias_map(t):
        return lambda s, te, na, grp, nxt: (te[jnp.minimum(s * n_t + t, na[0] - 1)], 0, 0)

    hbm = pl.BlockSpec(memory_space=pl.ANY)
    return pl.pallas_call(
        functools.partial(_expert_kernel, f, tme),
        grid_spec=pltpu.PrefetchScalarGridSpec(
            num_scalar_prefetch=4,
            grid=(n_tiles // n_t,),
            in_specs=[pl.BlockSpec((n_t * tme, d // 2), row_map), hbm, hbm]
                     + [pl.BlockSpec((1, 1, f2), bias_map(t)) for t in range(n_t)]
                     + [pl.BlockSpec((1, 1, d), bias_map(t)) for t in range(n_t)],
            out_specs=pl.BlockSpec((n_t * tme, d // 2), row_map),
            scratch_shapes=[pltpu.VMEM((2, d, f2), F32), pltpu.VMEM((2, f, d), F32),
                            pltpu.VMEM((d, f2), BF16), pltpu.VMEM((f, d), BF16),
                            pltpu.SemaphoreType.DMA((2, 2))]),
        out_shape=jax.ShapeDtypeStruct((r, d // 2), jnp.uint32),
        compiler_params=pltpu.CompilerParams(dimension_semantics=("arbitrary",), vmem_limit_bytes=VMEM_LIMIT),
        name="experts",
    )(tile_expert, n_active, tile_group, next_expert, xs, w_up, w_down,
      *([b_up.reshape(e, 1, f2)] * n_t), *([b_down.reshape(e, 1, d)] * n_t))


def _sc_mesh():
    return plsc.VectorSubcoreMesh(core_axis_name="core", subcore_axis_name="subcore")


def _sc_split(n_rows, max_chunk):
    info = plsc.get_sparse_core_info()
    n_workers = info.num_cores * info.num_subcores
    assert n_rows % (8 * n_workers) == 0
    per_w = n_rows // n_workers
    chunk = 8
    while chunk * 2 <= max_chunk and per_w % (chunk * 2) == 0:
        chunk *= 2
    return info.num_cores, n_workers, per_w, chunk


def _sc_dispatch(x, pos_t, n_rows):
    n, w = x.shape
    nc, nw, per_w, chunk = _sc_split(n, 32)
    n_ch = per_w // chunk
    idx = pos_t.reshape(TOP_K, nw, n_ch, chunk).transpose(1, 0, 2, 3).reshape(nw, TOP_K * n_ch, chunk)

    @functools.partial(
        pl.kernel, out_type=jax.ShapeDtypeStruct((n_rows, w), x.dtype), mesh=_sc_mesh(),
        scratch_types=[pltpu.VMEM((TOP_K * n_ch, chunk), jnp.int32), pltpu.VMEM((2, chunk, w), x.dtype),
                       pltpu.SemaphoreType.DMA((2,)), pltpu.SemaphoreType.DMA((2,))])
    def scatter_rows(x_hbm, i_hbm, o_hbm, idx_v, rows_v, rsem, wsem):
        wid = lax.axis_index("subcore") * nc + lax.axis_index("core")
        base = wid * per_w
        pltpu.sync_copy(i_hbm.at[wid], idx_v)

        def read(j, slot):
            return pltpu.make_async_copy(x_hbm.at[pl.ds(base + j * chunk, chunk)], rows_v.at[slot], rsem.at[slot])

        def write(j, slot, k):
            return pltpu.make_async_copy(rows_v.at[slot], o_hbm.at[idx_v.at[k * n_ch + j]], wsem.at[slot])

        read(0, 0).start()

        @pl.loop(0, n_ch, step=2)
        def _(j0):
            for b in range(2):
                j = j0 + b

                @pl.when(j < n_ch)
                def _():
                    read(j, b).wait()

                    @pl.when(j + 1 < n_ch)
                    def _():
                        @pl.when(j >= 1)
                        def _():
                            for k in range(TOP_K):
                                write(j - 1, 1 - b, k).wait()

                        read(j + 1, 1 - b).start()

                    for k in range(TOP_K):
                        write(j, b, k).start()

        for jj in range(max(n_ch - 2, 0), n_ch):
            for k in range(TOP_K):
                write(jj, jj % 2, k).wait()

    return scatter_rows(x, idx)


def _sc_gather(table, idx):
    m = idx.shape[0]
    w = table.shape[1]
    nc, _, per_w, chunk = _sc_split(m, 64)
    n_ch = per_w // chunk

    @functools.partial(
        pl.kernel, out_type=jax.ShapeDtypeStruct((m, w), table.dtype), mesh=_sc_mesh(),
        scratch_types=[pltpu.VMEM((per_w,), jnp.int32), pltpu.VMEM((2, chunk, w), table.dtype),
                       pltpu.SemaphoreType.DMA((2,)), pltpu.SemaphoreType.DMA((2,))])
    def gather_rows(t_hbm, i_hbm, o_hbm, idx_v, rows_v, gsem, wsem):
        wid = lax.axis_index("subcore") * nc + lax.axis_index("core")
        base = wid * per_w
        pltpu.sync_copy(i_hbm.at[pl.ds(base, per_w)], idx_v)

        def gather(j, slot):
            off = pl.multiple_of(j * chunk, chunk)
            return pltpu.make_async_copy(t_hbm.at[idx_v.at[pl.ds(off, chunk)]], rows_v.at[slot], gsem.at[slot])

        def write(j, slot):
            off = pl.multiple_of(j * chunk, chunk)
            return pltpu.make_async_copy(rows_v.at[slot], o_hbm.at[pl.ds(base + off, chunk)], wsem.at[slot])

        gather(0, 0).start()

        @pl.loop(0, n_ch, step=2)
        def _(j0):
            for b in range(2):
                j = j0 + b

                @pl.when(j < n_ch)
                def _():
                    gather(j, b).wait()

                    @pl.when(j + 1 < n_ch)
                    def _():
                        @pl.when(j >= 1)
                        def _():
                            write(j - 1, 1 - b).wait()

                        gather(j + 1, 1 - b).start()

                    write(j, b).start()

        for jj in range(max(n_ch - 2, 0), n_ch):
            write(jj, jj % 2).wait()

    return gather_rows(table, idx)


def _final_kernel(n_pt, d, h_ref, yg_ref, prob_ref, gtp_ref, gts_ref, g_ref, yp_ref, ys_ref):
    i = pl.program_id(0)
    p = prob_ref[...]
    moe_lo, moe_hi = None, None
    for k in range(TOP_K):
        lo, hi = _unpack_pair(yg_ref[k])
        pk = p[:, k:k + 1]
        moe_lo = pk * lo if moe_lo is None else moe_lo + pk * lo
        moe_hi = pk * hi if moe_hi is None else moe_hi + pk * hi
    moe = jnp.concatenate([moe_lo, moe_hi], axis=1)

    def body(gt, shape):
        h3 = h_ref[...].reshape(shape) + gt * moe.reshape(shape)
        ms = jnp.mean(h3 * h3, axis=-1, keepdims=True)
        return h3 * lax.rsqrt(ms + EPS) * g_ref[...]

    @pl.when(i < n_pt)
    def _():
        yp_ref[...] = body(gtp_ref[0], yp_ref.shape)

    @pl.when(i >= n_pt)
    def _():
        ys_ref[...] = body(gts_ref[0], ys_ref.shape)


def _final(tl, h, yg, probs, ada_p, ada_s, g_final, d):
    return pl.pallas_call(
        functools.partial(_final_kernel, tl.n_pt, d),
        grid=(tl.n,),
        in_specs=[tl.tok_spec(d), pl.BlockSpec((TOP_K, tl.tm, d // 2), lambda i: (0, i, 0)), tl.tok_spec(TOP_K),
                  tl.adap_spec(5, d), tl.adas_spec(5, d), _resident((1, 1, d))],
        out_specs=[tl.xp_spec(d), tl.xs_spec(d)],
        out_shape=[jax.ShapeDtypeStruct((tl.b, tl.t, d), F32), jax.ShapeDtypeStruct((tl.bs, tl.ts, d), F32)],
        compiler_params=pltpu.CompilerParams(dimension_semantics=("arbitrary",), vmem_limit_bytes=VMEM_LIMIT),
        name="final",
    )(h, yg, probs, ada_p, ada_s, g_final.reshape(1, 1, d))


def _pick(n, pref):
    t = min(n, pref)
    while n % t:
        t //= 2
    return t


def _forward(x_prompt, x_sample, c_prompt, c_sample, state_ret, state_gla, w_ada, b_ada, g_norm_mix, g_norm_ffn,
             w_in, w_gk_up, b_gk, g_gla_norm, w_ret_o, w_gla_o, w_out, w_router, b_router, w_up, b_up,
             w_down, b_down, g_final, *, tm, tb, gsz, tme):
    b, t, d = x_prompt.shape
    bs, ts, _ = x_sample.shape
    assert w_ada.shape[0] == 1, "single layer only"
    assert (b * t) % (2 * tb) == 0 and (b * t) % tm == 0
    e = N_EXPERTS
    tl = _Tiles(b, t, bs, ts, tm)
    n_tok = tl.n_tok

    ada = _ada(jnp.concatenate([c_prompt, c_sample], axis=0), w_ada[0], b_ada[0])
    ada_p = ada[:, :b].reshape(6, b, 1, d)
    ada_s = ada[:, b:].reshape(6, bs, 1, d)

    w_all = w_in[0].astype(BF16)
    w_gk = w_gk_up[0].astype(BF16)
    bgk = b_gk[0].reshape(1, -1)
    ggn = g_gla_norm[0].reshape(1, -1)
    w_r = w_router[0].T
    w_r_hi = w_r.astype(BF16)
    w_r_lo = (w_r - w_r_hi.astype(F32)).astype(BF16)
    route_w = (g_norm_ffn[0], w_ret_o[0].astype(BF16), w_gla_o[0].astype(BF16), w_out[0].astype(BF16),
               w_r_hi, w_r_lo, b_router[0])

    oret_p, ogla_p, mg_p, sret_p, sgla_p = _front_prompt(x_prompt, ada_p, g_norm_mix[0], w_all, tb,
                                                         w_gk, bgk, ggn)
    proj_s, glr_s = _inproj_sample(tl, x_sample, ada_s, g_norm_mix[0], w_all)
    oret_s, ogla_s, sret_s, sgla_s = _mix_sample(bs, ts, d, gsz, proj_s, glr_s, state_ret[0], state_gla[0],
                                                 w_gk, bgk, ggn)
    h, n2, idx_t, rank_t, probs, counts = _outproj(tl, oret_p, ogla_p, oret_s, ogla_s, mg_p, proj_s, x_prompt, x_sample,
                                               ada_p, ada_s, *route_w)

    counts = counts[:, 0]
    gsize = ((counts + tme - 1) // tme) * tme
    ends = jnp.cumsum(gsize)
    offs = ends - gsize
    experts = jnp.arange(e, dtype=jnp.int32)
    pos_t = jnp.sum(jnp.where(idx_t[..., None] == experts, offs, 0), axis=-1) + rank_t
    max_tiles = (n_tok * TOP_K) // tme + e
    n_active = (ends[-1] // tme).astype(jnp.int32).reshape(1)
    tile_start = jnp.arange(max_tiles, dtype=jnp.int32) * tme
    tile_expert = jnp.minimum(jnp.sum((ends[None, :] <= tile_start[:, None]).astype(jnp.int32), axis=1), e - 1)
    is_first = jnp.logical_and(tile_start < ends[-1],
                               jnp.concatenate([jnp.ones((1,), bool), tile_expert[1:] != tile_expert[:-1]]))
    tile_group = jnp.cumsum(is_first.astype(jnp.int32)) - 1
    later = jnp.logical_and(experts[None, :] > experts[:, None], counts[None, :] > 0)
    next_of = jnp.min(jnp.where(later, experts[None, :], e), axis=1)
    next_of = jnp.where(next_of == e, -1, next_of)
    next_expert = jnp.sum(jnp.where(tile_expert[:, None] == experts, next_of, 0), axis=1).astype(jnp.int32)

    xs = _sc_dispatch(n2, pos_t, max_tiles * tme)
    ys = _experts(xs, tile_expert, n_active, tile_group, next_expert, w_up[0], b_up[0], w_down[0], b_down[0], tme)
    yg = _sc_gather(ys, pos_t.reshape(-1)).reshape(TOP_K, n_tok, d // 2)

    y_p, y_s = _final(tl, h, yg, probs, ada_p, ada_s, g_final, d)
    return (y_p, y_s, sret_p[None], sgla_p[None], sret_s[None], sgla_s[None])


def kernel(x_prompt, x_sample, c_prompt, c_sample, state_ret, state_gla, w_ada, b_ada, g_norm_mix, g_norm_ffn,
           w_in, w_gk_up, b_gk, g_gla_norm, w_ret_o, w_gla_o, w_out, w_router, b_router, w_up, b_up,
           w_down, b_down, g_final):
    t = x_prompt.shape[1]
    bs, ts = x_sample.shape[0], x_sample.shape[1]
    return _forward(x_prompt, x_sample, c_prompt, c_sample, state_ret, state_gla, w_ada, b_ada, g_norm_mix,
                    g_norm_ffn, w_in, w_gk_up, b_gk, g_gla_norm, w_ret_o, w_gla_o, w_out, w_router, b_router,
                    w_up, b_up, w_down, b_down, g_final,
                    tm=_pick(bs * ts, 512), tb=_pick(t, 256), gsz=_pick(bs, 8), tme=512)
```

```python
import functools

import jax
import jax.numpy as jnp
from jax import lax
from jax.experimental import pallas as pl
from jax.experimental.pallas import tpu as pltpu
from jax.experimental.pallas import tpu_sc as plsc

F32 = jnp.float32
BF16 = jnp.bfloat16

N_HEADS = 4
GLA_GATE_RANK = 16
GLA_GATE_NORM = 16.0
GLA_CHUNK = 64
ROPE_BASE = 10000.0
N_EXPERTS = 32
TOP_K = 4
SWIGLU_LIMIT = 7.0
SWIGLU_ALPHA = 1.702
EPS = 1e-6
PAST_LEN = 16384
N_SEG = 8
EXPERT_ROW_SLABS = 2

VMEM_LIMIT = 56 * 1024 * 1024


def _mm(a, b):
    return jnp.dot(a, b, preferred_element_type=F32)


def _mm_nt(a, b):
    return lax.dot_general(a, b, (((1,), (1,)), ((), ())), preferred_element_type=F32)


def _silu(x):
    return x * jax.nn.sigmoid(x)


def _split_hi_lo(x):
    hi = x.astype(BF16)
    lo = (x - hi.astype(F32)).astype(BF16)
    return hi, lo


def _pack_pair(x):
    w = x.shape[1] // 2
    lo = lax.bitcast_convert_type(x[:, :w].astype(BF16).astype(F32), jnp.uint32)
    hi = lax.bitcast_convert_type(x[:, w:].astype(BF16).astype(F32), jnp.uint32)
    return (hi & jnp.uint32(0xFFFF0000)) | (lo >> 16)


def _unpack_pair(p):
    lo = lax.bitcast_convert_type(p << 16, F32)
    hi = lax.bitcast_convert_type(p & jnp.uint32(0xFFFF0000), F32)
    return lo, hi


def _rms_mod(x3, g, sc, sh):
    ms = jnp.mean(x3 * x3, axis=-1, keepdims=True)
    return x3 * lax.rsqrt(ms + EPS) * g * (1.0 + sc) + sh


def _resident(shape):
    zeros = (0,) * len(shape)
    return pl.BlockSpec(shape, lambda i: zeros, pipeline_mode=pl.Buffered(1))


def _const(shape):
    zeros = (0,) * len(shape)
    return pl.BlockSpec(shape, lambda i: zeros)


def _ada_kernel(c_ref, w_ref, b_ref, o_ref):
    cf = _silu(c_ref[...])
    o_ref[0] = _mm(cf.astype(BF16), w_ref[...].astype(BF16)) + b_ref[0]


def _ada(c_all, w_ada, b_ada):
    bc, d = c_all.shape
    n = w_ada.shape[1] // d
    return pl.pallas_call(
        _ada_kernel,
        grid=(n,),
        in_specs=[pl.BlockSpec((bc, d), lambda j: (0, 0)),
                  pl.BlockSpec((d, d), lambda j: (0, j)),
                  pl.BlockSpec((1, 1, d), lambda j: (j, 0, 0))],
        out_specs=pl.BlockSpec((1, bc, d), lambda j: (j, 0, 0)),
        out_shape=jax.ShapeDtypeStruct((n, bc, d), F32),
        compiler_params=pltpu.CompilerParams(dimension_semantics=("arbitrary",), vmem_limit_bytes=VMEM_LIMIT),
        name="ada",
    )(c_all, w_ada, b_ada.reshape(n, 1, d))


class _Tiles:
    def __init__(self, b, t, bs, ts, tm):
        assert t % tm == 0 and (bs * ts) % tm == 0 and tm % ts == 0
        self.b, self.t, self.bs, self.ts, self.tm = b, t, bs, ts, tm
        self.tpb = t // tm
        self.n_pt = b * self.tpb
        self.gs = tm // ts
        self.n_st = (bs * ts) // tm
        self.n = self.n_pt + self.n_st
        self.n_tok = b * t + bs * ts

    def xp_spec(self, d):
        last, tpb = self.n_pt - 1, self.tpb
        return pl.BlockSpec((1, self.tm, d), lambda i: (jnp.minimum(i, last) // tpb, jnp.minimum(i, last) % tpb, 0))

    def xs_spec(self, d):
        n_pt = self.n_pt
        return pl.BlockSpec((self.gs, self.ts, d), lambda i: (jnp.maximum(i - n_pt, 0), 0, 0))

    def adap_spec(self, which, d):
        last, tpb = self.n_pt - 1, self.tpb
        return pl.BlockSpec((1, 1, 1, d), lambda i: (which, jnp.minimum(i, last) // tpb, 0, 0))

    def adas_spec(self, which, d):
        n_pt = self.n_pt
        return pl.BlockSpec((1, self.gs, 1, d), lambda i: (which, jnp.maximum(i - n_pt, 0), 0, 0))

    def tok_spec(self, width):
        return pl.BlockSpec((self.tm, width), lambda i: (i, 0))

    def s_x_spec(self, d):
        return pl.BlockSpec((self.gs, self.ts, d), lambda i: (i, 0, 0))

    def s_ada_spec(self, which, d):
        return pl.BlockSpec((1, self.gs, 1, d), lambda i: (which, i, 0, 0))

    def s_row_spec(self, width):
        return pl.BlockSpec((self.tm, width), lambda i: (i, 0))

    def s_tok_spec(self, width):
        n_pt = self.n_pt
        return pl.BlockSpec((self.tm, width), lambda i: (n_pt + i, 0))


def _rope_tables(pos0, t, dk):
    half = dk // 2
    inv = ROPE_BASE ** (-jnp.arange(half, dtype=jnp.float32) / half)
    pos = pos0 + jnp.arange(t)
    ang = pos.astype(jnp.float32)[:, None] * inv[None, :]
    cos, sin = jnp.cos(ang), jnp.sin(ang)
    return jnp.concatenate([cos, cos], axis=-1), jnp.concatenate([-sin, sin], axis=-1)


def _ret_tables(c, dk, dv):
    h = N_HEADS
    log_gamma = jnp.log1p(-jnp.exp2(-5.0 - jnp.arange(h, dtype=jnp.float32)))
    idx = jnp.arange(c, dtype=jnp.float32)
    rel = idx[:, None] - idx[None, :]
    dmask = jnp.where(rel >= 0, jnp.exp(log_gamma[:, None, None] * jnp.maximum(rel, 0.0)), 0.0)
    kdec = jnp.exp(log_gamma[:, None] * (c - 1 - idx))
    qdec = jnp.exp(log_gamma[:, None] * (idx + 1.0))
    cdec = jnp.exp(log_gamma * c)
    return (dmask,
            jnp.broadcast_to(qdec[:, :, None], (h, c, dk)),
            jnp.broadcast_to(kdec[:, :, None], (h, c, dk)),
            jnp.broadcast_to(cdec[:, None, None], (h, 1, dv)))


def _rot(x, cos_f, sin_f):
    return x * cos_f + pltpu.roll(x, x.shape[-1] // 2, 1) * sin_f


def _cross_and_update(q_lhs, k_end, vh, states, masks):
    if masks is None:
        (s,) = states
        return _mm(q_lhs, s.astype(BF16)), [_mm(k_end.T.astype(BF16), vh)]
    cross, incs = None, []
    for s, m in zip(states, masks):
        c = _mm(q_lhs, s.astype(BF16))
        cross = c if cross is None else jnp.where(m, c, cross)
        incs.append(_mm(jnp.where(m, k_end, 0.0).T.astype(BF16), vh))
    return cross, incs


def _ret_head(q, k, vh, gh, states, masks, cos_f, sin_f, dmask, qdec, kdec, cdec):
    dk = q.shape[-1]
    q = _rot(q, cos_f, sin_f)
    k = _rot(k, cos_f, sin_f) * (dk ** -0.5)
    scores = _mm_nt(q.astype(BF16), k.astype(BF16)) * dmask
    cross, incs = _cross_and_update((q * qdec).astype(BF16), k * kdec, vh, states, masks)
    o = _mm(scores.astype(BF16), vh) + cross
    new_states = [cdec * s + u for s, u in zip(states, incs)]
    mu = jnp.mean(o, axis=-1, keepdims=True)
    oc = o - mu
    var = jnp.mean(oc * oc, axis=-1, keepdims=True)
    return _silu(gh) * (oc * lax.rsqrt(var + EPS)), new_states


def _gla_head(q, k, vh, gh, b, states, masks, c, gnorm, causal):
    dk = q.shape[-1]
    b_t = b.T
    if masks is None:
        b_last = b[c - 1:c, :]
    else:
        b_last = None
        for g, m in enumerate(masks):
            row = b[g * c + c - 1:g * c + c, :]
            b_last = row if b_last is None else jnp.where(m, row, b_last)
    q_in = (q * (dk ** -0.5) * jnp.exp(b)).astype(BF16)
    k_in = (k * jnp.exp(-b)).astype(BF16)
    scores = jnp.where(causal, _mm_nt(q_in, k_in), 0.0)
    cross, incs = _cross_and_update(q_in, k * jnp.exp(b_last - b), vh, states, masks)
    o = _mm(scores.astype(BF16), vh) + cross
    new_states = [jnp.exp(b_t[:, g * c + c - 1:g * c + c]) * s + u for g, (s, u) in enumerate(zip(states, incs))]
    o = o * lax.rsqrt(jnp.mean(o * o, axis=-1, keepdims=True) + EPS) * gnorm
    return _silu(gh) * o, new_states


def _log_a(glr, wgk, bgk):
    z = _mm(glr.astype(BF16), wgk) + bgk
    return (jnp.minimum(z, 0.0) - jnp.log1p(jnp.exp(-jnp.abs(z)))) / GLA_GATE_NORM


def _causal(c):
    return lax.broadcasted_iota(jnp.int32, (c, c), 0) >= lax.broadcasted_iota(jnp.int32, (c, c), 1)


def _gate_rank_columns(w_ref, d):
    start = (N_SEG - 2) * d
    return w_ref.at[:, start:start + GLA_GATE_RANK]


def _merge_gate_columns(w_ref, d):
    start = (N_SEG - 2) * d + GLA_GATE_RANK
    return w_ref[:, start:start + 2 * d]


def _w_seg(w_ref, wm_ref, seg, d):
    if seg < N_SEG - 2:
        return w_ref[:, seg * d:(seg + 1) * d]
    return wm_ref[:, (seg - (N_SEG - 2)) * d:(seg - (N_SEG - 3)) * d]


def _proj_block(d, x3, sh, sc, g, w_ref, wm_ref, wl_ref, proj_s, glr_s):
    n = _rms_mod(x3, g, sc, sh).reshape(-1, d).astype(BF16)
    for seg in range(N_SEG):
        proj_s[:, seg * d:(seg + 1) * d] = _mm(n, _w_seg(w_ref, wm_ref, seg, d)).astype(BF16)
    glr_s[...] = _mm(n, wl_ref[...])


def _mix_block(d, tb, proj_s, glr_s, cos_f, sin_f, dmask_ref, qdec_ref, kdec_ref, cdec_ref, tri, wgk, bgk, gnorm,
               sr_s, sg_s, oret_ref, ogla_ref, mg_ref, r_off):
    dk, dv, hq = d // 8, d // 4, d // 2
    rqk, rv, rg, gqk, gv, gg, mg = (i * d for i in range(7))
    for h in range(N_HEADS):
        o, (s_new,) = _ret_head(proj_s[:, rqk + h * dk:rqk + (h + 1) * dk].astype(F32),
                                proj_s[:, rqk + hq + h * dk:rqk + hq + (h + 1) * dk].astype(F32),
                                proj_s[:, rv + h * dv:rv + (h + 1) * dv],
                                proj_s[:, rg + h * dv:rg + (h + 1) * dv].astype(F32),
                                [sr_s[h]], None, cos_f, sin_f, dmask_ref[h], qdec_ref[h], kdec_ref[h],
                                cdec_ref[h])
        sr_s[h] = s_new
        oret_ref[r_off:r_off + tb, h * dv:(h + 1) * dv] = o.astype(BF16)

    la_hi, la_lo = _split_hi_lo(_log_a(glr_s[...], wgk, bgk))
    b = _mm(tri, la_hi) + _mm(tri, la_lo)
    cg = GLA_CHUNK
    n_c = tb // cg
    causal = _causal(cg)
    b_last = jnp.concatenate([jnp.broadcast_to(b[c * cg + cg - 1:c * cg + cg, :], (cg, hq)) for c in range(n_c)],
                             axis=0)
    gq = proj_s[:, gqk:gqk + hq].astype(F32)
    gk = proj_s[:, gqk + hq:gqk + 2 * hq].astype(F32)
    q_in = (gq * (dk ** -0.5) * jnp.exp(b)).astype(BF16)
    k_in = (gk * jnp.exp(-b)).astype(BF16)
    k_end = gk * jnp.exp(b_last - b)
    intra, incs, decs = {}, {}, {}
    for c in range(n_c):
        rows = slice(c * cg, (c + 1) * cg)
        for h in range(N_HEADS):
            cols = slice(h * dk, (h + 1) * dk)
            vh = proj_s[rows, gv + h * dv:gv + (h + 1) * dv]
            scores = jnp.where(causal, _mm_nt(q_in[rows, cols], k_in[rows, cols]), 0.0)
            intra[c, h] = _mm(scores.astype(BF16), vh)
            incs[c, h] = _mm(k_end[rows, cols].T.astype(BF16), vh)
            decs[c, h] = jnp.exp(b[rows, cols].T[:, cg - 1:cg])
    for h in range(N_HEADS):
        cols = slice(h * dk, (h + 1) * dk)
        s = sg_s[h]
        for c in range(n_c):
            rows = slice(c * cg, (c + 1) * cg)
            o = intra[c, h] + _mm(q_in[rows, cols], s.astype(BF16))
            s = decs[c, h] * s + incs[c, h]
            o = o * lax.rsqrt(jnp.mean(o * o, axis=-1, keepdims=True) + EPS) * gnorm
            gh = proj_s[rows, gg + h * dv:gg + (h + 1) * dv].astype(F32)
            ogla_ref[r_off + c * cg:r_off + (c + 1) * cg, h * dv:(h + 1) * dv] = (_silu(gh) * o).astype(BF16)
        sg_s[h] = s
    mg_ref[0, r_off:r_off + tb, :] = proj_s[:, mg:mg + d]
    mg_ref[1, r_off:r_off + tb, :] = proj_s[:, mg + d:mg + 2 * d]


ROUTE_SLABS = 4


def _rows2d(a3, tm, d):
    if a3.shape[0] == 1:
        return a3.reshape(1, d)
    return jnp.broadcast_to(a3, (a3.shape[0], tm // a3.shape[0], d)).reshape(tm, d)


def _route_block(d, out_ret, out_gla, mg_ret, mg_gla, x3, gt, sh, sc, g, wro_ref, wgo_ref, wo_ref, wrh_ref, wrl_ref,
                 br, utri_ref, eye_ref, carry_s, h_ref, n2_ref, idx_ref, rank_ref, prob_ref):
    rows = out_ret.shape[0]
    e = N_EXPERTS
    half = rows // ROUTE_SLABS
    slabs = [slice(s * half, (s + 1) * half) for s in range(ROUTE_SLABS)]
    x2 = x3.reshape(rows, d)
    gt2, sh2, sc2 = (_rows2d(v, rows, d) for v in (gt, sh, sc))
    g2 = g.reshape(1, d)

    def rows_of(v, sl):
        return v if v.shape[0] == 1 else v[sl]

    ab = [(_mm(out_ret[sl], wro_ref[...]), _mm(out_gla[sl], wgo_ref[...])) for sl in slabs]
    mix = [_mm((jax.nn.sigmoid(mg_ret[sl].astype(F32)) * a + jax.nn.sigmoid(mg_gla[sl].astype(F32)) * b).astype(BF16),
               wo_ref[...]) for sl, (a, b) in zip(slabs, ab)]
    logits = []
    for sl, m in zip(slabs, mix):
        hs = x2[sl] + rows_of(gt2, sl) * m
        h_ref[sl, :] = hs
        ms = jnp.mean(hs * hs, axis=-1, keepdims=True)
        n2 = hs * lax.rsqrt(ms + EPS) * g2 * (1.0 + rows_of(sc2, sl)) + rows_of(sh2, sl)
        n2_ref[sl, :] = _pack_pair(n2)
        n_hi, n_lo = _split_hi_lo(n2)
        logits.append(_mm_nt(wrh_ref[...], n_hi) + _mm_nt(wrh_ref[...], n_lo) + _mm_nt(wrl_ref[...], n_hi) + br)

    iota_e = lax.broadcasted_iota(jnp.int32, (e, half), 0)
    slot = lax.broadcasted_iota(jnp.int32, (TOP_K, half), 0)
    carry = carry_s[...]
    for sl, work in zip(slabs, logits):
        vals, idxs = [], []
        for _ in range(TOP_K):
            m = jnp.max(work, axis=0, keepdims=True)
            ik = jnp.min(jnp.where(work == m, iota_e, e), axis=0, keepdims=True)
            vals.append(m)
            idxs.append(ik)
            work = jnp.where(iota_e == ik, -jnp.inf, work)
        ex = [jnp.exp(v - vals[0]) for v in vals]
        den = ex[0] + ex[1] + ex[2] + ex[3]
        onehot = jnp.zeros((e, half), F32)
        for ik in idxs:
            onehot = onehot + (iota_e == ik).astype(F32)
        cum = _mm(onehot.astype(BF16), utri_ref[...]) + carry
        carry = carry + jnp.sum(onehot, axis=1, keepdims=True)
        idx_o = jnp.zeros((TOP_K, half), jnp.int32)
        rank_o = jnp.zeros((TOP_K, half), jnp.int32)
        prob_t = jnp.zeros((TOP_K, half), F32)
        for k in range(TOP_K):
            rk = jnp.sum(jnp.where(iota_e == idxs[k], cum, 0.0), axis=0, keepdims=True).astype(jnp.int32)
            idx_o = jnp.where(slot == k, idxs[k], idx_o)
            rank_o = jnp.where(slot == k, rk, rank_o)
            prob_t = jnp.where(slot == k, ex[k] / den, prob_t)
        idx_ref[:, sl] = idx_o
        rank_ref[:, sl] = rank_o
        p1 = prob_t.astype(BF16)
        r1 = prob_t - p1.astype(F32)
        p2 = r1.astype(BF16)
        p3 = (r1 - p2.astype(F32)).astype(BF16)
        pieces = jnp.concatenate([p1, p2, p3, jnp.zeros_like(p1)], axis=0)
        t = _mm_nt(eye_ref[...], pieces)
        prob_ref[sl, :] = t[:, 0:TOP_K] + t[:, TOP_K:2 * TOP_K] + t[:, 2 * TOP_K:3 * TOP_K]
    carry_s[...] = carry


def _frontp_kernel(d, tb, ntb, x0_ref, xa_ref, xb_ref, sh0_ref, sc0_ref, sha_ref, sca_ref, shb_ref, scb_ref,
                   g_ref, w_ref, cosa_ref, sina_ref, cosb_ref, sinb_ref,
                   dmask_ref, qdec_ref, kdec_ref, cdec_ref, tri_ref, wgk_ref, bgk_ref, gn_ref,
                   oret_ref, ogla_ref, mg_ref, sret_ref, sgla_ref, pa_s, pb_s, ga_s, gb_s, sr_s, sg_s, wm_ref):
    p = pl.program_id(0)
    blk = 2 * p
    g = g_ref[...]
    proj = functools.partial(_proj_block, d)
    mix = functools.partial(_mix_block, d, tb)
    tables = (dmask_ref, qdec_ref, kdec_ref, cdec_ref, tri_ref[...], wgk_ref[...], bgk_ref[...], gn_ref[...])

    wl_ref = _gate_rank_columns(w_ref, d)

    @pl.when(p == 0)
    def _():
        wm_ref[...] = _merge_gate_columns(w_ref, d)
        proj(x0_ref[...], sh0_ref[0], sc0_ref[0], g, w_ref, wm_ref, wl_ref, pa_s, ga_s)

    @pl.when(blk % ntb == 0)
    def _():
        sr_s[...] = jnp.zeros_like(sr_s)
        sg_s[...] = jnp.zeros_like(sg_s)

    proj(xa_ref[...], sha_ref[0], sca_ref[0], g, w_ref, wm_ref, wl_ref, pb_s, gb_s)
    mix(pa_s, ga_s, cosa_ref[...], sina_ref[...], *tables, sr_s, sg_s, oret_ref, ogla_ref, mg_ref, 0)
    proj(xb_ref[...], shb_ref[0], scb_ref[0], g, w_ref, wm_ref, wl_ref, pa_s, ga_s)
    mix(pb_s, gb_s, cosb_ref[...], sinb_ref[...], *tables, sr_s, sg_s, oret_ref, ogla_ref, mg_ref, tb)

    @pl.when((blk + 1) % ntb == ntb - 1)
    def _():
        sret_ref[0] = sr_s[...]
        sgla_ref[0] = sg_s[...]


def _chunk_tri(tb, cg):
    i = jnp.arange(tb)
    return ((i[:, None] >= i[None, :]) & (i[:, None] // cg == i[None, :] // cg)).astype(BF16)


def _front_prompt(x_p, ada_p, g_mix, w_all, tb, w_gk, b_gk, g_gla):
    b, t, d = x_p.shape
    dk, dv, hq, h = d // 8, d // 4, d // 2, N_HEADS
    ntb = t // tb
    n_blk = b * ntb
    n_tok = b * t
    assert ntb % 2 == 0
    cos_f, sin_f = _rope_tables(0, t, dk)
    dmask, qdec, kdec, cdec = _ret_tables(tb, dk, dv)
    tri = _chunk_tri(tb, GLA_CHUNK)

    def first(p):
        return 0 * p

    def even(p):
        return 2 * p

    def odd(p):
        return 2 * p + 1

    def nxt(p):
        return jnp.minimum(2 * p + 2, n_blk - 1)

    def x_spec(blk_of):
        return pl.BlockSpec((1, tb, d), lambda p: (blk_of(p) // ntb, blk_of(p) % ntb, 0))

    def ada_spec(which, blk_of):
        return pl.BlockSpec((1, 1, 1, d), lambda p: (which, blk_of(p) // ntb, 0, 0))

    def rope_spec(blk_of):
        return pl.BlockSpec((tb, dk), lambda p: (blk_of(p) % ntb, 0))

    state_spec = pl.BlockSpec((1, h, dk, dv), lambda p: ((2 * p) // ntb, 0, 0, 0))
    tok_spec = pl.BlockSpec((2 * tb, d), lambda p: (p, 0))
    return pl.pallas_call(
        functools.partial(_frontp_kernel, d, tb, ntb),
        grid=(n_blk // 2,),
        in_specs=[_resident((1, tb, d)), x_spec(odd), x_spec(nxt),
                  ada_spec(0, first), ada_spec(1, first), ada_spec(0, odd), ada_spec(1, odd),
                  ada_spec(0, nxt), ada_spec(1, nxt),
                  _resident((1, 1, d)), _resident(w_all.shape),
                  rope_spec(even), rope_spec(even), rope_spec(odd), rope_spec(odd),
                  _resident((h, tb, tb)), _resident((h, tb, dk)), _resident((h, tb, dk)), _resident((h, 1, dv)),
                  _resident((tb, tb)), _resident((GLA_GATE_RANK, hq)), _resident((1, hq)), _resident((1, dv))],
        out_specs=[tok_spec, tok_spec, pl.BlockSpec((2, 2 * tb, d), lambda p: (0, p, 0)), state_spec, state_spec],
        out_shape=[jax.ShapeDtypeStruct((n_tok, d), BF16), jax.ShapeDtypeStruct((n_tok, d), BF16),
                   jax.ShapeDtypeStruct((2, n_tok, d), BF16),
                   jax.ShapeDtypeStruct((b, h, dk, dv), F32), jax.ShapeDtypeStruct((b, h, dk, dv), F32)],
        scratch_shapes=[pltpu.VMEM((tb, N_SEG * d), BF16), pltpu.VMEM((tb, N_SEG * d), BF16),
                        pltpu.VMEM((tb, GLA_GATE_RANK), F32), pltpu.VMEM((tb, GLA_GATE_RANK), F32),
                        pltpu.VMEM((h, dk, dv), F32), pltpu.VMEM((h, dk, dv), F32), pltpu.VMEM((d, 2 * d), BF16)],
        compiler_params=pltpu.CompilerParams(dimension_semantics=("arbitrary",), vmem_limit_bytes=VMEM_LIMIT),
        name="front_prompt",
    )(x_p, x_p, x_p, ada_p, ada_p, ada_p, ada_p, ada_p, ada_p, g_mix.reshape(1, 1, d), w_all,
      cos_f, sin_f, cos_f, sin_f, dmask, qdec, kdec, cdec, tri, w_gk, b_gk, g_gla)


def _inproj_kernel(d, xs_ref, shs_ref, scs_ref, g_ref, w_ref, proj_ref, glr_ref, wm_ref):
    @pl.when(pl.program_id(0) == 0)
    def _():
        wm_ref[...] = _merge_gate_columns(w_ref, d)

    n = _rms_mod(xs_ref[...], g_ref[...], scs_ref[0], shs_ref[0]).reshape(-1, d).astype(BF16)
    for s in range(N_SEG):
        proj_ref[s] = _mm(n, _w_seg(w_ref, wm_ref, s, d)).astype(BF16)
    glr_ref[...] = _mm(n, _gate_rank_columns(w_ref, d)[...])


def _inproj_sample(tl, x_s, ada_s, g_mix, w_all):
    bs, ts, d = x_s.shape
    n_tok = bs * ts
    return pl.pallas_call(
        functools.partial(_inproj_kernel, d),
        grid=(tl.n_st,),
        in_specs=[tl.s_x_spec(d), tl.s_ada_spec(0, d), tl.s_ada_spec(1, d),
                  _resident((1, 1, d)), _resident(w_all.shape)],
        out_specs=[pl.BlockSpec((N_SEG, tl.tm, d), lambda i: (0, i, 0)), tl.s_row_spec(GLA_GATE_RANK)],
        out_shape=[jax.ShapeDtypeStruct((N_SEG, n_tok, d), BF16), jax.ShapeDtypeStruct((n_tok, GLA_GATE_RANK), F32)],
        scratch_shapes=[pltpu.VMEM((d, 2 * d), BF16)],
        compiler_params=pltpu.CompilerParams(dimension_semantics=("arbitrary",), vmem_limit_bytes=VMEM_LIMIT),
        name="inproj_sample",
    )(x_s, ada_s, ada_s, g_mix.reshape(1, 1, d), w_all)


def _mixs_kernel(d, ts, gsz, rqk_ref, rv_ref, rg_ref, gqk_ref, gv_ref, gg_ref, glr_ref, cos_ref, sin_ref,
                 dmask_ref, qdec_ref, kdec_ref, cdec_ref, wgk_ref, bgk_ref, gn_ref, sr_in, sg_in,
                 oret_ref, ogla_ref, sr_out, sg_out):
    dk, dv, hq = d // 8, d // 4, d // 2
    pair = 2 * ts
    cos_f, sin_f = cos_ref[...], sin_ref[...]
    gnorm = gn_ref[...]
    ri = lax.broadcasted_iota(jnp.int32, (pair, pair), 0)
    ci = lax.broadcasted_iota(jnp.int32, (pair, pair), 1)
    causal = jnp.logical_and(ri >= ci, (ri < ts) == (ci < ts))
    tri = causal.astype(F32).astype(BF16)
    first = lax.broadcasted_iota(jnp.int32, (pair, 1), 0) < ts
    masks = [first, jnp.logical_not(first)]

    def body(j, carry):
        rows = pl.ds(pl.multiple_of(j * pair, pair), pair)
        s0, s1 = 2 * j, 2 * j + 1
        la_hi, la_lo = _split_hi_lo(_log_a(glr_ref[rows, :], wgk_ref[...], bgk_ref[...]))
        b = _mm(tri, la_hi) + _mm(tri, la_lo)
        for h in range(N_HEADS):
            o, (n0, n1) = _ret_head(rqk_ref[0, rows, h * dk:(h + 1) * dk].astype(F32),
                                    rqk_ref[0, rows, hq + h * dk:hq + (h + 1) * dk].astype(F32),
                                    rv_ref[0, rows, h * dv:(h + 1) * dv],
                                    rg_ref[0, rows, h * dv:(h + 1) * dv].astype(F32),
                                    [sr_in[s0, h], sr_in[s1, h]], masks, cos_f, sin_f,
                                    dmask_ref[h], qdec_ref[h], kdec_ref[h], cdec_ref[h])
            sr_out[s0, h] = n0
            sr_out[s1, h] = n1
            oret_ref[rows, h * dv:(h + 1) * dv] = o.astype(BF16)
            o, (n0, n1) = _gla_head(gqk_ref[0, rows, h * dk:(h + 1) * dk].astype(F32),
                                    gqk_ref[0, rows, hq + h * dk:hq + (h + 1) * dk].astype(F32),
                                    gv_ref[0, rows, h * dv:(h + 1) * dv],
                                    gg_ref[0, rows, h * dv:(h + 1) * dv].astype(F32),
                                    b[:, h * dk:(h + 1) * dk], [sg_in[s0, h], sg_in[s1, h]], masks, ts,
                                    gnorm, causal)
            sg_out[s0, h] = n0
            sg_out[s1, h] = n1
            ogla_ref[rows, h * dv:(h + 1) * dv] = o.astype(BF16)
        return carry

    lax.fori_loop(0, gsz // 2, body, 0, unroll=2)


def _pair_tables(ts, dk, dv):
    cos_f, sin_f = _rope_tables(PAST_LEN, ts, dk)
    dmask, qdec, kdec, cdec = _ret_tables(ts, dk, dv)
    zero = jnp.zeros_like(dmask)
    dmask2 = jnp.concatenate([jnp.concatenate([dmask, zero], axis=2), jnp.concatenate([zero, dmask], axis=2)], axis=1)

    def twice(a, axis):
        return jnp.concatenate([a, a], axis=axis)

    return twice(cos_f, 0), twice(sin_f, 0), dmask2, twice(qdec, 1), twice(kdec, 1), cdec


def _mix_sample(bs, ts, d, gsz, proj, glr, state_ret, state_gla, w_gk, b_gk, g_gla):
    dk, dv, hq, h = d // 8, d // 4, d // 2, N_HEADS
    assert GLA_CHUNK % ts == 0 and bs % gsz == 0 and gsz % 4 == 0
    rows = gsz * ts
    pair = 2 * ts
    cos_f, sin_f, dmask, qdec, kdec, cdec = _pair_tables(ts, dk, dv)

    def seg(s):
        return pl.BlockSpec((1, rows, d), lambda i: (s, i, 0))

    state_spec = pl.BlockSpec((gsz, h, dk, dv), lambda i: (i, 0, 0, 0))
    tok_spec = pl.BlockSpec((rows, d), lambda i: (i, 0))
    return pl.pallas_call(
        functools.partial(_mixs_kernel, d, ts, gsz),
        grid=(bs // gsz,),
        in_specs=[seg(0), seg(1), seg(2), seg(3), seg(4), seg(5),
                  pl.BlockSpec((rows, GLA_GATE_RANK), lambda i: (i, 0)),
                  _const((pair, dk)), _const((pair, dk)),
                  _const((h, pair, pair)), _const((h, pair, dk)), _const((h, pair, dk)), _const((h, 1, dv)),
                  _const((GLA_GATE_RANK, hq)), _const((1, hq)), _const((1, dv)),
                  state_spec, state_spec],
        out_specs=[tok_spec, tok_spec, state_spec, state_spec],
        out_shape=[jax.ShapeDtypeStruct((bs * ts, d), BF16), jax.ShapeDtypeStruct((bs * ts, d), BF16),
                   jax.ShapeDtypeStruct((bs, h, dk, dv), F32), jax.ShapeDtypeStruct((bs, h, dk, dv), F32)],
        compiler_params=pltpu.CompilerParams(dimension_semantics=("arbitrary",), vmem_limit_bytes=VMEM_LIMIT),
        name="mix_sample",
    )(proj, proj, proj, proj, proj, proj, glr, cos_f, sin_f, dmask, qdec, kdec, cdec, w_gk, b_gk, g_gla,
      state_ret, state_gla)


def _outproj_kernel(n_pt, d, orp_ref, ogp_ref, ors_ref, ogs_ref, mgrp_ref, mggp_ref, mgrs_ref, mggs_ref,
                    xp_ref, xs_ref, gtp_ref, shp_ref, scp_ref, gts_ref, shs_ref, scs_ref, g_ref,
                    wro_ref, wgo_ref, wo_ref, wrh_ref, wrl_ref, br_ref, utri_ref, eye_ref,
                    h_ref, n2_ref, idx_ref, rank_ref, prob_ref, cnt_ref, carry_s):
    i = pl.program_id(0)

    @pl.when(i == 0)
    def _():
        carry_s[...] = jnp.zeros_like(carry_s)

    route = functools.partial(_route_block, d)
    tail = (g_ref[...], wro_ref, wgo_ref, wo_ref, wrh_ref, wrl_ref, br_ref[...], utri_ref, eye_ref, carry_s,
            h_ref, n2_ref, idx_ref, rank_ref, prob_ref)

    @pl.when(i < n_pt)
    def _():
        route(orp_ref[...], ogp_ref[...], mgrp_ref[0], mggp_ref[0], xp_ref[...], gtp_ref[0], shp_ref[0], scp_ref[0],
              *tail)

    @pl.when(i >= n_pt)
    def _():
        route(ors_ref[...], ogs_ref[...], mgrs_ref[0], mggs_ref[0], xs_ref[...], gts_ref[0], shs_ref[0], scs_ref[0],
              *tail)

    @pl.when(i == pl.num_programs(0) - 1)
    def _():
        cnt_ref[...] = carry_s[...].astype(jnp.int32)


def _outproj(tl, oret_p, ogla_p, oret_s, ogla_s, mg_p, proj_s, x_p, x_s, ada_p, ada_s, g_ffn,
             w_ret_o, w_gla_o, w_out, w_r_hi, w_r_lo, b_router):
    d = x_p.shape[-1]
    tm, e, n_pt = tl.tm, N_EXPERTS, tl.n_pt
    last = n_pt - 1
    p_spec = pl.BlockSpec((tm, d), lambda i: (jnp.minimum(i, last), 0))
    s_spec = pl.BlockSpec((tm, d), lambda i: (jnp.maximum(i - n_pt, 0), 0))

    def mgp_spec(seg):
        return pl.BlockSpec((1, tm, d), lambda i: (seg, jnp.minimum(i, last), 0))

    def mgs_spec(seg):
        return pl.BlockSpec((1, tm, d), lambda i: (seg, jnp.maximum(i - n_pt, 0), 0))

    slot_spec = pl.BlockSpec((TOP_K, tm), lambda i: (0, i))
    half = tm // ROUTE_SLABS
    token = jnp.arange(half)
    utri = (token[:, None] < token[None, :]).astype(BF16)
    eye = jnp.eye(half, dtype=BF16)

    return pl.pallas_call(
        functools.partial(_outproj_kernel, n_pt, d),
        grid=(tl.n,),
        in_specs=[p_spec, p_spec, s_spec, s_spec, mgp_spec(0), mgp_spec(1), mgs_spec(6), mgs_spec(7),
                  tl.xp_spec(d), tl.xs_spec(d),
                  tl.adap_spec(2, d), tl.adap_spec(3, d), tl.adap_spec(4, d),
                  tl.adas_spec(2, d), tl.adas_spec(3, d), tl.adas_spec(4, d),
                  _resident((1, 1, d)), _resident((d, d)), _resident((d, d)), _resident((d, d)),
                  _resident((e, d)), _resident((e, d)), _resident((e, 1)),
                  _resident((half, half)), _resident((half, half))],
        out_specs=[tl.tok_spec(d), tl.tok_spec(d // 2), slot_spec, slot_spec, tl.tok_spec(TOP_K),
                   pl.BlockSpec((e, 1), lambda i: (0, 0))],
        out_shape=[jax.ShapeDtypeStruct((tl.n_tok, d), F32), jax.ShapeDtypeStruct((tl.n_tok, d // 2), jnp.uint32),
                   jax.ShapeDtypeStruct((TOP_K, tl.n_tok), jnp.int32),
                   jax.ShapeDtypeStruct((TOP_K, tl.n_tok), jnp.int32),
                   jax.ShapeDtypeStruct((tl.n_tok, TOP_K), F32),
                   jax.ShapeDtypeStruct((e, 1), jnp.int32)],
        scratch_shapes=[pltpu.VMEM((e, 1), F32)],
        compiler_params=pltpu.CompilerParams(dimension_semantics=("arbitrary",), vmem_limit_bytes=VMEM_LIMIT),
        name="outproj",
    )(oret_p, ogla_p, oret_s, ogla_s, mg_p, mg_p, proj_s, proj_s, x_p, x_s, ada_p, ada_p, ada_p, ada_s, ada_s, ada_s,
      g_ffn.reshape(1, 1, d), w_ret_o, w_gla_o, w_out, w_r_hi, w_r_lo, b_router.reshape(e, 1), utri, eye)


EXPERT_TILES_PER_STEP = 2
EXPERT_TAIL_ROWS = 128


def _expert_kernel(f, tme, te_ref, na_ref, grp_ref, nxt_ref, tr_ref, x_ref, wu_hbm, wd_hbm, *rest):
    n_t = EXPERT_TILES_PER_STEP
    bu_refs, bd_refs = rest[:n_t], rest[n_t:2 * n_t]
    y_ref, wu_f, wd_f, wu_s, wd_s, sem = rest[2 * n_t:]
    step = pl.program_id(0)

    def fetch(expert, s):
        return (pltpu.make_async_copy(wu_hbm.at[expert], wu_f.at[s], sem.at[0, s]),
                pltpu.make_async_copy(wd_hbm.at[expert], wd_f.at[s], sem.at[1, s]))

    @pl.when(step == 0)
    def _():
        for c in fetch(te_ref[0], 0):
            c.start()

    slab = tme // EXPERT_ROW_SLABS
    half = x_ref.shape[1]
    for t in range(n_t):
        j = step * n_t + t
        active = j < na_ref[0]
        first = jnp.logical_or(j == 0, te_ref[j] != te_ref[jnp.maximum(j - 1, 0)])
        slot = grp_ref[j] % 2

        @pl.when(jnp.logical_and(active, first))
        def _(j=j, slot=slot):
            for c in fetch(te_ref[j], slot):
                c.wait()

            @pl.when(nxt_ref[j] >= 0)
            def _():
                for c in fetch(nxt_ref[j], 1 - slot):
                    c.start()

            wu_s[...] = wu_f[slot].astype(BF16)
            wd_s[...] = wd_f[slot].astype(BF16)

        def compute(rows, t=t):
            x_lo, x_hi = _unpack_pair(x_ref[rows, :])
            gu = _mm(x_lo.astype(BF16), wu_s[:half, :]) + _mm(x_hi.astype(BF16), wu_s[half:, :]) + bu_refs[t][0]
            gate = jnp.minimum(gu[:, :f], SWIGLU_LIMIT)
            up = jnp.clip(gu[:, f:], -SWIGLU_LIMIT, SWIGLU_LIMIT)
            act = (up + 1.0) * gate * jax.nn.sigmoid(SWIGLU_ALPHA * gate)
            y_ref[rows, :] = _pack_pair(_mm(act.astype(BF16), wd_s[...]) + bd_refs[t][0])

        full = tr_ref[j] == tme

        @pl.when(jnp.logical_and(active, full))
        def _(t=t, compute=compute):
            for s in range(EXPERT_ROW_SLABS):
                compute(slice(t * tme + s * slab, t * tme + (s + 1) * slab))

        @pl.when(jnp.logical_and(active, jnp.logical_not(full)))
        def _(t=t, j=j, compute=compute):
            def piece(s, carry):
                compute(pl.ds(pl.multiple_of(t * tme + s * EXPERT_TAIL_ROWS, EXPERT_TAIL_ROWS), EXPERT_TAIL_ROWS))
                return carry

            lax.fori_loop(0, (tr_ref[j] + EXPERT_TAIL_ROWS - 1) // EXPERT_TAIL_ROWS, piece, 0)


def _experts(xs, tile_expert, n_active, tile_group, next_expert, tile_rows, w_up, b_up, w_down, b_down, tme):
    r = xs.shape[0]
    e, d, f2 = w_up.shape
    f = f2 // 2
    n_t = EXPERT_TILES_PER_STEP
    n_tiles = r // tme
    assert n_tiles % n_t == 0 and tme % EXPERT_TAIL_ROWS == 0

    def row_map(s, te, na, grp, nxt, tr):
        return (jnp.minimum(s, (na[0] - 1) // n_t), 0)

    def bias_map(t):
        return lambda s, te, na, grp, nxt, tr: (te[jnp.minimum(s * n_t + t, na[0] - 1)], 0, 0)

    hbm = pl.BlockSpec(memory_space=pl.ANY)
    return pl.pallas_call(
        functools.partial(_expert_kernel, f, tme),
        grid_spec=pltpu.PrefetchScalarGridSpec(
            num_scalar_prefetch=5,
            grid=(n_tiles // n_t,),
            in_specs=[pl.BlockSpec((n_t * tme, d // 2), row_map), hbm, hbm]
                     + [pl.BlockSpec((1, 1, f2), bias_map(t)) for t in range(n_t)]
                     + [pl.BlockSpec((1, 1, d), bias_map(t)) for t in range(n_t)],
            out_specs=pl.BlockSpec((n_t * tme, d // 2), row_map),
            scratch_shapes=[pltpu.VMEM((2, d, f2), F32), pltpu.VMEM((2, f, d), F32),
                            pltpu.VMEM((d, f2), BF16), pltpu.VMEM((f, d), BF16),
                            pltpu.SemaphoreType.DMA((2, 2))]),
        out_shape=jax.ShapeDtypeStruct((r, d // 2), jnp.uint32),
        compiler_params=pltpu.CompilerParams(dimension_semantics=("arbitrary",), vmem_limit_bytes=VMEM_LIMIT),
        name="experts",
    )(tile_expert, n_active, tile_group, next_expert, tile_rows, xs, w_up, w_down,
      *([b_up.reshape(e, 1, f2)] * n_t), *([b_down.reshape(e, 1, d)] * n_t))


def _sc_mesh():
    return plsc.VectorSubcoreMesh(core_axis_name="core", subcore_axis_name="subcore")


def _sc_split(n_rows, max_chunk):
    info = plsc.get_sparse_core_info()
    n_workers = info.num_cores * info.num_subcores
    assert n_rows % (8 * n_workers) == 0
    per_w = n_rows // n_workers
    chunk = 8
    while chunk * 2 <= max_chunk and per_w % (chunk * 2) == 0:
        chunk *= 2
    return info.num_cores, n_workers, per_w, chunk


def _sc_dispatch(x, pos_t, n_rows):
    n, w = x.shape
    nc, nw, per_w, chunk = _sc_split(n, 32)
    n_ch = per_w // chunk
    idx = pos_t.reshape(TOP_K, nw, n_ch, chunk).transpose(1, 0, 2, 3).reshape(nw, TOP_K * n_ch, chunk)

    @functools.partial(
        pl.kernel, out_type=jax.ShapeDtypeStruct((n_rows, w), x.dtype), mesh=_sc_mesh(),
        scratch_types=[pltpu.VMEM((TOP_K * n_ch, chunk), jnp.int32), pltpu.VMEM((2, chunk, w), x.dtype),
                       pltpu.SemaphoreType.DMA((2,)), pltpu.SemaphoreType.DMA((2,))])
    def scatter_rows(x_hbm, i_hbm, o_hbm, idx_v, rows_v, rsem, wsem):
        wid = lax.axis_index("subcore") * nc + lax.axis_index("core")
        base = wid * per_w
        pltpu.sync_copy(i_hbm.at[wid], idx_v)

        def read(j, slot):
            return pltpu.make_async_copy(x_hbm.at[pl.ds(base + j * chunk, chunk)], rows_v.at[slot], rsem.at[slot])

        def write(j, slot, k):
            return pltpu.make_async_copy(rows_v.at[slot], o_hbm.at[idx_v.at[k * n_ch + j]], wsem.at[slot])

        read(0, 0).start()

        @pl.loop(0, n_ch, step=2)
        def _(j0):
            for b in range(2):
                j = j0 + b

                @pl.when(j < n_ch)
                def _():
                    read(j, b).wait()

                    @pl.when(j + 1 < n_ch)
                    def _():
                        @pl.when(j >= 1)
                        def _():
                            for k in range(TOP_K):
                                write(j - 1, 1 - b, k).wait()

                        read(j + 1, 1 - b).start()

                    for k in range(TOP_K):
                        write(j, b, k).start()

        for jj in range(max(n_ch - 2, 0), n_ch):
            for k in range(TOP_K):
                write(jj, jj % 2, k).wait()

    return scatter_rows(x, idx)


def _sc_gather(table, idx):
    m = idx.shape[0]
    w = table.shape[1]
    nc, _, per_w, chunk = _sc_split(m, 64)
    n_ch = per_w // chunk

    @functools.partial(
        pl.kernel, out_type=jax.ShapeDtypeStruct((m, w), table.dtype), mesh=_sc_mesh(),
        scratch_types=[pltpu.VMEM((per_w,), jnp.int32), pltpu.VMEM((2, chunk, w), table.dtype),
                       pltpu.SemaphoreType.DMA((2,)), pltpu.SemaphoreType.DMA((2,))])
    def gather_rows(t_hbm, i_hbm, o_hbm, idx_v, rows_v, gsem, wsem):
        wid = lax.axis_index("subcore") * nc + lax.axis_index("core")
        base = wid * per_w
        pltpu.sync_copy(i_hbm.at[pl.ds(base, per_w)], idx_v)

        def gather(j, slot):
            off = pl.multiple_of(j * chunk, chunk)
            return pltpu.make_async_copy(t_hbm.at[idx_v.at[pl.ds(off, chunk)]], rows_v.at[slot], gsem.at[slot])

        def write(j, slot):
            off = pl.multiple_of(j * chunk, chunk)
            return pltpu.make_async_copy(rows_v.at[slot], o_hbm.at[pl.ds(base + off, chunk)], wsem.at[slot])

        gather(0, 0).start()

        @pl.loop(0, n_ch, step=2)
        def _(j0):
            for b in range(2):
                j = j0 + b

                @pl.when(j < n_ch)
                def _():
                    gather(j, b).wait()

                    @pl.when(j + 1 < n_ch)
                    def _():
                        @pl.when(j >= 1)
                        def _():
                            write(j - 1, 1 - b).wait()

                        gather(j + 1, 1 - b).start()

                    write(j, b).start()

        for jj in range(max(n_ch - 2, 0), n_ch):
            write(jj, jj % 2).wait()

    return gather_rows(table, idx)


def _final_kernel(n_pt, d, h_ref, yg_ref, prob_ref, gtp_ref, gts_ref, g_ref, yp_ref, ys_ref):
    i = pl.program_id(0)
    p = prob_ref[...]
    moe_lo, moe_hi = None, None
    for k in range(TOP_K):
        lo, hi = _unpack_pair(yg_ref[k])
        pk = p[:, k:k + 1]
        moe_lo = pk * lo if moe_lo is None else moe_lo + pk * lo
        moe_hi = pk * hi if moe_hi is None else moe_hi + pk * hi
    moe = jnp.concatenate([moe_lo, moe_hi], axis=1)

    def body(gt, shape):
        h3 = h_ref[...].reshape(shape) + gt * moe.reshape(shape)
        ms = jnp.mean(h3 * h3, axis=-1, keepdims=True)
        return h3 * lax.rsqrt(ms + EPS) * g_ref[...]

    @pl.when(i < n_pt)
    def _():
        yp_ref[...] = body(gtp_ref[0], yp_ref.shape)

    @pl.when(i >= n_pt)
    def _():
        ys_ref[...] = body(gts_ref[0], ys_ref.shape)


def _final(tl, h, yg, probs, ada_p, ada_s, g_final, d):
    return pl.pallas_call(
        functools.partial(_final_kernel, tl.n_pt, d),
        grid=(tl.n,),
        in_specs=[tl.tok_spec(d), pl.BlockSpec((TOP_K, tl.tm, d // 2), lambda i: (0, i, 0)), tl.tok_spec(TOP_K),
                  tl.adap_spec(5, d), tl.adas_spec(5, d), _resident((1, 1, d))],
        out_specs=[tl.xp_spec(d), tl.xs_spec(d)],
        out_shape=[jax.ShapeDtypeStruct((tl.b, tl.t, d), F32), jax.ShapeDtypeStruct((tl.bs, tl.ts, d), F32)],
        compiler_params=pltpu.CompilerParams(dimension_semantics=("arbitrary",), vmem_limit_bytes=VMEM_LIMIT),
        name="final",
    )(h, yg, probs, ada_p, ada_s, g_final.reshape(1, 1, d))


def _pick(n, pref):
    t = min(n, pref)
    while n % t:
        t //= 2
    return t


def _forward(x_prompt, x_sample, c_prompt, c_sample, state_ret, state_gla, w_ada, b_ada, g_norm_mix, g_norm_ffn,
             w_in, w_gk_up, b_gk, g_gla_norm, w_ret_o, w_gla_o, w_out, w_router, b_router, w_up, b_up,
             w_down, b_down, g_final, *, tm, tb, gsz, tme):
    b, t, d = x_prompt.shape
    bs, ts, _ = x_sample.shape
    assert w_ada.shape[0] == 1, "single layer only"
    assert (b * t) % (2 * tb) == 0 and (b * t) % tm == 0
    e = N_EXPERTS
    tl = _Tiles(b, t, bs, ts, tm)
    n_tok = tl.n_tok

    ada = _ada(jnp.concatenate([c_prompt, c_sample], axis=0), w_ada[0], b_ada[0])
    ada_p = ada[:, :b].reshape(6, b, 1, d)
    ada_s = ada[:, b:].reshape(6, bs, 1, d)

    w_all = w_in[0].astype(BF16)
    w_gk = w_gk_up[0].astype(BF16)
    bgk = b_gk[0].reshape(1, -1)
    ggn = g_gla_norm[0].reshape(1, -1)
    w_r = w_router[0].T
    w_r_hi = w_r.astype(BF16)
    w_r_lo = (w_r - w_r_hi.astype(F32)).astype(BF16)
    route_w = (g_norm_ffn[0], w_ret_o[0].astype(BF16), w_gla_o[0].astype(BF16), w_out[0].astype(BF16),
               w_r_hi, w_r_lo, b_router[0])

    oret_p, ogla_p, mg_p, sret_p, sgla_p = _front_prompt(x_prompt, ada_p, g_norm_mix[0], w_all, tb,
                                                         w_gk, bgk, ggn)
    proj_s, glr_s = _inproj_sample(tl, x_sample, ada_s, g_norm_mix[0], w_all)
    oret_s, ogla_s, sret_s, sgla_s = _mix_sample(bs, ts, d, gsz, proj_s, glr_s, state_ret[0], state_gla[0],
                                                 w_gk, bgk, ggn)
    h, n2, idx_t, rank_t, probs, counts = _outproj(tl, oret_p, ogla_p, oret_s, ogla_s, mg_p, proj_s, x_prompt, x_sample,
                                               ada_p, ada_s, *route_w)

    counts = counts[:, 0]
    gsize = ((counts + tme - 1) // tme) * tme
    ends = jnp.cumsum(gsize)
    offs = ends - gsize
    experts = jnp.arange(e, dtype=jnp.int32)
    pos_t = jnp.sum(jnp.where(idx_t[..., None] == experts, offs, 0), axis=-1) + rank_t
    max_tiles = (n_tok * TOP_K) // tme + e
    n_active = (ends[-1] // tme).astype(jnp.int32).reshape(1)
    tile_start = jnp.arange(max_tiles, dtype=jnp.int32) * tme
    tile_expert = jnp.minimum(jnp.sum((ends[None, :] <= tile_start[:, None]).astype(jnp.int32), axis=1), e - 1)
    is_first = jnp.logical_and(tile_start < ends[-1],
                               jnp.concatenate([jnp.ones((1,), bool), tile_expert[1:] != tile_expert[:-1]]))
    tile_group = jnp.cumsum(is_first.astype(jnp.int32)) - 1
    later = jnp.logical_and(experts[None, :] > experts[:, None], counts[None, :] > 0)
    next_of = jnp.min(jnp.where(later, experts[None, :], e), axis=1)
    next_of = jnp.where(next_of == e, -1, next_of)
    next_expert = jnp.sum(jnp.where(tile_expert[:, None] == experts, next_of, 0), axis=1).astype(jnp.int32)

    group_end = jnp.sum(jnp.where(tile_expert[:, None] == experts, offs + counts, 0), axis=1)
    tile_rows = jnp.clip(group_end - tile_start, 0, tme).astype(jnp.int32)

    xs = _sc_dispatch(n2, pos_t, max_tiles * tme)
    ys = _experts(xs, tile_expert, n_active, tile_group, next_expert, tile_rows, w_up[0], b_up[0], w_down[0], b_down[0], tme)
    yg = _sc_gather(ys, pos_t.reshape(-1)).reshape(TOP_K, n_tok, d // 2)

    y_p, y_s = _final(tl, h, yg, probs, ada_p, ada_s, g_final, d)
    return (y_p, y_s, sret_p[None], sgla_p[None], sret_s[None], sgla_s[None])


def kernel(x_prompt, x_sample, c_prompt, c_sample, state_ret, state_gla, w_ada, b_ada, g_norm_mix, g_norm_ffn,
           w_in, w_gk_up, b_gk, g_gla_norm, w_ret_o, w_gla_o, w_out, w_router, b_router, w_up, b_up,
           w_down, b_down, g_final):
    t = x_prompt.shape[1]
    bs, ts = x_sample.shape[0], x_sample.shape[1]
    return _forward(x_prompt, x_sample, c_prompt, c_sample, state_ret, state_gla, w_ada, b_ada, g_norm_mix,
                    g_norm_ffn, w_in, w_gk_up, b_gk, g_gla_norm, w_ret_o, w_gla_o, w_out, w_router, b_router,
                    w_up, b_up, w_down, b_down, g_final,
                    tm=_pick(bs * ts, 512), tb=_pick(t, 256), gsz=_pick(bs, 8), tme=512)
```

```python
import functools

import jax
import jax.numpy as jnp
from jax import lax
from jax.experimental import pallas as pl
from jax.experimental.pallas import tpu as pltpu
from jax.experimental.pallas import tpu_sc as plsc

F32 = jnp.float32
BF16 = jnp.bfloat16

N_HEADS = 4
GLA_GATE_RANK = 16
GLA_GATE_NORM = 16.0
GLA_CHUNK = 64
ROPE_BASE = 10000.0
N_EXPERTS = 32
TOP_K = 4
SWIGLU_LIMIT = 7.0
SWIGLU_ALPHA = 1.702
EPS = 1e-6
PAST_LEN = 16384
SUBLANES = 8
N_SEG = 8
EXPERT_ROW_SLABS = 2

VMEM_LIMIT = 56 * 1024 * 1024


def _mm(a, b):
    return jnp.dot(a, b, preferred_element_type=F32)


def _mm_nt(a, b):
    return lax.dot_general(a, b, (((1,), (1,)), ((), ())), preferred_element_type=F32)


def _silu(x):
    return x * jax.nn.sigmoid(x)


def _split_hi_lo(x):
    hi = x.astype(BF16)
    lo = (x - hi.astype(F32)).astype(BF16)
    return hi, lo


def _pack_pair(x):
    w = x.shape[1] // 2
    lo = lax.bitcast_convert_type(x[:, :w].astype(BF16).astype(F32), jnp.uint32)
    hi = lax.bitcast_convert_type(x[:, w:].astype(BF16).astype(F32), jnp.uint32)
    return (hi & jnp.uint32(0xFFFF0000)) | (lo >> 16)


def _unpack_pair(p):
    lo = lax.bitcast_convert_type(p << 16, F32)
    hi = lax.bitcast_convert_type(p & jnp.uint32(0xFFFF0000), F32)
    return lo, hi


def _rms_mod(x3, g, sc, sh):
    ms = jnp.mean(x3 * x3, axis=-1, keepdims=True)
    return x3 * lax.rsqrt(ms + EPS) * g * (1.0 + sc) + sh


def _resident(shape):
    zeros = (0,) * len(shape)
    return pl.BlockSpec(shape, lambda i: zeros, pipeline_mode=pl.Buffered(1))


def _const(shape):
    zeros = (0,) * len(shape)
    return pl.BlockSpec(shape, lambda i: zeros)


def _ada_kernel(c_ref, w_ref, b_ref, o_ref):
    cf = _silu(c_ref[...])
    o_ref[0] = _mm(cf.astype(BF16), w_ref[...].astype(BF16)) + b_ref[0]


def _ada(c_all, w_ada, b_ada):
    bc, d = c_all.shape
    n = w_ada.shape[1] // d
    return pl.pallas_call(
        _ada_kernel,
        grid=(n,),
        in_specs=[pl.BlockSpec((bc, d), lambda j: (0, 0)),
                  pl.BlockSpec((d, d), lambda j: (0, j)),
                  pl.BlockSpec((1, 1, d), lambda j: (j, 0, 0))],
        out_specs=pl.BlockSpec((1, bc, d), lambda j: (j, 0, 0)),
        out_shape=jax.ShapeDtypeStruct((n, bc, d), F32),
        compiler_params=pltpu.CompilerParams(dimension_semantics=("arbitrary",), vmem_limit_bytes=VMEM_LIMIT),
        name="ada",
    )(c_all, w_ada, b_ada.reshape(n, 1, d))


class _Tiles:
    def __init__(self, b, t, bs, ts, tm):
        assert t % tm == 0 and (bs * ts) % tm == 0 and tm % ts == 0
        self.b, self.t, self.bs, self.ts, self.tm = b, t, bs, ts, tm
        self.tpb = t // tm
        self.n_pt = b * self.tpb
        self.gs = tm // ts
        self.n_st = (bs * ts) // tm
        self.n = self.n_pt + self.n_st
        self.n_tok = b * t + bs * ts

    def xp_spec(self, d):
        last, tpb = self.n_pt - 1, self.tpb
        return pl.BlockSpec((1, self.tm, d), lambda i: (jnp.minimum(i, last) // tpb, jnp.minimum(i, last) % tpb, 0))

    def xs_spec(self, d):
        n_pt = self.n_pt
        return pl.BlockSpec((self.gs, self.ts, d), lambda i: (jnp.maximum(i - n_pt, 0), 0, 0))

    def adap_spec(self, which, d):
        last, tpb = self.n_pt - 1, self.tpb
        return pl.BlockSpec((1, 1, 1, d), lambda i: (which, jnp.minimum(i, last) // tpb, 0, 0))

    def adas_spec(self, which, d):
        n_pt = self.n_pt
        return pl.BlockSpec((1, self.gs, 1, d), lambda i: (which, jnp.maximum(i - n_pt, 0), 0, 0))

    def tok_spec(self, width):
        return pl.BlockSpec((self.tm, width), lambda i: (i, 0))

    def s_x_spec(self, d):
        return pl.BlockSpec((self.gs, self.ts, d), lambda i: (i, 0, 0))

    def s_ada_spec(self, which, d):
        return pl.BlockSpec((1, self.gs, 1, d), lambda i: (which, i, 0, 0))

    def s_row_spec(self, width):
        return pl.BlockSpec((self.tm, width), lambda i: (i, 0))

    def s_tok_spec(self, width):
        n_pt = self.n_pt
        return pl.BlockSpec((self.tm, width), lambda i: (n_pt + i, 0))


def _rope_tables(pos0, t, dk):
    half = dk // 2
    inv = ROPE_BASE ** (-jnp.arange(half, dtype=jnp.float32) / half)
    pos = pos0 + jnp.arange(t)
    ang = pos.astype(jnp.float32)[:, None] * inv[None, :]
    cos, sin = jnp.cos(ang), jnp.sin(ang)
    return jnp.concatenate([cos, cos], axis=-1), jnp.concatenate([-sin, sin], axis=-1)


def _ret_tables(c, dk, dv):
    h = N_HEADS
    log_gamma = jnp.log1p(-jnp.exp2(-5.0 - jnp.arange(h, dtype=jnp.float32)))
    idx = jnp.arange(c, dtype=jnp.float32)
    rel = idx[:, None] - idx[None, :]
    dmask = jnp.where(rel >= 0, jnp.exp(log_gamma[:, None, None] * jnp.maximum(rel, 0.0)), 0.0)
    kdec = jnp.exp(log_gamma[:, None] * (c - 1 - idx))
    qdec = jnp.exp(log_gamma[:, None] * (idx + 1.0))
    cdec = jnp.exp(log_gamma * c)
    return (dmask,
            jnp.broadcast_to(qdec[:, :, None], (h, c, dk)),
            jnp.broadcast_to(kdec[:, :, None], (h, c, dk)),
            jnp.broadcast_to(cdec[:, None, None], (h, 1, dv)))


def _rot(x, cos_f, sin_f):
    return x * cos_f + pltpu.roll(x, x.shape[-1] // 2, 1) * sin_f


def _cross_and_update(q_lhs, k_end, vh, states, masks):
    if masks is None:
        (s,) = states
        return _mm(q_lhs, s.astype(BF16)), [_mm(k_end.T.astype(BF16), vh)]
    cross, incs = None, []
    for s, m in zip(states, masks):
        c = _mm(q_lhs, s.astype(BF16))
        cross = c if cross is None else jnp.where(m, c, cross)
        incs.append(_mm(jnp.where(m, k_end, 0.0).T.astype(BF16), vh))
    return cross, incs


def _ret_head(q, k, vh, gh, states, masks, cos_f, sin_f, dmask, qdec, kdec, cdec):
    dk = q.shape[-1]
    q = _rot(q, cos_f, sin_f)
    k = _rot(k, cos_f, sin_f) * (dk ** -0.5)
    scores = _mm_nt(q.astype(BF16), k.astype(BF16)) * dmask
    cross, incs = _cross_and_update((q * qdec).astype(BF16), k * kdec, vh, states, masks)
    o = _mm(scores.astype(BF16), vh) + cross
    new_states = [cdec * s + u for s, u in zip(states, incs)]
    mu = jnp.mean(o, axis=-1, keepdims=True)
    oc = o - mu
    var = jnp.mean(oc * oc, axis=-1, keepdims=True)
    return _silu(gh) * (oc * lax.rsqrt(var + EPS)), new_states


def _gla_head(q, k, vh, gh, b, states, masks, c, gnorm, causal):
    dk = q.shape[-1]
    b_t = b.T
    if masks is None:
        b_last = b[c - 1:c, :]
    else:
        b_last = None
        for g, m in enumerate(masks):
            row = b[g * c + c - 1:g * c + c, :]
            b_last = row if b_last is None else jnp.where(m, row, b_last)
    q_in = (q * (dk ** -0.5) * jnp.exp(b)).astype(BF16)
    k_in = (k * jnp.exp(-b)).astype(BF16)
    scores = jnp.where(causal, _mm_nt(q_in, k_in), 0.0)
    cross, incs = _cross_and_update(q_in, k * jnp.exp(b_last - b), vh, states, masks)
    o = _mm(scores.astype(BF16), vh) + cross
    new_states = [jnp.exp(b_t[:, g * c + c - 1:g * c + c]) * s + u for g, (s, u) in enumerate(zip(states, incs))]
    o = o * lax.rsqrt(jnp.mean(o * o, axis=-1, keepdims=True) + EPS) * gnorm
    return _silu(gh) * o, new_states


def _log_a(glr, wgk, bgk):
    z = _mm(glr.astype(BF16), wgk) + bgk
    return (jnp.minimum(z, 0.0) - jnp.log1p(jnp.exp(-jnp.abs(z)))) / GLA_GATE_NORM


def _causal(c):
    return lax.broadcasted_iota(jnp.int32, (c, c), 0) >= lax.broadcasted_iota(jnp.int32, (c, c), 1)


def _gate_rank_columns(w_ref, d):
    start = (N_SEG - 2) * d
    return w_ref.at[:, start:start + GLA_GATE_RANK]


def _merge_gate_columns(w_ref, d):
    start = (N_SEG - 2) * d + GLA_GATE_RANK
    return w_ref[:, start:start + 2 * d]


def _w_seg(w_ref, wm_ref, seg, d):
    if seg < N_SEG - 2:
        return w_ref[:, seg * d:(seg + 1) * d]
    return wm_ref[:, (seg - (N_SEG - 2)) * d:(seg - (N_SEG - 3)) * d]


def _proj_block(d, x3, sh, sc, g, w_ref, wm_ref, wl_ref, proj_s, glr_s):
    n = _rms_mod(x3, g, sc, sh).reshape(-1, d).astype(BF16)
    for seg in range(N_SEG):
        proj_s[:, seg * d:(seg + 1) * d] = _mm(n, _w_seg(w_ref, wm_ref, seg, d)).astype(BF16)
    glr_s[...] = _mm(n, wl_ref[...])


def _mix_block(d, tb, proj_s, glr_s, cos_f, sin_f, dmask_ref, qdec_ref, kdec_ref, cdec_ref, tri, wgk, bgk, gnorm,
               sr_s, sg_s, oret_ref, ogla_ref, mg_ref, r_off):
    dk, dv, hq = d // 8, d // 4, d // 2
    rqk, rv, rg, gqk, gv, gg, mg = (i * d for i in range(7))
    for h in range(N_HEADS):
        o, (s_new,) = _ret_head(proj_s[:, rqk + h * dk:rqk + (h + 1) * dk].astype(F32),
                                proj_s[:, rqk + hq + h * dk:rqk + hq + (h + 1) * dk].astype(F32),
                                proj_s[:, rv + h * dv:rv + (h + 1) * dv],
                                proj_s[:, rg + h * dv:rg + (h + 1) * dv].astype(F32),
                                [sr_s[h]], None, cos_f, sin_f, dmask_ref[h], qdec_ref[h], kdec_ref[h],
                                cdec_ref[h])
        sr_s[h] = s_new
        oret_ref[r_off:r_off + tb, h * dv:(h + 1) * dv] = o.astype(BF16)

    la_hi, la_lo = _split_hi_lo(_log_a(glr_s[...], wgk, bgk))
    b = _mm(tri, la_hi) + _mm(tri, la_lo)
    cg = GLA_CHUNK
    n_c = tb // cg
    causal = _causal(cg)
    b_last = jnp.concatenate([jnp.broadcast_to(b[c * cg + cg - 1:c * cg + cg, :], (cg, hq)) for c in range(n_c)],
                             axis=0)
    last_rows = jnp.concatenate([b[c * cg + cg - 1:c * cg + cg, :] for c in range(n_c)]
                                + [jnp.zeros((SUBLANES - n_c % SUBLANES, hq), F32)] * (n_c % SUBLANES != 0), axis=0)
    dec_cols = jnp.exp(last_rows.T)
    gq = proj_s[:, gqk:gqk + hq].astype(F32)
    gk = proj_s[:, gqk + hq:gqk + 2 * hq].astype(F32)
    q_in = (gq * (dk ** -0.5) * jnp.exp(b)).astype(BF16)
    k_in = (gk * jnp.exp(-b)).astype(BF16)
    k_end = gk * jnp.exp(b_last - b)
    intra, incs, decs = {}, {}, {}
    for c in range(n_c):
        rows = slice(c * cg, (c + 1) * cg)
        for h in range(N_HEADS):
            cols = slice(h * dk, (h + 1) * dk)
            vh = proj_s[rows, gv + h * dv:gv + (h + 1) * dv]
            scores = jnp.where(causal, _mm_nt(q_in[rows, cols], k_in[rows, cols]), 0.0)
            intra[c, h] = _mm(scores.astype(BF16), vh)
            incs[c, h] = _mm(k_end[rows, cols].T.astype(BF16), vh)
            decs[c, h] = dec_cols[cols, c:c + 1]
    for h in range(N_HEADS):
        cols = slice(h * dk, (h + 1) * dk)
        s = sg_s[h]
        for c in range(n_c):
            rows = slice(c * cg, (c + 1) * cg)
            o = intra[c, h] + _mm(q_in[rows, cols], s.astype(BF16))
            s = decs[c, h] * s + incs[c, h]
            o = o * lax.rsqrt(jnp.mean(o * o, axis=-1, keepdims=True) + EPS) * gnorm
            gh = proj_s[rows, gg + h * dv:gg + (h + 1) * dv].astype(F32)
            ogla_ref[r_off + c * cg:r_off + (c + 1) * cg, h * dv:(h + 1) * dv] = (_silu(gh) * o).astype(BF16)
        sg_s[h] = s
    mg_ref[0, r_off:r_off + tb, :] = proj_s[:, mg:mg + d]
    mg_ref[1, r_off:r_off + tb, :] = proj_s[:, mg + d:mg + 2 * d]


ROUTE_SLABS = 8


def _rows2d(a3, tm, d):
    if a3.shape[0] == 1:
        return a3.reshape(1, d)
    return jnp.broadcast_to(a3, (a3.shape[0], tm // a3.shape[0], d)).reshape(tm, d)


def _route_block(d, out_ret, out_gla, mg_ret, mg_gla, x3, gt, sh, sc, g, wro_ref, wgo_ref, wo_ref, wrh_ref, wrl_ref,
                 br, utri_ref, eye_ref, carry_s, h_ref, n2_ref, idx_ref, rank_ref, prob_ref):
    rows = out_ret.shape[0]
    e = N_EXPERTS
    half = rows // ROUTE_SLABS
    slabs = [slice(s * half, (s + 1) * half) for s in range(ROUTE_SLABS)]
    x2 = x3.reshape(rows, d)
    gt2, sh2, sc2 = (_rows2d(v, rows, d) for v in (gt, sh, sc))
    g2 = g.reshape(1, d)

    def rows_of(v, sl):
        return v if v.shape[0] == 1 else v[sl]

    ab = [(_mm(out_ret[sl], wro_ref[...]), _mm(out_gla[sl], wgo_ref[...])) for sl in slabs]
    mix = [_mm((jax.nn.sigmoid(mg_ret[sl].astype(F32)) * a + jax.nn.sigmoid(mg_gla[sl].astype(F32)) * b).astype(BF16),
               wo_ref[...]) for sl, (a, b) in zip(slabs, ab)]
    logits = []
    for sl, m in zip(slabs, mix):
        hs = x2[sl] + rows_of(gt2, sl) * m
        h_ref[sl, :] = hs
        ms = jnp.mean(hs * hs, axis=-1, keepdims=True)
        n2 = hs * lax.rsqrt(ms + EPS) * g2 * (1.0 + rows_of(sc2, sl)) + rows_of(sh2, sl)
        n2_ref[sl, :] = _pack_pair(n2)
        n_hi, n_lo = _split_hi_lo(n2)
        logits.append(_mm_nt(wrh_ref[...], n_hi) + _mm_nt(wrh_ref[...], n_lo) + _mm_nt(wrl_ref[...], n_hi) + br)

    iota_e = lax.broadcasted_iota(jnp.int32, (e, half), 0)
    slot = lax.broadcasted_iota(jnp.int32, (TOP_K, half), 0)
    carry = carry_s[...]
    for sl, work in zip(slabs, logits):
        vals, idxs = [], []
        for _ in range(TOP_K):
            m = jnp.max(work, axis=0, keepdims=True)
            ik = jnp.min(jnp.where(work == m, iota_e, e), axis=0, keepdims=True)
            vals.append(m)
            idxs.append(ik)
            work = jnp.where(iota_e == ik, -jnp.inf, work)
        ex = [jnp.exp(v - vals[0]) for v in vals]
        den = ex[0] + ex[1] + ex[2] + ex[3]
        onehot = jnp.zeros((e, half), F32)
        for ik in idxs:
            onehot = onehot + (iota_e == ik).astype(F32)
        cum = _mm(onehot.astype(BF16), utri_ref[...]) + carry
        carry = carry + jnp.sum(onehot, axis=1, keepdims=True)
        idx_o = jnp.zeros((TOP_K, half), jnp.int32)
        rank_o = jnp.zeros((TOP_K, half), jnp.int32)
        prob_t = jnp.zeros((TOP_K, half), F32)
        for k in range(TOP_K):
            rk = jnp.sum(jnp.where(iota_e == idxs[k], cum, 0.0), axis=0, keepdims=True).astype(jnp.int32)
            idx_o = jnp.where(slot == k, idxs[k], idx_o)
            rank_o = jnp.where(slot == k, rk, rank_o)
            prob_t = jnp.where(slot == k, ex[k] / den, prob_t)
        idx_ref[:, sl] = idx_o
        rank_ref[:, sl] = rank_o
        p1 = prob_t.astype(BF16)
        r1 = prob_t - p1.astype(F32)
        p2 = r1.astype(BF16)
        p3 = (r1 - p2.astype(F32)).astype(BF16)
        pieces = jnp.concatenate([p1, p2, p3, jnp.zeros_like(p1)], axis=0)
        t = _mm_nt(eye_ref[...], pieces)
        prob_ref[sl, :] = t[:, 0:TOP_K] + t[:, TOP_K:2 * TOP_K] + t[:, 2 * TOP_K:3 * TOP_K]
    carry_s[...] = carry


def _frontp_kernel(d, tb, ntb, x0_ref, xa_ref, xb_ref, sh0_ref, sc0_ref, sha_ref, sca_ref, shb_ref, scb_ref,
                   g_ref, w_ref, cosa_ref, sina_ref, cosb_ref, sinb_ref,
                   dmask_ref, qdec_ref, kdec_ref, cdec_ref, tri_ref, wgk_ref, bgk_ref, gn_ref,
                   oret_ref, ogla_ref, mg_ref, sret_ref, sgla_ref, pa_s, pb_s, ga_s, gb_s, sr_s, sg_s, wm_ref):
    p = pl.program_id(0)
    blk = 2 * p
    g = g_ref[...]
    proj = functools.partial(_proj_block, d)
    mix = functools.partial(_mix_block, d, tb)
    tables = (dmask_ref, qdec_ref, kdec_ref, cdec_ref, tri_ref[...], wgk_ref[...], bgk_ref[...], gn_ref[...])

    wl_ref = _gate_rank_columns(w_ref, d)

    @pl.when(p == 0)
    def _():
        wm_ref[...] = _merge_gate_columns(w_ref, d)
        proj(x0_ref[...], sh0_ref[0], sc0_ref[0], g, w_ref, wm_ref, wl_ref, pa_s, ga_s)

    @pl.when(blk % ntb == 0)
    def _():
        sr_s[...] = jnp.zeros_like(sr_s)
        sg_s[...] = jnp.zeros_like(sg_s)

    proj(xa_ref[...], sha_ref[0], sca_ref[0], g, w_ref, wm_ref, wl_ref, pb_s, gb_s)
    mix(pa_s, ga_s, cosa_ref[...], sina_ref[...], *tables, sr_s, sg_s, oret_ref, ogla_ref, mg_ref, 0)
    proj(xb_ref[...], shb_ref[0], scb_ref[0], g, w_ref, wm_ref, wl_ref, pa_s, ga_s)
    mix(pb_s, gb_s, cosb_ref[...], sinb_ref[...], *tables, sr_s, sg_s, oret_ref, ogla_ref, mg_ref, tb)

    @pl.when((blk + 1) % ntb == ntb - 1)
    def _():
        sret_ref[0] = sr_s[...]
        sgla_ref[0] = sg_s[...]


def _chunk_tri(tb, cg):
    i = jnp.arange(tb)
    return ((i[:, None] >= i[None, :]) & (i[:, None] // cg == i[None, :] // cg)).astype(BF16)


def _front_prompt(x_p, ada_p, g_mix, w_all, tb, w_gk, b_gk, g_gla):
    b, t, d = x_p.shape
    dk, dv, hq, h = d // 8, d // 4, d // 2, N_HEADS
    ntb = t // tb
    n_blk = b * ntb
    n_tok = b * t
    assert ntb % 2 == 0
    cos_f, sin_f = _rope_tables(0, t, dk)
    dmask, qdec, kdec, cdec = _ret_tables(tb, dk, dv)
    tri = _chunk_tri(tb, GLA_CHUNK)

    def first(p):
        return 0 * p

    def even(p):
        return 2 * p

    def odd(p):
        return 2 * p + 1

    def nxt(p):
        return jnp.minimum(2 * p + 2, n_blk - 1)

    def x_spec(blk_of):
        return pl.BlockSpec((1, tb, d), lambda p: (blk_of(p) // ntb, blk_of(p) % ntb, 0))

    def ada_spec(which, blk_of):
        return pl.BlockSpec((1, 1, 1, d), lambda p: (which, blk_of(p) // ntb, 0, 0))

    def rope_spec(blk_of):
        return pl.BlockSpec((tb, dk), lambda p: (blk_of(p) % ntb, 0))

    state_spec = pl.BlockSpec((1, h, dk, dv), lambda p: ((2 * p) // ntb, 0, 0, 0))
    tok_spec = pl.BlockSpec((2 * tb, d), lambda p: (p, 0))
    return pl.pallas_call(
        functools.partial(_frontp_kernel, d, tb, ntb),
        grid=(n_blk // 2,),
        in_specs=[_resident((1, tb, d)), x_spec(odd), x_spec(nxt),
                  ada_spec(0, first), ada_spec(1, first), ada_spec(0, odd), ada_spec(1, odd),
                  ada_spec(0, nxt), ada_spec(1, nxt),
                  _resident((1, 1, d)), _resident(w_all.shape),
                  rope_spec(even), rope_spec(even), rope_spec(odd), rope_spec(odd),
                  _resident((h, tb, tb)), _resident((h, tb, dk)), _resident((h, tb, dk)), _resident((h, 1, dv)),
                  _resident((tb, tb)), _resident((GLA_GATE_RANK, hq)), _resident((1, hq)), _resident((1, dv))],
        out_specs=[tok_spec, tok_spec, pl.BlockSpec((2, 2 * tb, d), lambda p: (0, p, 0)), state_spec, state_spec],
        out_shape=[jax.ShapeDtypeStruct((n_tok, d), BF16), jax.ShapeDtypeStruct((n_tok, d), BF16),
                   jax.ShapeDtypeStruct((2, n_tok, d), BF16),
                   jax.ShapeDtypeStruct((b, h, dk, dv), F32), jax.ShapeDtypeStruct((b, h, dk, dv), F32)],
        scratch_shapes=[pltpu.VMEM((tb, N_SEG * d), BF16), pltpu.VMEM((tb, N_SEG * d), BF16),
                        pltpu.VMEM((tb, GLA_GATE_RANK), F32), pltpu.VMEM((tb, GLA_GATE_RANK), F32),
                        pltpu.VMEM((h, dk, dv), F32), pltpu.VMEM((h, dk, dv), F32), pltpu.VMEM((d, 2 * d), BF16)],
        compiler_params=pltpu.CompilerParams(dimension_semantics=("arbitrary",), vmem_limit_bytes=VMEM_LIMIT),
        name="front_prompt",
    )(x_p, x_p, x_p, ada_p, ada_p, ada_p, ada_p, ada_p, ada_p, g_mix.reshape(1, 1, d), w_all,
      cos_f, sin_f, cos_f, sin_f, dmask, qdec, kdec, cdec, tri, w_gk, b_gk, g_gla)


def _inproj_kernel(d, xs_ref, shs_ref, scs_ref, g_ref, w_ref, proj_ref, glr_ref, wm_ref):
    @pl.when(pl.program_id(0) == 0)
    def _():
        wm_ref[...] = _merge_gate_columns(w_ref, d)

    n = _rms_mod(xs_ref[...], g_ref[...], scs_ref[0], shs_ref[0]).reshape(-1, d).astype(BF16)
    for s in range(N_SEG):
        proj_ref[s] = _mm(n, _w_seg(w_ref, wm_ref, s, d)).astype(BF16)
    glr_ref[...] = _mm(n, _gate_rank_columns(w_ref, d)[...])


def _inproj_sample(tl, x_s, ada_s, g_mix, w_all):
    bs, ts, d = x_s.shape
    n_tok = bs * ts
    return pl.pallas_call(
        functools.partial(_inproj_kernel, d),
        grid=(tl.n_st,),
        in_specs=[tl.s_x_spec(d), tl.s_ada_spec(0, d), tl.s_ada_spec(1, d),
                  _resident((1, 1, d)), _resident(w_all.shape)],
        out_specs=[pl.BlockSpec((N_SEG, tl.tm, d), lambda i: (0, i, 0)), tl.s_row_spec(GLA_GATE_RANK)],
        out_shape=[jax.ShapeDtypeStruct((N_SEG, n_tok, d), BF16), jax.ShapeDtypeStruct((n_tok, GLA_GATE_RANK), F32)],
        scratch_shapes=[pltpu.VMEM((d, 2 * d), BF16)],
        compiler_params=pltpu.CompilerParams(dimension_semantics=("arbitrary",), vmem_limit_bytes=VMEM_LIMIT),
        name="inproj_sample",
    )(x_s, ada_s, ada_s, g_mix.reshape(1, 1, d), w_all)


def _mixs_kernel(d, ts, gsz, rqk_ref, rv_ref, rg_ref, gqk_ref, gv_ref, gg_ref, glr_ref, cos_ref, sin_ref,
                 dmask_ref, qdec_ref, kdec_ref, cdec_ref, wgk_ref, bgk_ref, gn_ref, sr_in, sg_in,
                 oret_ref, ogla_ref, sr_out, sg_out):
    dk, dv, hq = d // 8, d // 4, d // 2
    pair = 2 * ts
    cos_f, sin_f = cos_ref[...], sin_ref[...]
    gnorm = gn_ref[...]
    ri = lax.broadcasted_iota(jnp.int32, (pair, pair), 0)
    ci = lax.broadcasted_iota(jnp.int32, (pair, pair), 1)
    causal = jnp.logical_and(ri >= ci, (ri < ts) == (ci < ts))
    tri = causal.astype(F32).astype(BF16)
    first = lax.broadcasted_iota(jnp.int32, (pair, 1), 0) < ts
    masks = [first, jnp.logical_not(first)]

    def body(j, carry):
        rows = pl.ds(pl.multiple_of(j * pair, pair), pair)
        s0, s1 = 2 * j, 2 * j + 1
        la_hi, la_lo = _split_hi_lo(_log_a(glr_ref[rows, :], wgk_ref[...], bgk_ref[...]))
        b = _mm(tri, la_hi) + _mm(tri, la_lo)
        for h in range(N_HEADS):
            o, (n0, n1) = _ret_head(rqk_ref[0, rows, h * dk:(h + 1) * dk].astype(F32),
                                    rqk_ref[0, rows, hq + h * dk:hq + (h + 1) * dk].astype(F32),
                                    rv_ref[0, rows, h * dv:(h + 1) * dv],
                                    rg_ref[0, rows, h * dv:(h + 1) * dv].astype(F32),
                                    [sr_in[s0, h], sr_in[s1, h]], masks, cos_f, sin_f,
                                    dmask_ref[h], qdec_ref[h], kdec_ref[h], cdec_ref[h])
            sr_out[s0, h] = n0
            sr_out[s1, h] = n1
            oret_ref[rows, h * dv:(h + 1) * dv] = o.astype(BF16)
            o, (n0, n1) = _gla_head(gqk_ref[0, rows, h * dk:(h + 1) * dk].astype(F32),
                                    gqk_ref[0, rows, hq + h * dk:hq + (h + 1) * dk].astype(F32),
                                    gv_ref[0, rows, h * dv:(h + 1) * dv],
                                    gg_ref[0, rows, h * dv:(h + 1) * dv].astype(F32),
                                    b[:, h * dk:(h + 1) * dk], [sg_in[s0, h], sg_in[s1, h]], masks, ts,
                                    gnorm, causal)
            sg_out[s0, h] = n0
            sg_out[s1, h] = n1
            ogla_ref[rows, h * dv:(h + 1) * dv] = o.astype(BF16)
        return carry

    lax.fori_loop(0, gsz // 2, body, 0, unroll=2)


def _pair_tables(ts, dk, dv):
    cos_f, sin_f = _rope_tables(PAST_LEN, ts, dk)
    dmask, qdec, kdec, cdec = _ret_tables(ts, dk, dv)
    zero = jnp.zeros_like(dmask)
    dmask2 = jnp.concatenate([jnp.concatenate([dmask, zero], axis=2), jnp.concatenate([zero, dmask], axis=2)], axis=1)

    def twice(a, axis):
        return jnp.concatenate([a, a], axis=axis)

    return twice(cos_f, 0), twice(sin_f, 0), dmask2, twice(qdec, 1), twice(kdec, 1), cdec


def _mix_sample(bs, ts, d, gsz, proj, glr, state_ret, state_gla, w_gk, b_gk, g_gla):
    dk, dv, hq, h = d // 8, d // 4, d // 2, N_HEADS
    assert GLA_CHUNK % ts == 0 and bs % gsz == 0 and gsz % 4 == 0
    rows = gsz * ts
    pair = 2 * ts
    cos_f, sin_f, dmask, qdec, kdec, cdec = _pair_tables(ts, dk, dv)

    def seg(s):
        return pl.BlockSpec((1, rows, d), lambda i: (s, i, 0))

    state_spec = pl.BlockSpec((gsz, h, dk, dv), lambda i: (i, 0, 0, 0))
    tok_spec = pl.BlockSpec((rows, d), lambda i: (i, 0))
    return pl.pallas_call(
        functools.partial(_mixs_kernel, d, ts, gsz),
        grid=(bs // gsz,),
        in_specs=[seg(0), seg(1), seg(2), seg(3), seg(4), seg(5),
                  pl.BlockSpec((rows, GLA_GATE_RANK), lambda i: (i, 0)),
                  _const((pair, dk)), _const((pair, dk)),
                  _const((h, pair, pair)), _const((h, pair, dk)), _const((h, pair, dk)), _const((h, 1, dv)),
                  _const((GLA_GATE_RANK, hq)), _const((1, hq)), _const((1, dv)),
                  state_spec, state_spec],
        out_specs=[tok_spec, tok_spec, state_spec, state_spec],
        out_shape=[jax.ShapeDtypeStruct((bs * ts, d), BF16), jax.ShapeDtypeStruct((bs * ts, d), BF16),
                   jax.ShapeDtypeStruct((bs, h, dk, dv), F32), jax.ShapeDtypeStruct((bs, h, dk, dv), F32)],
        compiler_params=pltpu.CompilerParams(dimension_semantics=("arbitrary",), vmem_limit_bytes=VMEM_LIMIT),
        name="mix_sample",
    )(proj, proj, proj, proj, proj, proj, glr, cos_f, sin_f, dmask, qdec, kdec, cdec, w_gk, b_gk, g_gla,
      state_ret, state_gla)


def _outproj_kernel(n_pt, d, orp_ref, ogp_ref, ors_ref, ogs_ref, mgrp_ref, mggp_ref, mgrs_ref, mggs_ref,
                    xp_ref, xs_ref, gtp_ref, shp_ref, scp_ref, gts_ref, shs_ref, scs_ref, g_ref,
                    wro_ref, wgo_ref, wo_ref, wrh_ref, wrl_ref, br_ref, utri_ref, eye_ref,
                    h_ref, n2_ref, idx_ref, rank_ref, prob_ref, cnt_ref, carry_s):
    i = pl.program_id(0)

    @pl.when(i == 0)
    def _():
        carry_s[...] = jnp.zeros_like(carry_s)

    route = functools.partial(_route_block, d)
    tail = (g_ref[...], wro_ref, wgo_ref, wo_ref, wrh_ref, wrl_ref, br_ref[...], utri_ref, eye_ref, carry_s,
            h_ref, n2_ref, idx_ref, rank_ref, prob_ref)

    @pl.when(i < n_pt)
    def _():
        route(orp_ref[...], ogp_ref[...], mgrp_ref[0], mggp_ref[0], xp_ref[...], gtp_ref[0], shp_ref[0], scp_ref[0],
              *tail)

    @pl.when(i >= n_pt)
    def _():
        route(ors_ref[...], ogs_ref[...], mgrs_ref[0], mggs_ref[0], xs_ref[...], gts_ref[0], shs_ref[0], scs_ref[0],
              *tail)

    @pl.when(i == pl.num_programs(0) - 1)
    def _():
        cnt_ref[...] = carry_s[...].astype(jnp.int32)


def _outproj(tl, oret_p, ogla_p, oret_s, ogla_s, mg_p, proj_s, x_p, x_s, ada_p, ada_s, g_ffn,
             w_ret_o, w_gla_o, w_out, w_r_hi, w_r_lo, b_router):
    d = x_p.shape[-1]
    tm, e, n_pt = tl.tm, N_EXPERTS, tl.n_pt
    last = n_pt - 1
    p_spec = pl.BlockSpec((tm, d), lambda i: (jnp.minimum(i, last), 0))
    s_spec = pl.BlockSpec((tm, d), lambda i: (jnp.maximum(i - n_pt, 0), 0))

    def mgp_spec(seg):
        return pl.BlockSpec((1, tm, d), lambda i: (seg, jnp.minimum(i, last), 0))

    def mgs_spec(seg):
        return pl.BlockSpec((1, tm, d), lambda i: (seg, jnp.maximum(i - n_pt, 0), 0))

    slot_spec = pl.BlockSpec((TOP_K, tm), lambda i: (0, i))
    half = tm // ROUTE_SLABS
    token = jnp.arange(half)
    utri = (token[:, None] < token[None, :]).astype(BF16)
    eye = jnp.eye(half, dtype=BF16)

    return pl.pallas_call(
        functools.partial(_outproj_kernel, n_pt, d),
        grid=(tl.n,),
        in_specs=[p_spec, p_spec, s_spec, s_spec, mgp_spec(0), mgp_spec(1), mgs_spec(6), mgs_spec(7),
                  tl.xp_spec(d), tl.xs_spec(d),
                  tl.adap_spec(2, d), tl.adap_spec(3, d), tl.adap_spec(4, d),
                  tl.adas_spec(2, d), tl.adas_spec(3, d), tl.adas_spec(4, d),
                  _resident((1, 1, d)), _resident((d, d)), _resident((d, d)), _resident((d, d)),
                  _resident((e, d)), _resident((e, d)), _resident((e, 1)),
                  _resident((half, half)), _resident((half, half))],
        out_specs=[tl.tok_spec(d), tl.tok_spec(d // 2), slot_spec, slot_spec, tl.tok_spec(TOP_K),
                   pl.BlockSpec((e, 1), lambda i: (0, 0))],
        out_shape=[jax.ShapeDtypeStruct((tl.n_tok, d), F32), jax.ShapeDtypeStruct((tl.n_tok, d // 2), jnp.uint32),
                   jax.ShapeDtypeStruct((TOP_K, tl.n_tok), jnp.int32),
                   jax.ShapeDtypeStruct((TOP_K, tl.n_tok), jnp.int32),
                   jax.ShapeDtypeStruct((tl.n_tok, TOP_K), F32),
                   jax.ShapeDtypeStruct((e, 1), jnp.int32)],
        scratch_shapes=[pltpu.VMEM((e, 1), F32)],
        compiler_params=pltpu.CompilerParams(dimension_semantics=("arbitrary",), vmem_limit_bytes=VMEM_LIMIT),
        name="outproj",
    )(oret_p, ogla_p, oret_s, ogla_s, mg_p, mg_p, proj_s, proj_s, x_p, x_s, ada_p, ada_p, ada_p, ada_s, ada_s, ada_s,
      g_ffn.reshape(1, 1, d), w_ret_o, w_gla_o, w_out, w_r_hi, w_r_lo, b_router.reshape(e, 1), utri, eye)


EXPERT_TILES_PER_STEP = 2
EXPERT_TAIL_ROWS = 128


def _expert_kernel(f, tme, te_ref, na_ref, grp_ref, nxt_ref, tr_ref, x_ref, wu_hbm, wd_hbm, *rest):
    n_t = EXPERT_TILES_PER_STEP
    bu_refs, bd_refs = rest[:n_t], rest[n_t:2 * n_t]
    y_ref, wu_f, wd_f, wu_s, wd_s, sem = rest[2 * n_t:]
    step = pl.program_id(0)

    def fetch(expert, s):
        return (pltpu.make_async_copy(wu_hbm.at[expert], wu_f.at[s], sem.at[0, s]),
                pltpu.make_async_copy(wd_hbm.at[expert], wd_f.at[s], sem.at[1, s]))

    @pl.when(step == 0)
    def _():
        for c in fetch(te_ref[0], 0):
            c.start()

    slab = tme // EXPERT_ROW_SLABS
    half = x_ref.shape[1]
    for t in range(n_t):
        j = step * n_t + t
        active = j < na_ref[0]
        first = jnp.logical_or(j == 0, te_ref[j] != te_ref[jnp.maximum(j - 1, 0)])
        slot = grp_ref[j] % 2

        @pl.when(jnp.logical_and(active, first))
        def _(j=j, slot=slot):
            for c in fetch(te_ref[j], slot):
                c.wait()

            @pl.when(nxt_ref[j] >= 0)
            def _():
                for c in fetch(nxt_ref[j], 1 - slot):
                    c.start()

            wu_s[...] = wu_f[slot].astype(BF16)
            wd_s[...] = wd_f[slot].astype(BF16)

        def compute(rows, t=t):
            x_lo, x_hi = _unpack_pair(x_ref[rows, :])
            gu = _mm(x_lo.astype(BF16), wu_s[:half, :]) + _mm(x_hi.astype(BF16), wu_s[half:, :]) + bu_refs[t][0]
            gate = jnp.minimum(gu[:, :f], SWIGLU_LIMIT)
            up = jnp.clip(gu[:, f:], -SWIGLU_LIMIT, SWIGLU_LIMIT)
            act = (up + 1.0) * gate * jax.nn.sigmoid(SWIGLU_ALPHA * gate)
            y_ref[rows, :] = _pack_pair(_mm(act.astype(BF16), wd_s[...]) + bd_refs[t][0])

        valid = tr_ref[j]
        short = [valid <= EXPERT_TAIL_ROWS,
                 jnp.logical_and(valid > EXPERT_TAIL_ROWS, valid <= 2 * EXPERT_TAIL_ROWS)]

        @pl.when(jnp.logical_and(active, valid > 2 * EXPERT_TAIL_ROWS))
        def _(t=t, compute=compute):
            for s in range(EXPERT_ROW_SLABS):
                compute(slice(t * tme + s * slab, t * tme + (s + 1) * slab))

        for n_pieces, cond in enumerate(short, start=1):
            @pl.when(jnp.logical_and(active, cond))
            def _(t=t, compute=compute, n_pieces=n_pieces):
                compute(slice(t * tme, t * tme + n_pieces * EXPERT_TAIL_ROWS))


def _experts(xs, tile_expert, n_active, tile_group, next_expert, tile_rows, w_up, b_up, w_down, b_down, tme):
    r = xs.shape[0]
    e, d, f2 = w_up.shape
    f = f2 // 2
    n_t = EXPERT_TILES_PER_STEP
    n_tiles = r // tme
    assert n_tiles % n_t == 0 and tme % EXPERT_TAIL_ROWS == 0

    def row_map(s, te, na, grp, nxt, tr):
        return (jnp.minimum(s, (na[0] - 1) // n_t), 0)

    def bias_map(t):
        return lambda s, te, na, grp, nxt, tr: (te[jnp.minimum(s * n_t + t, na[0] - 1)], 0, 0)

    hbm = pl.BlockSpec(memory_space=pl.ANY)
    return pl.pallas_call(
        functools.partial(_expert_kernel, f, tme),
        grid_spec=pltpu.PrefetchScalarGridSpec(
            num_scalar_prefetch=5,
            grid=(n_tiles // n_t,),
            in_specs=[pl.BlockSpec((n_t * tme, d // 2), row_map), hbm, hbm]
                     + [pl.BlockSpec((1, 1, f2), bias_map(t)) for t in range(n_t)]
                     + [pl.BlockSpec((1, 1, d), bias_map(t)) for t in range(n_t)],
            out_specs=pl.BlockSpec((n_t * tme, d // 2), row_map),
            scratch_shapes=[pltpu.VMEM((2, d, f2), F32), pltpu.VMEM((2, f, d), F32),
                            pltpu.VMEM((d, f2), BF16), pltpu.VMEM((f, d), BF16),
                            pltpu.SemaphoreType.DMA((2, 2))]),
        out_shape=jax.ShapeDtypeStruct((r, d // 2), jnp.uint32),
        compiler_params=pltpu.CompilerParams(dimension_semantics=("arbitrary",), vmem_limit_bytes=VMEM_LIMIT),
        name="experts",
    )(tile_expert, n_active, tile_group, next_expert, tile_rows, xs, w_up, w_down,
      *([b_up.reshape(e, 1, f2)] * n_t), *([b_down.reshape(e, 1, d)] * n_t))


def _sc_mesh():
    return plsc.VectorSubcoreMesh(core_axis_name="core", subcore_axis_name="subcore")


def _sc_split(n_rows, max_chunk):
    info = plsc.get_sparse_core_info()
    n_workers = info.num_cores * info.num_subcores
    assert n_rows % (8 * n_workers) == 0
    per_w = n_rows // n_workers
    chunk = 8
    while chunk * 2 <= max_chunk and per_w % (chunk * 2) == 0:
        chunk *= 2
    return info.num_cores, n_workers, per_w, chunk


def _sc_dispatch(x, pos_t, n_rows):
    n, w = x.shape
    nc, nw, per_w, chunk = _sc_split(n, 32)
    n_ch = per_w // chunk
    idx = pos_t.reshape(TOP_K, nw, n_ch, chunk).transpose(1, 0, 2, 3).reshape(nw, TOP_K * n_ch, chunk)

    @functools.partial(
        pl.kernel, out_type=jax.ShapeDtypeStruct((n_rows, w), x.dtype), mesh=_sc_mesh(),
        scratch_types=[pltpu.VMEM((TOP_K * n_ch, chunk), jnp.int32), pltpu.VMEM((2, chunk, w), x.dtype),
                       pltpu.SemaphoreType.DMA((2,)), pltpu.SemaphoreType.DMA((2,))])
    def scatter_rows(x_hbm, i_hbm, o_hbm, idx_v, rows_v, rsem, wsem):
        wid = lax.axis_index("subcore") * nc + lax.axis_index("core")
        base = wid * per_w
        pltpu.sync_copy(i_hbm.at[wid], idx_v)

        def read(j, slot):
            return pltpu.make_async_copy(x_hbm.at[pl.ds(base + j * chunk, chunk)], rows_v.at[slot], rsem.at[slot])

        def write(j, slot, k):
            return pltpu.make_async_copy(rows_v.at[slot], o_hbm.at[idx_v.at[k * n_ch + j]], wsem.at[slot])

        read(0, 0).start()

        @pl.loop(0, n_ch, step=2)
        def _(j0):
            for b in range(2):
                j = j0 + b

                @pl.when(j < n_ch)
                def _():
                    read(j, b).wait()

                    @pl.when(j + 1 < n_ch)
                    def _():
                        @pl.when(j >= 1)
                        def _():
                            for k in range(TOP_K):
                                write(j - 1, 1 - b, k).wait()

                        read(j + 1, 1 - b).start()

                    for k in range(TOP_K):
                        write(j, b, k).start()

        for jj in range(max(n_ch - 2, 0), n_ch):
            for k in range(TOP_K):
                write(jj, jj % 2, k).wait()

    return scatter_rows(x, idx)


def _sc_gather(table, idx):
    m = idx.shape[0]
    w = table.shape[1]
    nc, _, per_w, chunk = _sc_split(m, 64)
    n_ch = per_w // chunk

    @functools.partial(
        pl.kernel, out_type=jax.ShapeDtypeStruct((m, w), table.dtype), mesh=_sc_mesh(),
        scratch_types=[pltpu.VMEM((per_w,), jnp.int32), pltpu.VMEM((2, chunk, w), table.dtype),
                       pltpu.SemaphoreType.DMA((2,)), pltpu.SemaphoreType.DMA((2,))])
    def gather_rows(t_hbm, i_hbm, o_hbm, idx_v, rows_v, gsem, wsem):
        wid = lax.axis_index("subcore") * nc + lax.axis_index("core")
        base = wid * per_w
        pltpu.sync_copy(i_hbm.at[pl.ds(base, per_w)], idx_v)

        def gather(j, slot):
            off = pl.multiple_of(j * chunk, chunk)
            return pltpu.make_async_copy(t_hbm.at[idx_v.at[pl.ds(off, chunk)]], rows_v.at[slot], gsem.at[slot])

        def write(j, slot):
            off = pl.multiple_of(j * chunk, chunk)
            return pltpu.make_async_copy(rows_v.at[slot], o_hbm.at[pl.ds(base + off, chunk)], wsem.at[slot])

        gather(0, 0).start()

        @pl.loop(0, n_ch, step=2)
        def _(j0):
            for b in range(2):
                j = j0 + b

                @pl.when(j < n_ch)
                def _():
                    gather(j, b).wait()

                    @pl.when(j + 1 < n_ch)
                    def _():
                        @pl.when(j >= 1)
                        def _():
                            write(j - 1, 1 - b).wait()

                        gather(j + 1, 1 - b).start()

                    write(j, b).start()

        for jj in range(max(n_ch - 2, 0), n_ch):
            write(jj, jj % 2).wait()

    return gather_rows(table, idx)


def _final_kernel(n_pt, d, h_ref, yg_ref, prob_ref, gtp_ref, gts_ref, g_ref, yp_ref, ys_ref):
    i = pl.program_id(0)
    p = prob_ref[...]
    moe_lo, moe_hi = None, None
    for k in range(TOP_K):
        lo, hi = _unpack_pair(yg_ref[k])
        pk = p[:, k:k + 1]
        moe_lo = pk * lo if moe_lo is None else moe_lo + pk * lo
        moe_hi = pk * hi if moe_hi is None else moe_hi + pk * hi
    moe = jnp.concatenate([moe_lo, moe_hi], axis=1)

    def body(gt, shape):
        h3 = h_ref[...].reshape(shape) + gt * moe.reshape(shape)
        ms = jnp.mean(h3 * h3, axis=-1, keepdims=True)
        return h3 * lax.rsqrt(ms + EPS) * g_ref[...]

    @pl.when(i < n_pt)
    def _():
        yp_ref[...] = body(gtp_ref[0], yp_ref.shape)

    @pl.when(i >= n_pt)
    def _():
        ys_ref[...] = body(gts_ref[0], ys_ref.shape)


def _final(tl, h, yg, probs, ada_p, ada_s, g_final, d):
    return pl.pallas_call(
        functools.partial(_final_kernel, tl.n_pt, d),
        grid=(tl.n,),
        in_specs=[tl.tok_spec(d), pl.BlockSpec((TOP_K, tl.tm, d // 2), lambda i: (0, i, 0)), tl.tok_spec(TOP_K),
                  tl.adap_spec(5, d), tl.adas_spec(5, d), _resident((1, 1, d))],
        out_specs=[tl.xp_spec(d), tl.xs_spec(d)],
        out_shape=[jax.ShapeDtypeStruct((tl.b, tl.t, d), F32), jax.ShapeDtypeStruct((tl.bs, tl.ts, d), F32)],
        compiler_params=pltpu.CompilerParams(dimension_semantics=("arbitrary",), vmem_limit_bytes=VMEM_LIMIT),
        name="final",
    )(h, yg, probs, ada_p, ada_s, g_final.reshape(1, 1, d))


def _pick(n, pref):
    t = min(n, pref)
    while n % t:
        t //= 2
    return t


def _forward(x_prompt, x_sample, c_prompt, c_sample, state_ret, state_gla, w_ada, b_ada, g_norm_mix, g_norm_ffn,
             w_in, w_gk_up, b_gk, g_gla_norm, w_ret_o, w_gla_o, w_out, w_router, b_router, w_up, b_up,
             w_down, b_down, g_final, *, tm, tb, gsz, tme):
    b, t, d = x_prompt.shape
    bs, ts, _ = x_sample.shape
    assert w_ada.shape[0] == 1, "single layer only"
    assert (b * t) % (2 * tb) == 0 and (b * t) % tm == 0
    e = N_EXPERTS
    tl = _Tiles(b, t, bs, ts, tm)
    n_tok = tl.n_tok

    ada = _ada(jnp.concatenate([c_prompt, c_sample], axis=0), w_ada[0], b_ada[0])
    ada_p = ada[:, :b].reshape(6, b, 1, d)
    ada_s = ada[:, b:].reshape(6, bs, 1, d)

    w_all = w_in[0].astype(BF16)
    w_gk = w_gk_up[0].astype(BF16)
    bgk = b_gk[0].reshape(1, -1)
    ggn = g_gla_norm[0].reshape(1, -1)
    w_r = w_router[0].T
    w_r_hi = w_r.astype(BF16)
    w_r_lo = (w_r - w_r_hi.astype(F32)).astype(BF16)
    route_w = (g_norm_ffn[0], w_ret_o[0].astype(BF16), w_gla_o[0].astype(BF16), w_out[0].astype(BF16),
               w_r_hi, w_r_lo, b_router[0])

    oret_p, ogla_p, mg_p, sret_p, sgla_p = _front_prompt(x_prompt, ada_p, g_norm_mix[0], w_all, tb,
                                                         w_gk, bgk, ggn)
    proj_s, glr_s = _inproj_sample(tl, x_sample, ada_s, g_norm_mix[0], w_all)
    oret_s, ogla_s, sret_s, sgla_s = _mix_sample(bs, ts, d, gsz, proj_s, glr_s, state_ret[0], state_gla[0],
                                                 w_gk, bgk, ggn)
    h, n2, idx_t, rank_t, probs, counts = _outproj(tl, oret_p, ogla_p, oret_s, ogla_s, mg_p, proj_s, x_prompt, x_sample,
                                               ada_p, ada_s, *route_w)

    counts = counts[:, 0]
    gsize = ((counts + tme - 1) // tme) * tme
    ends = jnp.cumsum(gsize)
    offs = ends - gsize
    experts = jnp.arange(e, dtype=jnp.int32)
    pos_t = jnp.sum(jnp.where(idx_t[..., None] == experts, offs, 0), axis=-1) + rank_t
    max_tiles = (n_tok * TOP_K) // tme + e
    n_active = (ends[-1] // tme).astype(jnp.int32).reshape(1)
    tile_start = jnp.arange(max_tiles, dtype=jnp.int32) * tme
    tile_expert = jnp.minimum(jnp.sum((ends[None, :] <= tile_start[:, None]).astype(jnp.int32), axis=1), e - 1)
    is_first = jnp.logical_and(tile_start < ends[-1],
                               jnp.concatenate([jnp.ones((1,), bool), tile_expert[1:] != tile_expert[:-1]]))
    tile_group = jnp.cumsum(is_first.astype(jnp.int32)) - 1
    later = jnp.logical_and(experts[None, :] > experts[:, None], counts[None, :] > 0)
    next_of = jnp.min(jnp.where(later, experts[None, :], e), axis=1)
    next_of = jnp.where(next_of == e, -1, next_of)
    next_expert = jnp.sum(jnp.where(tile_expert[:, None] == experts, next_of, 0), axis=1).astype(jnp.int32)

    group_end = jnp.sum(jnp.where(tile_expert[:, None] == experts, offs + counts, 0), axis=1)
    tile_rows = jnp.clip(group_end - tile_start, 0, tme).astype(jnp.int32)

    xs = _sc_dispatch(n2, pos_t, max_tiles * tme)
    ys = _experts(xs, tile_expert, n_active, tile_group, next_expert, tile_rows, w_up[0], b_up[0], w_down[0], b_down[0], tme)
    yg = _sc_gather(ys, pos_t.reshape(-1)).reshape(TOP_K, n_tok, d // 2)

    y_p, y_s = _final(tl, h, yg, probs, ada_p, ada_s, g_final, d)
    return (y_p, y_s, sret_p[None], sgla_p[None], sret_s[None], sgla_s[None])


def kernel(x_prompt, x_sample, c_prompt, c_sample, state_ret, state_gla, w_ada, b_ada, g_norm_mix, g_norm_ffn,
           w_in, w_gk_up, b_gk, g_gla_norm, w_ret_o, w_gla_o, w_out, w_router, b_router, w_up, b_up,
           w_down, b_down, g_final):
    t = x_prompt.shape[1]
    bs, ts = x_sample.shape[0], x_sample.shape[1]
    return _forward(x_prompt, x_sample, c_prompt, c_sample, state_ret, state_gla, w_ada, b_ada, g_norm_mix,
                    g_norm_ffn, w_in, w_gk_up, b_gk, g_gla_norm, w_ret_o, w_gla_o, w_out, w_router, b_router,
                    w_up, b_up, w_down, b_down, g_final,
                    tm=_pick(bs * ts, 512), tb=_pick(t, 256), gsz=_pick(bs, 8), tme=512)
```

```python
import functools

import jax
import jax.numpy as jnp
from jax import lax
from jax.experimental import pallas as pl
from jax.experimental.pallas import tpu as pltpu
from jax.experimental.pallas import tpu_sc as plsc

F32 = jnp.float32
BF16 = jnp.bfloat16

N_HEADS = 4
GLA_GATE_RANK = 16
GLA_GATE_NORM = 16.0
GLA_CHUNK = 64
ROPE_BASE = 10000.0
N_EXPERTS = 32
TOP_K = 4
SWIGLU_LIMIT = 7.0
SWIGLU_ALPHA = 1.702
EPS = 1e-6
PAST_LEN = 16384
SUBLANES = 8
N_SEG = 8
EXPERT_ROW_SLABS = 2

VMEM_LIMIT = 56 * 1024 * 1024


def _mm(a, b):
    return jnp.dot(a, b, preferred_element_type=F32)


def _mm_nt(a, b):
    return lax.dot_general(a, b, (((1,), (1,)), ((), ())), preferred_element_type=F32)


def _silu(x):
    return x * jax.nn.sigmoid(x)


def _split_hi_lo(x):
    hi = x.astype(BF16)
    lo = (x - hi.astype(F32)).astype(BF16)
    return hi, lo


def _pack_pair(x):
    w = x.shape[1] // 2
    lo = lax.bitcast_convert_type(x[:, :w].astype(BF16).astype(F32), jnp.uint32)
    hi = lax.bitcast_convert_type(x[:, w:].astype(BF16).astype(F32), jnp.uint32)
    return (hi & jnp.uint32(0xFFFF0000)) | (lo >> 16)


def _unpack_pair(p):
    lo = lax.bitcast_convert_type(p << 16, F32)
    hi = lax.bitcast_convert_type(p & jnp.uint32(0xFFFF0000), F32)
    return lo, hi


def _rms_mod(x3, g, sc, sh):
    ms = jnp.mean(x3 * x3, axis=-1, keepdims=True)
    return x3 * lax.rsqrt(ms + EPS) * g * (1.0 + sc) + sh


def _resident(shape):
    zeros = (0,) * len(shape)
    return pl.BlockSpec(shape, lambda i: zeros, pipeline_mode=pl.Buffered(1))


def _const(shape):
    zeros = (0,) * len(shape)
    return pl.BlockSpec(shape, lambda i: zeros)


def _ada_kernel(c_ref, w_ref, b_ref, o_ref):
    cf = _silu(c_ref[...])
    o_ref[0] = _mm(cf.astype(BF16), w_ref[...].astype(BF16)) + b_ref[0]


def _ada(c_all, w_ada, b_ada):
    bc, d = c_all.shape
    n = w_ada.shape[1] // d
    return pl.pallas_call(
        _ada_kernel,
        grid=(n,),
        in_specs=[pl.BlockSpec((bc, d), lambda j: (0, 0)),
                  pl.BlockSpec((d, d), lambda j: (0, j)),
                  pl.BlockSpec((1, 1, d), lambda j: (j, 0, 0))],
        out_specs=pl.BlockSpec((1, bc, d), lambda j: (j, 0, 0)),
        out_shape=jax.ShapeDtypeStruct((n, bc, d), F32),
        compiler_params=pltpu.CompilerParams(dimension_semantics=("arbitrary",), vmem_limit_bytes=VMEM_LIMIT),
        name="ada",
    )(c_all, w_ada, b_ada.reshape(n, 1, d))


class _Tiles:
    def __init__(self, b, t, bs, ts, tm):
        assert t % tm == 0 and (bs * ts) % tm == 0 and tm % ts == 0
        self.b, self.t, self.bs, self.ts, self.tm = b, t, bs, ts, tm
        self.tpb = t // tm
        self.n_pt = b * self.tpb
        self.gs = tm // ts
        self.n_st = (bs * ts) // tm
        self.n = self.n_pt + self.n_st
        self.n_tok = b * t + bs * ts

    def xp_spec(self, d):
        last, tpb = self.n_pt - 1, self.tpb
        return pl.BlockSpec((1, self.tm, d), lambda i: (jnp.minimum(i, last) // tpb, jnp.minimum(i, last) % tpb, 0))

    def xs_spec(self, d):
        n_pt = self.n_pt
        return pl.BlockSpec((self.gs, self.ts, d), lambda i: (jnp.maximum(i - n_pt, 0), 0, 0))

    def adap_spec(self, which, d):
        last, tpb = self.n_pt - 1, self.tpb
        return pl.BlockSpec((1, 1, 1, d), lambda i: (which, jnp.minimum(i, last) // tpb, 0, 0))

    def adas_spec(self, which, d):
        n_pt = self.n_pt
        return pl.BlockSpec((1, self.gs, 1, d), lambda i: (which, jnp.maximum(i - n_pt, 0), 0, 0))

    def tok_spec(self, width):
        return pl.BlockSpec((self.tm, width), lambda i: (i, 0))

    def s_x_spec(self, d):
        return pl.BlockSpec((self.gs, self.ts, d), lambda i: (i, 0, 0))

    def s_ada_spec(self, which, d):
        return pl.BlockSpec((1, self.gs, 1, d), lambda i: (which, i, 0, 0))

    def s_row_spec(self, width):
        return pl.BlockSpec((self.tm, width), lambda i: (i, 0))

    def s_tok_spec(self, width):
        n_pt = self.n_pt
        return pl.BlockSpec((self.tm, width), lambda i: (n_pt + i, 0))


def _rope_tables(pos0, t, dk):
    half = dk // 2
    inv = ROPE_BASE ** (-jnp.arange(half, dtype=jnp.float32) / half)
    pos = pos0 + jnp.arange(t)
    ang = pos.astype(jnp.float32)[:, None] * inv[None, :]
    cos, sin = jnp.cos(ang), jnp.sin(ang)
    return jnp.concatenate([cos, cos], axis=-1), jnp.concatenate([-sin, sin], axis=-1)


def _ret_tables(c, dk, dv):
    h = N_HEADS
    log_gamma = jnp.log1p(-jnp.exp2(-5.0 - jnp.arange(h, dtype=jnp.float32)))
    idx = jnp.arange(c, dtype=jnp.float32)
    rel = idx[:, None] - idx[None, :]
    dmask = jnp.where(rel >= 0, jnp.exp(log_gamma[:, None, None] * jnp.maximum(rel, 0.0)), 0.0)
    kdec = jnp.exp(log_gamma[:, None] * (c - 1 - idx))
    qdec = jnp.exp(log_gamma[:, None] * (idx + 1.0))
    cdec = jnp.exp(log_gamma * c)
    return (dmask,
            jnp.broadcast_to(qdec[:, :, None], (h, c, dk)),
            jnp.broadcast_to(kdec[:, :, None], (h, c, dk)),
            jnp.broadcast_to(cdec[:, None, None], (h, 1, dv)))


def _rot(x, cos_f, sin_f):
    return x * cos_f + pltpu.roll(x, x.shape[-1] // 2, 1) * sin_f


def _cross_and_update(q_lhs, k_end, vh, states, masks):
    if masks is None:
        (s,) = states
        return _mm(q_lhs, s.astype(BF16)), [_mm(k_end.T.astype(BF16), vh)]
    cross, incs = None, []
    for s, m in zip(states, masks):
        c = _mm(q_lhs, s.astype(BF16))
        cross = c if cross is None else jnp.where(m, c, cross)
        incs.append(_mm(jnp.where(m, k_end, 0.0).T.astype(BF16), vh))
    return cross, incs


def _ret_head(q, k, vh, gh, states, masks, cos_f, sin_f, dmask, qdec, kdec, cdec):
    dk = q.shape[-1]
    q = _rot(q, cos_f, sin_f)
    k = _rot(k, cos_f, sin_f) * (dk ** -0.5)
    scores = _mm_nt(q.astype(BF16), k.astype(BF16)) * dmask
    cross, incs = _cross_and_update((q * qdec).astype(BF16), k * kdec, vh, states, masks)
    o = _mm(scores.astype(BF16), vh) + cross
    new_states = [cdec * s + u for s, u in zip(states, incs)]
    mu = jnp.mean(o, axis=-1, keepdims=True)
    oc = o - mu
    var = jnp.mean(oc * oc, axis=-1, keepdims=True)
    return _silu(gh) * (oc * lax.rsqrt(var + EPS)), new_states


def _gla_head(q, k, vh, gh, b, states, masks, c, gnorm, causal):
    dk = q.shape[-1]
    b_t = b.T
    if masks is None:
        b_last = b[c - 1:c, :]
    else:
        b_last = None
        for g, m in enumerate(masks):
            row = b[g * c + c - 1:g * c + c, :]
            b_last = row if b_last is None else jnp.where(m, row, b_last)
    q_in = (q * (dk ** -0.5) * jnp.exp(b)).astype(BF16)
    k_in = (k * jnp.exp(-b)).astype(BF16)
    scores = jnp.where(causal, _mm_nt(q_in, k_in), 0.0)
    cross, incs = _cross_and_update(q_in, k * jnp.exp(b_last - b), vh, states, masks)
    o = _mm(scores.astype(BF16), vh) + cross
    new_states = [jnp.exp(b_t[:, g * c + c - 1:g * c + c]) * s + u for g, (s, u) in enumerate(zip(states, incs))]
    o = o * lax.rsqrt(jnp.mean(o * o, axis=-1, keepdims=True) + EPS) * gnorm
    return _silu(gh) * o, new_states


def _log_a(glr, wgk, bgk):
    z = _mm(glr.astype(BF16), wgk) + bgk
    return (jnp.minimum(z, 0.0) - jnp.log1p(jnp.exp(-jnp.abs(z)))) / GLA_GATE_NORM


def _causal(c):
    return lax.broadcasted_iota(jnp.int32, (c, c), 0) >= lax.broadcasted_iota(jnp.int32, (c, c), 1)


def _gate_rank_columns(w_ref, d):
    start = (N_SEG - 2) * d
    return w_ref.at[:, start:start + GLA_GATE_RANK]


def _merge_gate_columns(w_ref, d):
    start = (N_SEG - 2) * d + GLA_GATE_RANK
    return w_ref[:, start:start + 2 * d]


def _w_seg(w_ref, wm_ref, seg, d):
    if seg < N_SEG - 2:
        return w_ref[:, seg * d:(seg + 1) * d]
    return wm_ref[:, (seg - (N_SEG - 2)) * d:(seg - (N_SEG - 3)) * d]


def _proj_block(d, x3, sh, sc, g, w_ref, wm_ref, wl_ref, proj_s, glr_s):
    n = _rms_mod(x3, g, sc, sh).reshape(-1, d).astype(BF16)
    for seg in range(N_SEG):
        proj_s[:, seg * d:(seg + 1) * d] = _mm(n, _w_seg(w_ref, wm_ref, seg, d)).astype(BF16)
    glr_s[...] = _mm(n, wl_ref[...])


def _mix_block(d, tb, proj_s, glr_s, cos_f, sin_f, dmask_ref, qdec_ref, kdec_ref, cdec_ref, tri, wgk, bgk, gnorm,
               sr_s, sg_s, oret_ref, ogla_ref, mg_ref, r_off):
    dk, dv, hq = d // 8, d // 4, d // 2
    rqk, rv, rg, gqk, gv, gg, mg = (i * d for i in range(7))
    for h in range(N_HEADS):
        o, (s_new,) = _ret_head(proj_s[:, rqk + h * dk:rqk + (h + 1) * dk].astype(F32),
                                proj_s[:, rqk + hq + h * dk:rqk + hq + (h + 1) * dk].astype(F32),
                                proj_s[:, rv + h * dv:rv + (h + 1) * dv],
                                proj_s[:, rg + h * dv:rg + (h + 1) * dv].astype(F32),
                                [sr_s[h]], None, cos_f, sin_f, dmask_ref[h], qdec_ref[h], kdec_ref[h],
                                cdec_ref[h])
        sr_s[h] = s_new
        oret_ref[r_off:r_off + tb, h * dv:(h + 1) * dv] = o.astype(BF16)

    la_hi, la_lo = _split_hi_lo(_log_a(glr_s[...], wgk, bgk))
    b = _mm(tri, la_hi) + _mm(tri, la_lo)
    cg = GLA_CHUNK
    n_c = tb // cg
    causal = _causal(cg)
    b_last = jnp.concatenate([jnp.broadcast_to(b[c * cg + cg - 1:c * cg + cg, :], (cg, hq)) for c in range(n_c)],
                             axis=0)
    last_rows = jnp.concatenate([b[c * cg + cg - 1:c * cg + cg, :] for c in range(n_c)]
                                + [jnp.zeros((SUBLANES - n_c % SUBLANES, hq), F32)] * (n_c % SUBLANES != 0), axis=0)
    dec_cols = jnp.exp(last_rows.T)
    gq = proj_s[:, gqk:gqk + hq].astype(F32)
    gk = proj_s[:, gqk + hq:gqk + 2 * hq].astype(F32)
    q_in = (gq * (dk ** -0.5) * jnp.exp(b)).astype(BF16)
    k_in = (gk * jnp.exp(-b)).astype(BF16)
    k_end = gk * jnp.exp(b_last - b)
    intra, incs, decs = {}, {}, {}
    for c in range(n_c):
        rows = slice(c * cg, (c + 1) * cg)
        for h in range(N_HEADS):
            cols = slice(h * dk, (h + 1) * dk)
            vh = proj_s[rows, gv + h * dv:gv + (h + 1) * dv]
            scores = jnp.where(causal, _mm_nt(q_in[rows, cols], k_in[rows, cols]), 0.0)
            intra[c, h] = _mm(scores.astype(BF16), vh)
            incs[c, h] = _mm(k_end[rows, cols].T.astype(BF16), vh)
            decs[c, h] = dec_cols[cols, c:c + 1]
    for h in range(N_HEADS):
        cols = slice(h * dk, (h + 1) * dk)
        s = sg_s[h]
        for c in range(n_c):
            rows = slice(c * cg, (c + 1) * cg)
            o = intra[c, h] + _mm(q_in[rows, cols], s.astype(BF16))
            s = decs[c, h] * s + incs[c, h]
            o = o * lax.rsqrt(jnp.mean(o * o, axis=-1, keepdims=True) + EPS) * gnorm
            gh = proj_s[rows, gg + h * dv:gg + (h + 1) * dv].astype(F32)
            ogla_ref[r_off + c * cg:r_off + (c + 1) * cg, h * dv:(h + 1) * dv] = (_silu(gh) * o).astype(BF16)
        sg_s[h] = s
    mg_ref[0, r_off:r_off + tb, :] = proj_s[:, mg:mg + d]
    mg_ref[1, r_off:r_off + tb, :] = proj_s[:, mg + d:mg + 2 * d]


ROUTE_SLABS = 4


def _rows2d(a3, tm, d):
    if a3.shape[0] == 1:
        return a3.reshape(1, d)
    return jnp.broadcast_to(a3, (a3.shape[0], tm // a3.shape[0], d)).reshape(tm, d)


def _route_block(d, out_ret, out_gla, mg_ret, mg_gla, x3, gt, sh, sc, g, wro_ref, wgo_ref, wo_ref, wrh_ref, wrl_ref,
                 br, utri_ref, eye_ref, carry_s, h_ref, n2_ref, idx_ref, rank_ref, prob_ref):
    rows = out_ret.shape[0]
    e = N_EXPERTS
    half = rows // ROUTE_SLABS
    slabs = [slice(s * half, (s + 1) * half) for s in range(ROUTE_SLABS)]
    x2 = x3.reshape(rows, d)
    gt2, sh2, sc2 = (_rows2d(v, rows, d) for v in (gt, sh, sc))
    g2 = g.reshape(1, d)

    def rows_of(v, sl):
        return v if v.shape[0] == 1 else v[sl]

    ab = [(_mm(out_ret[sl], wro_ref[...]), _mm(out_gla[sl], wgo_ref[...])) for sl in slabs]
    mix = [_mm((jax.nn.sigmoid(mg_ret[sl].astype(F32)) * a + jax.nn.sigmoid(mg_gla[sl].astype(F32)) * b).astype(BF16),
               wo_ref[...]) for sl, (a, b) in zip(slabs, ab)]
    logits = []
    for sl, m in zip(slabs, mix):
        hs = x2[sl] + rows_of(gt2, sl) * m
        h_ref[sl, :] = hs
        ms = jnp.mean(hs * hs, axis=-1, keepdims=True)
        n2 = hs * lax.rsqrt(ms + EPS) * g2 * (1.0 + rows_of(sc2, sl)) + rows_of(sh2, sl)
        n2_ref[sl, :] = _pack_pair(n2)
        n_hi, n_lo = _split_hi_lo(n2)
        logits.append(_mm_nt(wrh_ref[...], n_hi) + _mm_nt(wrh_ref[...], n_lo) + _mm_nt(wrl_ref[...], n_hi) + br)

    iota_e = lax.broadcasted_iota(jnp.int32, (e, half), 0)
    slot = lax.broadcasted_iota(jnp.int32, (TOP_K, half), 0)
    carry = carry_s[...]
    for sl, work in zip(slabs, logits):
        vals, idxs = [], []
        for _ in range(TOP_K):
            m = jnp.max(work, axis=0, keepdims=True)
            ik = jnp.min(jnp.where(work == m, iota_e, e), axis=0, keepdims=True)
            vals.append(m)
            idxs.append(ik)
            work = jnp.where(iota_e == ik, -jnp.inf, work)
        ex = [jnp.exp(v - vals[0]) for v in vals]
        den = ex[0] + ex[1] + ex[2] + ex[3]
        onehot = jnp.zeros((e, half), F32)
        for ik in idxs:
            onehot = onehot + (iota_e == ik).astype(F32)
        cum = _mm(onehot.astype(BF16), utri_ref[...]) + carry
        carry = carry + jnp.sum(onehot, axis=1, keepdims=True)
        idx_o = jnp.zeros((TOP_K, half), jnp.int32)
        rank_o = jnp.zeros((TOP_K, half), jnp.int32)
        prob_t = jnp.zeros((TOP_K, half), F32)
        for k in range(TOP_K):
            rk = jnp.sum(jnp.where(iota_e == idxs[k], cum, 0.0), axis=0, keepdims=True).astype(jnp.int32)
            idx_o = jnp.where(slot == k, idxs[k], idx_o)
            rank_o = jnp.where(slot == k, rk, rank_o)
            prob_t = jnp.where(slot == k, ex[k] / den, prob_t)
        idx_ref[:, sl] = idx_o
        rank_ref[:, sl] = rank_o
        p1 = prob_t.astype(BF16)
        r1 = prob_t - p1.astype(F32)
        p2 = r1.astype(BF16)
        p3 = (r1 - p2.astype(F32)).astype(BF16)
        pieces = jnp.concatenate([p1, p2, p3, jnp.zeros_like(p1)], axis=0)
        t = _mm_nt(eye_ref[...], pieces)
        prob_ref[sl, :] = t[:, 0:TOP_K] + t[:, TOP_K:2 * TOP_K] + t[:, 2 * TOP_K:3 * TOP_K]
    carry_s[...] = carry


def _frontp_kernel(d, tb, ntb, x0_ref, xa_ref, xb_ref, sh0_ref, sc0_ref, sha_ref, sca_ref, shb_ref, scb_ref,
                   g_ref, w_ref, cosa_ref, sina_ref, cosb_ref, sinb_ref,
                   dmask_ref, qdec_ref, kdec_ref, cdec_ref, tri_ref, wgk_ref, bgk_ref, gn_ref,
                   oret_ref, ogla_ref, mg_ref, sret_ref, sgla_ref, pa_s, pb_s, ga_s, gb_s, sr_s, sg_s, wm_ref):
    p = pl.program_id(0)
    blk = 2 * p
    g = g_ref[...]
    proj = functools.partial(_proj_block, d)
    mix = functools.partial(_mix_block, d, tb)
    tables = (dmask_ref, qdec_ref, kdec_ref, cdec_ref, tri_ref[...], wgk_ref[...], bgk_ref[...], gn_ref[...])

    wl_ref = _gate_rank_columns(w_ref, d)

    @pl.when(p == 0)
    def _():
        wm_ref[...] = _merge_gate_columns(w_ref, d)
        proj(x0_ref[...], sh0_ref[0], sc0_ref[0], g, w_ref, wm_ref, wl_ref, pa_s, ga_s)

    @pl.when(blk % ntb == 0)
    def _():
        sr_s[...] = jnp.zeros_like(sr_s)
        sg_s[...] = jnp.zeros_like(sg_s)

    proj(xa_ref[...], sha_ref[0], sca_ref[0], g, w_ref, wm_ref, wl_ref, pb_s, gb_s)
    mix(pa_s, ga_s, cosa_ref[...], sina_ref[...], *tables, sr_s, sg_s, oret_ref, ogla_ref, mg_ref, 0)
    proj(xb_ref[...], shb_ref[0], scb_ref[0], g, w_ref, wm_ref, wl_ref, pa_s, ga_s)
    mix(pb_s, gb_s, cosb_ref[...], sinb_ref[...], *tables, sr_s, sg_s, oret_ref, ogla_ref, mg_ref, tb)

    @pl.when((blk + 1) % ntb == ntb - 1)
    def _():
        sret_ref[0] = sr_s[...]
        sgla_ref[0] = sg_s[...]


def _chunk_tri(tb, cg):
    i = jnp.arange(tb)
    return ((i[:, None] >= i[None, :]) & (i[:, None] // cg == i[None, :] // cg)).astype(BF16)


def _front_prompt(x_p, ada_p, g_mix, w_all, tb, w_gk, b_gk, g_gla):
    b, t, d = x_p.shape
    dk, dv, hq, h = d // 8, d // 4, d // 2, N_HEADS
    ntb = t // tb
    n_blk = b * ntb
    n_tok = b * t
    assert ntb % 2 == 0
    cos_f, sin_f = _rope_tables(0, t, dk)
    dmask, qdec, kdec, cdec = _ret_tables(tb, dk, dv)
    tri = _chunk_tri(tb, GLA_CHUNK)

    def first(p):
        return 0 * p

    def even(p):
        return 2 * p

    def odd(p):
        return 2 * p + 1

    def nxt(p):
        return jnp.minimum(2 * p + 2, n_blk - 1)

    def x_spec(blk_of):
        return pl.BlockSpec((1, tb, d), lambda p: (blk_of(p) // ntb, blk_of(p) % ntb, 0))

    def ada_spec(which, blk_of):
        return pl.BlockSpec((1, 1, 1, d), lambda p: (which, blk_of(p) // ntb, 0, 0))

    def rope_spec(blk_of):
        return pl.BlockSpec((tb, dk), lambda p: (blk_of(p) % ntb, 0))

    state_spec = pl.BlockSpec((1, h, dk, dv), lambda p: ((2 * p) // ntb, 0, 0, 0))
    tok_spec = pl.BlockSpec((2 * tb, d), lambda p: (p, 0))
    return pl.pallas_call(
        functools.partial(_frontp_kernel, d, tb, ntb),
        grid=(n_blk // 2,),
        in_specs=[_resident((1, tb, d)), x_spec(odd), x_spec(nxt),
                  ada_spec(0, first), ada_spec(1, first), ada_spec(0, odd), ada_spec(1, odd),
                  ada_spec(0, nxt), ada_spec(1, nxt),
                  _resident((1, 1, d)), _resident(w_all.shape),
                  rope_spec(even), rope_spec(even), rope_spec(odd), rope_spec(odd),
                  _resident((h, tb, tb)), _resident((h, tb, dk)), _resident((h, tb, dk)), _resident((h, 1, dv)),
                  _resident((tb, tb)), _resident((GLA_GATE_RANK, hq)), _resident((1, hq)), _resident((1, dv))],
        out_specs=[tok_spec, tok_spec, pl.BlockSpec((2, 2 * tb, d), lambda p: (0, p, 0)), state_spec, state_spec],
        out_shape=[jax.ShapeDtypeStruct((n_tok, d), BF16), jax.ShapeDtypeStruct((n_tok, d), BF16),
                   jax.ShapeDtypeStruct((2, n_tok, d), BF16),
                   jax.ShapeDtypeStruct((b, h, dk, dv), F32), jax.ShapeDtypeStruct((b, h, dk, dv), F32)],
        scratch_shapes=[pltpu.VMEM((tb, N_SEG * d), BF16), pltpu.VMEM((tb, N_SEG * d), BF16),
                        pltpu.VMEM((tb, GLA_GATE_RANK), F32), pltpu.VMEM((tb, GLA_GATE_RANK), F32),
                        pltpu.VMEM((h, dk, dv), F32), pltpu.VMEM((h, dk, dv), F32), pltpu.VMEM((d, 2 * d), BF16)],
        compiler_params=pltpu.CompilerParams(dimension_semantics=("arbitrary",), vmem_limit_bytes=VMEM_LIMIT),
        name="front_prompt",
    )(x_p, x_p, x_p, ada_p, ada_p, ada_p, ada_p, ada_p, ada_p, g_mix.reshape(1, 1, d), w_all,
      cos_f, sin_f, cos_f, sin_f, dmask, qdec, kdec, cdec, tri, w_gk, b_gk, g_gla)


def _inproj_kernel(d, xs_ref, shs_ref, scs_ref, g_ref, w_ref, proj_ref, glr_ref, wm_ref):
    @pl.when(pl.program_id(0) == 0)
    def _():
        wm_ref[...] = _merge_gate_columns(w_ref, d)

    n = _rms_mod(xs_ref[...], g_ref[...], scs_ref[0], shs_ref[0]).reshape(-1, d).astype(BF16)
    for s in range(N_SEG):
        proj_ref[s] = _mm(n, _w_seg(w_ref, wm_ref, s, d)).astype(BF16)
    glr_ref[...] = _mm(n, _gate_rank_columns(w_ref, d)[...])


def _inproj_sample(tl, x_s, ada_s, g_mix, w_all):
    bs, ts, d = x_s.shape
    n_tok = bs * ts
    return pl.pallas_call(
        functools.partial(_inproj_kernel, d),
        grid=(tl.n_st,),
        in_specs=[tl.s_x_spec(d), tl.s_ada_spec(0, d), tl.s_ada_spec(1, d),
                  _resident((1, 1, d)), _resident(w_all.shape)],
        out_specs=[pl.BlockSpec((N_SEG, tl.tm, d), lambda i: (0, i, 0)), tl.s_row_spec(GLA_GATE_RANK)],
        out_shape=[jax.ShapeDtypeStruct((N_SEG, n_tok, d), BF16), jax.ShapeDtypeStruct((n_tok, GLA_GATE_RANK), F32)],
        scratch_shapes=[pltpu.VMEM((d, 2 * d), BF16)],
        compiler_params=pltpu.CompilerParams(dimension_semantics=("arbitrary",), vmem_limit_bytes=VMEM_LIMIT),
        name="inproj_sample",
    )(x_s, ada_s, ada_s, g_mix.reshape(1, 1, d), w_all)


def _mixs_kernel(d, ts, gsz, rqk_ref, rv_ref, rg_ref, gqk_ref, gv_ref, gg_ref, glr_ref, cos_ref, sin_ref,
                 dmask_ref, qdec_ref, kdec_ref, cdec_ref, wgk_ref, bgk_ref, gn_ref, sr_in, sg_in,
                 oret_ref, ogla_ref, sr_out, sg_out):
    dk, dv, hq = d // 8, d // 4, d // 2
    pair = 2 * ts
    cos_f, sin_f = cos_ref[...], sin_ref[...]
    gnorm = gn_ref[...]
    ri = lax.broadcasted_iota(jnp.int32, (pair, pair), 0)
    ci = lax.broadcasted_iota(jnp.int32, (pair, pair), 1)
    causal = jnp.logical_and(ri >= ci, (ri < ts) == (ci < ts))
    tri = causal.astype(F32).astype(BF16)
    first = lax.broadcasted_iota(jnp.int32, (pair, 1), 0) < ts
    masks = [first, jnp.logical_not(first)]

    def body(j, carry):
        rows = pl.ds(pl.multiple_of(j * pair, pair), pair)
        s0, s1 = 2 * j, 2 * j + 1
        la_hi, la_lo = _split_hi_lo(_log_a(glr_ref[rows, :], wgk_ref[...], bgk_ref[...]))
        b = _mm(tri, la_hi) + _mm(tri, la_lo)
        for h in range(N_HEADS):
            o, (n0, n1) = _ret_head(rqk_ref[0, rows, h * dk:(h + 1) * dk].astype(F32),
                                    rqk_ref[0, rows, hq + h * dk:hq + (h + 1) * dk].astype(F32),
                                    rv_ref[0, rows, h * dv:(h + 1) * dv],
                                    rg_ref[0, rows, h * dv:(h + 1) * dv].astype(F32),
                                    [sr_in[s0, h], sr_in[s1, h]], masks, cos_f, sin_f,
                                    dmask_ref[h], qdec_ref[h], kdec_ref[h], cdec_ref[h])
            sr_out[s0, h] = n0
            sr_out[s1, h] = n1
            oret_ref[rows, h * dv:(h + 1) * dv] = o.astype(BF16)
            o, (n0, n1) = _gla_head(gqk_ref[0, rows, h * dk:(h + 1) * dk].astype(F32),
                                    gqk_ref[0, rows, hq + h * dk:hq + (h + 1) * dk].astype(F32),
                                    gv_ref[0, rows, h * dv:(h + 1) * dv],
                                    gg_ref[0, rows, h * dv:(h + 1) * dv].astype(F32),
                                    b[:, h * dk:(h + 1) * dk], [sg_in[s0, h], sg_in[s1, h]], masks, ts,
                                    gnorm, causal)
            sg_out[s0, h] = n0
            sg_out[s1, h] = n1
            ogla_ref[rows, h * dv:(h + 1) * dv] = o.astype(BF16)
        return carry

    lax.fori_loop(0, gsz // 2, body, 0, unroll=2)


def _pair_tables(ts, dk, dv):
    cos_f, sin_f = _rope_tables(PAST_LEN, ts, dk)
    dmask, qdec, kdec, cdec = _ret_tables(ts, dk, dv)
    zero = jnp.zeros_like(dmask)
    dmask2 = jnp.concatenate([jnp.concatenate([dmask, zero], axis=2), jnp.concatenate([zero, dmask], axis=2)], axis=1)

    def twice(a, axis):
        return jnp.concatenate([a, a], axis=axis)

    return twice(cos_f, 0), twice(sin_f, 0), dmask2, twice(qdec, 1), twice(kdec, 1), cdec


def _mix_sample(bs, ts, d, gsz, proj, glr, state_ret, state_gla, w_gk, b_gk, g_gla):
    dk, dv, hq, h = d // 8, d // 4, d // 2, N_HEADS
    assert GLA_CHUNK % ts == 0 and bs % gsz == 0 and gsz % 4 == 0
    rows = gsz * ts
    pair = 2 * ts
    cos_f, sin_f, dmask, qdec, kdec, cdec = _pair_tables(ts, dk, dv)

    def seg(s):
        return pl.BlockSpec((1, rows, d), lambda i: (s, i, 0))

    state_spec = pl.BlockSpec((gsz, h, dk, dv), lambda i: (i, 0, 0, 0))
    tok_spec = pl.BlockSpec((rows, d), lambda i: (i, 0))
    return pl.pallas_call(
        functools.partial(_mixs_kernel, d, ts, gsz),
        grid=(bs // gsz,),
        in_specs=[seg(0), seg(1), seg(2), seg(3), seg(4), seg(5),
                  pl.BlockSpec((rows, GLA_GATE_RANK), lambda i: (i, 0)),
                  _const((pair, dk)), _const((pair, dk)),
                  _const((h, pair, pair)), _const((h, pair, dk)), _const((h, pair, dk)), _const((h, 1, dv)),
                  _const((GLA_GATE_RANK, hq)), _const((1, hq)), _const((1, dv)),
                  state_spec, state_spec],
        out_specs=[tok_spec, tok_spec, state_spec, state_spec],
        out_shape=[jax.ShapeDtypeStruct((bs * ts, d), BF16), jax.ShapeDtypeStruct((bs * ts, d), BF16),
                   jax.ShapeDtypeStruct((bs, h, dk, dv), F32), jax.ShapeDtypeStruct((bs, h, dk, dv), F32)],
        compiler_params=pltpu.CompilerParams(dimension_semantics=("arbitrary",), vmem_limit_bytes=VMEM_LIMIT),
        name="mix_sample",
    )(proj, proj, proj, proj, proj, proj, glr, cos_f, sin_f, dmask, qdec, kdec, cdec, w_gk, b_gk, g_gla,
      state_ret, state_gla)


def _outproj_kernel(n_pt, d, orp_ref, ogp_ref, ors_ref, ogs_ref, mgrp_ref, mggp_ref, mgrs_ref, mggs_ref,
                    xp_ref, xs_ref, gtp_ref, shp_ref, scp_ref, gts_ref, shs_ref, scs_ref, g_ref,
                    wro_ref, wgo_ref, wo_ref, wrh_ref, wrl_ref, br_ref, utri_ref, eye_ref,
                    h_ref, n2_ref, idx_ref, rank_ref, prob_ref, cnt_ref, carry_s):
    i = pl.program_id(0)

    @pl.when(i == 0)
    def _():
        carry_s[...] = jnp.zeros_like(carry_s)

    route = functools.partial(_route_block, d)
    tail = (g_ref[...], wro_ref, wgo_ref, wo_ref, wrh_ref, wrl_ref, br_ref[...], utri_ref, eye_ref, carry_s,
            h_ref, n2_ref, idx_ref, rank_ref, prob_ref)

    @pl.when(i < n_pt)
    def _():
        route(orp_ref[...], ogp_ref[...], mgrp_ref[0], mggp_ref[0], xp_ref[...], gtp_ref[0], shp_ref[0], scp_ref[0],
              *tail)

    @pl.when(i >= n_pt)
    def _():
        route(ors_ref[...], ogs_ref[...], mgrs_ref[0], mggs_ref[0], xs_ref[...], gts_ref[0], shs_ref[0], scs_ref[0],
              *tail)

    @pl.when(i == pl.num_programs(0) - 1)
    def _():
        cnt_ref[...] = carry_s[...].astype(jnp.int32)


def _outproj(tl, oret_p, ogla_p, oret_s, ogla_s, mg_p, proj_s, x_p, x_s, ada_p, ada_s, g_ffn,
             w_ret_o, w_gla_o, w_out, w_r_hi, w_r_lo, b_router):
    d = x_p.shape[-1]
    tm, e, n_pt = tl.tm, N_EXPERTS, tl.n_pt
    last = n_pt - 1
    p_spec = pl.BlockSpec((tm, d), lambda i: (jnp.minimum(i, last), 0))
    s_spec = pl.BlockSpec((tm, d), lambda i: (jnp.maximum(i - n_pt, 0), 0))

    def mgp_spec(seg):
        return pl.BlockSpec((1, tm, d), lambda i: (seg, jnp.minimum(i, last), 0))

    def mgs_spec(seg):
        return pl.BlockSpec((1, tm, d), lambda i: (seg, jnp.maximum(i - n_pt, 0), 0))

    slot_spec = pl.BlockSpec((TOP_K, tm), lambda i: (0, i))
    half = tm // ROUTE_SLABS
    token = jnp.arange(half)
    utri = (token[:, None] < token[None, :]).astype(BF16)
    eye = jnp.eye(half, dtype=BF16)

    return pl.pallas_call(
        functools.partial(_outproj_kernel, n_pt, d),
        grid=(tl.n,),
        in_specs=[p_spec, p_spec, s_spec, s_spec, mgp_spec(0), mgp_spec(1), mgs_spec(6), mgs_spec(7),
                  tl.xp_spec(d), tl.xs_spec(d),
                  tl.adap_spec(2, d), tl.adap_spec(3, d), tl.adap_spec(4, d),
                  tl.adas_spec(2, d), tl.adas_spec(3, d), tl.adas_spec(4, d),
                  _resident((1, 1, d)), _resident((d, d)), _resident((d, d)), _resident((d, d)),
                  _resident((e, d)), _resident((e, d)), _resident((e, 1)),
                  _resident((half, half)), _resident((half, half))],
        out_specs=[tl.tok_spec(d), tl.tok_spec(d // 2), slot_spec, slot_spec, tl.tok_spec(TOP_K),
                   pl.BlockSpec((e, 1), lambda i: (0, 0))],
        out_shape=[jax.ShapeDtypeStruct((tl.n_tok, d), F32), jax.ShapeDtypeStruct((tl.n_tok, d // 2), jnp.uint32),
                   jax.ShapeDtypeStruct((TOP_K, tl.n_tok), jnp.int32),
                   jax.ShapeDtypeStruct((TOP_K, tl.n_tok), jnp.int32),
                   jax.ShapeDtypeStruct((tl.n_tok, TOP_K), F32),
                   jax.ShapeDtypeStruct((e, 1), jnp.int32)],
        scratch_shapes=[pltpu.VMEM((e, 1), F32)],
        compiler_params=pltpu.CompilerParams(dimension_semantics=("arbitrary",), vmem_limit_bytes=VMEM_LIMIT),
        name="outproj",
    )(oret_p, ogla_p, oret_s, ogla_s, mg_p, mg_p, proj_s, proj_s, x_p, x_s, ada_p, ada_p, ada_p, ada_s, ada_s, ada_s,
      g_ffn.reshape(1, 1, d), w_ret_o, w_gla_o, w_out, w_r_hi, w_r_lo, b_router.reshape(e, 1), utri, eye)


EXPERT_TILES_PER_STEP = 2
EXPERT_TAIL_ROWS = 128


def _expert_kernel(f, tme, te_ref, na_ref, grp_ref, nxt_ref, tr_ref, x_ref, wu_hbm, wd_hbm, *rest):
    n_t = EXPERT_TILES_PER_STEP
    bu_refs, bd_refs = rest[:n_t], rest[n_t:2 * n_t]
    y_ref, wu_f, wd_f, wu_s, wd_s, sem = rest[2 * n_t:]
    step = pl.program_id(0)

    def fetch(expert, s):
        return (pltpu.make_async_copy(wu_hbm.at[expert], wu_f.at[s], sem.at[0, s]),
                pltpu.make_async_copy(wd_hbm.at[expert], wd_f.at[s], sem.at[1, s]))

    @pl.when(step == 0)
    def _():
        for c in fetch(te_ref[0], 0):
            c.start()

    slab = tme // EXPERT_ROW_SLABS
    half = x_ref.shape[1]
    for t in range(n_t):
        j = step * n_t + t
        active = j < na_ref[0]
        first = jnp.logical_or(j == 0, te_ref[j] != te_ref[jnp.maximum(j - 1, 0)])
        slot = grp_ref[j] % 2

        @pl.when(jnp.logical_and(active, first))
        def _(j=j, slot=slot):
            for c in fetch(te_ref[j], slot):
                c.wait()

            @pl.when(nxt_ref[j] >= 0)
            def _():
                for c in fetch(nxt_ref[j], 1 - slot):
                    c.start()

            wu_s[...] = wu_f[slot].astype(BF16)

        def compute(rows, convert_down, t=t, slot=slot):
            x_lo, x_hi = _unpack_pair(x_ref[rows, :])
            gu = _mm(x_lo.astype(BF16), wu_s[:half, :]) + _mm(x_hi.astype(BF16), wu_s[half:, :]) + bu_refs[t][0]
            if convert_down:
                wd_s[...] = wd_f[slot].astype(BF16)
            gate = jnp.minimum(gu[:, :f], SWIGLU_LIMIT)
            up = jnp.clip(gu[:, f:], -SWIGLU_LIMIT, SWIGLU_LIMIT)
            act = (up + 1.0) * gate * jax.nn.sigmoid(SWIGLU_ALPHA * gate)
            y_ref[rows, :] = _pack_pair(_mm(act.astype(BF16), wd_s[...]) + bd_refs[t][0])

        valid = tr_ref[j]
        short = [valid <= EXPERT_TAIL_ROWS,
                 jnp.logical_and(valid > EXPERT_TAIL_ROWS, valid <= 2 * EXPERT_TAIL_ROWS)]
        for is_first in (False, True):
            kind = jnp.logical_and(active, first == is_first)

            @pl.when(jnp.logical_and(kind, valid > 2 * EXPERT_TAIL_ROWS))
            def _(t=t, compute=compute, is_first=is_first):
                for s in range(EXPERT_ROW_SLABS):
                    compute(slice(t * tme + s * slab, t * tme + (s + 1) * slab), is_first and s == 0)

            for n_pieces, cond in enumerate(short, start=1):
                @pl.when(jnp.logical_and(kind, cond))
                def _(t=t, compute=compute, n_pieces=n_pieces, is_first=is_first):
                    compute(slice(t * tme, t * tme + n_pieces * EXPERT_TAIL_ROWS), is_first)


def _experts(xs, tile_expert, n_active, tile_group, next_expert, tile_rows, w_up, b_up, w_down, b_down, tme):
    r = xs.shape[0]
    e, d, f2 = w_up.shape
    f = f2 // 2
    n_t = EXPERT_TILES_PER_STEP
    n_tiles = r // tme
    assert n_tiles % n_t == 0 and tme % EXPERT_TAIL_ROWS == 0

    def row_map(s, te, na, grp, nxt, tr):
        return (jnp.minimum(s, (na[0] - 1) // n_t), 0)

    def bias_map(t):
        return lambda s, te, na, grp, nxt, tr: (te[jnp.minimum(s * n_t + t, na[0] - 1)], 0, 0)

    hbm = pl.BlockSpec(memory_space=pl.ANY)
    return pl.pallas_call(
        functools.partial(_expert_kernel, f, tme),
        grid_spec=pltpu.PrefetchScalarGridSpec(
            num_scalar_prefetch=5,
            grid=(n_tiles // n_t,),
            in_specs=[pl.BlockSpec((n_t * tme, d // 2), row_map), hbm, hbm]
                     + [pl.BlockSpec((1, 1, f2), bias_map(t)) for t in range(n_t)]
                     + [pl.BlockSpec((1, 1, d), bias_map(t)) for t in range(n_t)],
            out_specs=pl.BlockSpec((n_t * tme, d // 2), row_map),
            scratch_shapes=[pltpu.VMEM((2, d, f2), F32), pltpu.VMEM((2, f, d), F32),
                            pltpu.VMEM((d, f2), BF16), pltpu.VMEM((f, d), BF16),
                            pltpu.SemaphoreType.DMA((2, 2))]),
        out_shape=jax.ShapeDtypeStruct((r, d // 2), jnp.uint32),
        compiler_params=pltpu.CompilerParams(dimension_semantics=("arbitrary",), vmem_limit_bytes=VMEM_LIMIT),
        name="experts",
    )(tile_expert, n_active, tile_group, next_expert, tile_rows, xs, w_up, w_down,
      *([b_up.reshape(e, 1, f2)] * n_t), *([b_down.reshape(e, 1, d)] * n_t))


def _sc_mesh():
    return plsc.VectorSubcoreMesh(core_axis_name="core", subcore_axis_name="subcore")


def _sc_split(n_rows, max_chunk):
    info = plsc.get_sparse_core_info()
    n_workers = info.num_cores * info.num_subcores
    assert n_rows % (8 * n_workers) == 0
    per_w = n_rows // n_workers
    chunk = 8
    while chunk * 2 <= max_chunk and per_w % (chunk * 2) == 0:
        chunk *= 2
    return info.num_cores, n_workers, per_w, chunk


def _sc_dispatch(x, pos_t, n_rows):
    n, w = x.shape
    nc, nw, per_w, chunk = _sc_split(n, 32)
    n_ch = per_w // chunk
    idx = pos_t.reshape(TOP_K, nw, n_ch, chunk).transpose(1, 0, 2, 3).reshape(nw, TOP_K * n_ch, chunk)

    @functools.partial(
        pl.kernel, out_type=jax.ShapeDtypeStruct((n_rows, w), x.dtype), mesh=_sc_mesh(),
        scratch_types=[pltpu.VMEM((TOP_K * n_ch, chunk), jnp.int32), pltpu.VMEM((2, chunk, w), x.dtype),
                       pltpu.SemaphoreType.DMA((2,)), pltpu.SemaphoreType.DMA((2,))])
    def scatter_rows(x_hbm, i_hbm, o_hbm, idx_v, rows_v, rsem, wsem):
        wid = lax.axis_index("subcore") * nc + lax.axis_index("core")
        base = wid * per_w
        pltpu.sync_copy(i_hbm.at[wid], idx_v)

        def read(j, slot):
            return pltpu.make_async_copy(x_hbm.at[pl.ds(base + j * chunk, chunk)], rows_v.at[slot], rsem.at[slot])

        def write(j, slot, k):
            return pltpu.make_async_copy(rows_v.at[slot], o_hbm.at[idx_v.at[k * n_ch + j]], wsem.at[slot])

        read(0, 0).start()

        @pl.loop(0, n_ch, step=2)
        def _(j0):
            for b in range(2):
                j = j0 + b

                @pl.when(j < n_ch)
                def _():
                    read(j, b).wait()

                    @pl.when(j + 1 < n_ch)
                    def _():
                        @pl.when(j >= 1)
                        def _():
                            for k in range(TOP_K):
                                write(j - 1, 1 - b, k).wait()

                        read(j + 1, 1 - b).start()

                    for k in range(TOP_K):
                        write(j, b, k).start()

        for jj in range(max(n_ch - 2, 0), n_ch):
            for k in range(TOP_K):
                write(jj, jj % 2, k).wait()

    return scatter_rows(x, idx)


def _sc_gather(table, idx):
    m = idx.shape[0]
    w = table.shape[1]
    nc, _, per_w, chunk = _sc_split(m, 64)
    n_ch = per_w // chunk

    @functools.partial(
        pl.kernel, out_type=jax.ShapeDtypeStruct((m, w), table.dtype), mesh=_sc_mesh(),
        scratch_types=[pltpu.VMEM((per_w,), jnp.int32), pltpu.VMEM((2, chunk, w), table.dtype),
                       pltpu.SemaphoreType.DMA((2,)), pltpu.SemaphoreType.DMA((2,))])
    def gather_rows(t_hbm, i_hbm, o_hbm, idx_v, rows_v, gsem, wsem):
        wid = lax.axis_index("subcore") * nc + lax.axis_index("core")
        base = wid * per_w
        pltpu.sync_copy(i_hbm.at[pl.ds(base, per_w)], idx_v)

        def gather(j, slot):
            off = pl.multiple_of(j * chunk, chunk)
            return pltpu.make_async_copy(t_hbm.at[idx_v.at[pl.ds(off, chunk)]], rows_v.at[slot], gsem.at[slot])

        def write(j, slot):
            off = pl.multiple_of(j * chunk, chunk)
            return pltpu.make_async_copy(rows_v.at[slot], o_hbm.at[pl.ds(base + off, chunk)], wsem.at[slot])

        gather(0, 0).start()

        @pl.loop(0, n_ch, step=2)
        def _(j0):
            for b in range(2):
                j = j0 + b

                @pl.when(j < n_ch)
                def _():
                    gather(j, b).wait()

                    @pl.when(j + 1 < n_ch)
                    def _():
                        @pl.when(j >= 1)
                        def _():
                            write(j - 1, 1 - b).wait()

                        gather(j + 1, 1 - b).start()

                    write(j, b).start()

        for jj in range(max(n_ch - 2, 0), n_ch):
            write(jj, jj % 2).wait()

    return gather_rows(table, idx)


def _final_kernel(n_pt, d, h_ref, yg_ref, prob_ref, gtp_ref, gts_ref, g_ref, yp_ref, ys_ref):
    i = pl.program_id(0)
    p = prob_ref[...]
    moe_lo, moe_hi = None, None
    for k in range(TOP_K):
        lo, hi = _unpack_pair(yg_ref[k])
        pk = p[:, k:k + 1]
        moe_lo = pk * lo if moe_lo is None else moe_lo + pk * lo
        moe_hi = pk * hi if moe_hi is None else moe_hi + pk * hi
    moe = jnp.concatenate([moe_lo, moe_hi], axis=1)

    def body(gt, shape):
        h3 = h_ref[...].reshape(shape) + gt * moe.reshape(shape)
        ms = jnp.mean(h3 * h3, axis=-1, keepdims=True)
        return h3 * lax.rsqrt(ms + EPS) * g_ref[...]

    @pl.when(i < n_pt)
    def _():
        yp_ref[...] = body(gtp_ref[0], yp_ref.shape)

    @pl.when(i >= n_pt)
    def _():
        ys_ref[...] = body(gts_ref[0], ys_ref.shape)


def _final(tl, h, yg, probs, ada_p, ada_s, g_final, d):
    return pl.pallas_call(
        functools.partial(_final_kernel, tl.n_pt, d),
        grid=(tl.n,),
        in_specs=[tl.tok_spec(d), pl.BlockSpec((TOP_K, tl.tm, d // 2), lambda i: (0, i, 0)), tl.tok_spec(TOP_K),
                  tl.adap_spec(5, d), tl.adas_spec(5, d), _resident((1, 1, d))],
        out_specs=[tl.xp_spec(d), tl.xs_spec(d)],
        out_shape=[jax.ShapeDtypeStruct((tl.b, tl.t, d), F32), jax.ShapeDtypeStruct((tl.bs, tl.ts, d), F32)],
        compiler_params=pltpu.CompilerParams(dimension_semantics=("arbitrary",), vmem_limit_bytes=VMEM_LIMIT),
        name="final",
    )(h, yg, probs, ada_p, ada_s, g_final.reshape(1, 1, d))


def _pick(n, pref):
    t = min(n, pref)
    while n % t:
        t //= 2
    return t


def _forward(x_prompt, x_sample, c_prompt, c_sample, state_ret, state_gla, w_ada, b_ada, g_norm_mix, g_norm_ffn,
             w_in, w_gk_up, b_gk, g_gla_norm, w_ret_o, w_gla_o, w_out, w_router, b_router, w_up, b_up,
             w_down, b_down, g_final, *, tm, tb, gsz, tme):
    b, t, d = x_prompt.shape
    bs, ts, _ = x_sample.shape
    assert w_ada.shape[0] == 1, "single layer only"
    assert (b * t) % (2 * tb) == 0 and (b * t) % tm == 0
    e = N_EXPERTS
    tl = _Tiles(b, t, bs, ts, tm)
    n_tok = tl.n_tok

    ada = _ada(jnp.concatenate([c_prompt, c_sample], axis=0), w_ada[0], b_ada[0])
    ada_p = ada[:, :b].reshape(6, b, 1, d)
    ada_s = ada[:, b:].reshape(6, bs, 1, d)

    w_all = w_in[0].astype(BF16)
    w_gk = w_gk_up[0].astype(BF16)
    bgk = b_gk[0].reshape(1, -1)
    ggn = g_gla_norm[0].reshape(1, -1)
    w_r = w_router[0].T
    w_r_hi = w_r.astype(BF16)
    w_r_lo = (w_r - w_r_hi.astype(F32)).astype(BF16)
    route_w = (g_norm_ffn[0], w_ret_o[0].astype(BF16), w_gla_o[0].astype(BF16), w_out[0].astype(BF16),
               w_r_hi, w_r_lo, b_router[0])

    oret_p, ogla_p, mg_p, sret_p, sgla_p = _front_prompt(x_prompt, ada_p, g_norm_mix[0], w_all, tb,
                                                         w_gk, bgk, ggn)
    proj_s, glr_s = _inproj_sample(tl, x_sample, ada_s, g_norm_mix[0], w_all)
    oret_s, ogla_s, sret_s, sgla_s = _mix_sample(bs, ts, d, gsz, proj_s, glr_s, state_ret[0], state_gla[0],
                                                 w_gk, bgk, ggn)
    h, n2, idx_t, rank_t, probs, counts = _outproj(tl, oret_p, ogla_p, oret_s, ogla_s, mg_p, proj_s, x_prompt, x_sample,
                                               ada_p, ada_s, *route_w)

    counts = counts[:, 0]
    gsize = ((counts + tme - 1) // tme) * tme
    ends = jnp.cumsum(gsize)
    offs = ends - gsize
    experts = jnp.arange(e, dtype=jnp.int32)
    pos_t = jnp.sum(jnp.where(idx_t[..., None] == experts, offs, 0), axis=-1) + rank_t
    max_tiles = (n_tok * TOP_K) // tme + e
    n_active = (ends[-1] // tme).astype(jnp.int32).reshape(1)
    tile_start = jnp.arange(max_tiles, dtype=jnp.int32) * tme
    tile_expert = jnp.minimum(jnp.sum((ends[None, :] <= tile_start[:, None]).astype(jnp.int32), axis=1), e - 1)
    is_first = jnp.logical_and(tile_start < ends[-1],
                               jnp.concatenate([jnp.ones((1,), bool), tile_expert[1:] != tile_expert[:-1]]))
    tile_group = jnp.cumsum(is_first.astype(jnp.int32)) - 1
    later = jnp.logical_and(experts[None, :] > experts[:, None], counts[None, :] > 0)
    next_of = jnp.min(jnp.where(later, experts[None, :], e), axis=1)
    next_of = jnp.where(next_of == e, -1, next_of)
    next_expert = jnp.sum(jnp.where(tile_expert[:, None] == experts, next_of, 0), axis=1).astype(jnp.int32)

    group_end = jnp.sum(jnp.where(tile_expert[:, None] == experts, offs + counts, 0), axis=1)
    tile_rows = jnp.clip(group_end - tile_start, 0, tme).astype(jnp.int32)

    xs = _sc_dispatch(n2, pos_t, max_tiles * tme)
    ys = _experts(xs, tile_expert, n_active, tile_group, next_expert, tile_rows, w_up[0], b_up[0], w_down[0], b_down[0], tme)
    yg = _sc_gather(ys, pos_t.reshape(-1)).reshape(TOP_K, n_tok, d // 2)

    y_p, y_s = _final(tl, h, yg, probs, ada_p, ada_s, g_final, d)
    return (y_p, y_s, sret_p[None], sgla_p[None], sret_s[None], sgla_s[None])


def kernel(x_prompt, x_sample, c_prompt, c_sample, state_ret, state_gla, w_ada, b_ada, g_norm_mix, g_norm_ffn,
           w_in, w_gk_up, b_gk, g_gla_norm, w_ret_o, w_gla_o, w_out, w_router, b_router, w_up, b_up,
           w_down, b_down, g_final):
    t = x_prompt.shape[1]
    bs, ts = x_sample.shape[0], x_sample.shape[1]
    return _forward(x_prompt, x_sample, c_prompt, c_sample, state_ret, state_gla, w_ada, b_ada, g_norm_mix,
                    g_norm_ffn, w_in, w_gk_up, b_gk, g_gla_norm, w_ret_o, w_gla_o, w_out, w_router, b_router,
                    w_up, b_up, w_down, b_down, g_final,
                    tm=_pick(bs * ts, 512), tb=_pick(t, 256), gsz=_pick(bs, 8), tme=512)
```
